```python
import math
import jax
import jax.numpy as jnp
from jax import lax
import numpy as np

D_MODEL = 4096
BATCH = 2
SEQ = 4096
DEPTH = 2

MEM_LEN = 256
HEAD_DIM = 128
ROPE_THETA = 10000.0
EPS = 1e-6
Q_BLOCK = 128

A_HEADS = D_MODEL // (4 * HEAD_DIM)
A_QK = 2 * HEAD_DIM
A_V = 2 * HEAD_DIM
A_WIDTH = A_HEADS * A_QK
B_HEADS = D_MODEL // (2 * HEAD_DIM)
B_WIDTH = B_HEADS * HEAD_DIM
MOBA_BLOCK = 256
MOBA_TOPK = 3
MOBA_QCHUNK = 64
EVEN_IN = 3 * A_WIDTH + 3 * B_WIDTH
EVEN_MIX = A_HEADS * A_V + B_WIDTH
C_HEADS = D_MODEL // HEAD_DIM
C_Q_RANK = 1536
C_KV_RANK = 512
C_NOPE = 128
C_ROPE = 64
C_V = 128
IDX_HEADS = 64
IDX_DIM = 128
IDX_ROPE = 64
IDX_TOPK = 256
DSA_QCHUNK = 128
ODD_IN = C_Q_RANK + C_KV_RANK + C_ROPE + IDX_DIM + IDX_HEADS
X_HEADS = 4
X_DIM = X_HEADS * HEAD_DIM
D_FF = 11008
N_EXPERTS = 8
TOP_K = 2
D_FF_EXPERT = 6144

N_EVEN = (DEPTH + 1) // 2
N_ODD = DEPTH // 2
F32 = jnp.float32

kernel_name = 'hybrid_diff_moba_dsa_moe_trunk'


def rms_norm(x, g):
    xf = x.astype(F32)
    y = xf * lax.rsqrt(jnp.mean(xf * xf, axis=-1, keepdims=True) + EPS)
    return (y * g.astype(F32)).astype(x.dtype)


def rope_tables(seq, dim):
    inv_freq = ROPE_THETA ** (-jnp.arange(0, dim, 2, dtype=F32) / dim)
    ang = jnp.arange(seq, dtype=F32)[:, None] * inv_freq[None, :]
    return jnp.cos(ang), jnp.sin(ang)


def apply_rope(x, cos, sin):
    x1, x2 = jnp.split(x, 2, axis=-1)
    c = cos[None, :, None, :].astype(x.dtype)
    s = sin[None, :, None, :].astype(x.dtype)
    return jnp.concatenate([x1 * c - x2 * s, x2 * c + x1 * s], axis=-1)


def swiglu(h, wg, wu, wd):
    return (jax.nn.silu(h @ wg) * (h @ wu)) @ wd


def diff_attention(q, k, v, lam):
    B, S, H, _, hd = q.shape
    scale = hd ** -0.5
    kpos = jnp.arange(S)

    def block(i):
        start = i * Q_BLOCK
        qb = lax.dynamic_slice_in_dim(q, start, Q_BLOCK, axis=1)
        qpos = start + jnp.arange(Q_BLOCK)
        causal = kpos[None, :] <= qpos[:, None]
        s = jnp.einsum('bqhmd,bkhmd->bhmqk', qb, k, preferred_element_type=F32) * scale
        p = jax.nn.softmax(jnp.where(causal, s, -jnp.inf), axis=-1)
        w = p[:, :, 0] - lam * p[:, :, 1]
        return jnp.einsum('bhqk,bkhd->bqhd', w.astype(v.dtype), v)

    out = lax.map(block, jnp.arange(S // Q_BLOCK))
    return jnp.moveaxis(out, 0, 1).reshape(B, S, H, v.shape[-1])


def moba_attention(q, k, v):
    B, S, H, hd = q.shape
    scale = hd ** -0.5
    nb = max(-(-S // MOBA_BLOCK), MOBA_TOPK)
    pad = nb * MOBA_BLOCK - S
    qt = jnp.transpose(q, (0, 2, 1, 3))
    padw = ((0, 0), (0, 0), (0, pad), (0, 0))
    kb = jnp.pad(jnp.transpose(k, (0, 2, 1, 3)), padw).reshape(B, H, nb, MOBA_BLOCK, hd)
    vb = jnp.pad(jnp.transpose(v, (0, 2, 1, 3)), padw).reshape(B, H, nb, MOBA_BLOCK, hd)
    kmean = jnp.mean(kb.astype(F32), axis=3)
    qblk = jnp.arange(S) // MOBA_BLOCK
    gate = jnp.einsum('bhsd,bhnd->bhsn', qt.astype(F32), kmean)
    past = jnp.arange(nb)[None, :] < qblk[:, None]
    _, top_idx = lax.top_k(jnp.where(past, gate, -jnp.inf), MOBA_TOPK)
    top_ok = top_idx < qblk[None, None, :, None]
    bi = jnp.arange(B)[:, None, None]
    hi = jnp.arange(H)[None, :, None]
    offs = jnp.arange(MOBA_BLOCK)

    def chunk(i):
        start = i * MOBA_QCHUNK
        blk = start // MOBA_BLOCK
        qc = lax.dynamic_slice_in_dim(qt, start, MOBA_QCHUNK, axis=2)
        sc = lax.dynamic_slice_in_dim(top_idx, start, MOBA_QCHUNK, axis=2)
        okc = lax.dynamic_slice_in_dim(top_ok, start, MOBA_QCHUNK, axis=2)
        qpos = start + jnp.arange(MOBA_QCHUNK)
        k_own = lax.dynamic_index_in_dim(kb, blk, axis=2, keepdims=False)
        v_own = lax.dynamic_index_in_dim(vb, blk, axis=2, keepdims=False)
        own_causal = (blk * MOBA_BLOCK + offs)[None, :] <= qpos[:, None]
        s_own = jnp.einsum('bhqd,bhpd->bhqp', qc, k_own, preferred_element_type=F32) * scale
        scores = [jnp.where(own_causal, s_own, -jnp.inf)]
        for j in range(MOBA_TOPK):
            kg = kb[bi, hi, sc[..., j]]
            s = jnp.einsum('bhqd,bhqpd->bhqp', qc, kg, preferred_element_type=F32) * scale
            scores.append(jnp.where(okc[..., j, None], s, -jnp.inf))
        p = jax.nn.softmax(jnp.concatenate(scores, axis=-1), axis=-1).astype(vb.dtype)
        p = p.reshape(B, H, MOBA_QCHUNK, MOBA_TOPK + 1, MOBA_BLOCK)
        out = jnp.einsum('bhqp,bhpd->bhqd', p[:, :, :, 0], v_own)
        for j in range(MOBA_TOPK):
            out = out + jnp.einsum('bhqp,bhqpd->bhqd', p[:, :, :, j + 1], vb[bi, hi, sc[..., j]])
        return out

    out = lax.map(chunk, jnp.arange(S // MOBA_QCHUNK))
    return jnp.transpose(out, (1, 0, 3, 2, 4)).reshape(B, S, H, hd)


def dsa_attention(q_nope, q_rope, ckv, k_rope, iq, ik, iw, w_uk, w_uv):
    B, S, H, _ = q_nope.shape
    n_keep = min(IDX_TOPK, S // 4)
    scale = (C_NOPE + C_ROPE) ** -0.5
    kpos = jnp.arange(S)
    bi = jnp.arange(B)[:, None, None]

    def chunk(i):
        start = i * DSA_QCHUNK
        sl = lambda a: lax.dynamic_slice_in_dim(a, start, DSA_QCHUNK, axis=1)
        qpos = start + jnp.arange(DSA_QCHUNK)
        causal = kpos[None, :] <= qpos[:, None]
        rel = jax.nn.relu(jnp.einsum('bqhd,bsd->bqhs', sl(iq), ik, preferred_element_type=F32))
        score = jnp.einsum('bqhs,bqh->bqs', rel, sl(iw).astype(F32))
        _, top = lax.top_k(jnp.where(causal, score, -jnp.inf), n_keep)
        ckv_g = ckv[bi, top]
        kr_g = k_rope[bi, top]
        q_lat = jnp.einsum('bqhn,rhn->bqhr', sl(q_nope), w_uk)
        s = (jnp.einsum('bqhr,bqkr->bhqk', q_lat, ckv_g, preferred_element_type=F32)
             + jnp.einsum('bqhr,bqkr->bhqk', sl(q_rope), kr_g, preferred_element_type=F32)) * scale
        ok = (top <= qpos[None, :, None])[:, None]
        p = jax.nn.softmax(jnp.where(ok, s, -jnp.inf), axis=-1).astype(ckv.dtype)
        o_lat = jnp.einsum('bhqk,bqkr->bqhr', p, ckv_g)
        return jnp.einsum('bqhr,rhv->bqhv', o_lat, w_uv)

    out = lax.map(chunk, jnp.arange(S // DSA_QCHUNK))
    return jnp.moveaxis(out, 0, 1).reshape(B, S, H, C_V)


def even_mixer(h, layer_idx, w_in, a_qn, a_kn, lq1, lk1, lq2, lk2, a_subln, b_qn, b_kn, w_out, cos, sin):
    B, S, _ = h.shape
    cuts = [A_WIDTH, 2 * A_WIDTH, 3 * A_WIDTH, 3 * A_WIDTH + B_WIDTH, 3 * A_WIDTH + 2 * B_WIDTH]
    aq, ak, av, bq, bk, bv = jnp.split(h @ w_in, cuts, axis=-1)
    aq = apply_rope(rms_norm(aq.reshape(B, S, 2 * A_HEADS, HEAD_DIM), a_qn), cos, sin)
    ak = apply_rope(rms_norm(ak.reshape(B, S, 2 * A_HEADS, HEAD_DIM), a_kn), cos, sin)
    aq = aq.reshape(B, S, A_HEADS, 2, HEAD_DIM)
    ak = ak.reshape(B, S, A_HEADS, 2, HEAD_DIM)
    av = av.reshape(B, S, A_HEADS, A_V)
    lam_init = 0.8 - 0.6 * math.exp(-0.3 * layer_idx)
    lam = (jnp.exp(jnp.sum(lq1.astype(F32) * lk1.astype(F32)))
           - jnp.exp(jnp.sum(lq2.astype(F32) * lk2.astype(F32))) + lam_init)
    ya = rms_norm(diff_attention(aq, ak, av, lam), a_subln) * (1.0 - lam_init)
    bq = apply_rope(rms_norm(bq.reshape(B, S, B_HEADS, HEAD_DIM), b_qn), cos, sin)
    bk = apply_rope(rms_norm(bk.reshape(B, S, B_HEADS, HEAD_DIM), b_kn), cos, sin)
    bv = bv.reshape(B, S, B_HEADS, HEAD_DIM)
    yb = moba_attention(bq, bk, bv)
    y = jnp.concatenate([ya.reshape(B, S, A_HEADS * A_V), yb.reshape(B, S, B_WIDTH)], axis=-1)
    return y @ w_out


def odd_mixer(h, w_in, qa_norm, w_qb, q_norm, kv_norm, kr_norm, w_uk, w_uv, w_iqb, ik_norm, w_out, cos, sin):
    B, S, _ = h.shape
    c1 = C_Q_RANK
    c2 = c1 + C_KV_RANK
    c3 = c2 + C_ROPE
    c4 = c3 + IDX_DIM
    qa, ckv, kr, ik, iw = jnp.split(h @ w_in, [c1, c2, c3, c4], axis=-1)
    qa = rms_norm(qa, qa_norm)
    q = rms_norm((qa @ w_qb).reshape(B, S, C_HEADS, C_NOPE + C_ROPE), q_norm)
    q_nope = q[..., :C_NOPE]
    q_rope = apply_rope(q[..., C_NOPE:], cos, sin)
    ckv = rms_norm(ckv, kv_norm)
    kr = apply_rope(rms_norm(kr, kr_norm)[:, :, None, :], cos, sin)[:, :, 0]
    iq = (qa @ w_iqb).reshape(B, S, IDX_HEADS, IDX_DIM)
    iq = jnp.concatenate([apply_rope(iq[..., :IDX_ROPE], cos, sin), iq[..., IDX_ROPE:]], axis=-1)
    ik = rms_norm(ik, ik_norm)[:, :, None, :]
    ik = jnp.concatenate([apply_rope(ik[..., :IDX_ROPE], cos, sin), ik[..., IDX_ROPE:]], axis=-1)[:, :, 0]
    iw = iw * (IDX_HEADS ** -0.5 * IDX_DIM ** -0.5)
    y = dsa_attention(q_nope, q_rope, ckv, kr, iq, ik, iw, w_uk, w_uv)
    return y.reshape(B, S, C_HEADS * C_V) @ w_out


def cross_attention(h, m, wq, wk, wv, wo, qn, kn):
    B, S, _ = h.shape
    M = m.shape[1]
    q = rms_norm((h @ wq).reshape(B, S, X_HEADS, HEAD_DIM), qn)
    k = rms_norm((m @ wk).reshape(B, M, X_HEADS, HEAD_DIM), kn)
    v = (m @ wv).reshape(B, M, X_HEADS, HEAD_DIM)
    s = jnp.einsum('bshd,bmhd->bhsm', q, k, preferred_element_type=F32) * HEAD_DIM ** -0.5
    p = jax.nn.softmax(s, axis=-1).astype(v.dtype)
    o = jnp.einsum('bhsm,bmhd->bshd', p, v).reshape(B, S, X_DIM)
    return o @ wo


def moe_swiglu(h, router, wg, wu, wd):
    B, S, D = h.shape
    t = h.reshape(B * S, D)
    logits = jnp.dot(t, router, preferred_element_type=F32)
    top_v, top_i = lax.top_k(logits, TOP_K)
    top_g = jax.nn.softmax(top_v, axis=-1)
    gate = jnp.einsum('nk,nke->ne', top_g, jax.nn.one_hot(top_i, N_EXPERTS, dtype=F32)).astype(t.dtype)
    out = jnp.zeros_like(t)
    for e in range(N_EXPERTS):
        out = out + gate[:, e:e + 1] * swiglu(t, wg[e], wu[e], wd[e])
    return out.reshape(B, S, D)


def setup_inputs(seed: int = 0) -> dict:
    key = jax.random.key(seed)
    ks = iter(jax.random.split(key, 64))

    def nrm(shape, fan_in):
        return jax.random.normal(next(ks), shape, F32) * (fan_in ** -0.5)

    def gain(shape):
        return 1.0 + 0.02 * jax.random.normal(next(ks), shape, F32)

    def small(shape, std):
        return std * jax.random.normal(next(ks), shape, F32)

    D = D_MODEL
    return {
        'x': jax.random.normal(next(ks), (BATCH, SEQ, D), F32),
        'mem': jax.random.normal(next(ks), (BATCH, MEM_LEN, D), F32),
        'norm_mix': gain((DEPTH, D)),
        'norm_xattn': gain((DEPTH, D)),
        'norm_mem': gain((DEPTH, D)),
        'norm_ffn': gain((DEPTH, D)),
        'ev_w_in': nrm((N_EVEN, D, EVEN_IN), D),
        'ev_a_qnorm': gain((N_EVEN, HEAD_DIM)),
        'ev_a_knorm': gain((N_EVEN, HEAD_DIM)),
        'ev_lambda_q1': small((N_EVEN, HEAD_DIM), 0.1),
        'ev_lambda_k1': small((N_EVEN, HEAD_DIM), 0.1),
        'ev_lambda_q2': small((N_EVEN, HEAD_DIM), 0.1),
        'ev_lambda_k2': small((N_EVEN, HEAD_DIM), 0.1),
        'ev_a_subln': gain((N_EVEN, A_V)),
        'ev_b_qnorm': gain((N_EVEN, HEAD_DIM)),
        'ev_b_knorm': gain((N_EVEN, HEAD_DIM)),
        'ev_w_out': nrm((N_EVEN, EVEN_MIX, D), EVEN_MIX),
        'od_w_in': nrm((N_ODD, D, ODD_IN), D),
        'od_qa_norm': gain((N_ODD, C_Q_RANK)),
        'od_w_qb': nrm((N_ODD, C_Q_RANK, C_HEADS * (C_NOPE + C_ROPE)), C_Q_RANK),
        'od_q_norm': gain((N_ODD, C_NOPE + C_ROPE)),
        'od_kv_norm': gain((N_ODD, C_KV_RANK)),
        'od_kr_norm': gain((N_ODD, C_ROPE)),
        'od_w_uk': nrm((N_ODD, C_KV_RANK, C_HEADS, C_NOPE), C_KV_RANK),
        'od_w_uv': nrm((N_ODD, C_KV_RANK, C_HEADS, C_V), C_KV_RANK),
        'od_w_iqb': nrm((N_ODD, C_Q_RANK, IDX_HEADS * IDX_DIM), C_Q_RANK),
        'od_ik_norm': gain((N_ODD, IDX_DIM)),
        'od_w_out': nrm((N_ODD, C_HEADS * C_V, D), C_HEADS * C_V),
        'xa_wq': nrm((DEPTH, D, X_DIM), D),
        'xa_wk': nrm((DEPTH, D, X_DIM), D),
        'xa_wv': nrm((DEPTH, D, X_DIM), D),
        'xa_wo': nrm((DEPTH, X_DIM, D), X_DIM),
        'xa_qnorm': gain((DEPTH, HEAD_DIM)),
        'xa_knorm': gain((DEPTH, HEAD_DIM)),
        'ffn_wg': nrm((N_EVEN, D, D_FF), D),
        'ffn_wu': nrm((N_EVEN, D, D_FF), D),
        'ffn_wd': nrm((N_EVEN, D_FF, D), D_FF),
        'moe_router': nrm((N_ODD, D, N_EXPERTS), D),
        'moe_wg': nrm((N_ODD, N_EXPERTS, D, D_FF_EXPERT), D),
        'moe_wu': nrm((N_ODD, N_EXPERTS, D, D_FF_EXPERT), D),
        'moe_wd': nrm((N_ODD, N_EXPERTS, D_FF_EXPERT, D), D_FF_EXPERT),
    }


def reference(x, mem, norm_mix, norm_xattn, norm_mem, norm_ffn,
              ev_w_in, ev_a_qnorm, ev_a_knorm, ev_lambda_q1, ev_lambda_k1, ev_lambda_q2, ev_lambda_k2,
              ev_a_subln, ev_b_qnorm, ev_b_knorm, ev_w_out,
              od_w_in, od_qa_norm, od_w_qb, od_q_norm, od_kv_norm, od_kr_norm, od_w_uk, od_w_uv,
              od_w_iqb, od_ik_norm, od_w_out,
              xa_wq, xa_wk, xa_wv, xa_wo, xa_qnorm, xa_knorm,
              ffn_wg, ffn_wu, ffn_wd,
              moe_router, moe_wg, moe_wu, moe_wd):
    S = x.shape[1]
    cos_h, sin_h = rope_tables(S, HEAD_DIM)
    cos_r, sin_r = rope_tables(S, C_ROPE)
    for l in range(DEPTH):
        i = l // 2
        h = rms_norm(x, norm_mix[l])
        if l % 2 == 0:
            x = x + even_mixer(h, l, ev_w_in[i], ev_a_qnorm[i], ev_a_knorm[i],
                               ev_lambda_q1[i], ev_lambda_k1[i], ev_lambda_q2[i], ev_lambda_k2[i],
                               ev_a_subln[i], ev_b_qnorm[i], ev_b_knorm[i], ev_w_out[i], cos_h, sin_h)
        else:
            x = x + odd_mixer(h, od_w_in[i], od_qa_norm[i], od_w_qb[i], od_q_norm[i], od_kv_norm[i],
                              od_kr_norm[i], od_w_uk[i], od_w_uv[i], od_w_iqb[i], od_ik_norm[i],
                              od_w_out[i], cos_r, sin_r)
        x = x + cross_attention(rms_norm(x, norm_xattn[l]), rms_norm(mem, norm_mem[l]),
                                xa_wq[l], xa_wk[l], xa_wv[l], xa_wo[l], xa_qnorm[l], xa_knorm[l])
        h = rms_norm(x, norm_ffn[l])
        if l % 2 == 0:
            x = x + swiglu(h, ffn_wg[i], ffn_wu[i], ffn_wd[i])
        else:
            x = x + moe_swiglu(h, moe_router[i], moe_wg[i], moe_wu[i], moe_wd[i])
    return x
```

```python
import functools
import math

import jax
import jax.numpy as jnp
import numpy as np
from jax import lax
from jax.experimental import pallas as pl
from jax.experimental.pallas import tpu as pltpu

F32 = jnp.float32
BF16 = jnp.bfloat16
I32 = jnp.int32

LANE = 128
V7X_VMEM_BYTES = 64 * 1024 * 1024
VMEM_CAP = V7X_VMEM_BYTES - 8 * 1024 * 1024

HEAD_DIM = 128
ROPE_THETA = 10000.0
EPS = 1e-6
MOBA_BLOCK = 256
MOBA_TOPK = 3
C_NOPE = 128
C_ROPE = 64
IDX_DIM = 128
IDX_ROPE = 64
IDX_TOPK = 256
TOP_K = 2
NEG = -1e30
INT_MIN = -(2 ** 31)


def _tile(dim, pref, align):
    t = min(pref, dim)
    t -= t % align
    while t >= align:
        if dim % t == 0:
            return t
        t -= align
    return dim


def _params(sem, est_bytes):
    limit = int(min(max(est_bytes * 1.3 + (4 << 20), 32 << 20), VMEM_CAP))
    return pltpu.CompilerParams(dimension_semantics=sem, vmem_limit_bytes=limit)


def _nt(a, b):
    return lax.dot_general(a, b, (((1,), (1,)), ((), ())), preferred_element_type=F32)


def _softmax_step(s, m, l, acc, v):
    m_new = jnp.maximum(m, jnp.max(s, axis=-1, keepdims=True))
    alpha = jnp.exp(m - m_new)
    p = jnp.exp(s - m_new)
    l_new = alpha * l + jnp.sum(p, axis=-1, keepdims=True)
    acc_new = alpha * acc + jnp.dot(p.astype(v.dtype), v, preferred_element_type=F32)
    return m_new, l_new, acc_new


def _rmsnorm_body(x_ref, g_ref, o_ref):
    x = x_ref[...].astype(F32)
    ms = jnp.mean(x * x, axis=-1, keepdims=True)
    o_ref[...] = (x * lax.rsqrt(ms + EPS) * g_ref[...]).astype(o_ref.dtype)


def _rmsnorm(x, g, out_dtype, tm=256):
    M, D = x.shape
    tm = _tile(M, tm, 16)
    est = 2 * tm * D * (4 + 4)
    return pl.pallas_call(
        _rmsnorm_body,
        grid=(M // tm,),
        in_specs=[pl.BlockSpec((tm, D), lambda i: (i, 0)), pl.BlockSpec((1, D), lambda i: (0, 0))],
        out_specs=pl.BlockSpec((tm, D), lambda i: (i, 0)),
        out_shape=jax.ShapeDtypeStruct((M, D), out_dtype),
        compiler_params=_params(("parallel",), est),
        name="rmsnorm",
    )(x, g.reshape(1, D).astype(F32))


def _norm_router_body(x_ref, g_ref, rt_ref, h_ref, gate_ref, sel_ref, *, n_exp):
    x = x_ref[...]
    ms = jnp.mean(x * x, axis=-1, keepdims=True)
    h = x * lax.rsqrt(ms + EPS) * g_ref[...]
    h_ref[...] = h
    lane = lax.broadcasted_iota(I32, gate_ref.shape, 1)
    logits = jnp.full(gate_ref.shape, -jnp.inf, F32)
    for e in range(n_exp):
        col = jnp.sum(h * rt_ref[e:e + 1, :], axis=-1, keepdims=True)
        logits = jnp.where(lane == e, col, logits)
    m1 = jnp.max(logits, axis=-1, keepdims=True)
    i1 = jnp.min(jnp.where(logits == m1, lane, LANE), axis=-1, keepdims=True)
    rest = jnp.where(lane == i1, -jnp.inf, logits)
    m2 = jnp.max(rest, axis=-1, keepdims=True)
    i2 = jnp.min(jnp.where(rest == m2, lane, LANE), axis=-1, keepdims=True)
    e2 = jnp.exp(m2 - m1)
    den = 1.0 + e2
    gate_ref[...] = jnp.where(lane == i1, 1.0 / den, 0.0) + jnp.where(lane == i2, e2 / den, 0.0)
    sel_ref[...] = jnp.where((lane == i1) | (lane == i2), 1, 0).astype(I32)


def _norm_router(x, g, router, tm=256):
    M, D = x.shape
    n_exp = router.shape[1]
    tm = _tile(M, tm, 8)
    est = 2 * tm * D * 8 + 2 * 8 * D * 4
    h, gate, sel = pl.pallas_call(
        functools.partial(_norm_router_body, n_exp=n_exp),
        grid=(M // tm,),
        in_specs=[pl.BlockSpec((tm, D), lambda i: (i, 0)), pl.BlockSpec((1, D), lambda i: (0, 0)),
                  pl.BlockSpec((n_exp, D), lambda i: (0, 0))],
        out_specs=[pl.BlockSpec((tm, D), lambda i: (i, 0)), pl.BlockSpec((tm, LANE), lambda i: (i, 0)),
                   pl.BlockSpec((tm, LANE), lambda i: (i, 0))],
        out_shape=[jax.ShapeDtypeStruct((M, D), F32), jax.ShapeDtypeStruct((M, LANE), F32),
                   jax.ShapeDtypeStruct((M, LANE), I32)],
        compiler_params=_params(("parallel",), est),
        name="norm_router",
    )(x, g.reshape(1, D).astype(F32), router.T.astype(F32))
    return h, gate[:, :n_exp], sel[:, :n_exp]


EPI_NONE, EPI_NORM, EPI_NORM_ROPE, EPI_ROPE = 0, 1, 2, 3


def _head_epilogue(x, g, cos, sin, mode, rope_half):
    if mode in (EPI_NORM, EPI_NORM_ROPE):
        ms = jnp.mean(x * x, axis=-1, keepdims=True)
        x = x * lax.rsqrt(ms + EPS) * g
    if mode in (EPI_NORM_ROPE, EPI_ROPE):
        if rope_half == LANE // 2:
            r = pltpu.roll(x, LANE // 2, axis=1)
        else:
            lane = lax.broadcasted_iota(I32, x.shape, 1)
            first = (lane % (2 * rope_half)) < rope_half
            r = jnp.where(first, pltpu.roll(x, LANE - rope_half, axis=1), pltpu.roll(x, rope_half, axis=1))
        x = x * cos + r * sin
    return x


def _mm_body(flags_ref, a_ref, w_ref, *rest, nk, has_resid, modes, rope_half, tn):
    rest = list(rest)
    resid_ref = rest.pop(0) if has_resid else None
    gain_ref = cos_ref = sin_ref = None
    if modes:
        gain_ref, cos_ref, sin_ref = rest.pop(0), rest.pop(0), rest.pop(0)
    o_ref = rest.pop(0)
    acc_ref = rest.pop(0) if nk > 1 else None
    j = pl.program_id(1)
    k = pl.program_id(2)
    part = jnp.dot(a_ref[...].astype(BF16), w_ref[...], preferred_element_type=F32)

    def finish(acc):
        if modes:
            flag = flags_ref[j]

            @pl.when(flag == EPI_NONE)
            def _():
                o_ref[...] = acc.astype(o_ref.dtype)

            for mode in modes:
                @pl.when(flag == mode)
                def _(mode=mode):
                    for c in range(tn // LANE):
                        sl = slice(c * LANE, (c + 1) * LANE)
                        y = _head_epilogue(acc[:, sl], gain_ref[:, sl], cos_ref[...], sin_ref[...], mode, rope_half)
                        o_ref[:, sl] = y.astype(o_ref.dtype)
        elif has_resid:
            o_ref[...] = (resid_ref[...] + acc).astype(o_ref.dtype)
        else:
            o_ref[...] = acc.astype(o_ref.dtype)

    if nk == 1:
        finish(part)
    else:
        @pl.when(k == 0)
        def _():
            acc_ref[...] = part

        @pl.when((k > 0) & (k < nk - 1))
        def _():
            acc_ref[...] += part

        @pl.when(k == nk - 1)
        def _():
            finish(acc_ref[...] + part)


def _matmul(a, w, *, out_dtype, name, resid=None, epi=None, tm=1024, tn=1024, tk=4096):
    M, K = a.shape
    N = w.shape[1]
    tm = _tile(epi["cos"].shape[0] if epi else M, tm, 16)
    tn = epi["tn"] if epi else _tile(N, tn, LANE)
    tk = _tile(K, tk, LANE)
    nk = K // tk
    a_bytes = a.dtype.itemsize
    o_bytes = jnp.dtype(out_dtype).itemsize
    modes = tuple(epi["modes"]) if epi else ()
    in_specs = [pl.BlockSpec((tm, tk), lambda i, j, k, f: (i, k)),
                pl.BlockSpec((tk, tn), lambda i, j, k, f: (k, j))]
    args = [a, w]
    est = 2 * (tm * tk * a_bytes + tk * tn * 2 + tm * tn * o_bytes) + 3 * tm * tn * 4
    if resid is not None:
        in_specs.append(pl.BlockSpec((tm, tn), lambda i, j, k, f: (i, j)))
        args.append(resid)
        est += 2 * tm * tn * 4
    if epi:
        ns = epi["cos"].shape[0] // tm
        in_specs += [pl.BlockSpec((1, tn), lambda i, j, k, f: (0, j)),
                     pl.BlockSpec((tm, LANE), lambda i, j, k, f: (i % ns, 0)),
                     pl.BlockSpec((tm, LANE), lambda i, j, k, f: (i % ns, 0))]
        args += [epi["gain"].reshape(1, N).astype(F32), epi["cos"], epi["sin"]]
        flags = jnp.asarray(epi["flags"], I32)
        est += 4 * tm * LANE * 4
    else:
        flags = jnp.zeros((N // tn,), I32)
    body = functools.partial(_mm_body, nk=nk, has_resid=resid is not None, modes=modes,
                             rope_half=epi["rope_half"] if epi else 0, tn=tn)
    return pl.pallas_call(
        body,
        grid_spec=pltpu.PrefetchScalarGridSpec(
            num_scalar_prefetch=1,
            grid=(M // tm, N // tn, nk),
            in_specs=in_specs,
            out_specs=pl.BlockSpec((tm, tn), lambda i, j, k, f: (i, j)),
            scratch_shapes=[pltpu.VMEM((tm, tn), F32)] if nk > 1 else [],
        ),
        out_shape=jax.ShapeDtypeStruct((M, N), out_dtype),
        compiler_params=_params(("parallel", "parallel", "arbitrary"), est),
        name=name,
    )(flags, *args)


def _swiglu_body(te_ref, na_ref, a_ref, wg_ref, wu_ref, o_ref):
    m = pl.program_id(1)

    @pl.when(m < na_ref[0])
    def _():
        a = a_ref[...]
        g = jnp.dot(a, wg_ref[...], preferred_element_type=F32)
        u = jnp.dot(a, wu_ref[...], preferred_element_type=F32)
        o_ref[...] = (g * jax.nn.sigmoid(g) * u).astype(o_ref.dtype)

    @pl.when(m >= na_ref[0])
    def _():
        o_ref[...] = jnp.zeros(o_ref.shape, o_ref.dtype)


def _swiglu_up(a, wg, wu, tile_expert, n_active, *, tm, tn, name):
    M, D = a.shape
    F = wg.shape[2]
    est = 2 * (tm * D * 2 + 2 * D * tn * 2 + tm * tn * 2) + 4 * tm * tn * 4
    return pl.pallas_call(
        _swiglu_body,
        grid_spec=pltpu.PrefetchScalarGridSpec(
            num_scalar_prefetch=2,
            grid=(F // tn, M // tm),
            in_specs=[pl.BlockSpec((tm, D), lambda n, m, te, na: (m, 0)),
                      pl.BlockSpec((None, D, tn), lambda n, m, te, na: (te[m], 0, n)),
                      pl.BlockSpec((None, D, tn), lambda n, m, te, na: (te[m], 0, n))],
            out_specs=pl.BlockSpec((tm, tn), lambda n, m, te, na: (m, n)),
        ),
        out_shape=jax.ShapeDtypeStruct((M, F), BF16),
        compiler_params=_params(("parallel", "arbitrary"), est),
        name=name,
    )(tile_expert, n_active, a, wg, wu)


def _down_body(te_ref, na_ref, a_ref, w_ref, rg_ref, o_ref):
    m = pl.program_id(1)

    @pl.when(m < na_ref[0])
    def _():
        o_ref[...] = jnp.dot(a_ref[...], w_ref[...], preferred_element_type=F32) * rg_ref[...]

    @pl.when(m >= na_ref[0])
    def _():
        o_ref[...] = jnp.zeros(o_ref.shape, o_ref.dtype)


def _grouped_down(a, wd, row_gate, tile_expert, n_active, *, tm, tn, name):
    M, F = a.shape
    D = wd.shape[2]
    est = 2 * (tm * F * 2 + F * tn * 2 + tm * tn * 4 + tm * LANE * 4) + 2 * tm * tn * 4
    return pl.pallas_call(
        _down_body,
        grid_spec=pltpu.PrefetchScalarGridSpec(
            num_scalar_prefetch=2,
            grid=(D // tn, M // tm),
            in_specs=[pl.BlockSpec((tm, F), lambda n, m, te, na: (m, 0)),
                      pl.BlockSpec((None, F, tn), lambda n, m, te, na: (te[m], 0, n)),
                      pl.BlockSpec((tm, 1), lambda n, m, te, na: (m, 0))],
            out_specs=pl.BlockSpec((tm, tn), lambda n, m, te, na: (m, n)),
        ),
        out_shape=jax.ShapeDtypeStruct((M, D), F32),
        compiler_params=_params(("parallel", "arbitrary"), est),
        name=name,
    )(tile_expert, n_active, a, wd, row_gate)


def _gather_body(idx_ref, src_ref, o_ref, buf_ref, sem, *, tg):
    base = pl.program_id(0) * tg

    def row_copy(i, row):
        return pltpu.make_async_copy(src_ref.at[pl.ds(row, 1)], buf_ref.at[pl.ds(i, 1)], sem.at[0])

    def start(i, c):
        row_copy(i, idx_ref[base + i]).start()
        return c

    def wait(i, c):
        row_copy(i, 0).wait()
        return c

    lax.fori_loop(0, tg, start, 0)
    lax.fori_loop(0, tg, wait, 0)
    o_ref[...] = buf_ref[...].astype(o_ref.dtype)


def _gather_rows(src, idx, out_dtype, tg=256):
    P = idx.shape[0]
    D = src.shape[1]
    tg = _tile(P, tg, 16)
    est = tg * D * 4 + 2 * tg * D * 2
    return pl.pallas_call(
        functools.partial(_gather_body, tg=tg),
        grid_spec=pltpu.PrefetchScalarGridSpec(
            num_scalar_prefetch=1,
            grid=(P // tg,),
            in_specs=[pl.BlockSpec(memory_space=pl.ANY)],
            out_specs=pl.BlockSpec((tg, D), lambda i, idx: (i, 0)),
            scratch_shapes=[pltpu.VMEM((tg, D), src.dtype), pltpu.SemaphoreType.DMA((1,))],
        ),
        out_shape=jax.ShapeDtypeStruct((P, D), out_dtype),
        compiler_params=_params(("arbitrary",), est),
        name="moe_dispatch_gather",
    )(idx, src)


def _combine_body(slot_ref, x_ref, y_ref, o_ref, buf_ref, sem, *, tc):
    base = pl.program_id(0) * tc

    def row_copy(i, k, row):
        return pltpu.make_async_copy(y_ref.at[pl.ds(row, 1)], buf_ref.at[k, pl.ds(i, 1)], sem.at[0])

    def start(i, c):
        for k in range(TOP_K):
            row_copy(i, k, slot_ref[(base + i) * TOP_K + k]).start()
        return c

    def wait(i, c):
        for k in range(TOP_K):
            row_copy(i, k, 0).wait()
        return c

    lax.fori_loop(0, tc, start, 0)
    lax.fori_loop(0, tc, wait, 0)
    o_ref[...] = x_ref[...] + (buf_ref[0] + buf_ref[1])


def _moe_combine(x, y, slots, tc=128):
    M, D = x.shape
    tc = _tile(M, tc, 8)
    est = 2 * tc * D * 4 + 4 * tc * D * 4
    return pl.pallas_call(
        functools.partial(_combine_body, tc=tc),
        grid_spec=pltpu.PrefetchScalarGridSpec(
            num_scalar_prefetch=1,
            grid=(M // tc,),
            in_specs=[pl.BlockSpec((tc, D), lambda i, s: (i, 0)), pl.BlockSpec(memory_space=pl.ANY)],
            out_specs=pl.BlockSpec((tc, D), lambda i, s: (i, 0)),
            scratch_shapes=[pltpu.VMEM((TOP_K, tc, D), F32), pltpu.SemaphoreType.DMA((1,))],
        ),
        out_shape=jax.ShapeDtypeStruct((M, D), F32),
        compiler_params=_params(("arbitrary",), est),
        name="moe_combine",
    )(slots.reshape(-1), x, y)


def _diff_attn_body(q_ref, k_ref, v_ref, lq1_ref, lk1_ref, lq2_ref, lk2_ref, g_ref, o_ref, *, tq, lam_init):
    qi = pl.program_id(2)
    dv = v_ref.shape[-1]
    row = lax.broadcasted_iota(I32, (tq, tq), 0)
    col = lax.broadcasted_iota(I32, (tq, tq), 1)
    outs = []
    for mi in range(2):
        sl = slice(mi * HEAD_DIM, (mi + 1) * HEAD_DIM)
        q = q_ref[:, sl]

        def body(n, carry, q=q, sl=sl):
            off = pl.multiple_of(n * tq, tq)
            s = _nt(q, k_ref[pl.ds(off, tq), sl])
            return _softmax_step(s, *carry, v_ref[pl.ds(off, tq), :])

        init = (jnp.full((tq, 1), NEG, F32), jnp.zeros((tq, 1), F32), jnp.zeros((tq, dv), F32))
        carry = lax.fori_loop(0, qi, body, init)
        off = pl.multiple_of(qi * tq, tq)
        s = jnp.where(col <= row, _nt(q, k_ref[pl.ds(off, tq), sl]), NEG)
        _, l, acc = _softmax_step(s, *carry, v_ref[pl.ds(off, tq), :])
        outs.append(acc / l)
    lam = (jnp.exp(jnp.sum(lq1_ref[...] * lk1_ref[...], axis=-1, keepdims=True))
           - jnp.exp(jnp.sum(lq2_ref[...] * lk2_ref[...], axis=-1, keepdims=True)) + lam_init)
    y = outs[0] - lam * outs[1]
    ms = jnp.mean(y * y, axis=-1, keepdims=True)
    o_ref[...] = (y * lax.rsqrt(ms + EPS) * g_ref[...] * (1.0 - lam_init)).astype(o_ref.dtype)


def _diff_attention(qkv, n_heads, lam_params, subln, lam_init, tq=256):
    B, S, _ = qkv.shape
    w = 2 * HEAD_DIM
    tq = _tile(S, tq, LANE)
    vec = pl.BlockSpec((1, HEAD_DIM), lambda b, h, i: (0, 0))
    est = 2 * (tq * w * 2 * 2 + 2 * S * w * 2) + 8 * tq * tq * 4
    return pl.pallas_call(
        functools.partial(_diff_attn_body, tq=tq, lam_init=lam_init),
        grid=(B, n_heads, S // tq),
        in_specs=[pl.BlockSpec((None, tq, w), lambda b, h, i: (b, i, h)),
                  pl.BlockSpec((None, S, w), lambda b, h, i: (b, 0, n_heads + h)),
                  pl.BlockSpec((None, S, w), lambda b, h, i: (b, 0, 2 * n_heads + h)),
                  vec, vec, vec, vec,
                  pl.BlockSpec((1, w), lambda b, h, i: (0, 0))],
        out_specs=pl.BlockSpec((None, tq, w), lambda b, h, i: (b, i, h)),
        out_shape=jax.ShapeDtypeStruct((B, S, n_heads * w), BF16),
        compiler_params=_params(("parallel", "parallel", "arbitrary"), est),
        name="diff_attention",
    )(qkv, qkv, qkv, *[p.reshape(1, HEAD_DIM).astype(F32) for p in lam_params],
      subln.reshape(1, w).astype(F32))


def _moba_body(q_ref, k_ref, v_ref, o_ref, kmean_ref, *, nb):
    qi = pl.program_id(2)
    blk = MOBA_BLOCK

    @pl.when(qi == 0)
    def _():
        for n in range(nb):
            kb = k_ref[n * blk:(n + 1) * blk, :].astype(F32)
            kmean_ref[n:n + 1, :] = jnp.mean(kb, axis=0, keepdims=True)

    q = q_ref[...]
    km = kmean_ref[...]
    km_hi = km.astype(BF16)
    km_lo = (km - km_hi.astype(F32)).astype(BF16)
    gate = _nt(q, km_hi) + _nt(q, km_lo)
    lane = lax.broadcasted_iota(I32, gate.shape, 1)
    gate = jnp.where(lane < qi, gate, -jnp.inf)
    sel = jnp.zeros(gate.shape, jnp.bool_)
    for _ in range(MOBA_TOPK):
        mx = jnp.max(gate, axis=-1, keepdims=True)
        idx = jnp.min(jnp.where(gate == mx, lane, nb), axis=-1, keepdims=True)
        pick = (lane == idx) & (mx > -jnp.inf)
        sel = sel | pick
        gate = jnp.where(lane == idx, -jnp.inf, gate)
    bias = jnp.where(sel, 0.0, NEG)

    def body(n, carry):
        off = pl.multiple_of(n * blk, blk)
        s = _nt(q, k_ref[pl.ds(off, blk), :])
        s = s + jnp.sum(jnp.where(lane == n, bias, 0.0), axis=-1, keepdims=True)
        return _softmax_step(s, *carry, v_ref[pl.ds(off, blk), :])

    init = (jnp.full((blk, 1), NEG, F32), jnp.zeros((blk, 1), F32), jnp.zeros((blk, HEAD_DIM), F32))
    carry = lax.fori_loop(0, qi, body, init)
    off = pl.multiple_of(qi * blk, blk)
    row = lax.broadcasted_iota(I32, (blk, blk), 0)
    col = lax.broadcasted_iota(I32, (blk, blk), 1)
    s = jnp.where(col <= row, _nt(q, k_ref[pl.ds(off, blk), :]), NEG)
    _, l, acc = _softmax_step(s, *carry, v_ref[pl.ds(off, blk), :])
    o_ref[...] = (acc / l).astype(o_ref.dtype)


def _moba_attention(qkv, n_heads, q_col, k_col, v_col):
    B, S, _ = qkv.shape
    assert S % MOBA_BLOCK == 0 and S // MOBA_BLOCK >= MOBA_TOPK
    nb = S // MOBA_BLOCK
    d = HEAD_DIM
    est = 2 * (2 * MOBA_BLOCK * d * 2 + 2 * S * d * 2) + 8 * MOBA_BLOCK * MOBA_BLOCK * 4
    return pl.pallas_call(
        functools.partial(_moba_body, nb=nb),
        grid=(B, n_heads, nb),
        in_specs=[pl.BlockSpec((None, MOBA_BLOCK, d), lambda b, h, i: (b, i, q_col + h)),
                  pl.BlockSpec((None, S, d), lambda b, h, i: (b, 0, k_col + h)),
                  pl.BlockSpec((None, S, d), lambda b, h, i: (b, 0, v_col + h))],
        out_specs=pl.BlockSpec((None, MOBA_BLOCK, d), lambda b, h, i: (b, i, h)),
        out_shape=jax.ShapeDtypeStruct((B, S, n_heads * d), BF16),
        scratch_shapes=[pltpu.VMEM((nb, d), F32)],
        compiler_params=_params(("parallel", "parallel", "arbitrary"), est),
        name="moba_attention",
    )(qkv, qkv, qkv)


def _xattn_body(q_ref, k_ref, v_ref, o_ref, *, n_heads):
    for h in range(n_heads):
        sl = slice(h * HEAD_DIM, (h + 1) * HEAD_DIM)
        s = _nt(q_ref[:, sl], k_ref[:, sl])
        m = jnp.max(s, axis=-1, keepdims=True)
        p = jnp.exp(s - m)
        l = jnp.sum(p, axis=-1, keepdims=True)
        o = jnp.dot(p.astype(BF16), v_ref[:, sl], preferred_element_type=F32)
        o_ref[:, sl] = (o / l).astype(o_ref.dtype)


def _cross_attention(q, kv, n_heads, tq=512):
    B, S, X = q.shape
    M = kv.shape[1]
    tq = _tile(S, tq, 16)
    est = 2 * (2 * tq * X * 2 + 2 * M * X * 2) + 6 * tq * M * 4
    return pl.pallas_call(
        functools.partial(_xattn_body, n_heads=n_heads),
        grid=(B, S // tq),
        in_specs=[pl.BlockSpec((None, tq, X), lambda b, i: (b, i, 0)),
                  pl.BlockSpec((None, M, X), lambda b, i: (b, 0, 0)),
                  pl.BlockSpec((None, M, X), lambda b, i: (b, 0, 1))],
        out_specs=pl.BlockSpec((None, tq, X), lambda b, i: (b, i, 0)),
        out_shape=jax.ShapeDtypeStruct((B, S, X), BF16),
        compiler_params=_params(("parallel", "parallel"), est),
        name="cross_attention",
    )(q, kv, kv)


def _odd_prep_body(x_ref, gqa_ref, gkv_ref, gkr_ref, gik_ref, cos_ref, sin_ref,
                   qa_ref, ckv_ref, kr_ref, ik_ref, iw_ref, *, c1, c2, iw_scale):
    def norm(x, g):
        ms = jnp.mean(x * x, axis=-1, keepdims=True)
        return x * lax.rsqrt(ms + EPS) * g

    def rope(y):
        lane = lax.broadcasted_iota(I32, y.shape, 1)
        half = C_ROPE // 2
        first = (lane % C_ROPE) < half
        r = jnp.where(first, pltpu.roll(y, LANE - half, axis=1), pltpu.roll(y, half, axis=1))
        return y * cos_ref[...] + r * sin_ref[...]

    qa_ref[...] = norm(x_ref[:, :c1], gqa_ref[...]).astype(qa_ref.dtype)
    ckv_ref[...] = norm(x_ref[:, c1:c2], gkv_ref[...]).astype(ckv_ref.dtype)
    slab_a = x_ref[:, c2:c2 + LANE]
    slab_b = x_ref[:, c2 + LANE:c2 + 2 * LANE]
    lane = lax.broadcasted_iota(I32, slab_a.shape, 1)
    low = lane < C_ROPE
    ms = jnp.sum(jnp.where(low, slab_a * slab_a, 0.0), axis=-1, keepdims=True) / C_ROPE
    kr = rope(slab_a * lax.rsqrt(ms + EPS) * gkr_ref[...])
    kr_ref[0] = kr.astype(kr_ref.dtype)
    kr_ref[1] = pltpu.roll(kr, C_ROPE, axis=1).astype(kr_ref.dtype)
    rot_a = pltpu.roll(slab_a, C_ROPE, axis=1)
    rot_b = pltpu.roll(slab_b, C_ROPE, axis=1)
    ik = jnp.where(low, rot_a, rot_b)
    ik_ref[...] = rope(norm(ik, gik_ref[...])).astype(ik_ref.dtype)
    iw_ref[...] = rot_b * iw_scale


def _odd_prep(x, c1, c2, g_qa, g_kv, g_kr, g_ik, cos_p, sin_p, iw_scale, tm=256):
    B, S, C = x.shape
    assert C == c2 + 2 * LANE
    tm = _tile(S, tm, 16)
    vec = lambda n: pl.BlockSpec((1, n), lambda b, i: (0, 0))
    tab = pl.BlockSpec((tm, LANE), lambda b, i: (i, 0))
    est = 2 * tm * C * 4 * 2
    return pl.pallas_call(
        functools.partial(_odd_prep_body, c1=c1, c2=c2, iw_scale=iw_scale),
        grid=(B, S // tm),
        in_specs=[pl.BlockSpec((None, tm, C), lambda b, i: (b, i, 0)),
                  vec(c1), vec(c2 - c1), vec(LANE), vec(LANE), tab, tab],
        out_specs=[pl.BlockSpec((None, tm, c1), lambda b, i: (b, i, 0)),
                   pl.BlockSpec((None, tm, c2 - c1), lambda b, i: (b, i, 0)),
                   pl.BlockSpec((None, 2, tm, LANE), lambda b, i: (b, 0, i, 0)),
                   pl.BlockSpec((None, tm, LANE), lambda b, i: (b, i, 0)),
                   pl.BlockSpec((None, tm, LANE), lambda b, i: (b, i, 0))],
        out_shape=[jax.ShapeDtypeStruct((B, S, c1), BF16),
                   jax.ShapeDtypeStruct((B, S, c2 - c1), BF16),
                   jax.ShapeDtypeStruct((B, 2, S, LANE), BF16),
                   jax.ShapeDtypeStruct((B, S, LANE), BF16),
                   jax.ShapeDtypeStruct((B, S, LANE), F32)],
        compiler_params=_params(("parallel", "parallel"), est),
        name="odd_prep",
    )(x, g_qa.reshape(1, -1), g_kv.reshape(1, -1), g_kr.reshape(1, -1), g_ik.reshape(1, -1), cos_p, sin_p)


def _q_prep_body(x_ref, gn_ref, gr_ref, cos_ref, sin_ref, qn_ref, qr_ref, *, n_heads):
    nope_w = n_heads * C_NOPE
    width = C_NOPE + C_ROPE
    half = C_ROPE // 2
    lane = lax.broadcasted_iota(I32, (x_ref.shape[0], LANE), 1)
    low = lane < C_ROPE
    first = (lane % C_ROPE) < half
    for p in range(n_heads // 2):
        n0 = x_ref[:, (2 * p) * LANE:(2 * p + 1) * LANE]
        n1 = x_ref[:, (2 * p + 1) * LANE:(2 * p + 2) * LANE]
        r = x_ref[:, nope_w + p * LANE:nope_w + (p + 1) * LANE]
        r2 = r * r
        ss0 = jnp.sum(n0 * n0, axis=-1, keepdims=True) + jnp.sum(jnp.where(low, r2, 0.0), axis=-1, keepdims=True)
        ss1 = jnp.sum(n1 * n1, axis=-1, keepdims=True) + jnp.sum(jnp.where(low, 0.0, r2), axis=-1, keepdims=True)
        inv0 = lax.rsqrt(ss0 / width + EPS)
        inv1 = lax.rsqrt(ss1 / width + EPS)
        qn_ref[:, (2 * p) * LANE:(2 * p + 1) * LANE] = (n0 * inv0 * gn_ref[...]).astype(qn_ref.dtype)
        qn_ref[:, (2 * p + 1) * LANE:(2 * p + 2) * LANE] = (n1 * inv1 * gn_ref[...]).astype(qn_ref.dtype)
        y = r * jnp.where(low, inv0, inv1) * gr_ref[...]
        rot = jnp.where(first, pltpu.roll(y, LANE - half, axis=1), pltpu.roll(y, half, axis=1))
        qr_ref[:, p * LANE:(p + 1) * LANE] = (y * cos_ref[...] + rot * sin_ref[...]).astype(qr_ref.dtype)


def _q_prep(x, n_heads, g_nope, g_rope2, cos_q, sin_q, tm=256):
    B, S, C = x.shape
    tm = _tile(S, tm, 16)
    nope_w, rope_w = n_heads * C_NOPE, n_heads * C_ROPE
    vec = pl.BlockSpec((1, LANE), lambda b, i: (0, 0))
    tab = pl.BlockSpec((tm, LANE), lambda b, i: (i, 0))
    est = 2 * tm * C * 6
    return pl.pallas_call(
        functools.partial(_q_prep_body, n_heads=n_heads),
        grid=(B, S // tm),
        in_specs=[pl.BlockSpec((None, tm, C), lambda b, i: (b, i, 0)), vec, vec, tab, tab],
        out_specs=[pl.BlockSpec((None, tm, nope_w), lambda b, i: (b, i, 0)),
                   pl.BlockSpec((None, tm, rope_w), lambda b, i: (b, i, 0))],
        out_shape=[jax.ShapeDtypeStruct((B, S, nope_w), BF16), jax.ShapeDtypeStruct((B, S, rope_w), BF16)],
        compiler_params=_params(("parallel", "parallel"), est),
        name="dsa_q_prep",
    )(x, g_nope.reshape(1, LANE), g_rope2.reshape(1, LANE), cos_q, sin_q)


def _indexer_body(ik_ref, iq_ref, iwt_ref, o_ref, key_ref, *, tq, n_heads, n_keep, chunk):
    S = ik_ref.shape[0]
    qi = pl.program_id(1)
    n_tiles = qi + 1
    t_idx = qi * tq + lax.broadcasted_iota(I32, (tq, tq), 1)
    s_loc = lax.broadcasted_iota(I32, (tq, tq), 0)

    def score_tile(kt, c):
        off = pl.multiple_of(kt * tq, tq)
        ikt = ik_ref[pl.ds(off, tq), :]
        acc = jnp.zeros((tq, tq), F32)
        for h in range(n_heads):
            r = _nt(ikt, iq_ref[:, h * IDX_DIM:(h + 1) * IDX_DIM])
            acc = acc + jnp.maximum(r, 0.0) * iwt_ref[h:h + 1, :]
        acc = jnp.where(off + s_loc <= t_idx, acc, -jnp.inf)
        bits = lax.bitcast_convert_type(acc, I32)
        key_ref[pl.ds(off, tq), :] = bits ^ ((bits >> 31) & 0x7FFFFFFF)
        return c

    lax.fori_loop(0, n_tiles, score_tile, 0)

    def count_ge(cand):
        def body(c, cnt):
            off = pl.multiple_of(c * chunk, chunk)
            hit = jnp.where(key_ref[pl.ds(off, chunk), :] >= cand, 1, 0).astype(I32)
            return cnt + jnp.sum(hit.reshape(chunk // 8, 8, tq), axis=0)
        cnt = lax.fori_loop(0, n_tiles * (tq // chunk), body, jnp.zeros((8, tq), I32))
        return jnp.sum(cnt, axis=0, keepdims=True)

    thr = jnp.where(count_ge(jnp.zeros((1, tq), I32)) >= n_keep, 0, INT_MIN).astype(I32)

    def bit_step(i, thr):
        cand = thr + lax.shift_left(jnp.int32(1), 30 - i)
        return jnp.where(count_ge(cand) >= n_keep, cand, thr)

    thr = lax.fori_loop(0, 31, bit_step, thr)

    def out_tile(kt, c):
        off = pl.multiple_of(kt * tq, tq)
        ok = (key_ref[pl.ds(off, tq), :] >= thr) & (off + s_loc <= t_idx)
        o_ref[pl.ds(off, tq), :] = jnp.where(ok, 0.0, NEG).astype(o_ref.dtype)
        return c

    def neg_tile(kt, c):
        off = pl.multiple_of(kt * tq, tq)
        o_ref[pl.ds(off, tq), :] = jnp.full((tq, tq), NEG, o_ref.dtype)
        return c

    lax.fori_loop(0, n_tiles, out_tile, 0)
    lax.fori_loop(n_tiles, S // tq, neg_tile, 0)


def _indexer(ik, iq, iwt, n_keep, tq=256):
    B, S, _ = ik.shape
    n_heads = iwt.shape[1]
    tq = _tile(S, tq, LANE)
    est = 2 * (S * IDX_DIM * 2 + tq * n_heads * IDX_DIM * 2 + n_heads * tq * 4 + S * tq * 2) + S * tq * 4 + 8 * tq * tq * 4
    return pl.pallas_call(
        functools.partial(_indexer_body, tq=tq, n_heads=n_heads, n_keep=n_keep, chunk=64),
        grid=(B, S // tq),
        in_specs=[pl.BlockSpec((None, S, IDX_DIM), lambda b, i: (b, 0, 0)),
                  pl.BlockSpec((None, tq, n_heads * IDX_DIM), lambda b, i: (b, i, 0)),
                  pl.BlockSpec((None, n_heads, tq), lambda b, i: (b, 0, i))],
        out_specs=pl.BlockSpec((None, S, tq), lambda b, i: (b, 0, i)),
        out_shape=jax.ShapeDtypeStruct((B, S, S), BF16),
        scratch_shapes=[pltpu.VMEM((S, tq), I32)],
        compiler_params=_params(("parallel", "arbitrary"), est),
        name="dsa_indexer",
    )(ik, iq, iwt)


def _dsa_attn_body(qn_ref, qr_ref, kn_ref, kr_ref, v_ref, bias_ref, o_ref, *, tq, hg):
    qi = pl.program_id(2)
    for hh in range(hg):
        sl = slice(hh * C_NOPE, (hh + 1) * C_NOPE)
        qn = qn_ref[:, sl]
        qr = qr_ref[:, (hh // 2) * LANE:(hh // 2 + 1) * LANE]

        def body(n, carry, qn=qn, qr=qr, sl=sl, par=hh % 2):
            off = pl.multiple_of(n * tq, tq)
            s = (_nt(qn, kn_ref[pl.ds(off, tq), sl]) + _nt(qr, kr_ref[par, pl.ds(off, tq), :])
                 + bias_ref[n].astype(F32))
            return _softmax_step(s, *carry, v_ref[pl.ds(off, tq), sl])

        init = (jnp.full((tq, 1), NEG, F32), jnp.zeros((tq, 1), F32), jnp.zeros((tq, C_NOPE), F32))
        _, l, acc = lax.fori_loop(0, qi + 1, body, init)
        o_ref[:, sl] = (acc / l).astype(o_ref.dtype)


def _dsa_attention(qn, qr, kv, kr2, bias, n_heads, tq, hg=4):
    B, S, _ = qn.shape
    hg = min(hg, n_heads)
    assert hg % 2 == 0 and n_heads % hg == 0
    ng = n_heads // hg
    nk = S // tq
    est = 2 * (tq * hg * 192 * 2 + 2 * S * hg * LANE * 2 + 2 * S * LANE * 2 + nk * tq * tq * 2 + tq * hg * LANE * 2) \
        + 8 * tq * tq * 4
    return pl.pallas_call(
        functools.partial(_dsa_attn_body, tq=tq, hg=hg),
        grid=(B, ng, S // tq),
        in_specs=[pl.BlockSpec((None, tq, hg * C_NOPE), lambda b, g, i: (b, i, g)),
                  pl.BlockSpec((None, tq, hg * C_ROPE), lambda b, g, i: (b, i, g)),
                  pl.BlockSpec((None, S, hg * C_NOPE), lambda b, g, i: (b, 0, g)),
                  pl.BlockSpec((None, 2, S, LANE), lambda b, g, i: (b, 0, 0, 0)),
                  pl.BlockSpec((None, S, hg * C_NOPE), lambda b, g, i: (b, 0, ng + g)),
                  pl.BlockSpec((None, None, nk, tq, tq), lambda b, g, i: (b, i, 0, 0, 0))],
        out_specs=pl.BlockSpec((None, tq, hg * C_NOPE), lambda b, g, i: (b, i, g)),
        out_shape=jax.ShapeDtypeStruct((B, S, n_heads * C_NOPE), BF16),
        compiler_params=_params(("parallel", "parallel", "arbitrary"), est),
        name="dsa_attention",
    )(qn, qr, kv, kr2, kv, bias)


def _rope_tables(seq, dim):
    inv_freq = ROPE_THETA ** (-jnp.arange(0, dim, 2, dtype=F32) / dim)
    ang = jnp.arange(seq, dtype=F32)[:, None] * inv_freq[None, :]
    return jnp.cos(ang), jnp.sin(ang)


def _cross_block(x, mem, l, norm_xattn, norm_mem, xa_wq, xa_wk, xa_wv, xa_wo, xa_qnorm, xa_knorm, B, S):
    N, D = x.shape
    X = xa_wq.shape[2]
    n_heads = X // HEAD_DIM
    M = mem.shape[1]
    h = _rmsnorm(x, norm_xattn[l], BF16)
    mn = _rmsnorm(mem.reshape(B * M, D), norm_mem[l], BF16)
    ones = jnp.ones((S, LANE), F32)
    q = _matmul(h, xa_wq[l].astype(BF16), out_dtype=BF16, name="xattn_q",
                epi=dict(flags=[EPI_NORM] * (X // _tile(X, 1024, LANE)), modes=(EPI_NORM,), rope_half=0,
                         tn=_tile(X, 1024, LANE), cos=ones, sin=ones,
                         gain=jnp.tile(xa_qnorm[l] * HEAD_DIM ** -0.5, n_heads)))
    wkv = jnp.concatenate([xa_wk[l], xa_wv[l]], axis=1).astype(BF16)
    tn = _tile(X, 1024, LANE)
    kv = _matmul(mn, wkv, out_dtype=BF16, name="xattn_kv",
                 epi=dict(flags=[EPI_NORM] * (X // tn) + [EPI_NONE] * (X // tn), modes=(EPI_NORM,), rope_half=0,
                          tn=tn, cos=jnp.ones((M, LANE), F32), sin=jnp.ones((M, LANE), F32),
                          gain=jnp.concatenate([jnp.tile(xa_knorm[l], n_heads), jnp.ones((X,), F32)])))
    o = _cross_attention(q.reshape(B, S, X), kv.reshape(B, M, 2 * X), n_heads)
    return _matmul(o.reshape(N, X), xa_wo[l].astype(BF16), out_dtype=F32, name="xattn_out", resid=x)


def _even_mixer(x, l, i, B, S, norm_mix, ev_w_in, ev_a_qnorm, ev_a_knorm, lam_params, ev_a_subln,
                ev_b_qnorm, ev_b_knorm, ev_w_out):
    N, D = x.shape
    a_heads = D // (4 * HEAD_DIM)
    b_heads = D // (2 * HEAD_DIM)
    aw = a_heads * 2 * HEAD_DIM
    bw = b_heads * HEAD_DIM
    width = 3 * aw + 3 * bw
    scale = HEAD_DIM ** -0.5
    cos, sin = _rope_tables(S, HEAD_DIM)
    cos2 = jnp.concatenate([cos, cos], axis=-1)
    sin2 = jnp.concatenate([-sin, sin], axis=-1)
    tn = _tile(math.gcd(aw, bw), 1024, LANE)
    seg = [(aw, EPI_NORM_ROPE, ev_a_qnorm[i] * scale), (aw, EPI_NORM_ROPE, ev_a_knorm[i]), (aw, EPI_NONE, None),
           (bw, EPI_NORM_ROPE, ev_b_qnorm[i] * scale), (bw, EPI_NORM_ROPE, ev_b_knorm[i]), (bw, EPI_NONE, None)]
    flags, gains = [], []
    for w, flag, g in seg:
        flags += [flag] * (w // tn)
        gains.append(jnp.ones((w,), F32) if g is None else jnp.tile(g.astype(F32), w // HEAD_DIM))
    h = _rmsnorm(x, norm_mix[l], BF16)
    qkv = _matmul(h, ev_w_in[i].astype(BF16), out_dtype=BF16, name="even_in",
                  epi=dict(flags=flags, modes=(EPI_NORM_ROPE,), rope_half=HEAD_DIM // 2, tn=tn,
                           cos=cos2, sin=sin2, gain=jnp.concatenate(gains)))
    qkv = qkv.reshape(B, S, width)
    lam_init = 0.8 - 0.6 * math.exp(-0.3 * l)
    ya = _diff_attention(qkv, a_heads, lam_params, ev_a_subln[i], lam_init)
    c0 = 3 * aw // HEAD_DIM
    yb = _moba_attention(qkv, b_heads, c0, c0 + b_heads, c0 + 2 * b_heads)
    y = jnp.concatenate([ya, yb], axis=-1).reshape(N, aw + bw)
    return _matmul(y, ev_w_out[i].astype(BF16), out_dtype=F32, name="even_out", resid=x)


def _odd_mixer(x, l, i, B, S, norm_mix, od_w_in, od_qa_norm, od_w_qb, od_q_norm, od_kv_norm, od_kr_norm,
               od_w_uk, od_w_uv, od_w_iqb, od_ik_norm, od_w_out):
    N, D = x.shape
    c1 = od_qa_norm.shape[1]
    kv_rank = od_kv_norm.shape[1]
    c2 = c1 + kv_rank
    n_heads = od_w_uk.shape[2]
    idx_heads = od_w_iqb.shape[2] // IDX_DIM
    assert od_w_in.shape[2] == c2 + C_ROPE + IDX_DIM + idx_heads and idx_heads == C_ROPE
    scale = (C_NOPE + C_ROPE) ** -0.5
    cos, sin = _rope_tables(S, C_ROPE)
    one, zero = jnp.ones((S, C_ROPE), F32), jnp.zeros((S, C_ROPE), F32)
    cos_p = jnp.concatenate([cos, cos, one], axis=-1)
    sin_p = jnp.concatenate([-sin, sin, zero], axis=-1)
    cos_q = jnp.concatenate([cos, cos, cos, cos], axis=-1)
    sin_q = jnp.concatenate([-sin, sin, -sin, sin], axis=-1)

    h = _rmsnorm(x, norm_mix[l], BF16)
    proj = _matmul(h, od_w_in[i].astype(BF16), out_dtype=F32, name="odd_in")
    g_kr = jnp.concatenate([od_kr_norm[i], jnp.zeros((LANE - C_ROPE,), F32)])
    qa, ckv, kr2, ik, iw = _odd_prep(proj.reshape(B, S, -1), c1, c2, od_qa_norm[i], od_kv_norm[i], g_kr,
                                     od_ik_norm[i], cos_p, sin_p, idx_heads ** -0.5 * IDX_DIM ** -0.5)
    qa = qa.reshape(N, c1)
    wqb = od_w_qb[i].reshape(c1, n_heads, C_NOPE + C_ROPE)
    wqb = jnp.concatenate([wqb[:, :, :C_NOPE].reshape(c1, -1), wqb[:, :, C_NOPE:].reshape(c1, -1)], axis=1)
    qraw = _matmul(qa, wqb.astype(BF16), out_dtype=F32, name="odd_qb")
    qn, qr = _q_prep(qraw.reshape(B, S, -1), n_heads, od_q_norm[i][:C_NOPE] * scale,
                     jnp.tile(od_q_norm[i][C_NOPE:], 2) * scale, cos_q, sin_q)
    tn = _tile(idx_heads * IDX_DIM, 1024, LANE)
    iq = _matmul(qa, od_w_iqb[i].astype(BF16), out_dtype=BF16, name="odd_iqb",
                 epi=dict(flags=[EPI_ROPE] * (idx_heads * IDX_DIM // tn), modes=(EPI_ROPE,), rope_half=IDX_ROPE // 2,
                          tn=tn, cos=cos_p, sin=sin_p, gain=jnp.ones((idx_heads * IDX_DIM,), F32)))
    wkv = jnp.concatenate([od_w_uk[i].reshape(kv_rank, -1), od_w_uv[i].reshape(kv_rank, -1)], axis=1)
    kv = _matmul(ckv.reshape(N, kv_rank), wkv.astype(BF16), out_dtype=BF16, name="odd_kv")

    n_keep = min(IDX_TOPK, S // 4)
    tq = _tile(S, 256, LANE)
    iwt = jnp.swapaxes(iw[:, :, :idx_heads], 1, 2)
    bias_t = _indexer(ik, iq.reshape(B, S, -1), iwt, n_keep, tq)
    nq = S // tq
    bias = bias_t.reshape(B, nq, tq, nq, tq).transpose(0, 3, 1, 4, 2)
    y = _dsa_attention(qn, qr, kv.reshape(B, S, -1), kr2, bias, n_heads, tq)
    return _matmul(y.reshape(N, -1), od_w_out[i].astype(BF16), out_dtype=F32, name="odd_out", resid=x)


def _dense_ffn(x, g, wg, wu, wd):
    N, D = x.shape
    F = wg.shape[1]
    h = _rmsnorm(x, g, BF16)
    tm = _tile(N, 2048, 16)
    tn = _tile(F, 512, LANE)
    nt = N // tm
    hid = _swiglu_up(h, wg.astype(BF16)[None], wu.astype(BF16)[None], jnp.zeros((nt,), I32),
                     jnp.full((1,), nt, I32), tm=tm, tn=tn, name="ffn_up")
    return _matmul(hid, wd.astype(BF16), out_dtype=F32, name="ffn_down", resid=x, tm=512, tn=1024,
                   tk=F if F <= 4096 else _tile(F, F // 2, LANE))


def _moe_ffn(x, g, router, wg, wu, wd, tm=512):
    N, D = x.shape
    E, _, F = wg.shape
    h, gate, sel = _norm_router(x, g, router)
    tm = _tile(N, tm, 16)
    cnt = jnp.sum(sel, axis=0)
    tiles_e = (cnt + tm - 1) // tm
    tile_end = jnp.cumsum(tiles_e)
    start = (tile_end - tiles_e) * tm
    rank = jnp.cumsum(sel, axis=0) - sel
    P = N * TOP_K + E * tm
    n_tiles = P // tm
    slot = jnp.where(sel > 0, start[None, :] + rank, P)
    tok = jnp.broadcast_to(jnp.arange(N, dtype=I32)[:, None], (N, E))
    tok_of_slot = jnp.zeros((P,), I32).at[slot.reshape(-1)].set(tok.reshape(-1), mode="drop")
    gate_of_slot = jnp.zeros((P,), F32).at[slot.reshape(-1)].set(gate.reshape(-1), mode="drop")
    tile_expert = jnp.minimum(jnp.searchsorted(tile_end, jnp.arange(n_tiles, dtype=I32), side="right"),
                              E - 1).astype(I32)
    n_active = tile_end[-1:].astype(I32)
    slots2 = jnp.sort(slot, axis=1)[:, :TOP_K].astype(I32)

    xs = _gather_rows(h, tok_of_slot, BF16)
    hid = _swiglu_up(xs, wg.astype(BF16), wu.astype(BF16), tile_expert, n_active,
                     tm=tm, tn=_tile(F, 512, LANE), name="moe_up")
    y = _grouped_down(hid, wd.astype(BF16), gate_of_slot.reshape(P, 1), tile_expert, n_active,
                      tm=tm, tn=_tile(D, 512, LANE), name="moe_down")
    return _moe_combine(x, y, slots2)


def kernel(x, mem, norm_mix, norm_xattn, norm_mem, norm_ffn, ev_w_in, ev_a_qnorm, ev_a_knorm, ev_lambda_q1, ev_lambda_k1, ev_lambda_q2, ev_lambda_k2, ev_a_subln, ev_b_qnorm, ev_b_knorm, ev_w_out, od_w_in, od_qa_norm, od_w_qb, od_q_norm, od_kv_norm, od_kr_norm, od_w_uk, od_w_uv, od_w_iqb, od_ik_norm, od_w_out, xa_wq, xa_wk, xa_wv, xa_wo, xa_qnorm, xa_knorm, ffn_wg, ffn_wu, ffn_wd, moe_router, moe_wg, moe_wu, moe_wd):
    B, S, D = x.shape
    depth = norm_mix.shape[0]
    x = x.reshape(B * S, D)
    for l in range(depth):
        i = l // 2
        if l % 2 == 0:
            x = _even_mixer(x, l, i, B, S, norm_mix, ev_w_in, ev_a_qnorm, ev_a_knorm,
                            (ev_lambda_q1[i], ev_lambda_k1[i], ev_lambda_q2[i], ev_lambda_k2[i]),
                            ev_a_subln, ev_b_qnorm, ev_b_knorm, ev_w_out)
        else:
            x = _odd_mixer(x, l, i, B, S, norm_mix, od_w_in, od_qa_norm, od_w_qb, od_q_norm, od_kv_norm,
                           od_kr_norm, od_w_uk, od_w_uv, od_w_iqb, od_ik_norm, od_w_out)
        x = _cross_block(x, mem, l, norm_xattn, norm_mem, xa_wq, xa_wk, xa_wv, xa_wo, xa_qnorm, xa_knorm, B, S)
        if l % 2 == 0:
            x = _dense_ffn(x, norm_ffn[l], ffn_wg[i], ffn_wu[i], ffn_wd[i])
        else:
            x = _moe_ffn(x, norm_ffn[l], moe_router[i], moe_wg[i], moe_wu[i], moe_wd[i])
    return x.reshape(B, S, D)
```

```python
import functools
import math

import jax
import jax.numpy as jnp
import numpy as np
from jax import lax
from jax.experimental import pallas as pl
from jax.experimental.pallas import tpu as pltpu

F32 = jnp.float32
BF16 = jnp.bfloat16
I32 = jnp.int32

LANE = 128
V7X_VMEM_BYTES = 64 * 1024 * 1024
VMEM_CAP = V7X_VMEM_BYTES - 4 * 1024 * 1024

HEAD_DIM = 128
ROPE_THETA = 10000.0
EPS = 1e-6
MOBA_BLOCK = 256
MOBA_TOPK = 3
C_NOPE = 128
C_ROPE = 64
IDX_DIM = 128
IDX_ROPE = 64
IDX_TOPK = 256
TOP_K = 2
NEG = -1e30
LOG2E = math.log2(math.e)
INT_MIN = -(2 ** 31)


def _tile(dim, pref, align):
    t = min(pref, dim)
    t -= t % align
    while t >= align:
        if dim % t == 0:
            return t
        t -= align
    return dim


def _params(sem, est_bytes):
    limit = int(min(max(est_bytes * 1.3 + (4 << 20), 32 << 20), VMEM_CAP))
    return pltpu.CompilerParams(dimension_semantics=sem, vmem_limit_bytes=limit)


def _nt(a, b):
    return lax.dot_general(a, b, (((1,), (1,)), ((), ())), preferred_element_type=F32)


def _softmax_step(st, m, l, acc, vt):
    m_new = jnp.maximum(m, jnp.max(st, axis=0, keepdims=True))
    alpha = jnp.exp2(m - m_new)
    p = jnp.exp2(st - m_new)
    l_new = alpha * l + jnp.sum(p, axis=0, keepdims=True)
    acc_new = alpha * acc + jnp.dot(vt, p.astype(vt.dtype), preferred_element_type=F32)
    return m_new, l_new, acc_new


def _init_carry(n_chains, tq, dv):
    return tuple((jnp.full((1, tq), NEG, F32), jnp.zeros((1, tq), F32), jnp.zeros((dv, tq), F32))
                 for _ in range(n_chains))


def _value_tiles(v, n_heads, tk):
    B, S, C = v.shape
    return v.reshape(B, S // tk, tk, n_heads, C // n_heads).transpose(0, 3, 1, 4, 2)


def _rmsnorm_body(x_ref, g_ref, o_ref):
    x = x_ref[...].astype(F32)
    ms = jnp.mean(x * x, axis=-1, keepdims=True)
    o_ref[...] = (x * lax.rsqrt(ms + EPS) * g_ref[...]).astype(o_ref.dtype)


def _rmsnorm(x, g, out_dtype, tm=256):
    M, D = x.shape
    tm = _tile(M, tm, 16)
    est = 2 * tm * D * (4 + 4)
    return pl.pallas_call(
        _rmsnorm_body,
        grid=(M // tm,),
        in_specs=[pl.BlockSpec((tm, D), lambda i: (i, 0)), pl.BlockSpec((1, D), lambda i: (0, 0))],
        out_specs=pl.BlockSpec((tm, D), lambda i: (i, 0)),
        out_shape=jax.ShapeDtypeStruct((M, D), out_dtype),
        compiler_params=_params(("parallel",), est),
        name="rmsnorm",
    )(x, g.reshape(1, D).astype(F32))


def _norm_router_body(x_ref, g_ref, rt_ref, h_ref, gate_ref, sel_ref, *, n_exp):
    x = x_ref[...]
    ms = jnp.mean(x * x, axis=-1, keepdims=True)
    h = x * lax.rsqrt(ms + EPS) * g_ref[...]
    h_ref[...] = h
    lane = lax.broadcasted_iota(I32, gate_ref.shape, 1)
    logits = jnp.full(gate_ref.shape, -jnp.inf, F32)
    for e in range(n_exp):
        col = jnp.sum(h * rt_ref[e:e + 1, :], axis=-1, keepdims=True)
        logits = jnp.where(lane == e, col, logits)
    m1 = jnp.max(logits, axis=-1, keepdims=True)
    i1 = jnp.min(jnp.where(logits == m1, lane, LANE), axis=-1, keepdims=True)
    rest = jnp.where(lane == i1, -jnp.inf, logits)
    m2 = jnp.max(rest, axis=-1, keepdims=True)
    i2 = jnp.min(jnp.where(rest == m2, lane, LANE), axis=-1, keepdims=True)
    e2 = jnp.exp(m2 - m1)
    den = 1.0 + e2
    gate_ref[...] = jnp.where(lane == i1, 1.0 / den, 0.0) + jnp.where(lane == i2, e2 / den, 0.0)
    sel_ref[...] = jnp.where((lane == i1) | (lane == i2), 1, 0).astype(I32)


def _norm_router(x, g, router, tm=256):
    M, D = x.shape
    n_exp = router.shape[1]
    tm = _tile(M, tm, 8)
    est = 2 * tm * D * 8 + 2 * 8 * D * 4
    h, gate, sel = pl.pallas_call(
        functools.partial(_norm_router_body, n_exp=n_exp),
        grid=(M // tm,),
        in_specs=[pl.BlockSpec((tm, D), lambda i: (i, 0)), pl.BlockSpec((1, D), lambda i: (0, 0)),
                  pl.BlockSpec((n_exp, D), lambda i: (0, 0))],
        out_specs=[pl.BlockSpec((tm, D), lambda i: (i, 0)), pl.BlockSpec((tm, LANE), lambda i: (i, 0)),
                   pl.BlockSpec((tm, LANE), lambda i: (i, 0))],
        out_shape=[jax.ShapeDtypeStruct((M, D), F32), jax.ShapeDtypeStruct((M, LANE), F32),
                   jax.ShapeDtypeStruct((M, LANE), I32)],
        compiler_params=_params(("parallel",), est),
        name="norm_router",
    )(x, g.reshape(1, D).astype(F32), router.T.astype(F32))
    return h, gate[:, :n_exp], sel[:, :n_exp]


EPI_NONE, EPI_NORM, EPI_NORM_ROPE, EPI_ROPE = 0, 1, 2, 3


def _head_epilogue(x, g, cos, sin, mode, rope_half):
    if mode in (EPI_NORM, EPI_NORM_ROPE):
        ms = jnp.mean(x * x, axis=-1, keepdims=True)
        x = x * lax.rsqrt(ms + EPS) * g
    if mode in (EPI_NORM_ROPE, EPI_ROPE):
        if rope_half == LANE // 2:
            r = pltpu.roll(x, LANE // 2, axis=1)
        else:
            lane = lax.broadcasted_iota(I32, x.shape, 1)
            first = (lane % (2 * rope_half)) < rope_half
            r = jnp.where(first, pltpu.roll(x, LANE - rope_half, axis=1), pltpu.roll(x, rope_half, axis=1))
        x = x * cos + r * sin
    return x


def _mm_body(flags_ref, a_ref, w_ref, *rest, nk, has_resid, modes, rope_half, tn):
    rest = list(rest)
    resid_ref = rest.pop(0) if has_resid else None
    gain_ref = cos_ref = sin_ref = None
    if modes:
        gain_ref, cos_ref, sin_ref = rest.pop(0), rest.pop(0), rest.pop(0)
    o_ref = rest.pop(0)
    acc_ref = rest.pop(0) if nk > 1 else None
    j = pl.program_id(1)
    k = pl.program_id(2)
    part = jnp.dot(a_ref[...].astype(BF16), w_ref[...], preferred_element_type=F32)

    def finish(acc):
        if modes:
            flag = flags_ref[j]

            @pl.when(flag == EPI_NONE)
            def _():
                o_ref[...] = acc.astype(o_ref.dtype)

            for mode in modes:
                @pl.when(flag == mode)
                def _(mode=mode):
                    for c in range(tn // LANE):
                        sl = slice(c * LANE, (c + 1) * LANE)
                        y = _head_epilogue(acc[:, sl], gain_ref[:, sl], cos_ref[...], sin_ref[...], mode, rope_half)
                        o_ref[:, sl] = y.astype(o_ref.dtype)
        elif has_resid:
            o_ref[...] = (resid_ref[...] + acc).astype(o_ref.dtype)
        else:
            o_ref[...] = acc.astype(o_ref.dtype)

    if nk == 1:
        finish(part)
    else:
        @pl.when(k == 0)
        def _():
            acc_ref[...] = part

        @pl.when((k > 0) & (k < nk - 1))
        def _():
            acc_ref[...] += part

        @pl.when(k == nk - 1)
        def _():
            finish(acc_ref[...] + part)


def _matmul(a, w, *, out_dtype, name, resid=None, epi=None, tm=1024, tn=1024, tk=4096):
    M, K = a.shape
    N = w.shape[1]
    tm = _tile(epi["cos"].shape[0] if epi else M, tm, 16)
    tn = epi["tn"] if epi else _tile(N, tn, LANE)
    tk = _tile(K, tk, LANE)
    nk = K // tk
    a_bytes = a.dtype.itemsize
    o_bytes = jnp.dtype(out_dtype).itemsize
    modes = tuple(epi["modes"]) if epi else ()
    in_specs = [pl.BlockSpec((tm, tk), lambda i, j, k, f: (i, k)),
                pl.BlockSpec((tk, tn), lambda i, j, k, f: (k, j))]
    args = [a, w]
    est = 2 * (tm * tk * a_bytes + tk * tn * 2 + tm * tn * o_bytes) + 3 * tm * tn * 4
    if resid is not None:
        in_specs.append(pl.BlockSpec((tm, tn), lambda i, j, k, f: (i, j)))
        args.append(resid)
        est += 2 * tm * tn * 4
    if epi:
        ns = epi["cos"].shape[0] // tm
        in_specs += [pl.BlockSpec((1, tn), lambda i, j, k, f: (0, j)),
                     pl.BlockSpec((tm, LANE), lambda i, j, k, f: (i % ns, 0)),
                     pl.BlockSpec((tm, LANE), lambda i, j, k, f: (i % ns, 0))]
        args += [epi["gain"].reshape(1, N).astype(F32), epi["cos"], epi["sin"]]
        flags = jnp.asarray(epi["flags"], I32)
        est += 4 * tm * LANE * 4
    else:
        flags = jnp.zeros((N // tn,), I32)
    body = functools.partial(_mm_body, nk=nk, has_resid=resid is not None, modes=modes,
                             rope_half=epi["rope_half"] if epi else 0, tn=tn)
    return pl.pallas_call(
        body,
        grid_spec=pltpu.PrefetchScalarGridSpec(
            num_scalar_prefetch=1,
            grid=(M // tm, N // tn, nk),
            in_specs=in_specs,
            out_specs=pl.BlockSpec((tm, tn), lambda i, j, k, f: (i, j)),
            scratch_shapes=[pltpu.VMEM((tm, tn), F32)] if nk > 1 else [],
        ),
        out_shape=jax.ShapeDtypeStruct((M, N), out_dtype),
        compiler_params=_params(("parallel", "parallel", "arbitrary"), est),
        name=name,
    )(flags, *args)


def _weights_changed(te_ref, m):
    return (m == 0) | (te_ref[m] != te_ref[jnp.maximum(m - 1, 0)])


def _swiglu_body(te_ref, na_ref, a_ref, wg_ref, wu_ref, o_ref, wgb_ref, wub_ref):
    m = pl.program_id(1)

    @pl.when(_weights_changed(te_ref, m))
    def _():
        wgb_ref[...] = wg_ref[...].astype(BF16)
        wub_ref[...] = wu_ref[...].astype(BF16)

    @pl.when(m < na_ref[0])
    def _():
        a = a_ref[...]
        g = jnp.dot(a, wgb_ref[...], preferred_element_type=F32)
        u = jnp.dot(a, wub_ref[...], preferred_element_type=F32)
        o_ref[...] = (g * jax.nn.sigmoid(g) * u).astype(o_ref.dtype)

    @pl.when(m >= na_ref[0])
    def _():
        o_ref[...] = jnp.zeros(o_ref.shape, o_ref.dtype)


def _swiglu_up(a, wg, wu, tile_expert, n_active, *, tm, tn, name):
    M, D = a.shape
    F = wg.shape[2]
    est = 2 * (tm * D * 2 + 2 * D * tn * 4 + tm * tn * 2) + 2 * D * tn * 2 + 4 * tm * tn * 4
    return pl.pallas_call(
        _swiglu_body,
        grid_spec=pltpu.PrefetchScalarGridSpec(
            num_scalar_prefetch=2,
            grid=(F // tn, M // tm),
            in_specs=[pl.BlockSpec((tm, D), lambda n, m, te, na: (m, 0)),
                      pl.BlockSpec((None, D, tn), lambda n, m, te, na: (te[m], 0, n)),
                      pl.BlockSpec((None, D, tn), lambda n, m, te, na: (te[m], 0, n))],
            out_specs=pl.BlockSpec((tm, tn), lambda n, m, te, na: (m, n)),
            scratch_shapes=[pltpu.VMEM((D, tn), BF16), pltpu.VMEM((D, tn), BF16)],
        ),
        out_shape=jax.ShapeDtypeStruct((M, F), BF16),
        compiler_params=_params(("parallel", "arbitrary"), est),
        name=name,
    )(tile_expert, n_active, a, wg, wu)


def _down_body(te_ref, na_ref, a_ref, w_ref, o_ref, wb_ref):
    m = pl.program_id(1)

    @pl.when(_weights_changed(te_ref, m))
    def _():
        wb_ref[...] = w_ref[...].astype(BF16)

    @pl.when(m < na_ref[0])
    def _():
        o_ref[...] = jnp.dot(a_ref[...], wb_ref[...], preferred_element_type=F32)

    @pl.when(m >= na_ref[0])
    def _():
        o_ref[...] = jnp.zeros(o_ref.shape, o_ref.dtype)


def _grouped_down(a, wd, tile_expert, n_active, *, tm, tn, name):
    M, F = a.shape
    D = wd.shape[2]
    est = 2 * (tm * F * 2 + F * tn * 4 + tm * tn * 4) + F * tn * 2 + 2 * tm * tn * 4
    return pl.pallas_call(
        _down_body,
        grid_spec=pltpu.PrefetchScalarGridSpec(
            num_scalar_prefetch=2,
            grid=(D // tn, M // tm),
            in_specs=[pl.BlockSpec((tm, F), lambda n, m, te, na: (m, 0)),
                      pl.BlockSpec((None, F, tn), lambda n, m, te, na: (te[m], 0, n))],
            out_specs=pl.BlockSpec((tm, tn), lambda n, m, te, na: (m, n)),
            scratch_shapes=[pltpu.VMEM((F, tn), BF16)],
        ),
        out_shape=jax.ShapeDtypeStruct((M, D), F32),
        compiler_params=_params(("parallel", "arbitrary"), est),
        name=name,
    )(tile_expert, n_active, a, wd)


def _gather_body(idx_ref, src_ref, o_ref, buf_ref, sem, *, tg):
    base = pl.program_id(0) * tg

    def row_copy(i, row):
        return pltpu.make_async_copy(src_ref.at[pl.ds(row, 1)], buf_ref.at[pl.ds(i, 1)], sem.at[0])

    def start(i, c):
        row_copy(i, idx_ref[base + i]).start()
        return c

    def wait(i, c):
        row_copy(i, 0).wait()
        return c

    lax.fori_loop(0, tg, start, 0)
    lax.fori_loop(0, tg, wait, 0)
    o_ref[...] = buf_ref[...].astype(o_ref.dtype)


def _gather_rows(src, idx, out_dtype, tg=256):
    P = idx.shape[0]
    D = src.shape[1]
    tg = _tile(P, tg, 16)
    est = tg * D * 4 + 2 * tg * D * 2
    return pl.pallas_call(
        functools.partial(_gather_body, tg=tg),
        grid_spec=pltpu.PrefetchScalarGridSpec(
            num_scalar_prefetch=1,
            grid=(P // tg,),
            in_specs=[pl.BlockSpec(memory_space=pl.ANY)],
            out_specs=pl.BlockSpec((tg, D), lambda i, idx: (i, 0)),
            scratch_shapes=[pltpu.VMEM((tg, D), src.dtype), pltpu.SemaphoreType.DMA((1,))],
        ),
        out_shape=jax.ShapeDtypeStruct((P, D), out_dtype),
        compiler_params=_params(("arbitrary",), est),
        name="moe_dispatch_gather",
    )(idx, src)


def _combine_body(slot_ref, x_ref, g_ref, y_ref, o_ref, buf_ref, sem, *, tc):
    base = pl.program_id(0) * tc

    def row_copy(i, k, row):
        return pltpu.make_async_copy(y_ref.at[pl.ds(row, 1)], buf_ref.at[k, pl.ds(i, 1)], sem.at[0])

    def start(i, c):
        for k in range(TOP_K):
            row_copy(i, k, slot_ref[(base + i) * TOP_K + k]).start()
        return c

    def wait(i, c):
        for k in range(TOP_K):
            row_copy(i, k, 0).wait()
        return c

    lax.fori_loop(0, tc, start, 0)
    lax.fori_loop(0, tc, wait, 0)
    g = g_ref[...]
    o_ref[...] = x_ref[...] + (g[:, 0:1] * buf_ref[0] + g[:, 1:2] * buf_ref[1])


def _moe_combine(x, y, slots, gates, tc=128):
    M, D = x.shape
    tc = _tile(M, tc, 8)
    est = 2 * tc * D * 4 + 4 * tc * D * 4
    return pl.pallas_call(
        functools.partial(_combine_body, tc=tc),
        grid_spec=pltpu.PrefetchScalarGridSpec(
            num_scalar_prefetch=1,
            grid=(M // tc,),
            in_specs=[pl.BlockSpec((tc, D), lambda i, s: (i, 0)), pl.BlockSpec((tc, TOP_K), lambda i, s: (i, 0)),
                      pl.BlockSpec(memory_space=pl.ANY)],
            out_specs=pl.BlockSpec((tc, D), lambda i, s: (i, 0)),
            scratch_shapes=[pltpu.VMEM((TOP_K, tc, D), F32), pltpu.SemaphoreType.DMA((1,))],
        ),
        out_shape=jax.ShapeDtypeStruct((M, D), F32),
        compiler_params=_params(("arbitrary",), est),
        name="moe_combine",
    )(slots.reshape(-1), x, gates, y)


def _diff_attn_body(q_ref, k_ref, vt_ref, lq1_ref, lk1_ref, lq2_ref, lk2_ref, g_ref, o_ref, *, tq, hg, lam_init):
    qi = pl.program_id(2)
    w = 2 * HEAD_DIM
    cols = [slice(hh * w + mi * HEAD_DIM, hh * w + (mi + 1) * HEAD_DIM) for hh in range(hg) for mi in range(2)]
    qs = [q_ref[:, c] for c in cols]

    def step(n, carry, mask):
        off = pl.multiple_of(n * tq, tq)
        sts = [_nt(k_ref[pl.ds(off, tq), col_sl], qs[c]) for c, col_sl in enumerate(cols)]
        if mask is not None:
            sts = [jnp.where(mask, st, NEG) for st in sts]
        return tuple(_softmax_step(sts[c], *carry[c], vt_ref[c // 2, n]) for c in range(len(cols)))

    carry = lax.fori_loop(0, qi, lambda n, c: step(n, c, None), _init_carry(len(cols), tq, w))
    key = lax.broadcasted_iota(I32, (tq, tq), 0)
    qry = lax.broadcasted_iota(I32, (tq, tq), 1)
    carry = step(qi, carry, key <= qry)
    lam = (jnp.exp(jnp.sum(lq1_ref[...] * lk1_ref[...], axis=-1, keepdims=True))
           - jnp.exp(jnp.sum(lq2_ref[...] * lk2_ref[...], axis=-1, keepdims=True)) + lam_init)
    for hh in range(hg):
        (_, l1, a1), (_, l2, a2) = carry[2 * hh], carry[2 * hh + 1]
        y = (a1 / l1 - lam * (a2 / l2)).T
        ms = jnp.mean(y * y, axis=-1, keepdims=True)
        hs = slice(hh * w, (hh + 1) * w)
        o_ref[:, hs] = (y * lax.rsqrt(ms + EPS) * g_ref[...] * (1.0 - lam_init)).astype(o_ref.dtype)


def _diff_attention(qkv, vt, n_heads, lam_params, subln, lam_init, tq, hg=4):
    B, S, _ = qkv.shape
    w = 2 * HEAD_DIM
    hg = math.gcd(hg, n_heads)
    ng = n_heads // hg
    vec = pl.BlockSpec((1, HEAD_DIM), lambda b, g, i: (0, 0))
    est = 2 * (2 * tq * hg * w * 2 + 2 * S * hg * w * 2) + 2 * hg * (3 * tq * tq * 4 + tq * w * 4)
    return pl.pallas_call(
        functools.partial(_diff_attn_body, tq=tq, hg=hg, lam_init=lam_init),
        grid=(B, ng, S // tq),
        in_specs=[pl.BlockSpec((None, tq, hg * w), lambda b, g, i: (b, i, g)),
                  pl.BlockSpec((None, S, hg * w), lambda b, g, i: (b, 0, ng + g)),
                  pl.BlockSpec((None, hg, S // tq, w, tq), lambda b, g, i: (b, g, 0, 0, 0)),
                  vec, vec, vec, vec,
                  pl.BlockSpec((1, w), lambda b, g, i: (0, 0))],
        out_specs=pl.BlockSpec((None, tq, hg * w), lambda b, g, i: (b, i, g)),
        out_shape=jax.ShapeDtypeStruct((B, S, n_heads * w), BF16),
        compiler_params=_params(("parallel", "parallel", "arbitrary"), est),
        name="diff_attention",
    )(qkv, qkv, vt, *[p.reshape(1, HEAD_DIM).astype(F32) for p in lam_params],
      subln.reshape(1, w).astype(F32))


def _moba_body(q_ref, k_ref, vt_ref, o_ref, kmean_ref, bias_ref, *, nb, hg):
    qi = pl.program_id(2)
    blk = MOBA_BLOCK
    d = HEAD_DIM
    heads = [slice(hh * d, (hh + 1) * d) for hh in range(hg)]

    @pl.when(qi == 0)
    def _():
        for hh, hs in enumerate(heads):
            for n in range(nb):
                kb = k_ref[n * blk:(n + 1) * blk, hs].astype(F32)
                kmean_ref[hh, n:n + 1, :] = jnp.mean(kb, axis=0, keepdims=True)

    blk_id = lax.broadcasted_iota(I32, (nb, blk), 0)
    qs = []
    for hh, hs in enumerate(heads):
        q = q_ref[:, hs]
        km = kmean_ref[hh]
        km_hi = km.astype(BF16)
        km_lo = (km - km_hi.astype(F32)).astype(BF16)
        gate = _nt(km_hi, q) + _nt(km_lo, q)
        gate = jnp.where(blk_id < qi, gate, -jnp.inf)
        sel = jnp.zeros(gate.shape, jnp.bool_)
        for _ in range(MOBA_TOPK):
            mx = jnp.max(gate, axis=0, keepdims=True)
            idx = jnp.min(jnp.where(gate == mx, blk_id, nb), axis=0, keepdims=True)
            sel = sel | ((blk_id == idx) & (mx > -jnp.inf))
            gate = jnp.where(blk_id == idx, -jnp.inf, gate)
        qs.append(q)
        bias_ref[hh] = jnp.where(sel, 0.0, NEG)

    def step(n, carry, mask):
        off = pl.multiple_of(n * blk, blk)
        sts = [_nt(k_ref[pl.ds(off, blk), hs], qs[hh]) for hh, hs in enumerate(heads)]
        if mask is None:
            sts = [st + bias_ref[hh, pl.ds(n, 1), :] for hh, st in enumerate(sts)]
        else:
            sts = [jnp.where(mask, st, NEG) for st in sts]
        return tuple(_softmax_step(sts[hh], *carry[hh], vt_ref[hh, n]) for hh in range(hg))

    carry = lax.fori_loop(0, qi, lambda n, c: step(n, c, None), _init_carry(hg, blk, d))
    key = lax.broadcasted_iota(I32, (blk, blk), 0)
    qry = lax.broadcasted_iota(I32, (blk, blk), 1)
    carry = step(qi, carry, key <= qry)
    for hh, hs in enumerate(heads):
        _, l, acc = carry[hh]
        o_ref[:, hs] = (acc / l).T.astype(o_ref.dtype)


def _moba_attention(qkv, vt, n_heads, q_col, k_col, hg=8):
    B, S, _ = qkv.shape
    assert S % MOBA_BLOCK == 0 and S // MOBA_BLOCK >= MOBA_TOPK
    nb = S // MOBA_BLOCK
    d = HEAD_DIM
    hg = math.gcd(math.gcd(hg, n_heads), math.gcd(q_col, k_col))
    est = 2 * (2 * MOBA_BLOCK * hg * d * 2 + 2 * S * hg * d * 2) + hg * (3 * MOBA_BLOCK * MOBA_BLOCK * 4)
    return pl.pallas_call(
        functools.partial(_moba_body, nb=nb, hg=hg),
        grid=(B, n_heads // hg, nb),
        in_specs=[pl.BlockSpec((None, MOBA_BLOCK, hg * d), lambda b, g, i: (b, i, q_col // hg + g)),
                  pl.BlockSpec((None, S, hg * d), lambda b, g, i: (b, 0, k_col // hg + g)),
                  pl.BlockSpec((None, hg, nb, d, MOBA_BLOCK), lambda b, g, i: (b, g, 0, 0, 0))],
        out_specs=pl.BlockSpec((None, MOBA_BLOCK, hg * d), lambda b, g, i: (b, i, g)),
        out_shape=jax.ShapeDtypeStruct((B, S, n_heads * d), BF16),
        scratch_shapes=[pltpu.VMEM((hg, nb, d), F32), pltpu.VMEM((hg, nb, MOBA_BLOCK), F32)],
        compiler_params=_params(("parallel", "parallel", "arbitrary"), est),
        name="moba_attention",
    )(qkv, qkv, vt)


def _xattn_body(q_ref, k_ref, v_ref, o_ref, *, n_heads):
    for h in range(n_heads):
        sl = slice(h * HEAD_DIM, (h + 1) * HEAD_DIM)
        s = _nt(q_ref[:, sl], k_ref[:, sl])
        m = jnp.max(s, axis=-1, keepdims=True)
        p = jnp.exp2(s - m)
        l = jnp.sum(p, axis=-1, keepdims=True)
        o = jnp.dot(p.astype(BF16), v_ref[:, sl], preferred_element_type=F32)
        o_ref[:, sl] = (o / l).astype(o_ref.dtype)


def _cross_attention(q, kv, n_heads, tq=512):
    B, S, X = q.shape
    M = kv.shape[1]
    tq = _tile(S, tq, 16)
    est = 2 * (2 * tq * X * 2 + 2 * M * X * 2) + 6 * tq * M * 4
    return pl.pallas_call(
        functools.partial(_xattn_body, n_heads=n_heads),
        grid=(B, S // tq),
        in_specs=[pl.BlockSpec((None, tq, X), lambda b, i: (b, i, 0)),
                  pl.BlockSpec((None, M, X), lambda b, i: (b, 0, 0)),
                  pl.BlockSpec((None, M, X), lambda b, i: (b, 0, 1))],
        out_specs=pl.BlockSpec((None, tq, X), lambda b, i: (b, i, 0)),
        out_shape=jax.ShapeDtypeStruct((B, S, X), BF16),
        compiler_params=_params(("parallel", "parallel"), est),
        name="cross_attention",
    )(q, kv, kv)


def _odd_prep_body(x_ref, gqa_ref, gkv_ref, gkr_ref, gik_ref, cos_ref, sin_ref,
                   qa_ref, ckv_ref, kr_ref, ik_ref, iw_ref, *, c1, c2, iw_scale):
    def norm(x, g):
        ms = jnp.mean(x * x, axis=-1, keepdims=True)
        return x * lax.rsqrt(ms + EPS) * g

    def rope(y):
        lane = lax.broadcasted_iota(I32, y.shape, 1)
        half = C_ROPE // 2
        first = (lane % C_ROPE) < half
        r = jnp.where(first, pltpu.roll(y, LANE - half, axis=1), pltpu.roll(y, half, axis=1))
        return y * cos_ref[...] + r * sin_ref[...]

    qa_ref[...] = norm(x_ref[:, :c1], gqa_ref[...]).astype(qa_ref.dtype)
    ckv_ref[...] = norm(x_ref[:, c1:c2], gkv_ref[...]).astype(ckv_ref.dtype)
    slab_a = x_ref[:, c2:c2 + LANE]
    slab_b = x_ref[:, c2 + LANE:c2 + 2 * LANE]
    lane = lax.broadcasted_iota(I32, slab_a.shape, 1)
    low = lane < C_ROPE
    ms = jnp.sum(jnp.where(low, slab_a * slab_a, 0.0), axis=-1, keepdims=True) / C_ROPE
    kr = rope(slab_a * lax.rsqrt(ms + EPS) * gkr_ref[...])
    kr_ref[0] = kr.astype(kr_ref.dtype)
    kr_ref[1] = pltpu.roll(kr, C_ROPE, axis=1).astype(kr_ref.dtype)
    rot_a = pltpu.roll(slab_a, C_ROPE, axis=1)
    rot_b = pltpu.roll(slab_b, C_ROPE, axis=1)
    ik = jnp.where(low, rot_a, rot_b)
    ik_ref[...] = rope(norm(ik, gik_ref[...])).astype(ik_ref.dtype)
    iw_ref[...] = rot_b * iw_scale


def _odd_prep(x, c1, c2, g_qa, g_kv, g_kr, g_ik, cos_p, sin_p, iw_scale, tm=256):
    B, S, C = x.shape
    assert C == c2 + 2 * LANE
    tm = _tile(S, tm, 16)
    vec = lambda n: pl.BlockSpec((1, n), lambda b, i: (0, 0))
    tab = pl.BlockSpec((tm, LANE), lambda b, i: (i, 0))
    est = 2 * tm * C * 4 * 2
    return pl.pallas_call(
        functools.partial(_odd_prep_body, c1=c1, c2=c2, iw_scale=iw_scale),
        grid=(B, S // tm),
        in_specs=[pl.BlockSpec((None, tm, C), lambda b, i: (b, i, 0)),
                  vec(c1), vec(c2 - c1), vec(LANE), vec(LANE), tab, tab],
        out_specs=[pl.BlockSpec((None, tm, c1), lambda b, i: (b, i, 0)),
                   pl.BlockSpec((None, tm, c2 - c1), lambda b, i: (b, i, 0)),
                   pl.BlockSpec((None, 2, tm, LANE), lambda b, i: (b, 0, i, 0)),
                   pl.BlockSpec((None, tm, LANE), lambda b, i: (b, i, 0)),
                   pl.BlockSpec((None, tm, LANE), lambda b, i: (b, i, 0))],
        out_shape=[jax.ShapeDtypeStruct((B, S, c1), BF16),
                   jax.ShapeDtypeStruct((B, S, c2 - c1), BF16),
                   jax.ShapeDtypeStruct((B, 2, S, LANE), BF16),
                   jax.ShapeDtypeStruct((B, S, LANE), BF16),
                   jax.ShapeDtypeStruct((B, S, LANE), F32)],
        compiler_params=_params(("parallel", "parallel"), est),
        name="odd_prep",
    )(x, g_qa.reshape(1, -1), g_kv.reshape(1, -1), g_kr.reshape(1, -1), g_ik.reshape(1, -1), cos_p, sin_p)


def _q_prep_body(x_ref, gn_ref, gr_ref, cos_ref, sin_ref, qn_ref, qr_ref, *, n_heads):
    nope_w = n_heads * C_NOPE
    width = C_NOPE + C_ROPE
    half = C_ROPE // 2
    lane = lax.broadcasted_iota(I32, (x_ref.shape[0], LANE), 1)
    low = lane < C_ROPE
    first = (lane % C_ROPE) < half
    for p in range(n_heads // 2):
        n0 = x_ref[:, (2 * p) * LANE:(2 * p + 1) * LANE]
        n1 = x_ref[:, (2 * p + 1) * LANE:(2 * p + 2) * LANE]
        r = x_ref[:, nope_w + p * LANE:nope_w + (p + 1) * LANE]
        r2 = r * r
        ss0 = jnp.sum(n0 * n0, axis=-1, keepdims=True) + jnp.sum(jnp.where(low, r2, 0.0), axis=-1, keepdims=True)
        ss1 = jnp.sum(n1 * n1, axis=-1, keepdims=True) + jnp.sum(jnp.where(low, 0.0, r2), axis=-1, keepdims=True)
        inv0 = lax.rsqrt(ss0 / width + EPS)
        inv1 = lax.rsqrt(ss1 / width + EPS)
        qn_ref[:, (2 * p) * LANE:(2 * p + 1) * LANE] = (n0 * inv0 * gn_ref[...]).astype(qn_ref.dtype)
        qn_ref[:, (2 * p + 1) * LANE:(2 * p + 2) * LANE] = (n1 * inv1 * gn_ref[...]).astype(qn_ref.dtype)
        y = r * jnp.where(low, inv0, inv1) * gr_ref[...]
        rot = jnp.where(first, pltpu.roll(y, LANE - half, axis=1), pltpu.roll(y, half, axis=1))
        qr_ref[:, p * LANE:(p + 1) * LANE] = (y * cos_ref[...] + rot * sin_ref[...]).astype(qr_ref.dtype)


def _q_prep(x, n_heads, g_nope, g_rope2, cos_q, sin_q, tm=256):
    B, S, C = x.shape
    tm = _tile(S, tm, 16)
    nope_w, rope_w = n_heads * C_NOPE, n_heads * C_ROPE
    vec = pl.BlockSpec((1, LANE), lambda b, i: (0, 0))
    tab = pl.BlockSpec((tm, LANE), lambda b, i: (i, 0))
    est = 2 * tm * C * 6
    return pl.pallas_call(
        functools.partial(_q_prep_body, n_heads=n_heads),
        grid=(B, S // tm),
        in_specs=[pl.BlockSpec((None, tm, C), lambda b, i: (b, i, 0)), vec, vec, tab, tab],
        out_specs=[pl.BlockSpec((None, tm, nope_w), lambda b, i: (b, i, 0)),
                   pl.BlockSpec((None, tm, rope_w), lambda b, i: (b, i, 0))],
        out_shape=[jax.ShapeDtypeStruct((B, S, nope_w), BF16), jax.ShapeDtypeStruct((B, S, rope_w), BF16)],
        compiler_params=_params(("parallel", "parallel"), est),
        name="dsa_q_prep",
    )(x, g_nope.reshape(1, LANE), g_rope2.reshape(1, LANE), cos_q, sin_q)


def _indexer_body(ik_ref, iq_ref, iwt_ref, o_ref, key_ref, *, tq, n_heads, n_keep, chunk):
    S = ik_ref.shape[0]
    qi = pl.program_id(1)
    n_tiles = qi + 1
    t_idx = qi * tq + lax.broadcasted_iota(I32, (tq, tq), 1)
    s_loc = lax.broadcasted_iota(I32, (tq, tq), 0)

    def score_tile(kt, c):
        off = pl.multiple_of(kt * tq, tq)
        ikt = ik_ref[pl.ds(off, tq), :]
        acc = jnp.zeros((tq, tq), F32)
        for h in range(n_heads):
            r = _nt(ikt, iq_ref[:, h * IDX_DIM:(h + 1) * IDX_DIM])
            acc = acc + jnp.maximum(r, 0.0) * iwt_ref[h:h + 1, :]
        acc = jnp.where(off + s_loc <= t_idx, acc, -jnp.inf)
        bits = lax.bitcast_convert_type(acc, I32)
        key_ref[pl.ds(off, tq), :] = bits ^ ((bits >> 31) & 0x7FFFFFFF)
        return c

    lax.fori_loop(0, n_tiles, score_tile, 0)

    def count_ge(cand):
        def body(c, cnt):
            off = pl.multiple_of(c * chunk, chunk)
            hit = jnp.where(key_ref[pl.ds(off, chunk), :] >= cand, 1, 0).astype(I32)
            return cnt + jnp.sum(hit.reshape(chunk // 8, 8, tq), axis=0)
        cnt = lax.fori_loop(0, n_tiles * (tq // chunk), body, jnp.zeros((8, tq), I32))
        return jnp.sum(cnt, axis=0, keepdims=True)

    thr = jnp.where(count_ge(jnp.zeros((1, tq), I32)) >= n_keep, 0, INT_MIN).astype(I32)

    def bit_step(i, thr):
        cand = thr + lax.shift_left(jnp.int32(1), 30 - i)
        return jnp.where(count_ge(cand) >= n_keep, cand, thr)

    thr = lax.fori_loop(0, 31, bit_step, thr)

    def out_tile(kt, c):
        off = pl.multiple_of(kt * tq, tq)
        ok = (key_ref[pl.ds(off, tq), :] >= thr) & (off + s_loc <= t_idx)
        o_ref[pl.ds(off, tq), :] = jnp.where(ok, 0.0, NEG).astype(o_ref.dtype)
        return c

    def neg_tile(kt, c):
        off = pl.multiple_of(kt * tq, tq)
        o_ref[pl.ds(off, tq), :] = jnp.full((tq, tq), NEG, o_ref.dtype)
        return c

    lax.fori_loop(0, n_tiles, out_tile, 0)
    lax.fori_loop(n_tiles, S // tq, neg_tile, 0)


def _indexer(ik, iq, iwt, n_keep, tq=256):
    B, S, _ = ik.shape
    n_heads = iwt.shape[1]
    tq = _tile(S, tq, LANE)
    est = 2 * (S * IDX_DIM * 2 + tq * n_heads * IDX_DIM * 2 + n_heads * tq * 4 + S * tq * 2) + S * tq * 4 + 8 * tq * tq * 4
    return pl.pallas_call(
        functools.partial(_indexer_body, tq=tq, n_heads=n_heads, n_keep=n_keep, chunk=64),
        grid=(B, S // tq),
        in_specs=[pl.BlockSpec((None, S, IDX_DIM), lambda b, i: (b, 0, 0)),
                  pl.BlockSpec((None, tq, n_heads * IDX_DIM), lambda b, i: (b, i, 0)),
                  pl.BlockSpec((None, n_heads, tq), lambda b, i: (b, 0, i))],
        out_specs=pl.BlockSpec((None, S, tq), lambda b, i: (b, 0, i)),
        out_shape=jax.ShapeDtypeStruct((B, S, S), BF16),
        scratch_shapes=[pltpu.VMEM((S, tq), I32)],
        compiler_params=_params(("parallel", "arbitrary"), est),
        name="dsa_indexer",
    )(ik, iq, iwt)


def _dsa_attn_body(qn_ref, qr_ref, kn_ref, kr_ref, vt_ref, bias_ref, o_ref, *, tq, hg):
    qi = pl.program_id(2)
    heads = [slice(hh * C_NOPE, (hh + 1) * C_NOPE) for hh in range(hg)]
    qs = [jnp.concatenate([qn_ref[:, hs], qr_ref[:, (hh // 2) * LANE:(hh // 2 + 1) * LANE]], axis=1)
          for hh, hs in enumerate(heads)]

    def body(n, carry):
        off = pl.multiple_of(n * tq, tq)
        bias = bias_ref[pl.ds(off, tq), :].astype(F32)
        kr = [kr_ref[par, pl.ds(off, tq), :] for par in range(2)]
        sts = [_nt(jnp.concatenate([kn_ref[pl.ds(off, tq), hs], kr[hh % 2]], axis=1), qs[hh])
               for hh, hs in enumerate(heads)]
        return tuple(_softmax_step(bias + sts[hh], *carry[hh], vt_ref[hh, n]) for hh in range(hg))

    carry = lax.fori_loop(0, qi + 1, body, _init_carry(hg, tq, C_NOPE))
    for hh, hs in enumerate(heads):
        _, l, acc = carry[hh]
        o_ref[:, hs] = (acc / l).T.astype(o_ref.dtype)


def _dsa_attention(qn, qr, kv, kr2, vt, bias, n_heads, tq, hg=8):
    B, S, _ = qn.shape
    hg = min(hg, n_heads)
    assert hg % 2 == 0 and n_heads % hg == 0
    ng = n_heads // hg
    nk = S // tq
    est = 2 * (tq * hg * 192 * 2 + 2 * S * hg * LANE * 2 + 2 * S * LANE * 2 + S * tq * 2 + tq * hg * LANE * 2) \
        + hg * 3 * tq * tq * 4
    return pl.pallas_call(
        functools.partial(_dsa_attn_body, tq=tq, hg=hg),
        grid=(B, ng, S // tq),
        in_specs=[pl.BlockSpec((None, tq, hg * C_NOPE), lambda b, g, i: (b, i, g)),
                  pl.BlockSpec((None, tq, hg * C_ROPE), lambda b, g, i: (b, i, g)),
                  pl.BlockSpec((None, S, hg * C_NOPE), lambda b, g, i: (b, 0, g)),
                  pl.BlockSpec((None, 2, S, LANE), lambda b, g, i: (b, 0, 0, 0)),
                  pl.BlockSpec((None, hg, nk, C_NOPE, tq), lambda b, g, i: (b, g, 0, 0, 0)),
                  pl.BlockSpec((None, S, tq), lambda b, g, i: (b, 0, i))],
        out_specs=pl.BlockSpec((None, tq, hg * C_NOPE), lambda b, g, i: (b, i, g)),
        out_shape=jax.ShapeDtypeStruct((B, S, n_heads * C_NOPE), BF16),
        compiler_params=_params(("parallel", "parallel", "arbitrary"), est),
        name="dsa_attention",
    )(qn, qr, kv, kr2, vt, bias)


def _rope_tables(seq, dim):
    inv_freq = ROPE_THETA ** (-jnp.arange(0, dim, 2, dtype=F32) / dim)
    ang = jnp.arange(seq, dtype=F32)[:, None] * inv_freq[None, :]
    return jnp.cos(ang), jnp.sin(ang)


def _cross_block(x, mem, l, norm_xattn, norm_mem, xa_wq, xa_wk, xa_wv, xa_wo, xa_qnorm, xa_knorm, B, S):
    N, D = x.shape
    X = xa_wq.shape[2]
    n_heads = X // HEAD_DIM
    M = mem.shape[1]
    h = _rmsnorm(x, norm_xattn[l], BF16)
    mn = _rmsnorm(mem.reshape(B * M, D), norm_mem[l], BF16)
    ones = jnp.ones((S, LANE), F32)
    q = _matmul(h, xa_wq[l].astype(BF16), out_dtype=BF16, name="xattn_q",
                epi=dict(flags=[EPI_NORM] * (X // _tile(X, 1024, LANE)), modes=(EPI_NORM,), rope_half=0,
                         tn=_tile(X, 1024, LANE), cos=ones, sin=ones,
                         gain=jnp.tile(xa_qnorm[l] * (HEAD_DIM ** -0.5 * LOG2E), n_heads)))
    wkv = jnp.concatenate([xa_wk[l], xa_wv[l]], axis=1).astype(BF16)
    tn = _tile(X, 1024, LANE)
    kv = _matmul(mn, wkv, out_dtype=BF16, name="xattn_kv",
                 epi=dict(flags=[EPI_NORM] * (X // tn) + [EPI_NONE] * (X // tn), modes=(EPI_NORM,), rope_half=0,
                          tn=tn, cos=jnp.ones((M, LANE), F32), sin=jnp.ones((M, LANE), F32),
                          gain=jnp.concatenate([jnp.tile(xa_knorm[l], n_heads), jnp.ones((X,), F32)])))
    o = _cross_attention(q.reshape(B, S, X), kv.reshape(B, M, 2 * X), n_heads)
    return _matmul(o.reshape(N, X), xa_wo[l].astype(BF16), out_dtype=F32, name="xattn_out", resid=x)


def _even_mixer(x, l, i, B, S, norm_mix, ev_w_in, ev_a_qnorm, ev_a_knorm, lam_params, ev_a_subln,
                ev_b_qnorm, ev_b_knorm, ev_w_out):
    N, D = x.shape
    a_heads = D // (4 * HEAD_DIM)
    b_heads = D // (2 * HEAD_DIM)
    aw = a_heads * 2 * HEAD_DIM
    bw = b_heads * HEAD_DIM
    width = 3 * aw + 3 * bw
    scale = HEAD_DIM ** -0.5 * LOG2E
    cos, sin = _rope_tables(S, HEAD_DIM)
    cos2 = jnp.concatenate([cos, cos], axis=-1)
    sin2 = jnp.concatenate([-sin, sin], axis=-1)
    tn = _tile(math.gcd(aw, bw), 1024, LANE)
    seg = [(aw, EPI_NORM_ROPE, ev_a_qnorm[i] * scale), (aw, EPI_NORM_ROPE, ev_a_knorm[i]), (aw, EPI_NONE, None),
           (bw, EPI_NORM_ROPE, ev_b_qnorm[i] * scale), (bw, EPI_NORM_ROPE, ev_b_knorm[i]), (bw, EPI_NONE, None)]
    flags, gains = [], []
    for w, flag, g in seg:
        flags += [flag] * (w // tn)
        gains.append(jnp.ones((w,), F32) if g is None else jnp.tile(g.astype(F32), w // HEAD_DIM))
    h = _rmsnorm(x, norm_mix[l], BF16)
    qkv = _matmul(h, ev_w_in[i].astype(BF16), out_dtype=BF16, name="even_in",
                  epi=dict(flags=flags, modes=(EPI_NORM_ROPE,), rope_half=HEAD_DIM // 2, tn=tn,
                           cos=cos2, sin=sin2, gain=jnp.concatenate(gains)))
    qkv = qkv.reshape(B, S, width)
    lam_init = 0.8 - 0.6 * math.exp(-0.3 * l)
    tq = _tile(S, 256, LANE)
    ya = _diff_attention(qkv, _value_tiles(qkv[:, :, 2 * aw:3 * aw], a_heads, tq), a_heads, lam_params,
                         ev_a_subln[i], lam_init, tq)
    c0 = 3 * aw // HEAD_DIM
    yb = _moba_attention(qkv, _value_tiles(qkv[:, :, 3 * aw + 2 * bw:], b_heads, MOBA_BLOCK), b_heads,
                         c0, c0 + b_heads)
    y = jnp.concatenate([ya, yb], axis=-1).reshape(N, aw + bw)
    return _matmul(y, ev_w_out[i].astype(BF16), out_dtype=F32, name="even_out", resid=x)


def _odd_mixer(x, l, i, B, S, norm_mix, od_w_in, od_qa_norm, od_w_qb, od_q_norm, od_kv_norm, od_kr_norm,
               od_w_uk, od_w_uv, od_w_iqb, od_ik_norm, od_w_out):
    N, D = x.shape
    c1 = od_qa_norm.shape[1]
    kv_rank = od_kv_norm.shape[1]
    c2 = c1 + kv_rank
    n_heads = od_w_uk.shape[2]
    idx_heads = od_w_iqb.shape[2] // IDX_DIM
    assert od_w_in.shape[2] == c2 + C_ROPE + IDX_DIM + idx_heads and idx_heads == C_ROPE
    scale = (C_NOPE + C_ROPE) ** -0.5 * LOG2E
    cos, sin = _rope_tables(S, C_ROPE)
    one, zero = jnp.ones((S, C_ROPE), F32), jnp.zeros((S, C_ROPE), F32)
    cos_p = jnp.concatenate([cos, cos, one], axis=-1)
    sin_p = jnp.concatenate([-sin, sin, zero], axis=-1)
    cos_q = jnp.concatenate([cos, cos, cos, cos], axis=-1)
    sin_q = jnp.concatenate([-sin, sin, -sin, sin], axis=-1)

    h = _rmsnorm(x, norm_mix[l], BF16)
    proj = _matmul(h, od_w_in[i].astype(BF16), out_dtype=F32, name="odd_in")
    g_kr = jnp.concatenate([od_kr_norm[i], jnp.zeros((LANE - C_ROPE,), F32)])
    qa, ckv, kr2, ik, iw = _odd_prep(proj.reshape(B, S, -1), c1, c2, od_qa_norm[i], od_kv_norm[i], g_kr,
                                     od_ik_norm[i], cos_p, sin_p, idx_heads ** -0.5 * IDX_DIM ** -0.5)
    qa = qa.reshape(N, c1)
    wqb = od_w_qb[i].reshape(c1, n_heads, C_NOPE + C_ROPE)
    wqb = jnp.concatenate([wqb[:, :, :C_NOPE].reshape(c1, -1), wqb[:, :, C_NOPE:].reshape(c1, -1)], axis=1)
    qraw = _matmul(qa, wqb.astype(BF16), out_dtype=F32, name="odd_qb")
    qn, qr = _q_prep(qraw.reshape(B, S, -1), n_heads, od_q_norm[i][:C_NOPE] * scale,
                     jnp.tile(od_q_norm[i][C_NOPE:], 2) * scale, cos_q, sin_q)
    tn = _tile(idx_heads * IDX_DIM, 1024, LANE)
    iq = _matmul(qa, od_w_iqb[i].astype(BF16), out_dtype=BF16, name="odd_iqb",
                 epi=dict(flags=[EPI_ROPE] * (idx_heads * IDX_DIM // tn), modes=(EPI_ROPE,), rope_half=IDX_ROPE // 2,
                          tn=tn, cos=cos_p, sin=sin_p, gain=jnp.ones((idx_heads * IDX_DIM,), F32)))
    wkv = jnp.concatenate([od_w_uk[i].reshape(kv_rank, -1), od_w_uv[i].reshape(kv_rank, -1)], axis=1)
    kv = _matmul(ckv.reshape(N, kv_rank), wkv.astype(BF16), out_dtype=BF16, name="odd_kv")

    n_keep = min(IDX_TOPK, S // 4)
    tq = _tile(S, 256, LANE)
    iwt = jnp.swapaxes(iw[:, :, :idx_heads], 1, 2)
    bias = _indexer(ik, iq.reshape(B, S, -1), iwt, n_keep, tq)
    kv = kv.reshape(B, S, -1)
    vt = _value_tiles(kv[:, :, n_heads * C_NOPE:], n_heads, tq)
    y = _dsa_attention(qn, qr, kv, kr2, vt, bias, n_heads, tq)
    return _matmul(y.reshape(N, -1), od_w_out[i].astype(BF16), out_dtype=F32, name="odd_out", resid=x)


def _dense_ffn(x, g, wg, wu, wd):
    N, D = x.shape
    F = wg.shape[1]
    h = _rmsnorm(x, g, BF16)
    tm = _tile(N, 1024, 16)
    tn = _tile(F, 512, LANE)
    nt = N // tm
    hid = _swiglu_up(h, wg[None], wu[None], jnp.zeros((nt,), I32), jnp.full((1,), nt, I32),
                     tm=tm, tn=tn, name="ffn_up")
    return _matmul(hid, wd.astype(BF16), out_dtype=F32, name="ffn_down", resid=x, tm=512, tn=1024,
                   tk=F if F <= 4096 else _tile(F, F // 2, LANE))


def _moe_ffn(x, g, router, wg, wu, wd, tm=512):
    N, D = x.shape
    E, _, F = wg.shape
    h, gate, sel = _norm_router(x, g, router)
    tm = _tile(N, tm, 16)
    cnt = jnp.sum(sel, axis=0)
    tiles_e = (cnt + tm - 1) // tm
    tile_end = jnp.cumsum(tiles_e)
    start = (tile_end - tiles_e) * tm
    rank = jnp.cumsum(sel, axis=0) - sel
    P = N * TOP_K + E * tm
    n_tiles = P // tm
    slot = (start[None, :] + rank).astype(I32)
    lane = jnp.arange(E, dtype=I32)[None, :]
    e2 = jnp.stack([jnp.min(jnp.where(sel > 0, lane, E), axis=1), jnp.max(jnp.where(sel > 0, lane, -1), axis=1)], 1)
    slots2 = jnp.take_along_axis(slot, e2, axis=1)
    gates2 = jnp.take_along_axis(gate, e2, axis=1)
    tok2 = jnp.broadcast_to(jnp.arange(N, dtype=I32)[:, None], (N, TOP_K))
    tok_of_slot = jnp.zeros((P,), I32).at[slots2.reshape(-1)].set(tok2.reshape(-1), unique_indices=True)
    tile_expert = jnp.minimum(jnp.searchsorted(tile_end, jnp.arange(n_tiles, dtype=I32), side="right"),
                              E - 1).astype(I32)
    n_active = tile_end[-1:].astype(I32)

    xs = _gather_rows(h, tok_of_slot, BF16)
    hid = _swiglu_up(xs, wg, wu, tile_expert, n_active, tm=tm, tn=_tile(F, 512, LANE), name="moe_up")
    y = _grouped_down(hid, wd, tile_expert, n_active, tm=tm, tn=_tile(D, 512, LANE), name="moe_down")
    return _moe_combine(x, y, slots2, gates2)


def kernel(x, mem, norm_mix, norm_xattn, norm_mem, norm_ffn, ev_w_in, ev_a_qnorm, ev_a_knorm, ev_lambda_q1, ev_lambda_k1, ev_lambda_q2, ev_lambda_k2, ev_a_subln, ev_b_qnorm, ev_b_knorm, ev_w_out, od_w_in, od_qa_norm, od_w_qb, od_q_norm, od_kv_norm, od_kr_norm, od_w_uk, od_w_uv, od_w_iqb, od_ik_norm, od_w_out, xa_wq, xa_wk, xa_wv, xa_wo, xa_qnorm, xa_knorm, ffn_wg, ffn_wu, ffn_wd, moe_router, moe_wg, moe_wu, moe_wd):
    B, S, D = x.shape
    depth = norm_mix.shape[0]
    x = x.reshape(B * S, D)
    for l in range(depth):
        i = l // 2
        if l % 2 == 0:
            x = _even_mixer(x, l, i, B, S, norm_mix, ev_w_in, ev_a_qnorm, ev_a_knorm,
                            (ev_lambda_q1[i], ev_lambda_k1[i], ev_lambda_q2[i], ev_lambda_k2[i]),
                            ev_a_subln, ev_b_qnorm, ev_b_knorm, ev_w_out)
        else:
            x = _odd_mixer(x, l, i, B, S, norm_mix, od_w_in, od_qa_norm, od_w_qb, od_q_norm, od_kv_norm,
                           od_kr_norm, od_w_uk, od_w_uv, od_w_iqb, od_ik_norm, od_w_out)
        x = _cross_block(x, mem, l, norm_xattn, norm_mem, xa_wq, xa_wk, xa_wv, xa_wo, xa_qnorm, xa_knorm, B, S)
        if l % 2 == 0:
            x = _dense_ffn(x, norm_ffn[l], ffn_wg[i], ffn_wu[i], ffn_wd[i])
        else:
            x = _moe_ffn(x, norm_ffn[l], moe_router[i], moe_wg[i], moe_wu[i], moe_wd[i])
    return x.reshape(B, S, D)
```

```python
import functools
import math

import jax
import jax.numpy as jnp
import numpy as np
from jax import lax
from jax.experimental import pallas as pl
from jax.experimental.pallas import tpu as pltpu

F32 = jnp.float32
BF16 = jnp.bfloat16
I32 = jnp.int32

LANE = 128
MXU_WIDTH = 256
V7X_VMEM_BYTES = 64 * 1024 * 1024
VMEM_CAP = V7X_VMEM_BYTES - 4 * 1024 * 1024

HEAD_DIM = 128
ROPE_THETA = 10000.0
EPS = 1e-6
MOBA_BLOCK = 256
MOBA_TOPK = 3
C_NOPE = 128
C_ROPE = 64
IDX_DIM = 128
IDX_ROPE = 64
IDX_TOPK = 256
TOP_K = 2
NEG = -1e30
LOG2E = math.log2(math.e)
INT_MIN = -(2 ** 31)


def _tile(dim, pref, align):
    t = min(pref, dim)
    t -= t % align
    while t >= align:
        if dim % t == 0:
            return t
        t -= align
    return dim


def _params(sem, est_bytes):
    limit = int(min(max(est_bytes * 1.3 + (4 << 20), 32 << 20), VMEM_CAP))
    return pltpu.CompilerParams(dimension_semantics=sem, vmem_limit_bytes=limit)


def _nt(a, b):
    return lax.dot_general(a, b, (((1,), (1,)), ((), ())), preferred_element_type=F32)


def _softmax_steps(sts, carry, s_ref, acc_ref, value_tile):
    for c, st in enumerate(sts):
        s_ref[c] = st
    new = []
    for c in range(len(sts)):
        m_new = jnp.maximum(carry[c], jnp.max(s_ref[c], axis=0, keepdims=True))
        alpha = jnp.exp2(carry[c] - m_new)
        p = jnp.exp2(s_ref[c] - m_new)
        new.append(m_new)
        acc_ref[c] = alpha * acc_ref[c] + jnp.dot(value_tile(c), p.astype(BF16), preferred_element_type=F32)
    return tuple(new)


def _normalized(acc_ref, c, dv):
    acc = acc_ref[c]
    return acc[:dv] / acc[dv:dv + 1]


def _init_carry(n_chains, tq):
    return tuple(jnp.full((1, tq), NEG, F32) for _ in range(n_chains))


ONES_ROWS = 16


def _value_tiles(v, n_heads, tk):
    B, S, C = v.shape
    vt = v.reshape(B, S // tk, tk, n_heads, C // n_heads).transpose(0, 3, 1, 4, 2)
    extra = jnp.zeros(vt.shape[:3] + (ONES_ROWS, tk), v.dtype).at[..., 0, :].set(1)
    return jnp.concatenate([vt, extra], axis=3)


def _rmsnorm_body(x_ref, g_ref, o_ref):
    x = x_ref[...].astype(F32)
    ms = jnp.mean(x * x, axis=-1, keepdims=True)
    o_ref[...] = (x * lax.rsqrt(ms + EPS) * g_ref[...]).astype(o_ref.dtype)


def _rmsnorm(x, g, out_dtype, tm=256):
    M, D = x.shape
    tm = _tile(M, tm, 16)
    est = 2 * tm * D * (4 + 4)
    return pl.pallas_call(
        _rmsnorm_body,
        grid=(M // tm,),
        in_specs=[pl.BlockSpec((tm, D), lambda i: (i, 0)), pl.BlockSpec((1, D), lambda i: (0, 0))],
        out_specs=pl.BlockSpec((tm, D), lambda i: (i, 0)),
        out_shape=jax.ShapeDtypeStruct((M, D), out_dtype),
        compiler_params=_params(("parallel",), est),
        name="rmsnorm",
    )(x, g.reshape(1, D).astype(F32))


def _norm_router_body(x_ref, g_ref, rt_ref, h_ref, gate_ref, sel_ref, *, n_exp):
    x = x_ref[...]
    ms = jnp.mean(x * x, axis=-1, keepdims=True)
    h = x * lax.rsqrt(ms + EPS) * g_ref[...]
    h_ref[...] = h
    lane = lax.broadcasted_iota(I32, gate_ref.shape, 1)
    logits = jnp.full(gate_ref.shape, -jnp.inf, F32)
    for e in range(n_exp):
        col = jnp.sum(h * rt_ref[e:e + 1, :], axis=-1, keepdims=True)
        logits = jnp.where(lane == e, col, logits)
    m1 = jnp.max(logits, axis=-1, keepdims=True)
    i1 = jnp.min(jnp.where(logits == m1, lane, LANE), axis=-1, keepdims=True)
    rest = jnp.where(lane == i1, -jnp.inf, logits)
    m2 = jnp.max(rest, axis=-1, keepdims=True)
    i2 = jnp.min(jnp.where(rest == m2, lane, LANE), axis=-1, keepdims=True)
    e2 = jnp.exp(m2 - m1)
    den = 1.0 + e2
    gate_ref[...] = jnp.where(lane == i1, 1.0 / den, 0.0) + jnp.where(lane == i2, e2 / den, 0.0)
    sel_ref[...] = jnp.where((lane == i1) | (lane == i2), 1, 0).astype(I32)


def _norm_router(x, g, router, tm=256):
    M, D = x.shape
    n_exp = router.shape[1]
    tm = _tile(M, tm, 8)
    est = 2 * tm * D * 8 + 2 * 8 * D * 4
    h, gate, sel = pl.pallas_call(
        functools.partial(_norm_router_body, n_exp=n_exp),
        grid=(M // tm,),
        in_specs=[pl.BlockSpec((tm, D), lambda i: (i, 0)), pl.BlockSpec((1, D), lambda i: (0, 0)),
                  pl.BlockSpec((n_exp, D), lambda i: (0, 0))],
        out_specs=[pl.BlockSpec((tm, D), lambda i: (i, 0)), pl.BlockSpec((tm, LANE), lambda i: (i, 0)),
                   pl.BlockSpec((tm, LANE), lambda i: (i, 0))],
        out_shape=[jax.ShapeDtypeStruct((M, D), F32), jax.ShapeDtypeStruct((M, LANE), F32),
                   jax.ShapeDtypeStruct((M, LANE), I32)],
        compiler_params=_params(("parallel",), est),
        name="norm_router",
    )(x, g.reshape(1, D).astype(F32), router.T.astype(F32))
    return h, gate[:, :n_exp], sel[:, :n_exp]


EPI_NONE, EPI_NORM, EPI_NORM_ROPE, EPI_ROPE = 0, 1, 2, 3


def _head_epilogue(x, g, cos, sin, mode, rope_half):
    if mode in (EPI_NORM, EPI_NORM_ROPE):
        ss = jnp.dot((x * x).astype(BF16), jnp.ones((LANE, LANE), BF16), preferred_element_type=F32)
        x = x * lax.rsqrt(ss * (1.0 / LANE) + EPS) * g
    if mode in (EPI_NORM_ROPE, EPI_ROPE):
        if rope_half == LANE // 2:
            r = pltpu.roll(x, LANE // 2, axis=1)
        else:
            lane = lax.broadcasted_iota(I32, x.shape, 1)
            first = (lane % (2 * rope_half)) < rope_half
            r = jnp.where(first, pltpu.roll(x, LANE - rope_half, axis=1), pltpu.roll(x, rope_half, axis=1))
        x = x * cos + r * sin
    return x


def _mm_body(flags_ref, a_ref, w_ref, *rest, nk, has_resid, modes, rope_half, tn, cw):
    rest = list(rest)
    resid_ref = rest.pop(0) if has_resid else None
    gain_ref = cos_ref = sin_ref = None
    if modes:
        gain_ref, cos_ref, sin_ref = rest.pop(0), rest.pop(0), rest.pop(0)
    o_ref = rest.pop(0)
    acc_ref = rest.pop(0) if nk > 1 else None
    j = pl.program_id(1)
    k = pl.program_id(2)

    def emit(cols, acc, mode):
        if mode != EPI_NONE:
            for c in range(cols.start, cols.stop, LANE):
                sl = slice(c, c + LANE)
                y = _head_epilogue(acc[:, sl.start - cols.start:sl.stop - cols.start], gain_ref[:, sl],
                                   cos_ref[...], sin_ref[...], mode, rope_half)
                o_ref[:, sl] = y.astype(o_ref.dtype)
        elif has_resid:
            o_ref[:, cols] = (resid_ref[:, cols] + acc).astype(o_ref.dtype)
        else:
            o_ref[:, cols] = acc.astype(o_ref.dtype)

    if nk == 1:
        def run(mode):
            pending = None
            for c in range(0, tn, cw):
                cols = slice(c, c + cw)
                acc = jnp.dot(a_ref[...], w_ref[:, cols], preferred_element_type=F32)
                if pending is not None:
                    emit(*pending, mode)
                pending = (cols, acc)
            emit(*pending, mode)

        if modes:
            flag = flags_ref[j]
            for mode in (EPI_NONE,) + modes:
                pl.when(flag == mode)(functools.partial(run, mode))
        else:
            run(EPI_NONE)
    else:
        part = jnp.dot(a_ref[...], w_ref[...], preferred_element_type=F32)

        @pl.when(k == 0)
        def _():
            acc_ref[...] = part

        @pl.when((k > 0) & (k < nk - 1))
        def _():
            acc_ref[...] += part

        @pl.when(k == nk - 1)
        def _():
            emit(slice(0, tn), acc_ref[...] + part, EPI_NONE)


def _matmul(a, w, *, out_dtype, name, resid=None, epi=None, tm=1024, tn=1024, tk=4096):
    M, K = a.shape
    N = w.shape[1]
    tm = _tile(epi["cos"].shape[0] if epi else M, tm, 16)
    tn = epi["tn"] if epi else _tile(N, tn, LANE)
    tk = _tile(K, tk, LANE)
    nk = K // tk
    assert a.dtype == BF16 and w.dtype == BF16 and (nk == 1 or not epi)
    a_bytes = a.dtype.itemsize
    o_bytes = jnp.dtype(out_dtype).itemsize
    modes = tuple(epi["modes"]) if epi else ()
    in_specs = [pl.BlockSpec((tm, tk), lambda i, j, k, f: (i, k)),
                pl.BlockSpec((tk, tn), lambda i, j, k, f: (k, j))]
    args = [a, w]
    est = 2 * (tm * tk * a_bytes + tk * tn * 2 + tm * tn * o_bytes) + 3 * tm * tn * 4
    if resid is not None:
        in_specs.append(pl.BlockSpec((tm, tn), lambda i, j, k, f: (i, j)))
        args.append(resid)
        est += 2 * tm * tn * 4
    if epi:
        ns = epi["cos"].shape[0] // tm
        in_specs += [pl.BlockSpec((1, tn), lambda i, j, k, f: (0, j)),
                     pl.BlockSpec((tm, LANE), lambda i, j, k, f: (i % ns, 0)),
                     pl.BlockSpec((tm, LANE), lambda i, j, k, f: (i % ns, 0))]
        args += [epi["gain"].reshape(1, N).astype(F32), epi["cos"], epi["sin"]]
        flags = jnp.asarray(epi["flags"], I32)
        est += 4 * tm * LANE * 4
    else:
        flags = jnp.zeros((N // tn,), I32)
    body = functools.partial(_mm_body, nk=nk, has_resid=resid is not None, modes=modes,
                             rope_half=epi["rope_half"] if epi else 0, tn=tn, cw=math.gcd(tn, MXU_WIDTH))
    return pl.pallas_call(
        body,
        grid_spec=pltpu.PrefetchScalarGridSpec(
            num_scalar_prefetch=1,
            grid=(M // tm, N // tn, nk),
            in_specs=in_specs,
            out_specs=pl.BlockSpec((tm, tn), lambda i, j, k, f: (i, j)),
            scratch_shapes=[pltpu.VMEM((tm, tn), F32)] if nk > 1 else [],
        ),
        out_shape=jax.ShapeDtypeStruct((M, N), out_dtype),
        compiler_params=_params(("parallel", "parallel", "arbitrary"), est),
        name=name,
    )(flags, *args)


def _weights_changed(te_ref, m):
    return (m == 0) | (te_ref[m] != te_ref[jnp.maximum(m - 1, 0)])


def _swiglu_body(te_ref, na_ref, a_ref, wg_ref, wu_ref, o_ref, wgb_ref, wub_ref):
    m = pl.program_id(1)

    @pl.when(_weights_changed(te_ref, m))
    def _():
        wgb_ref[...] = wg_ref[...].astype(BF16)
        wub_ref[...] = wu_ref[...].astype(BF16)

    @pl.when(m < na_ref[0])
    def _():
        tn = o_ref.shape[1]
        cw = math.gcd(tn, MXU_WIDTH)

        def emit(cols, g, u):
            o_ref[:, cols] = (g * jax.nn.sigmoid(g) * u).astype(o_ref.dtype)

        pending = None
        for c in range(0, tn, cw):
            cols = slice(c, c + cw)
            g = jnp.dot(a_ref[...], wgb_ref[:, cols], preferred_element_type=F32)
            u = jnp.dot(a_ref[...], wub_ref[:, cols], preferred_element_type=F32)
            if pending is not None:
                emit(*pending)
            pending = (cols, g, u)
        emit(*pending)

    @pl.when(m >= na_ref[0])
    def _():
        o_ref[...] = jnp.zeros(o_ref.shape, o_ref.dtype)


def _swiglu_up(a, wg, wu, tile_expert, n_active, *, tm, tn, name):
    M, D = a.shape
    F = wg.shape[2]
    est = 2 * (tm * D * 2 + 2 * D * tn * 4 + tm * tn * 2) + 2 * D * tn * 2 + 4 * tm * tn * 4
    return pl.pallas_call(
        _swiglu_body,
        grid_spec=pltpu.PrefetchScalarGridSpec(
            num_scalar_prefetch=2,
            grid=(F // tn, M // tm),
            in_specs=[pl.BlockSpec((tm, D), lambda n, m, te, na: (m, 0)),
                      pl.BlockSpec((None, D, tn), lambda n, m, te, na: (te[m], 0, n)),
                      pl.BlockSpec((None, D, tn), lambda n, m, te, na: (te[m], 0, n))],
            out_specs=pl.BlockSpec((tm, tn), lambda n, m, te, na: (m, n)),
            scratch_shapes=[pltpu.VMEM((D, tn), BF16), pltpu.VMEM((D, tn), BF16)],
        ),
        out_shape=jax.ShapeDtypeStruct((M, F), BF16),
        compiler_params=_params(("parallel", "arbitrary"), est),
        name=name,
    )(tile_expert, n_active, a, wg, wu)


def _down_body(te_ref, na_ref, a_ref, w_ref, o_ref, wb_ref):
    m = pl.program_id(1)

    @pl.when(_weights_changed(te_ref, m))
    def _():
        wb_ref[...] = w_ref[...].astype(BF16)

    @pl.when(m < na_ref[0])
    def _():
        o_ref[...] = jnp.dot(a_ref[...], wb_ref[...], preferred_element_type=F32)

    @pl.when(m >= na_ref[0])
    def _():
        o_ref[...] = jnp.zeros(o_ref.shape, o_ref.dtype)


def _grouped_down(a, wd, tile_expert, n_active, *, tm, tn, name):
    M, F = a.shape
    D = wd.shape[2]
    est = 2 * (tm * F * 2 + F * tn * 4 + tm * tn * 4) + F * tn * 2 + 2 * tm * tn * 4
    return pl.pallas_call(
        _down_body,
        grid_spec=pltpu.PrefetchScalarGridSpec(
            num_scalar_prefetch=2,
            grid=(D // tn, M // tm),
            in_specs=[pl.BlockSpec((tm, F), lambda n, m, te, na: (m, 0)),
                      pl.BlockSpec((None, F, tn), lambda n, m, te, na: (te[m], 0, n))],
            out_specs=pl.BlockSpec((tm, tn), lambda n, m, te, na: (m, n)),
            scratch_shapes=[pltpu.VMEM((F, tn), BF16)],
        ),
        out_shape=jax.ShapeDtypeStruct((M, D), F32),
        compiler_params=_params(("parallel", "arbitrary"), est),
        name=name,
    )(tile_expert, n_active, a, wd)


GATHER_UNROLL = 8


def _gather_body(idx_ref, src_ref, o_ref, buf_ref, sem, *, tg):
    step = pl.program_id(0)

    def row_copy(slot, i, row):
        return pltpu.make_async_copy(src_ref.at[pl.ds(row, 1)], buf_ref.at[slot, pl.ds(i, 1)], sem.at[slot])

    def issue(s):
        slot = s % 2

        def start(i, c):
            row_copy(slot, i, idx_ref[s * tg + i]).start()
            return c

        lax.fori_loop(0, tg, start, 0, unroll=GATHER_UNROLL)

    @pl.when(step == 0)
    def _():
        issue(step)

    @pl.when(step + 1 < pl.num_programs(0))
    def _():
        issue(step + 1)

    slot = step % 2

    def wait(i, c):
        row_copy(slot, i, 0).wait()
        return c

    lax.fori_loop(0, tg, wait, 0, unroll=GATHER_UNROLL)
    o_ref[...] = buf_ref[slot].astype(o_ref.dtype)


def _gather_rows(src, idx, out_dtype, tg=256):
    P = idx.shape[0]
    D = src.shape[1]
    tg = _tile(P, tg, 16)
    est = 2 * tg * D * 4 + 2 * tg * D * 2
    return pl.pallas_call(
        functools.partial(_gather_body, tg=tg),
        grid_spec=pltpu.PrefetchScalarGridSpec(
            num_scalar_prefetch=1,
            grid=(P // tg,),
            in_specs=[pl.BlockSpec(memory_space=pl.ANY)],
            out_specs=pl.BlockSpec((tg, D), lambda i, idx: (i, 0)),
            scratch_shapes=[pltpu.VMEM((2, tg, D), src.dtype), pltpu.SemaphoreType.DMA((2,))],
        ),
        out_shape=jax.ShapeDtypeStruct((P, D), out_dtype),
        compiler_params=_params(("arbitrary",), est),
        name="moe_dispatch_gather",
    )(idx, src)


def _combine_body(slot_ref, x_ref, g_ref, y_ref, o_ref, buf_ref, sem, *, tc):
    step = pl.program_id(0)

    def row_copy(slot, i, k, row):
        return pltpu.make_async_copy(y_ref.at[pl.ds(row, 1)], buf_ref.at[slot, k, pl.ds(i, 1)], sem.at[slot])

    def issue(s):
        slot = s % 2

        def start(i, c):
            for k in range(TOP_K):
                row_copy(slot, i, k, slot_ref[(s * tc + i) * TOP_K + k]).start()
            return c

        lax.fori_loop(0, tc, start, 0, unroll=GATHER_UNROLL // TOP_K)

    @pl.when(step == 0)
    def _():
        issue(step)

    @pl.when(step + 1 < pl.num_programs(0))
    def _():
        issue(step + 1)

    slot = step % 2

    def wait(i, c):
        for k in range(TOP_K):
            row_copy(slot, i, k, 0).wait()
        return c

    lax.fori_loop(0, tc, wait, 0, unroll=GATHER_UNROLL // TOP_K)
    g = g_ref[...]
    o_ref[...] = x_ref[...] + (g[:, 0:1] * buf_ref[slot, 0] + g[:, 1:2] * buf_ref[slot, 1])


def _moe_combine(x, y, slots, gates, tc=128):
    M, D = x.shape
    tc = _tile(M, tc, 8)
    est = 4 * tc * D * 4 + 2 * TOP_K * tc * D * 4
    return pl.pallas_call(
        functools.partial(_combine_body, tc=tc),
        grid_spec=pltpu.PrefetchScalarGridSpec(
            num_scalar_prefetch=1,
            grid=(M // tc,),
            in_specs=[pl.BlockSpec((tc, D), lambda i, s: (i, 0)), pl.BlockSpec((tc, TOP_K), lambda i, s: (i, 0)),
                      pl.BlockSpec(memory_space=pl.ANY)],
            out_specs=pl.BlockSpec((tc, D), lambda i, s: (i, 0)),
            scratch_shapes=[pltpu.VMEM((2, TOP_K, tc, D), F32), pltpu.SemaphoreType.DMA((2,))],
        ),
        out_shape=jax.ShapeDtypeStruct((M, D), F32),
        compiler_params=_params(("arbitrary",), est),
        name="moe_combine",
    )(slots.reshape(-1), x, gates, y)


def _diff_attn_body(q_ref, k_ref, vt_ref, lq1_ref, lk1_ref, lq2_ref, lk2_ref, g_ref, o_ref, s_ref, acc_ref,
                    *, tq, hg, lam_init):
    qi = pl.program_id(2)
    w = 2 * HEAD_DIM
    cols = [slice(hh * w + mi * HEAD_DIM, hh * w + (mi + 1) * HEAD_DIM) for hh in range(hg) for mi in range(2)]
    qs = [q_ref[:, c] for c in cols]

    def step(n, carry, mask):
        off = pl.multiple_of(n * tq, tq)
        sts = [_nt(k_ref[pl.ds(off, tq), col_sl], qs[c]) for c, col_sl in enumerate(cols)]
        if mask is not None:
            sts = [jnp.where(mask, st, NEG) for st in sts]
        return _softmax_steps(sts, carry, s_ref, acc_ref, lambda c: vt_ref[c // 2, n])

    acc_ref[...] = jnp.zeros(acc_ref.shape, F32)
    carry = lax.fori_loop(0, qi, lambda n, c: step(n, c, None), _init_carry(len(cols), tq))
    key = lax.broadcasted_iota(I32, (tq, tq), 0)
    qry = lax.broadcasted_iota(I32, (tq, tq), 1)
    step(qi, carry, key <= qry)
    lam = (jnp.exp(jnp.sum(lq1_ref[...] * lk1_ref[...], axis=-1, keepdims=True))
           - jnp.exp(jnp.sum(lq2_ref[...] * lk2_ref[...], axis=-1, keepdims=True)) + lam_init)
    for hh in range(hg):
        y = (_normalized(acc_ref, 2 * hh, w) - lam * _normalized(acc_ref, 2 * hh + 1, w)).T
        ms = jnp.mean(y * y, axis=-1, keepdims=True)
        hs = slice(hh * w, (hh + 1) * w)
        o_ref[:, hs] = (y * lax.rsqrt(ms + EPS) * g_ref[...] * (1.0 - lam_init)).astype(o_ref.dtype)


def _diff_attention(qkv, vt, n_heads, lam_params, subln, lam_init, tq, hg=4):
    B, S, _ = qkv.shape
    w = 2 * HEAD_DIM
    hg = math.gcd(hg, n_heads)
    ng = n_heads // hg
    vec = pl.BlockSpec((1, HEAD_DIM), lambda b, g, i: (0, 0))
    est = 2 * (2 * tq * hg * w * 2 + 2 * S * hg * w * 2) + 2 * hg * (3 * tq * tq * 4 + tq * w * 4)
    return pl.pallas_call(
        functools.partial(_diff_attn_body, tq=tq, hg=hg, lam_init=lam_init),
        grid=(B, ng, S // tq),
        in_specs=[pl.BlockSpec((None, tq, hg * w), lambda b, g, i: (b, i, g)),
                  pl.BlockSpec((None, S, hg * w), lambda b, g, i: (b, 0, ng + g)),
                  pl.BlockSpec((None, hg, S // tq, w + ONES_ROWS, tq), lambda b, g, i: (b, g, 0, 0, 0)),
                  vec, vec, vec, vec,
                  pl.BlockSpec((1, w), lambda b, g, i: (0, 0))],
        out_specs=pl.BlockSpec((None, tq, hg * w), lambda b, g, i: (b, i, g)),
        out_shape=jax.ShapeDtypeStruct((B, S, n_heads * w), BF16),
        scratch_shapes=[pltpu.VMEM((2 * hg, tq, tq), F32), pltpu.VMEM((2 * hg, w + ONES_ROWS, tq), F32)],
        compiler_params=_params(("parallel", "parallel", "arbitrary"), est),
        name="diff_attention",
    )(qkv, qkv, vt, *[p.reshape(1, HEAD_DIM).astype(F32) for p in lam_params],
      subln.reshape(1, w).astype(F32))


def _moba_body(q_ref, k_ref, vt_ref, o_ref, kmean_ref, bias_ref, s_ref, acc_ref, *, nb, hg):
    qi = pl.program_id(2)
    blk = MOBA_BLOCK
    d = HEAD_DIM
    heads = [slice(hh * d, (hh + 1) * d) for hh in range(hg)]

    @pl.when(qi == 0)
    def _():
        for hh, hs in enumerate(heads):
            for n in range(nb):
                kb = k_ref[n * blk:(n + 1) * blk, hs].astype(F32)
                kmean_ref[hh, n:n + 1, :] = jnp.mean(kb, axis=0, keepdims=True)

    blk_id = lax.broadcasted_iota(I32, (nb, blk), 0)
    qs = []
    for hh, hs in enumerate(heads):
        q = q_ref[:, hs]
        km = kmean_ref[hh]
        km_hi = km.astype(BF16)
        km_lo = (km - km_hi.astype(F32)).astype(BF16)
        gate = _nt(km_hi, q) + _nt(km_lo, q)
        gate = jnp.where(blk_id < qi, gate, -jnp.inf)
        sel = jnp.zeros(gate.shape, jnp.bool_)
        for _ in range(MOBA_TOPK):
            mx = jnp.max(gate, axis=0, keepdims=True)
            idx = jnp.min(jnp.where(gate == mx, blk_id, nb), axis=0, keepdims=True)
            sel = sel | ((blk_id == idx) & (mx > -jnp.inf))
            gate = jnp.where(blk_id == idx, -jnp.inf, gate)
        qs.append(q)
        bias_ref[hh] = jnp.where(sel, 0.0, NEG)

    def step(n, carry, mask):
        off = pl.multiple_of(n * blk, blk)
        sts = [_nt(k_ref[pl.ds(off, blk), hs], qs[hh]) for hh, hs in enumerate(heads)]
        if mask is None:
            sts = [st + bias_ref[hh, pl.ds(n, 1), :] for hh, st in enumerate(sts)]
        else:
            sts = [jnp.where(mask, st, NEG) for st in sts]
        return _softmax_steps(sts, carry, s_ref, acc_ref, lambda hh: vt_ref[hh, n])

    acc_ref[...] = jnp.zeros(acc_ref.shape, F32)
    carry = lax.fori_loop(0, qi, lambda n, c: step(n, c, None), _init_carry(hg, blk))
    key = lax.broadcasted_iota(I32, (blk, blk), 0)
    qry = lax.broadcasted_iota(I32, (blk, blk), 1)
    step(qi, carry, key <= qry)
    for hh, hs in enumerate(heads):
        o_ref[:, hs] = _normalized(acc_ref, hh, d).T.astype(o_ref.dtype)


def _moba_attention(qkv, vt, n_heads, q_col, k_col, hg=8):
    B, S, _ = qkv.shape
    assert S % MOBA_BLOCK == 0 and S // MOBA_BLOCK >= MOBA_TOPK
    nb = S // MOBA_BLOCK
    d = HEAD_DIM
    hg = math.gcd(math.gcd(hg, n_heads), math.gcd(q_col, k_col))
    est = 2 * (2 * MOBA_BLOCK * hg * d * 2 + 2 * S * hg * d * 2) + hg * (3 * MOBA_BLOCK * MOBA_BLOCK * 4)
    return pl.pallas_call(
        functools.partial(_moba_body, nb=nb, hg=hg),
        grid=(B, n_heads // hg, nb),
        in_specs=[pl.BlockSpec((None, MOBA_BLOCK, hg * d), lambda b, g, i: (b, i, q_col // hg + g)),
                  pl.BlockSpec((None, S, hg * d), lambda b, g, i: (b, 0, k_col // hg + g)),
                  pl.BlockSpec((None, hg, nb, d + ONES_ROWS, MOBA_BLOCK), lambda b, g, i: (b, g, 0, 0, 0))],
        out_specs=pl.BlockSpec((None, MOBA_BLOCK, hg * d), lambda b, g, i: (b, i, g)),
        out_shape=jax.ShapeDtypeStruct((B, S, n_heads * d), BF16),
        scratch_shapes=[pltpu.VMEM((hg, nb, d), F32), pltpu.VMEM((hg, nb, MOBA_BLOCK), F32),
                        pltpu.VMEM((hg, MOBA_BLOCK, MOBA_BLOCK), F32),
                        pltpu.VMEM((hg, d + ONES_ROWS, MOBA_BLOCK), F32)],
        compiler_params=_params(("parallel", "parallel", "arbitrary"), est),
        name="moba_attention",
    )(qkv, qkv, vt)


def _xattn_body(q_ref, k_ref, v_ref, o_ref, *, n_heads):
    for h in range(n_heads):
        sl = slice(h * HEAD_DIM, (h + 1) * HEAD_DIM)
        s = _nt(q_ref[:, sl], k_ref[:, sl])
        m = jnp.max(s, axis=-1, keepdims=True)
        p = jnp.exp2(s - m)
        l = jnp.sum(p, axis=-1, keepdims=True)
        o = jnp.dot(p.astype(BF16), v_ref[:, sl], preferred_element_type=F32)
        o_ref[:, sl] = (o / l).astype(o_ref.dtype)


def _cross_attention(q, kv, n_heads, tq=512):
    B, S, X = q.shape
    M = kv.shape[1]
    tq = _tile(S, tq, 16)
    est = 2 * (2 * tq * X * 2 + 2 * M * X * 2) + 6 * tq * M * 4
    return pl.pallas_call(
        functools.partial(_xattn_body, n_heads=n_heads),
        grid=(B, S // tq),
        in_specs=[pl.BlockSpec((None, tq, X), lambda b, i: (b, i, 0)),
                  pl.BlockSpec((None, M, X), lambda b, i: (b, 0, 0)),
                  pl.BlockSpec((None, M, X), lambda b, i: (b, 0, 1))],
        out_specs=pl.BlockSpec((None, tq, X), lambda b, i: (b, i, 0)),
        out_shape=jax.ShapeDtypeStruct((B, S, X), BF16),
        compiler_params=_params(("parallel", "parallel"), est),
        name="cross_attention",
    )(q, kv, kv)


def _odd_prep_body(x_ref, gqa_ref, gkv_ref, gkr_ref, gik_ref, cos_ref, sin_ref,
                   qa_ref, ckv_ref, kr_ref, ik_ref, iw_ref, *, c1, c2, iw_scale):
    def norm(x, g):
        ms = jnp.mean(x * x, axis=-1, keepdims=True)
        return x * lax.rsqrt(ms + EPS) * g

    def rope(y):
        lane = lax.broadcasted_iota(I32, y.shape, 1)
        half = C_ROPE // 2
        first = (lane % C_ROPE) < half
        r = jnp.where(first, pltpu.roll(y, LANE - half, axis=1), pltpu.roll(y, half, axis=1))
        return y * cos_ref[...] + r * sin_ref[...]

    qa_ref[...] = norm(x_ref[:, :c1], gqa_ref[...]).astype(qa_ref.dtype)
    ckv_ref[...] = norm(x_ref[:, c1:c2], gkv_ref[...]).astype(ckv_ref.dtype)
    slab_a = x_ref[:, c2:c2 + LANE]
    slab_b = x_ref[:, c2 + LANE:c2 + 2 * LANE]
    lane = lax.broadcasted_iota(I32, slab_a.shape, 1)
    low = lane < C_ROPE
    ms = jnp.sum(jnp.where(low, slab_a * slab_a, 0.0), axis=-1, keepdims=True) / C_ROPE
    kr = rope(slab_a * lax.rsqrt(ms + EPS) * gkr_ref[...])
    kr_ref[0] = kr.astype(kr_ref.dtype)
    kr_ref[1] = pltpu.roll(kr, C_ROPE, axis=1).astype(kr_ref.dtype)
    rot_a = pltpu.roll(slab_a, C_ROPE, axis=1)
    rot_b = pltpu.roll(slab_b, C_ROPE, axis=1)
    ik = jnp.where(low, rot_a, rot_b)
    ik_ref[...] = rope(norm(ik, gik_ref[...])).astype(ik_ref.dtype)
    iw_ref[...] = rot_b * iw_scale


def _odd_prep(x, c1, c2, g_qa, g_kv, g_kr, g_ik, cos_p, sin_p, iw_scale, tm=256):
    B, S, C = x.shape
    assert C == c2 + 2 * LANE
    tm = _tile(S, tm, 16)
    vec = lambda n: pl.BlockSpec((1, n), lambda b, i: (0, 0))
    tab = pl.BlockSpec((tm, LANE), lambda b, i: (i, 0))
    est = 2 * tm * C * 4 * 2
    return pl.pallas_call(
        functools.partial(_odd_prep_body, c1=c1, c2=c2, iw_scale=iw_scale),
        grid=(B, S // tm),
        in_specs=[pl.BlockSpec((None, tm, C), lambda b, i: (b, i, 0)),
                  vec(c1), vec(c2 - c1), vec(LANE), vec(LANE), tab, tab],
        out_specs=[pl.BlockSpec((None, tm, c1), lambda b, i: (b, i, 0)),
                   pl.BlockSpec((None, tm, c2 - c1), lambda b, i: (b, i, 0)),
                   pl.BlockSpec((None, 2, tm, LANE), lambda b, i: (b, 0, i, 0)),
                   pl.BlockSpec((None, tm, LANE), lambda b, i: (b, i, 0)),
                   pl.BlockSpec((None, tm, LANE), lambda b, i: (b, i, 0))],
        out_shape=[jax.ShapeDtypeStruct((B, S, c1), BF16),
                   jax.ShapeDtypeStruct((B, S, c2 - c1), BF16),
                   jax.ShapeDtypeStruct((B, 2, S, LANE), BF16),
                   jax.ShapeDtypeStruct((B, S, LANE), BF16),
                   jax.ShapeDtypeStruct((B, S, LANE), F32)],
        compiler_params=_params(("parallel", "parallel"), est),
        name="odd_prep",
    )(x, g_qa.reshape(1, -1), g_kv.reshape(1, -1), g_kr.reshape(1, -1), g_ik.reshape(1, -1), cos_p, sin_p)


def _q_prep_body(x_ref, gn_ref, gr_ref, cos_ref, sin_ref, qn_ref, qr_ref, *, n_heads):
    nope_w = n_heads * C_NOPE
    width = C_NOPE + C_ROPE
    half = C_ROPE // 2
    lane = lax.broadcasted_iota(I32, (x_ref.shape[0], LANE), 1)
    low = lane < C_ROPE
    first = (lane % C_ROPE) < half
    for p in range(n_heads // 2):
        n0 = x_ref[:, (2 * p) * LANE:(2 * p + 1) * LANE]
        n1 = x_ref[:, (2 * p + 1) * LANE:(2 * p + 2) * LANE]
        r = x_ref[:, nope_w + p * LANE:nope_w + (p + 1) * LANE]
        r2 = r * r
        ss0 = jnp.sum(n0 * n0, axis=-1, keepdims=True) + jnp.sum(jnp.where(low, r2, 0.0), axis=-1, keepdims=True)
        ss1 = jnp.sum(n1 * n1, axis=-1, keepdims=True) + jnp.sum(jnp.where(low, 0.0, r2), axis=-1, keepdims=True)
        inv0 = lax.rsqrt(ss0 / width + EPS)
        inv1 = lax.rsqrt(ss1 / width + EPS)
        qn_ref[:, (2 * p) * LANE:(2 * p + 1) * LANE] = (n0 * inv0 * gn_ref[...]).astype(qn_ref.dtype)
        qn_ref[:, (2 * p + 1) * LANE:(2 * p + 2) * LANE] = (n1 * inv1 * gn_ref[...]).astype(qn_ref.dtype)
        y = r * jnp.where(low, inv0, inv1) * gr_ref[...]
        rot = jnp.where(first, pltpu.roll(y, LANE - half, axis=1), pltpu.roll(y, half, axis=1))
        qr_ref[:, p * LANE:(p + 1) * LANE] = (y * cos_ref[...] + rot * sin_ref[...]).astype(qr_ref.dtype)


def _q_prep(x, n_heads, g_nope, g_rope2, cos_q, sin_q, tm=256):
    B, S, C = x.shape
    tm = _tile(S, tm, 16)
    nope_w, rope_w = n_heads * C_NOPE, n_heads * C_ROPE
    vec = pl.BlockSpec((1, LANE), lambda b, i: (0, 0))
    tab = pl.BlockSpec((tm, LANE), lambda b, i: (i, 0))
    est = 2 * tm * C * 6
    return pl.pallas_call(
        functools.partial(_q_prep_body, n_heads=n_heads),
        grid=(B, S // tm),
        in_specs=[pl.BlockSpec((None, tm, C), lambda b, i: (b, i, 0)), vec, vec, tab, tab],
        out_specs=[pl.BlockSpec((None, tm, nope_w), lambda b, i: (b, i, 0)),
                   pl.BlockSpec((None, tm, rope_w), lambda b, i: (b, i, 0))],
        out_shape=[jax.ShapeDtypeStruct((B, S, nope_w), BF16), jax.ShapeDtypeStruct((B, S, rope_w), BF16)],
        compiler_params=_params(("parallel", "parallel"), est),
        name="dsa_q_prep",
    )(x, g_nope.reshape(1, LANE), g_rope2.reshape(1, LANE), cos_q, sin_q)


def _indexer_body(ik_ref, iq_ref, iwt_ref, o_ref, key_ref, *, tq, n_heads, n_keep, chunk):
    S = ik_ref.shape[0]
    qi = pl.program_id(1)
    n_tiles = qi + 1
    t_idx = qi * tq + lax.broadcasted_iota(I32, (tq, tq), 1)
    s_loc = lax.broadcasted_iota(I32, (tq, tq), 0)

    def score_tile(kt, c):
        off = pl.multiple_of(kt * tq, tq)
        ikt = ik_ref[pl.ds(off, tq), :]
        acc = jnp.zeros((tq, tq), F32)
        for h in range(n_heads):
            r = _nt(ikt, iq_ref[:, h * IDX_DIM:(h + 1) * IDX_DIM])
            acc = acc + jnp.maximum(r, 0.0) * iwt_ref[h:h + 1, :]
        acc = jnp.where(off + s_loc <= t_idx, acc, -jnp.inf)
        bits = lax.bitcast_convert_type(acc, I32)
        key_ref[pl.ds(off, tq), :] = bits ^ ((bits >> 31) & 0x7FFFFFFF)
        return c

    lax.fori_loop(0, n_tiles, score_tile, 0)

    def count_ge(cand):
        def body(c, cnt):
            off = pl.multiple_of(c * chunk, chunk)
            hit = jnp.where(key_ref[pl.ds(off, chunk), :] >= cand, 1, 0).astype(I32)
            return cnt + jnp.sum(hit.reshape(chunk // 8, 8, tq), axis=0)
        cnt = lax.fori_loop(0, n_tiles * (tq // chunk), body, jnp.zeros((8, tq), I32))
        return jnp.sum(cnt, axis=0, keepdims=True)

    thr = jnp.where(count_ge(jnp.zeros((1, tq), I32)) >= n_keep, 0, INT_MIN).astype(I32)

    def bit_step(i, thr):
        cand = thr + lax.shift_left(jnp.int32(1), 30 - i)
        return jnp.where(count_ge(cand) >= n_keep, cand, thr)

    thr = lax.fori_loop(0, 31, bit_step, thr)

    def out_tile(kt, c):
        off = pl.multiple_of(kt * tq, tq)
        ok = (key_ref[pl.ds(off, tq), :] >= thr) & (off + s_loc <= t_idx)
        o_ref[pl.ds(off, tq), :] = jnp.where(ok, 0.0, NEG).astype(o_ref.dtype)
        return c

    def neg_tile(kt, c):
        off = pl.multiple_of(kt * tq, tq)
        o_ref[pl.ds(off, tq), :] = jnp.full((tq, tq), NEG, o_ref.dtype)
        return c

    lax.fori_loop(0, n_tiles, out_tile, 0)
    lax.fori_loop(n_tiles, S // tq, neg_tile, 0)


def _indexer(ik, iq, iwt, n_keep, tq=256):
    B, S, _ = ik.shape
    n_heads = iwt.shape[1]
    tq = _tile(S, tq, LANE)
    est = 2 * (S * IDX_DIM * 2 + tq * n_heads * IDX_DIM * 2 + n_heads * tq * 4 + S * tq * 2) + S * tq * 4 + 8 * tq * tq * 4
    return pl.pallas_call(
        functools.partial(_indexer_body, tq=tq, n_heads=n_heads, n_keep=n_keep, chunk=tq),
        grid=(B, S // tq),
        in_specs=[pl.BlockSpec((None, S, IDX_DIM), lambda b, i: (b, 0, 0)),
                  pl.BlockSpec((None, tq, n_heads * IDX_DIM), lambda b, i: (b, i, 0)),
                  pl.BlockSpec((None, n_heads, tq), lambda b, i: (b, 0, i))],
        out_specs=pl.BlockSpec((None, S, tq), lambda b, i: (b, 0, i)),
        out_shape=jax.ShapeDtypeStruct((B, S, S), BF16),
        scratch_shapes=[pltpu.VMEM((S, tq), I32)],
        compiler_params=_params(("parallel", "arbitrary"), est),
        name="dsa_indexer",
    )(ik, iq, iwt)


def _dsa_attn_body(qn_ref, qr_ref, kn_ref, kr_ref, vt_ref, bias_ref, o_ref, s_ref, acc_ref, *, tq, hg):
    qi = pl.program_id(2)
    heads = [slice(hh * C_NOPE, (hh + 1) * C_NOPE) for hh in range(hg)]
    qs = [jnp.concatenate([qn_ref[:, hs], qr_ref[:, (hh // 2) * LANE:(hh // 2 + 1) * LANE]], axis=1)
          for hh, hs in enumerate(heads)]

    def scores(n):
        off = pl.multiple_of(n * tq, tq)
        bias = bias_ref[pl.ds(off, tq), :].astype(F32)
        kr = [kr_ref[par, pl.ds(off, tq), :] for par in range(2)]
        return tuple(bias + _nt(jnp.concatenate([kn_ref[pl.ds(off, tq), hs], kr[hh % 2]], axis=1), qs[hh])
                     for hh, hs in enumerate(heads))

    def body(n, carry):
        return _softmax_steps(scores(n), carry, s_ref, acc_ref, lambda hh: vt_ref[hh, n])

    acc_ref[...] = jnp.zeros(acc_ref.shape, F32)
    lax.fori_loop(0, qi + 1, body, _init_carry(hg, tq))
    for hh, hs in enumerate(heads):
        o_ref[:, hs] = _normalized(acc_ref, hh, C_NOPE).T.astype(o_ref.dtype)


def _dsa_attention(qn, qr, kv, kr2, vt, bias, n_heads, tq, hg=8):
    B, S, _ = qn.shape
    hg = min(hg, n_heads)
    assert hg % 2 == 0 and n_heads % hg == 0
    ng = n_heads // hg
    nk = S // tq
    est = 2 * (tq * hg * 192 * 2 + 2 * S * hg * LANE * 2 + 2 * S * LANE * 2 + S * tq * 2 + tq * hg * LANE * 2) \
        + hg * 3 * tq * tq * 4
    return pl.pallas_call(
        functools.partial(_dsa_attn_body, tq=tq, hg=hg),
        grid=(B, ng, S // tq),
        in_specs=[pl.BlockSpec((None, tq, hg * C_NOPE), lambda b, g, i: (b, i, g)),
                  pl.BlockSpec((None, tq, hg * C_ROPE), lambda b, g, i: (b, i, g)),
                  pl.BlockSpec((None, S, hg * C_NOPE), lambda b, g, i: (b, 0, g)),
                  pl.BlockSpec((None, 2, S, LANE), lambda b, g, i: (b, 0, 0, 0)),
                  pl.BlockSpec((None, hg, nk, C_NOPE + ONES_ROWS, tq), lambda b, g, i: (b, g, 0, 0, 0)),
                  pl.BlockSpec((None, S, tq), lambda b, g, i: (b, 0, i))],
        out_specs=pl.BlockSpec((None, tq, hg * C_NOPE), lambda b, g, i: (b, i, g)),
        out_shape=jax.ShapeDtypeStruct((B, S, n_heads * C_NOPE), BF16),
        scratch_shapes=[pltpu.VMEM((hg, tq, tq), F32), pltpu.VMEM((hg, C_NOPE + ONES_ROWS, tq), F32)],
        compiler_params=_params(("parallel", "parallel", "arbitrary"), est),
        name="dsa_attention",
    )(qn, qr, kv, kr2, vt, bias)


def _rope_tables(seq, dim):
    inv_freq = ROPE_THETA ** (-jnp.arange(0, dim, 2, dtype=F32) / dim)
    ang = jnp.arange(seq, dtype=F32)[:, None] * inv_freq[None, :]
    return jnp.cos(ang), jnp.sin(ang)


def _cross_block(x, mem, l, norm_xattn, norm_mem, xa_wq, xa_wk, xa_wv, xa_wo, xa_qnorm, xa_knorm, B, S):
    N, D = x.shape
    X = xa_wq.shape[2]
    n_heads = X // HEAD_DIM
    M = mem.shape[1]
    h = _rmsnorm(x, norm_xattn[l], BF16)
    mn = _rmsnorm(mem.reshape(B * M, D), norm_mem[l], BF16)
    ones = jnp.ones((S, LANE), F32)
    q = _matmul(h, xa_wq[l].astype(BF16), out_dtype=BF16, name="xattn_q",
                epi=dict(flags=[EPI_NORM] * (X // _tile(X, 1024, LANE)), modes=(EPI_NORM,), rope_half=0,
                         tn=_tile(X, 1024, LANE), cos=ones, sin=ones,
                         gain=jnp.tile(xa_qnorm[l] * (HEAD_DIM ** -0.5 * LOG2E), n_heads)))
    wkv = jnp.concatenate([xa_wk[l], xa_wv[l]], axis=1).astype(BF16)
    tn = _tile(X, 1024, LANE)
    kv = _matmul(mn, wkv, out_dtype=BF16, name="xattn_kv",
                 epi=dict(flags=[EPI_NORM] * (X // tn) + [EPI_NONE] * (X // tn), modes=(EPI_NORM,), rope_half=0,
                          tn=tn, cos=jnp.ones((M, LANE), F32), sin=jnp.ones((M, LANE), F32),
                          gain=jnp.concatenate([jnp.tile(xa_knorm[l], n_heads), jnp.ones((X,), F32)])))
    o = _cross_attention(q.reshape(B, S, X), kv.reshape(B, M, 2 * X), n_heads)
    return _matmul(o.reshape(N, X), xa_wo[l].astype(BF16), out_dtype=F32, name="xattn_out", resid=x)


def _even_mixer(x, l, i, B, S, norm_mix, ev_w_in, ev_a_qnorm, ev_a_knorm, lam_params, ev_a_subln,
                ev_b_qnorm, ev_b_knorm, ev_w_out):
    N, D = x.shape
    a_heads = D // (4 * HEAD_DIM)
    b_heads = D // (2 * HEAD_DIM)
    aw = a_heads * 2 * HEAD_DIM
    bw = b_heads * HEAD_DIM
    width = 3 * aw + 3 * bw
    scale = HEAD_DIM ** -0.5 * LOG2E
    cos, sin = _rope_tables(S, HEAD_DIM)
    cos2 = jnp.concatenate([cos, cos], axis=-1)
    sin2 = jnp.concatenate([-sin, sin], axis=-1)
    tn = _tile(math.gcd(aw, bw), 1024, LANE)
    seg = [(aw, EPI_NORM_ROPE, ev_a_qnorm[i] * scale), (aw, EPI_NORM_ROPE, ev_a_knorm[i]), (aw, EPI_NONE, None),
           (bw, EPI_NORM_ROPE, ev_b_qnorm[i] * scale), (bw, EPI_NORM_ROPE, ev_b_knorm[i]), (bw, EPI_NONE, None)]
    flags, gains = [], []
    for w, flag, g in seg:
        flags += [flag] * (w // tn)
        gains.append(jnp.ones((w,), F32) if g is None else jnp.tile(g.astype(F32), w // HEAD_DIM))
    h = _rmsnorm(x, norm_mix[l], BF16)
    qkv = _matmul(h, ev_w_in[i].astype(BF16), out_dtype=BF16, name="even_in",
                  epi=dict(flags=flags, modes=(EPI_NORM_ROPE,), rope_half=HEAD_DIM // 2, tn=tn,
                           cos=cos2, sin=sin2, gain=jnp.concatenate(gains)))
    qkv = qkv.reshape(B, S, width)
    lam_init = 0.8 - 0.6 * math.exp(-0.3 * l)
    tq = _tile(S, 256, LANE)
    ya = _diff_attention(qkv, _value_tiles(qkv[:, :, 2 * aw:3 * aw], a_heads, tq), a_heads, lam_params,
                         ev_a_subln[i], lam_init, tq)
    c0 = 3 * aw // HEAD_DIM
    yb = _moba_attention(qkv, _value_tiles(qkv[:, :, 3 * aw + 2 * bw:], b_heads, MOBA_BLOCK), b_heads,
                         c0, c0 + b_heads)
    y = jnp.concatenate([ya, yb], axis=-1).reshape(N, aw + bw)
    return _matmul(y, ev_w_out[i].astype(BF16), out_dtype=F32, name="even_out", resid=x)


def _odd_mixer(x, l, i, B, S, norm_mix, od_w_in, od_qa_norm, od_w_qb, od_q_norm, od_kv_norm, od_kr_norm,
               od_w_uk, od_w_uv, od_w_iqb, od_ik_norm, od_w_out):
    N, D = x.shape
    c1 = od_qa_norm.shape[1]
    kv_rank = od_kv_norm.shape[1]
    c2 = c1 + kv_rank
    n_heads = od_w_uk.shape[2]
    idx_heads = od_w_iqb.shape[2] // IDX_DIM
    assert od_w_in.shape[2] == c2 + C_ROPE + IDX_DIM + idx_heads and idx_heads == C_ROPE
    scale = (C_NOPE + C_ROPE) ** -0.5 * LOG2E
    cos, sin = _rope_tables(S, C_ROPE)
    one, zero = jnp.ones((S, C_ROPE), F32), jnp.zeros((S, C_ROPE), F32)
    cos_p = jnp.concatenate([cos, cos, one], axis=-1)
    sin_p = jnp.concatenate([-sin, sin, zero], axis=-1)
    cos_q = jnp.concatenate([cos, cos, cos, cos], axis=-1)
    sin_q = jnp.concatenate([-sin, sin, -sin, sin], axis=-1)

    h = _rmsnorm(x, norm_mix[l], BF16)
    proj = _matmul(h, od_w_in[i].astype(BF16), out_dtype=F32, name="odd_in")
    g_kr = jnp.concatenate([od_kr_norm[i], jnp.zeros((LANE - C_ROPE,), F32)])
    qa, ckv, kr2, ik, iw = _odd_prep(proj.reshape(B, S, -1), c1, c2, od_qa_norm[i], od_kv_norm[i], g_kr,
                                     od_ik_norm[i], cos_p, sin_p, idx_heads ** -0.5 * IDX_DIM ** -0.5)
    qa = qa.reshape(N, c1)
    wqb = od_w_qb[i].reshape(c1, n_heads, C_NOPE + C_ROPE)
    wqb = jnp.concatenate([wqb[:, :, :C_NOPE].reshape(c1, -1), wqb[:, :, C_NOPE:].reshape(c1, -1)], axis=1)
    qraw = _matmul(qa, wqb.astype(BF16), out_dtype=F32, name="odd_qb")
    qn, qr = _q_prep(qraw.reshape(B, S, -1), n_heads, od_q_norm[i][:C_NOPE] * scale,
                     jnp.tile(od_q_norm[i][C_NOPE:], 2) * scale, cos_q, sin_q)
    tn = _tile(idx_heads * IDX_DIM, 1024, LANE)
    iq = _matmul(qa, od_w_iqb[i].astype(BF16), out_dtype=BF16, name="odd_iqb",
                 epi=dict(flags=[EPI_ROPE] * (idx_heads * IDX_DIM // tn), modes=(EPI_ROPE,), rope_half=IDX_ROPE // 2,
                          tn=tn, cos=cos_p, sin=sin_p, gain=jnp.ones((idx_heads * IDX_DIM,), F32)))
    wkv = jnp.concatenate([od_w_uk[i].reshape(kv_rank, -1), od_w_uv[i].reshape(kv_rank, -1)], axis=1)
    kv = _matmul(ckv.reshape(N, kv_rank), wkv.astype(BF16), out_dtype=BF16, name="odd_kv")

    n_keep = min(IDX_TOPK, S // 4)
    tq = _tile(S, 256, LANE)
    iwt = jnp.swapaxes(iw[:, :, :idx_heads], 1, 2)
    bias = _indexer(ik, iq.reshape(B, S, -1), iwt, n_keep, tq)
    kv = kv.reshape(B, S, -1)
    vt = _value_tiles(kv[:, :, n_heads * C_NOPE:], n_heads, tq)
    y = _dsa_attention(qn, qr, kv, kr2, vt, bias, n_heads, tq)
    return _matmul(y.reshape(N, -1), od_w_out[i].astype(BF16), out_dtype=F32, name="odd_out", resid=x)


def _dense_ffn(x, g, wg, wu, wd):
    N, D = x.shape
    F = wg.shape[1]
    h = _rmsnorm(x, g, BF16)
    tm = _tile(N, 1024, 16)
    tn = _tile(F, 512, LANE)
    nt = N // tm
    hid = _swiglu_up(h, wg[None], wu[None], jnp.zeros((nt,), I32), jnp.full((1,), nt, I32),
                     tm=tm, tn=tn, name="ffn_up")
    return _matmul(hid, wd.astype(BF16), out_dtype=F32, name="ffn_down", resid=x, tm=512, tn=1024,
                   tk=F if F <= 4096 else _tile(F, F // 2, LANE))


def _moe_ffn(x, g, router, wg, wu, wd, tm=512):
    N, D = x.shape
    E, _, F = wg.shape
    h, gate, sel = _norm_router(x, g, router)
    tm = _tile(N, tm, 16)
    cnt = jnp.sum(sel, axis=0)
    tiles_e = (cnt + tm - 1) // tm
    tile_end = jnp.cumsum(tiles_e)
    start = (tile_end - tiles_e) * tm
    rank = jnp.cumsum(sel, axis=0) - sel
    P = N * TOP_K + E * tm
    n_tiles = P // tm
    slot = (start[None, :] + rank).astype(I32)
    lane = jnp.arange(E, dtype=I32)[None, :]
    e2 = jnp.stack([jnp.min(jnp.where(sel > 0, lane, E), axis=1), jnp.max(jnp.where(sel > 0, lane, -1), axis=1)], 1)
    slots2 = jnp.take_along_axis(slot, e2, axis=1)
    gates2 = jnp.take_along_axis(gate, e2, axis=1)
    tok2 = jnp.broadcast_to(jnp.arange(N, dtype=I32)[:, None], (N, TOP_K))
    tok_of_slot = jnp.zeros((P,), I32).at[slots2.reshape(-1)].set(tok2.reshape(-1), unique_indices=True)
    tile_expert = jnp.minimum(jnp.searchsorted(tile_end, jnp.arange(n_tiles, dtype=I32), side="right"),
                              E - 1).astype(I32)
    n_active = tile_end[-1:].astype(I32)

    xs = _gather_rows(h, tok_of_slot, BF16)
    hid = _swiglu_up(xs, wg, wu, tile_expert, n_active, tm=tm, tn=_tile(F, 512, LANE), name="moe_up")
    y = _grouped_down(hid, wd, tile_expert, n_active, tm=tm, tn=_tile(D, 512, LANE), name="moe_down")
    return _moe_combine(x, y, slots2, gates2)


def kernel(x, mem, norm_mix, norm_xattn, norm_mem, norm_ffn, ev_w_in, ev_a_qnorm, ev_a_knorm, ev_lambda_q1, ev_lambda_k1, ev_lambda_q2, ev_lambda_k2, ev_a_subln, ev_b_qnorm, ev_b_knorm, ev_w_out, od_w_in, od_qa_norm, od_w_qb, od_q_norm, od_kv_norm, od_kr_norm, od_w_uk, od_w_uv, od_w_iqb, od_ik_norm, od_w_out, xa_wq, xa_wk, xa_wv, xa_wo, xa_qnorm, xa_knorm, ffn_wg, ffn_wu, ffn_wd, moe_router, moe_wg, moe_wu, moe_wd):
    B, S, D = x.shape
    depth = norm_mix.shape[0]
    x = x.reshape(B * S, D)
    for l in range(depth):
        i = l // 2
        if l % 2 == 0:
            x = _even_mixer(x, l, i, B, S, norm_mix, ev_w_in, ev_a_qnorm, ev_a_knorm,
                            (ev_lambda_q1[i], ev_lambda_k1[i], ev_lambda_q2[i], ev_lambda_k2[i]),
                            ev_a_subln, ev_b_qnorm, ev_b_knorm, ev_w_out)
        else:
            x = _odd_mixer(x, l, i, B, S, norm_mix, od_w_in, od_qa_norm, od_w_qb, od_q_norm, od_kv_norm,
                           od_kr_norm, od_w_uk, od_w_uv, od_w_iqb, od_ik_norm, od_w_out)
        x = _cross_block(x, mem, l, norm_xattn, norm_mem, xa_wq, xa_wk, xa_wv, xa_wo, xa_qnorm, xa_knorm, B, S)
        if l % 2 == 0:
            x = _dense_ffn(x, norm_ffn[l], ffn_wg[i], ffn_wu[i], ffn_wd[i])
        else:
            x = _moe_ffn(x, norm_ffn[l], moe_router[i], moe_wg[i], moe_wu[i], moe_wd[i])
    return x.reshape(B, S, D)
```

```python
import functools
import math

import jax
import jax.numpy as jnp
import numpy as np
from jax import lax
from jax.experimental import pallas as pl
from jax.experimental.pallas import tpu as pltpu

F32 = jnp.float32
BF16 = jnp.bfloat16
I32 = jnp.int32

LANE = 128
MXU_WIDTH = 256
V7X_VMEM_BYTES = 64 * 1024 * 1024
VMEM_CAP = V7X_VMEM_BYTES - 4 * 1024 * 1024

HEAD_DIM = 128
ROPE_THETA = 10000.0
EPS = 1e-6
MOBA_BLOCK = 256
MOBA_TOPK = 3
C_NOPE = 128
C_ROPE = 64
IDX_DIM = 128
IDX_ROPE = 64
IDX_TOPK = 256
TOP_K = 2
NEG = -1e30
LOG2E = math.log2(math.e)
INT_MIN = -(2 ** 31)


def _tile(dim, pref, align):
    t = min(pref, dim)
    t -= t % align
    while t >= align:
        if dim % t == 0:
            return t
        t -= align
    return dim


def _params(sem, est_bytes):
    limit = int(min(max(est_bytes * 1.3 + (4 << 20), 32 << 20), VMEM_CAP))
    return pltpu.CompilerParams(dimension_semantics=sem, vmem_limit_bytes=limit)


def _nt(a, b):
    return lax.dot_general(a, b, (((1,), (1,)), ((), ())), preferred_element_type=F32)


def _softmax_steps(sts, carry, s_ref, acc_ref, value_tile):
    for c, st in enumerate(sts):
        s_ref[c] = st
    new = []
    for c in range(len(sts)):
        m_new = jnp.maximum(carry[c], jnp.max(s_ref[c], axis=0, keepdims=True))
        alpha = jnp.exp2(carry[c] - m_new)
        p = jnp.exp2(s_ref[c] - m_new)
        new.append(m_new)
        acc_ref[c] = alpha * acc_ref[c] + jnp.dot(value_tile(c), p.astype(BF16), preferred_element_type=F32)
    return tuple(new)


def _normalized(acc_ref, c, dv):
    acc = acc_ref[c]
    return acc[:dv] / acc[dv:dv + 1]


def _init_carry(n_chains, tq):
    return tuple(jnp.full((1, tq), NEG, F32) for _ in range(n_chains))


ONES_ROWS = 16


def _value_tiles(v, n_heads, tk):
    B, S, C = v.shape
    vt = v.reshape(B, S // tk, tk, n_heads, C // n_heads).transpose(0, 3, 1, 4, 2)
    extra = jnp.zeros(vt.shape[:3] + (ONES_ROWS, tk), v.dtype).at[..., 0, :].set(1)
    return jnp.concatenate([vt, extra], axis=3)


def _rmsnorm_body(x_ref, g_ref, o_ref):
    x = x_ref[...].astype(F32)
    ms = jnp.mean(x * x, axis=-1, keepdims=True)
    o_ref[...] = (x * lax.rsqrt(ms + EPS) * g_ref[...]).astype(o_ref.dtype)


def _rmsnorm(x, g, out_dtype, tm=256):
    M, D = x.shape
    tm = _tile(M, tm, 16)
    est = 2 * tm * D * (4 + 4)
    return pl.pallas_call(
        _rmsnorm_body,
        grid=(M // tm,),
        in_specs=[pl.BlockSpec((tm, D), lambda i: (i, 0)), pl.BlockSpec((1, D), lambda i: (0, 0))],
        out_specs=pl.BlockSpec((tm, D), lambda i: (i, 0)),
        out_shape=jax.ShapeDtypeStruct((M, D), out_dtype),
        compiler_params=_params(("parallel",), est),
        name="rmsnorm",
    )(x, g.reshape(1, D).astype(F32))


def _norm_router_body(x_ref, g_ref, rt_ref, h_ref, gate_ref, sel_ref, *, n_exp):
    x = x_ref[...]
    ms = jnp.mean(x * x, axis=-1, keepdims=True)
    h = x * lax.rsqrt(ms + EPS) * g_ref[...]
    h_ref[...] = h
    lane = lax.broadcasted_iota(I32, gate_ref.shape, 1)
    logits = jnp.full(gate_ref.shape, -jnp.inf, F32)
    for e in range(n_exp):
        col = jnp.sum(h * rt_ref[e:e + 1, :], axis=-1, keepdims=True)
        logits = jnp.where(lane == e, col, logits)
    m1 = jnp.max(logits, axis=-1, keepdims=True)
    i1 = jnp.min(jnp.where(logits == m1, lane, LANE), axis=-1, keepdims=True)
    rest = jnp.where(lane == i1, -jnp.inf, logits)
    m2 = jnp.max(rest, axis=-1, keepdims=True)
    i2 = jnp.min(jnp.where(rest == m2, lane, LANE), axis=-1, keepdims=True)
    e2 = jnp.exp(m2 - m1)
    den = 1.0 + e2
    gate_ref[...] = jnp.where(lane == i1, 1.0 / den, 0.0) + jnp.where(lane == i2, e2 / den, 0.0)
    sel_ref[...] = jnp.where((lane == i1) | (lane == i2), 1, 0).astype(I32)


def _norm_router(x, g, router, tm=256):
    M, D = x.shape
    n_exp = router.shape[1]
    tm = _tile(M, tm, 8)
    est = 2 * tm * D * 8 + 2 * 8 * D * 4
    h, gate, sel = pl.pallas_call(
        functools.partial(_norm_router_body, n_exp=n_exp),
        grid=(M // tm,),
        in_specs=[pl.BlockSpec((tm, D), lambda i: (i, 0)), pl.BlockSpec((1, D), lambda i: (0, 0)),
                  pl.BlockSpec((n_exp, D), lambda i: (0, 0))],
        out_specs=[pl.BlockSpec((tm, D), lambda i: (i, 0)), pl.BlockSpec((tm, LANE), lambda i: (i, 0)),
                   pl.BlockSpec((tm, LANE), lambda i: (i, 0))],
        out_shape=[jax.ShapeDtypeStruct((M, D), F32), jax.ShapeDtypeStruct((M, LANE), F32),
                   jax.ShapeDtypeStruct((M, LANE), I32)],
        compiler_params=_params(("parallel",), est),
        name="norm_router",
    )(x, g.reshape(1, D).astype(F32), router.T.astype(F32))
    return h, gate[:, :n_exp], sel[:, :n_exp]


EPI_NONE, EPI_NORM, EPI_NORM_ROPE, EPI_ROPE = 0, 1, 2, 3


def _head_epilogue(x, g, cos, sin, mode, rope_half):
    if mode in (EPI_NORM, EPI_NORM_ROPE):
        ss = jnp.dot((x * x).astype(BF16), jnp.ones((LANE, LANE), BF16), preferred_element_type=F32)
        x = x * lax.rsqrt(ss * (1.0 / LANE) + EPS) * g
    if mode in (EPI_NORM_ROPE, EPI_ROPE):
        if rope_half == LANE // 2:
            r = pltpu.roll(x, LANE // 2, axis=1)
        else:
            lane = lax.broadcasted_iota(I32, x.shape, 1)
            first = (lane % (2 * rope_half)) < rope_half
            r = jnp.where(first, pltpu.roll(x, LANE - rope_half, axis=1), pltpu.roll(x, rope_half, axis=1))
        x = x * cos + r * sin
    return x


def _mm_body(flags_ref, a_ref, w_ref, *rest, nk, has_resid, modes, rope_half, tn, cw):
    rest = list(rest)
    resid_ref = rest.pop(0) if has_resid else None
    gain_ref = cos_ref = sin_ref = None
    if modes:
        gain_ref, cos_ref, sin_ref = rest.pop(0), rest.pop(0), rest.pop(0)
    o_ref = rest.pop(0)
    acc_ref = rest.pop(0) if nk > 1 else None
    j = pl.program_id(1)
    k = pl.program_id(2)

    def emit(cols, acc, mode):
        if mode != EPI_NONE:
            for c in range(cols.start, cols.stop, LANE):
                sl = slice(c, c + LANE)
                y = _head_epilogue(acc[:, sl.start - cols.start:sl.stop - cols.start], gain_ref[:, sl],
                                   cos_ref[...], sin_ref[...], mode, rope_half)
                o_ref[:, sl] = y.astype(o_ref.dtype)
        elif has_resid:
            o_ref[:, cols] = (resid_ref[:, cols] + acc).astype(o_ref.dtype)
        else:
            o_ref[:, cols] = acc.astype(o_ref.dtype)

    if nk == 1:
        def run(mode):
            pending = None
            for c in range(0, tn, cw):
                cols = slice(c, c + cw)
                acc = jnp.dot(a_ref[...], w_ref[:, cols], preferred_element_type=F32)
                if pending is not None:
                    emit(*pending, mode)
                pending = (cols, acc)
            emit(*pending, mode)

        if modes:
            flag = flags_ref[j]
            for mode in (EPI_NONE,) + modes:
                pl.when(flag == mode)(functools.partial(run, mode))
        else:
            run(EPI_NONE)
    else:
        part = jnp.dot(a_ref[...], w_ref[...], preferred_element_type=F32)

        @pl.when(k == 0)
        def _():
            acc_ref[...] = part

        @pl.when((k > 0) & (k < nk - 1))
        def _():
            acc_ref[...] += part

        @pl.when(k == nk - 1)
        def _():
            emit(slice(0, tn), acc_ref[...] + part, EPI_NONE)


def _matmul(a, w, *, out_dtype, name, resid=None, epi=None, tm=1024, tn=1024, tk=4096):
    M, K = a.shape
    N = w.shape[1]
    tm = _tile(epi["cos"].shape[0] if epi else M, tm, 16)
    tn = epi["tn"] if epi else _tile(N, tn, LANE)
    tk = _tile(K, tk, LANE)
    nk = K // tk
    assert a.dtype == BF16 and w.dtype == BF16 and (nk == 1 or not epi)
    a_bytes = a.dtype.itemsize
    o_bytes = jnp.dtype(out_dtype).itemsize
    modes = tuple(epi["modes"]) if epi else ()
    in_specs = [pl.BlockSpec((tm, tk), lambda i, j, k, f: (i, k)),
                pl.BlockSpec((tk, tn), lambda i, j, k, f: (k, j))]
    args = [a, w]
    est = 2 * (tm * tk * a_bytes + tk * tn * 2 + tm * tn * o_bytes) + 3 * tm * tn * 4
    if resid is not None:
        in_specs.append(pl.BlockSpec((tm, tn), lambda i, j, k, f: (i, j)))
        args.append(resid)
        est += 2 * tm * tn * 4
    if epi:
        ns = epi["cos"].shape[0] // tm
        in_specs += [pl.BlockSpec((1, tn), lambda i, j, k, f: (0, j)),
                     pl.BlockSpec((tm, LANE), lambda i, j, k, f: (i % ns, 0)),
                     pl.BlockSpec((tm, LANE), lambda i, j, k, f: (i % ns, 0))]
        args += [epi["gain"].reshape(1, N).astype(F32), epi["cos"], epi["sin"]]
        flags = jnp.asarray(epi["flags"], I32)
        est += 4 * tm * LANE * 4
    else:
        flags = jnp.zeros((N // tn,), I32)
    body = functools.partial(_mm_body, nk=nk, has_resid=resid is not None, modes=modes,
                             rope_half=epi["rope_half"] if epi else 0, tn=tn, cw=math.gcd(tn, MXU_WIDTH))
    return pl.pallas_call(
        body,
        grid_spec=pltpu.PrefetchScalarGridSpec(
            num_scalar_prefetch=1,
            grid=(M // tm, N // tn, nk),
            in_specs=in_specs,
            out_specs=pl.BlockSpec((tm, tn), lambda i, j, k, f: (i, j)),
            scratch_shapes=[pltpu.VMEM((tm, tn), F32)] if nk > 1 else [],
        ),
        out_shape=jax.ShapeDtypeStruct((M, N), out_dtype),
        compiler_params=_params(("parallel", "parallel", "arbitrary"), est),
        name=name,
    )(flags, *args)


def _segments(tile_expert, n_active):
    T = tile_expert.shape[0]
    idx = jnp.arange(T, dtype=I32)
    prev = jnp.concatenate([tile_expert[:1] - 1, tile_expert[:-1]])
    first = (idx < n_active[0]) & (tile_expert != prev)
    first_idx = jnp.where(first, idx, T)
    after = jnp.concatenate([lax.cummin(first_idx[::-1])[::-1][1:], jnp.full((1,), T, I32)])
    nxt = jnp.where(after < T, tile_expert[jnp.minimum(after, T - 1)], -1)
    return first.astype(I32), nxt.astype(I32)


def _stream_weights(te_ref, first_ref, next_ref, w_refs, stage_refs, cast_refs, sem, tn):
    n = pl.program_id(0)
    m = pl.program_id(1)

    def copies(e, nn):
        cols = pl.ds(pl.multiple_of(nn * tn, tn), tn)
        return [pltpu.make_async_copy(w.at[e, :, cols], st, sem.at[i])
                for i, (w, st) in enumerate(zip(w_refs, stage_refs))]

    @pl.when(first_ref[m] == 1)
    def _():
        @pl.when((n == 0) & (m == 0))
        def _():
            for c in copies(te_ref[0], 0):
                c.start()

        for c in copies(te_ref[m], n):
            c.wait()
        for st, cb in zip(stage_refs, cast_refs):
            cb[...] = st[...].astype(BF16)
        last = next_ref[m] < 0
        e2 = jnp.where(last, te_ref[0], next_ref[m])
        n2 = jnp.where(last, n + 1, n)

        @pl.when(n2 < pl.num_programs(0))
        def _():
            for c in copies(e2, n2):
                c.start()


def _swiglu_body(te_ref, na_ref, first_ref, next_ref, a_ref, wg_ref, wu_ref, o_ref,
                 wgf_ref, wuf_ref, wgb_ref, wub_ref, sem):
    m = pl.program_id(1)
    tn = o_ref.shape[1]
    _stream_weights(te_ref, first_ref, next_ref, (wg_ref, wu_ref), (wgf_ref, wuf_ref), (wgb_ref, wub_ref), sem, tn)

    @pl.when(m < na_ref[0])
    def _():
        cw = math.gcd(tn, MXU_WIDTH)

        def emit(cols, g, u):
            o_ref[:, cols] = (g * jax.nn.sigmoid(g) * u).astype(o_ref.dtype)

        pending = None
        for c in range(0, tn, cw):
            cols = slice(c, c + cw)
            g = jnp.dot(a_ref[...], wgb_ref[:, cols], preferred_element_type=F32)
            u = jnp.dot(a_ref[...], wub_ref[:, cols], preferred_element_type=F32)
            if pending is not None:
                emit(*pending)
            pending = (cols, g, u)
        emit(*pending)

    @pl.when(m >= na_ref[0])
    def _():
        o_ref[...] = jnp.zeros(o_ref.shape, o_ref.dtype)


def _swiglu_up(a, wg, wu, tile_expert, n_active, *, tm, tn, name):
    M, D = a.shape
    F = wg.shape[2]
    first, nxt = _segments(tile_expert, n_active)
    est = 2 * (tm * D * 2 + tm * tn * 2) + 2 * D * tn * (4 + 2) + 4 * tm * tn * 4
    return pl.pallas_call(
        _swiglu_body,
        grid_spec=pltpu.PrefetchScalarGridSpec(
            num_scalar_prefetch=4,
            grid=(F // tn, M // tm),
            in_specs=[pl.BlockSpec((tm, D), lambda n, m, *_: (m, 0)),
                      pl.BlockSpec(memory_space=pl.ANY), pl.BlockSpec(memory_space=pl.ANY)],
            out_specs=pl.BlockSpec((tm, tn), lambda n, m, *_: (m, n)),
            scratch_shapes=[pltpu.VMEM((D, tn), F32), pltpu.VMEM((D, tn), F32),
                            pltpu.VMEM((D, tn), BF16), pltpu.VMEM((D, tn), BF16),
                            pltpu.SemaphoreType.DMA((2,))],
        ),
        out_shape=jax.ShapeDtypeStruct((M, F), BF16),
        compiler_params=_params(("arbitrary", "arbitrary"), est),
        name=name,
    )(tile_expert, n_active, first, nxt, a, wg, wu)


def _down_body(te_ref, na_ref, first_ref, next_ref, a_ref, w_ref, o_ref, wf_ref, wb_ref, sem):
    m = pl.program_id(1)
    _stream_weights(te_ref, first_ref, next_ref, (w_ref,), (wf_ref,), (wb_ref,), sem, o_ref.shape[1])

    @pl.when(m < na_ref[0])
    def _():
        o_ref[...] = jnp.dot(a_ref[...], wb_ref[...], preferred_element_type=F32)

    @pl.when(m >= na_ref[0])
    def _():
        o_ref[...] = jnp.zeros(o_ref.shape, o_ref.dtype)


def _grouped_down(a, wd, tile_expert, n_active, *, tm, tn, name):
    M, F = a.shape
    D = wd.shape[2]
    first, nxt = _segments(tile_expert, n_active)
    est = 2 * (tm * F * 2 + tm * tn * 4) + F * tn * (4 + 2) + 2 * tm * tn * 4
    return pl.pallas_call(
        _down_body,
        grid_spec=pltpu.PrefetchScalarGridSpec(
            num_scalar_prefetch=4,
            grid=(D // tn, M // tm),
            in_specs=[pl.BlockSpec((tm, F), lambda n, m, *_: (m, 0)), pl.BlockSpec(memory_space=pl.ANY)],
            out_specs=pl.BlockSpec((tm, tn), lambda n, m, *_: (m, n)),
            scratch_shapes=[pltpu.VMEM((F, tn), F32), pltpu.VMEM((F, tn), BF16), pltpu.SemaphoreType.DMA((1,))],
        ),
        out_shape=jax.ShapeDtypeStruct((M, D), F32),
        compiler_params=_params(("arbitrary", "arbitrary"), est),
        name=name,
    )(tile_expert, n_active, first, nxt, a, wd)


GATHER_UNROLL = 8


def _gather_body(idx_ref, src_ref, o_ref, buf_ref, sem, *, tg):
    step = pl.program_id(0)

    def row_copy(slot, i, row):
        return pltpu.make_async_copy(src_ref.at[pl.ds(row, 1)], buf_ref.at[slot, pl.ds(i, 1)], sem.at[slot])

    def issue(s):
        slot = s % 2

        def start(i, c):
            row_copy(slot, i, idx_ref[s * tg + i]).start()
            return c

        lax.fori_loop(0, tg, start, 0, unroll=GATHER_UNROLL)

    @pl.when(step == 0)
    def _():
        issue(step)

    @pl.when(step + 1 < pl.num_programs(0))
    def _():
        issue(step + 1)

    slot = step % 2

    def wait(i, c):
        row_copy(slot, i, 0).wait()
        return c

    lax.fori_loop(0, tg, wait, 0, unroll=GATHER_UNROLL)
    o_ref[...] = buf_ref[slot].astype(o_ref.dtype)


def _gather_rows(src, idx, out_dtype, tg=256):
    P = idx.shape[0]
    D = src.shape[1]
    tg = _tile(P, tg, 16)
    est = 2 * tg * D * 4 + 2 * tg * D * 2
    return pl.pallas_call(
        functools.partial(_gather_body, tg=tg),
        grid_spec=pltpu.PrefetchScalarGridSpec(
            num_scalar_prefetch=1,
            grid=(P // tg,),
            in_specs=[pl.BlockSpec(memory_space=pl.ANY)],
            out_specs=pl.BlockSpec((tg, D), lambda i, idx: (i, 0)),
            scratch_shapes=[pltpu.VMEM((2, tg, D), src.dtype), pltpu.SemaphoreType.DMA((2,))],
        ),
        out_shape=jax.ShapeDtypeStruct((P, D), out_dtype),
        compiler_params=_params(("arbitrary",), est),
        name="moe_dispatch_gather",
    )(idx, src)


def _combine_body(slot_ref, x_ref, g_ref, y_ref, o_ref, buf_ref, sem, *, tc):
    step = pl.program_id(0)

    def row_copy(slot, i, k, row):
        return pltpu.make_async_copy(y_ref.at[pl.ds(row, 1)], buf_ref.at[slot, k, pl.ds(i, 1)], sem.at[slot])

    def issue(s):
        slot = s % 2

        def start(i, c):
            for k in range(TOP_K):
                row_copy(slot, i, k, slot_ref[(s * tc + i) * TOP_K + k]).start()
            return c

        lax.fori_loop(0, tc, start, 0, unroll=GATHER_UNROLL // TOP_K)

    @pl.when(step == 0)
    def _():
        issue(step)

    @pl.when(step + 1 < pl.num_programs(0))
    def _():
        issue(step + 1)

    slot = step % 2

    def wait(i, c):
        for k in range(TOP_K):
            row_copy(slot, i, k, 0).wait()
        return c

    lax.fori_loop(0, tc, wait, 0, unroll=GATHER_UNROLL // TOP_K)
    g = g_ref[...]
    o_ref[...] = x_ref[...] + (g[:, 0:1] * buf_ref[slot, 0] + g[:, 1:2] * buf_ref[slot, 1])


def _moe_combine(x, y, slots, gates, tc=128):
    M, D = x.shape
    tc = _tile(M, tc, 8)
    est = 4 * tc * D * 4 + 2 * TOP_K * tc * D * 4
    return pl.pallas_call(
        functools.partial(_combine_body, tc=tc),
        grid_spec=pltpu.PrefetchScalarGridSpec(
            num_scalar_prefetch=1,
            grid=(M // tc,),
            in_specs=[pl.BlockSpec((tc, D), lambda i, s: (i, 0)), pl.BlockSpec((tc, TOP_K), lambda i, s: (i, 0)),
                      pl.BlockSpec(memory_space=pl.ANY)],
            out_specs=pl.BlockSpec((tc, D), lambda i, s: (i, 0)),
            scratch_shapes=[pltpu.VMEM((2, TOP_K, tc, D), F32), pltpu.SemaphoreType.DMA((2,))],
        ),
        out_shape=jax.ShapeDtypeStruct((M, D), F32),
        compiler_params=_params(("arbitrary",), est),
        name="moe_combine",
    )(slots.reshape(-1), x, gates, y)


def _diff_attn_body(q_ref, k_ref, vt_ref, lq1_ref, lk1_ref, lq2_ref, lk2_ref, g_ref, o_ref, s_ref, acc_ref,
                    *, tq, hg, lam_init):
    qi = pl.program_id(2)
    w = 2 * HEAD_DIM
    cols = [slice(hh * w + mi * HEAD_DIM, hh * w + (mi + 1) * HEAD_DIM) for hh in range(hg) for mi in range(2)]
    qs = [q_ref[:, c] for c in cols]

    def step(n, carry, mask):
        off = pl.multiple_of(n * tq, tq)
        sts = [_nt(k_ref[pl.ds(off, tq), col_sl], qs[c]) for c, col_sl in enumerate(cols)]
        if mask is not None:
            sts = [jnp.where(mask, st, NEG) for st in sts]
        return _softmax_steps(sts, carry, s_ref, acc_ref, lambda c: vt_ref[c // 2, n])

    acc_ref[...] = jnp.zeros(acc_ref.shape, F32)
    carry = lax.fori_loop(0, qi, lambda n, c: step(n, c, None), _init_carry(len(cols), tq))
    key = lax.broadcasted_iota(I32, (tq, tq), 0)
    qry = lax.broadcasted_iota(I32, (tq, tq), 1)
    step(qi, carry, key <= qry)
    lam = (jnp.exp(jnp.sum(lq1_ref[...] * lk1_ref[...], axis=-1, keepdims=True))
           - jnp.exp(jnp.sum(lq2_ref[...] * lk2_ref[...], axis=-1, keepdims=True)) + lam_init)
    for hh in range(hg):
        y = (_normalized(acc_ref, 2 * hh, w) - lam * _normalized(acc_ref, 2 * hh + 1, w)).T
        ms = jnp.mean(y * y, axis=-1, keepdims=True)
        hs = slice(hh * w, (hh + 1) * w)
        o_ref[:, hs] = (y * lax.rsqrt(ms + EPS) * g_ref[...] * (1.0 - lam_init)).astype(o_ref.dtype)


def _diff_attention(qkv, vt, n_heads, lam_params, subln, lam_init, tq, hg=4):
    B, S, _ = qkv.shape
    w = 2 * HEAD_DIM
    hg = math.gcd(hg, n_heads)
    ng = n_heads // hg
    vec = pl.BlockSpec((1, HEAD_DIM), lambda b, g, i: (0, 0))
    est = 2 * (2 * tq * hg * w * 2 + 2 * S * hg * w * 2) + 2 * hg * (3 * tq * tq * 4 + tq * w * 4)
    return pl.pallas_call(
        functools.partial(_diff_attn_body, tq=tq, hg=hg, lam_init=lam_init),
        grid=(B, ng, S // tq),
        in_specs=[pl.BlockSpec((None, tq, hg * w), lambda b, g, i: (b, i, g)),
                  pl.BlockSpec((None, S, hg * w), lambda b, g, i: (b, 0, ng + g)),
                  pl.BlockSpec((None, hg, S // tq, w + ONES_ROWS, tq), lambda b, g, i: (b, g, 0, 0, 0)),
                  vec, vec, vec, vec,
                  pl.BlockSpec((1, w), lambda b, g, i: (0, 0))],
        out_specs=pl.BlockSpec((None, tq, hg * w), lambda b, g, i: (b, i, g)),
        out_shape=jax.ShapeDtypeStruct((B, S, n_heads * w), BF16),
        scratch_shapes=[pltpu.VMEM((2 * hg, tq, tq), F32), pltpu.VMEM((2 * hg, w + ONES_ROWS, tq), F32)],
        compiler_params=_params(("parallel", "parallel", "arbitrary"), est),
        name="diff_attention",
    )(qkv, qkv, vt, *[p.reshape(1, HEAD_DIM).astype(F32) for p in lam_params],
      subln.reshape(1, w).astype(F32))


def _moba_body(q_ref, k_ref, vt_ref, o_ref, kmean_ref, bias_ref, s_ref, acc_ref, *, nb, hg):
    qi = pl.program_id(2)
    blk = MOBA_BLOCK
    d = HEAD_DIM
    heads = [slice(hh * d, (hh + 1) * d) for hh in range(hg)]

    @pl.when(qi == 0)
    def _():
        for hh, hs in enumerate(heads):
            for n in range(nb):
                kb = k_ref[n * blk:(n + 1) * blk, hs].astype(F32)
                kmean_ref[hh, n:n + 1, :] = jnp.mean(kb, axis=0, keepdims=True)

    blk_id = lax.broadcasted_iota(I32, (nb, blk), 0)
    qs = []
    for hh, hs in enumerate(heads):
        q = q_ref[:, hs]
        km = kmean_ref[hh]
        km_hi = km.astype(BF16)
        km_lo = (km - km_hi.astype(F32)).astype(BF16)
        gate = _nt(km_hi, q) + _nt(km_lo, q)
        gate = jnp.where(blk_id < qi, gate, -jnp.inf)
        sel = jnp.zeros(gate.shape, jnp.bool_)
        for _ in range(MOBA_TOPK):
            mx = jnp.max(gate, axis=0, keepdims=True)
            idx = jnp.min(jnp.where(gate == mx, blk_id, nb), axis=0, keepdims=True)
            sel = sel | ((blk_id == idx) & (mx > -jnp.inf))
            gate = jnp.where(blk_id == idx, -jnp.inf, gate)
        qs.append(q)
        bias_ref[hh] = jnp.where(sel, 0.0, NEG)

    def step(n, carry, mask):
        off = pl.multiple_of(n * blk, blk)
        sts = [_nt(k_ref[pl.ds(off, blk), hs], qs[hh]) for hh, hs in enumerate(heads)]
        if mask is None:
            sts = [st + bias_ref[hh, pl.ds(n, 1), :] for hh, st in enumerate(sts)]
        else:
            sts = [jnp.where(mask, st, NEG) for st in sts]
        return _softmax_steps(sts, carry, s_ref, acc_ref, lambda hh: vt_ref[hh, n])

    acc_ref[...] = jnp.zeros(acc_ref.shape, F32)
    carry = lax.fori_loop(0, qi, lambda n, c: step(n, c, None), _init_carry(hg, blk))
    key = lax.broadcasted_iota(I32, (blk, blk), 0)
    qry = lax.broadcasted_iota(I32, (blk, blk), 1)
    step(qi, carry, key <= qry)
    for hh, hs in enumerate(heads):
        o_ref[:, hs] = _normalized(acc_ref, hh, d).T.astype(o_ref.dtype)


def _moba_attention(qkv, vt, n_heads, q_col, k_col, hg=8):
    B, S, _ = qkv.shape
    assert S % MOBA_BLOCK == 0 and S // MOBA_BLOCK >= MOBA_TOPK
    nb = S // MOBA_BLOCK
    d = HEAD_DIM
    hg = math.gcd(math.gcd(hg, n_heads), math.gcd(q_col, k_col))
    est = 2 * (2 * MOBA_BLOCK * hg * d * 2 + 2 * S * hg * d * 2) + hg * (3 * MOBA_BLOCK * MOBA_BLOCK * 4)
    return pl.pallas_call(
        functools.partial(_moba_body, nb=nb, hg=hg),
        grid=(B, n_heads // hg, nb),
        in_specs=[pl.BlockSpec((None, MOBA_BLOCK, hg * d), lambda b, g, i: (b, i, q_col // hg + g)),
                  pl.BlockSpec((None, S, hg * d), lambda b, g, i: (b, 0, k_col // hg + g)),
                  pl.BlockSpec((None, hg, nb, d + ONES_ROWS, MOBA_BLOCK), lambda b, g, i: (b, g, 0, 0, 0))],
        out_specs=pl.BlockSpec((None, MOBA_BLOCK, hg * d), lambda b, g, i: (b, i, g)),
        out_shape=jax.ShapeDtypeStruct((B, S, n_heads * d), BF16),
        scratch_shapes=[pltpu.VMEM((hg, nb, d), F32), pltpu.VMEM((hg, nb, MOBA_BLOCK), F32),
                        pltpu.VMEM((hg, MOBA_BLOCK, MOBA_BLOCK), F32),
                        pltpu.VMEM((hg, d + ONES_ROWS, MOBA_BLOCK), F32)],
        compiler_params=_params(("parallel", "parallel", "arbitrary"), est),
        name="moba_attention",
    )(qkv, qkv, vt)


def _xattn_body(q_ref, k_ref, v_ref, o_ref, *, n_heads):
    for h in range(n_heads):
        sl = slice(h * HEAD_DIM, (h + 1) * HEAD_DIM)
        s = _nt(q_ref[:, sl], k_ref[:, sl])
        m = jnp.max(s, axis=-1, keepdims=True)
        p = jnp.exp2(s - m)
        l = jnp.sum(p, axis=-1, keepdims=True)
        o = jnp.dot(p.astype(BF16), v_ref[:, sl], preferred_element_type=F32)
        o_ref[:, sl] = (o / l).astype(o_ref.dtype)


def _cross_attention(q, kv, n_heads, tq=512):
    B, S, X = q.shape
    M = kv.shape[1]
    tq = _tile(S, tq, 16)
    est = 2 * (2 * tq * X * 2 + 2 * M * X * 2) + 6 * tq * M * 4
    return pl.pallas_call(
        functools.partial(_xattn_body, n_heads=n_heads),
        grid=(B, S // tq),
        in_specs=[pl.BlockSpec((None, tq, X), lambda b, i: (b, i, 0)),
                  pl.BlockSpec((None, M, X), lambda b, i: (b, 0, 0)),
                  pl.BlockSpec((None, M, X), lambda b, i: (b, 0, 1))],
        out_specs=pl.BlockSpec((None, tq, X), lambda b, i: (b, i, 0)),
        out_shape=jax.ShapeDtypeStruct((B, S, X), BF16),
        compiler_params=_params(("parallel", "parallel"), est),
        name="cross_attention",
    )(q, kv, kv)


def _odd_prep_body(x_ref, gqa_ref, gkv_ref, gkr_ref, gik_ref, cos_ref, sin_ref,
                   qa_ref, ckv_ref, kr_ref, ik_ref, iw_ref, *, c1, c2, iw_scale):
    def norm(x, g):
        ms = jnp.mean(x * x, axis=-1, keepdims=True)
        return x * lax.rsqrt(ms + EPS) * g

    def rope(y):
        lane = lax.broadcasted_iota(I32, y.shape, 1)
        half = C_ROPE // 2
        first = (lane % C_ROPE) < half
        r = jnp.where(first, pltpu.roll(y, LANE - half, axis=1), pltpu.roll(y, half, axis=1))
        return y * cos_ref[...] + r * sin_ref[...]

    qa_ref[...] = norm(x_ref[:, :c1], gqa_ref[...]).astype(qa_ref.dtype)
    ckv_ref[...] = norm(x_ref[:, c1:c2], gkv_ref[...]).astype(ckv_ref.dtype)
    slab_a = x_ref[:, c2:c2 + LANE]
    slab_b = x_ref[:, c2 + LANE:c2 + 2 * LANE]
    lane = lax.broadcasted_iota(I32, slab_a.shape, 1)
    low = lane < C_ROPE
    ms = jnp.sum(jnp.where(low, slab_a * slab_a, 0.0), axis=-1, keepdims=True) / C_ROPE
    kr = rope(slab_a * lax.rsqrt(ms + EPS) * gkr_ref[...])
    kr_ref[0] = kr.astype(kr_ref.dtype)
    kr_ref[1] = pltpu.roll(kr, C_ROPE, axis=1).astype(kr_ref.dtype)
    rot_a = pltpu.roll(slab_a, C_ROPE, axis=1)
    rot_b = pltpu.roll(slab_b, C_ROPE, axis=1)
    ik = jnp.where(low, rot_a, rot_b)
    ik_ref[...] = rope(norm(ik, gik_ref[...])).astype(ik_ref.dtype)
    iw_ref[...] = rot_b * iw_scale


def _odd_prep(x, c1, c2, g_qa, g_kv, g_kr, g_ik, cos_p, sin_p, iw_scale, tm=256):
    B, S, C = x.shape
    assert C == c2 + 2 * LANE
    tm = _tile(S, tm, 16)
    vec = lambda n: pl.BlockSpec((1, n), lambda b, i: (0, 0))
    tab = pl.BlockSpec((tm, LANE), lambda b, i: (i, 0))
    est = 2 * tm * C * 4 * 2
    return pl.pallas_call(
        functools.partial(_odd_prep_body, c1=c1, c2=c2, iw_scale=iw_scale),
        grid=(B, S // tm),
        in_specs=[pl.BlockSpec((None, tm, C), lambda b, i: (b, i, 0)),
                  vec(c1), vec(c2 - c1), vec(LANE), vec(LANE), tab, tab],
        out_specs=[pl.BlockSpec((None, tm, c1), lambda b, i: (b, i, 0)),
                   pl.BlockSpec((None, tm, c2 - c1), lambda b, i: (b, i, 0)),
                   pl.BlockSpec((None, 2, tm, LANE), lambda b, i: (b, 0, i, 0)),
                   pl.BlockSpec((None, tm, LANE), lambda b, i: (b, i, 0)),
                   pl.BlockSpec((None, tm, LANE), lambda b, i: (b, i, 0))],
        out_shape=[jax.ShapeDtypeStruct((B, S, c1), BF16),
                   jax.ShapeDtypeStruct((B, S, c2 - c1), BF16),
                   jax.ShapeDtypeStruct((B, 2, S, LANE), BF16),
                   jax.ShapeDtypeStruct((B, S, LANE), BF16),
                   jax.ShapeDtypeStruct((B, S, LANE), F32)],
        compiler_params=_params(("parallel", "parallel"), est),
        name="odd_prep",
    )(x, g_qa.reshape(1, -1), g_kv.reshape(1, -1), g_kr.reshape(1, -1), g_ik.reshape(1, -1), cos_p, sin_p)


def _q_prep_body(x_ref, gn_ref, gr_ref, cos_ref, sin_ref, qn_ref, qr_ref, *, n_heads):
    nope_w = n_heads * C_NOPE
    width = C_NOPE + C_ROPE
    half = C_ROPE // 2
    lane = lax.broadcasted_iota(I32, (x_ref.shape[0], LANE), 1)
    low = lane < C_ROPE
    first = (lane % C_ROPE) < half
    for p in range(n_heads // 2):
        n0 = x_ref[:, (2 * p) * LANE:(2 * p + 1) * LANE]
        n1 = x_ref[:, (2 * p + 1) * LANE:(2 * p + 2) * LANE]
        r = x_ref[:, nope_w + p * LANE:nope_w + (p + 1) * LANE]
        r2 = r * r
        ss0 = jnp.sum(n0 * n0, axis=-1, keepdims=True) + jnp.sum(jnp.where(low, r2, 0.0), axis=-1, keepdims=True)
        ss1 = jnp.sum(n1 * n1, axis=-1, keepdims=True) + jnp.sum(jnp.where(low, 0.0, r2), axis=-1, keepdims=True)
        inv0 = lax.rsqrt(ss0 / width + EPS)
        inv1 = lax.rsqrt(ss1 / width + EPS)
        qn_ref[:, (2 * p) * LANE:(2 * p + 1) * LANE] = (n0 * inv0 * gn_ref[...]).astype(qn_ref.dtype)
        qn_ref[:, (2 * p + 1) * LANE:(2 * p + 2) * LANE] = (n1 * inv1 * gn_ref[...]).astype(qn_ref.dtype)
        y = r * jnp.where(low, inv0, inv1) * gr_ref[...]
        rot = jnp.where(first, pltpu.roll(y, LANE - half, axis=1), pltpu.roll(y, half, axis=1))
        qr_ref[:, p * LANE:(p + 1) * LANE] = (y * cos_ref[...] + rot * sin_ref[...]).astype(qr_ref.dtype)


def _q_prep(x, n_heads, g_nope, g_rope2, cos_q, sin_q, tm=256):
    B, S, C = x.shape
    tm = _tile(S, tm, 16)
    nope_w, rope_w = n_heads * C_NOPE, n_heads * C_ROPE
    vec = pl.BlockSpec((1, LANE), lambda b, i: (0, 0))
    tab = pl.BlockSpec((tm, LANE), lambda b, i: (i, 0))
    est = 2 * tm * C * 6
    return pl.pallas_call(
        functools.partial(_q_prep_body, n_heads=n_heads),
        grid=(B, S // tm),
        in_specs=[pl.BlockSpec((None, tm, C), lambda b, i: (b, i, 0)), vec, vec, tab, tab],
        out_specs=[pl.BlockSpec((None, tm, nope_w), lambda b, i: (b, i, 0)),
                   pl.BlockSpec((None, tm, rope_w), lambda b, i: (b, i, 0))],
        out_shape=[jax.ShapeDtypeStruct((B, S, nope_w), BF16), jax.ShapeDtypeStruct((B, S, rope_w), BF16)],
        compiler_params=_params(("parallel", "parallel"), est),
        name="dsa_q_prep",
    )(x, g_nope.reshape(1, LANE), g_rope2.reshape(1, LANE), cos_q, sin_q)


def _indexer_body(ik_ref, iq_ref, iwt_ref, o_ref, key_ref, *, tq, n_heads, n_keep, chunk):
    S = ik_ref.shape[0]
    qi = pl.program_id(1)
    n_tiles = qi + 1
    t_idx = qi * tq + lax.broadcasted_iota(I32, (tq, tq), 1)
    s_loc = lax.broadcasted_iota(I32, (tq, tq), 0)

    def score_tile(kt, c):
        off = pl.multiple_of(kt * tq, tq)
        ikt = ik_ref[pl.ds(off, tq), :]
        acc = jnp.zeros((tq, tq), F32)
        for h in range(n_heads):
            r = _nt(ikt, iq_ref[:, h * IDX_DIM:(h + 1) * IDX_DIM])
            acc = acc + jnp.maximum(r, 0.0) * iwt_ref[h:h + 1, :]
        acc = jnp.where(off + s_loc <= t_idx, acc, -jnp.inf)
        bits = lax.bitcast_convert_type(acc, I32)
        key_ref[pl.ds(off, tq), :] = bits ^ ((bits >> 31) & 0x7FFFFFFF)
        return c

    lax.fori_loop(0, n_tiles, score_tile, 0)

    def count_ge(cand):
        def body(c, cnt):
            off = pl.multiple_of(c * chunk, chunk)
            hit = jnp.where(key_ref[pl.ds(off, chunk), :] >= cand, 1, 0).astype(I32)
            return cnt + jnp.sum(hit.reshape(chunk // 8, 8, tq), axis=0)
        cnt = lax.fori_loop(0, n_tiles * (tq // chunk), body, jnp.zeros((8, tq), I32))
        return jnp.sum(cnt, axis=0, keepdims=True)

    thr = jnp.where(count_ge(jnp.zeros((1, tq), I32)) >= n_keep, 0, INT_MIN).astype(I32)

    def bit_step(i, thr):
        cand = thr + lax.shift_left(jnp.int32(1), 30 - i)
        return jnp.where(count_ge(cand) >= n_keep, cand, thr)

    thr = lax.fori_loop(0, 31, bit_step, thr)

    def out_tile(kt, c):
        off = pl.multiple_of(kt * tq, tq)
        ok = (key_ref[pl.ds(off, tq), :] >= thr) & (off + s_loc <= t_idx)
        o_ref[pl.ds(off, tq), :] = jnp.where(ok, 0.0, NEG).astype(o_ref.dtype)
        return c

    def neg_tile(kt, c):
        off = pl.multiple_of(kt * tq, tq)
        o_ref[pl.ds(off, tq), :] = jnp.full((tq, tq), NEG, o_ref.dtype)
        return c

    lax.fori_loop(0, n_tiles, out_tile, 0)
    lax.fori_loop(n_tiles, S // tq, neg_tile, 0)


def _indexer(ik, iq, iwt, n_keep, tq=256):
    B, S, _ = ik.shape
    n_heads = iwt.shape[1]
    tq = _tile(S, tq, LANE)
    est = 2 * (S * IDX_DIM * 2 + tq * n_heads * IDX_DIM * 2 + n_heads * tq * 4 + S * tq * 2) + S * tq * 4 + 8 * tq * tq * 4
    return pl.pallas_call(
        functools.partial(_indexer_body, tq=tq, n_heads=n_heads, n_keep=n_keep, chunk=tq),
        grid=(B, S // tq),
        in_specs=[pl.BlockSpec((None, S, IDX_DIM), lambda b, i: (b, 0, 0)),
                  pl.BlockSpec((None, tq, n_heads * IDX_DIM), lambda b, i: (b, i, 0)),
                  pl.BlockSpec((None, n_heads, tq), lambda b, i: (b, 0, i))],
        out_specs=pl.BlockSpec((None, S, tq), lambda b, i: (b, 0, i)),
        out_shape=jax.ShapeDtypeStruct((B, S, S), BF16),
        scratch_shapes=[pltpu.VMEM((S, tq), I32)],
        compiler_params=_params(("parallel", "arbitrary"), est),
        name="dsa_indexer",
    )(ik, iq, iwt)


def _dsa_attn_body(qn_ref, qr_ref, kn_ref, kr_ref, vt_ref, bias_ref, o_ref, s_ref, acc_ref, *, tq, hg):
    qi = pl.program_id(2)
    heads = [slice(hh * C_NOPE, (hh + 1) * C_NOPE) for hh in range(hg)]
    qs = [jnp.concatenate([qn_ref[:, hs], qr_ref[:, (hh // 2) * LANE:(hh // 2 + 1) * LANE]], axis=1)
          for hh, hs in enumerate(heads)]

    def scores(n):
        off = pl.multiple_of(n * tq, tq)
        bias = bias_ref[pl.ds(off, tq), :].astype(F32)
        kr = [kr_ref[par, pl.ds(off, tq), :] for par in range(2)]
        return tuple(bias + _nt(jnp.concatenate([kn_ref[pl.ds(off, tq), hs], kr[hh % 2]], axis=1), qs[hh])
                     for hh, hs in enumerate(heads))

    def body(n, carry):
        return _softmax_steps(scores(n), carry, s_ref, acc_ref, lambda hh: vt_ref[hh, n])

    acc_ref[...] = jnp.zeros(acc_ref.shape, F32)
    lax.fori_loop(0, qi + 1, body, _init_carry(hg, tq))
    for hh, hs in enumerate(heads):
        o_ref[:, hs] = _normalized(acc_ref, hh, C_NOPE).T.astype(o_ref.dtype)


def _dsa_attention(qn, qr, kv, kr2, vt, bias, n_heads, tq, hg=8):
    B, S, _ = qn.shape
    hg = min(hg, n_heads)
    assert hg % 2 == 0 and n_heads % hg == 0
    ng = n_heads // hg
    nk = S // tq
    est = 2 * (tq * hg * 192 * 2 + 2 * S * hg * LANE * 2 + 2 * S * LANE * 2 + S * tq * 2 + tq * hg * LANE * 2) \
        + hg * 3 * tq * tq * 4
    return pl.pallas_call(
        functools.partial(_dsa_attn_body, tq=tq, hg=hg),
        grid=(B, ng, S // tq),
        in_specs=[pl.BlockSpec((None, tq, hg * C_NOPE), lambda b, g, i: (b, i, g)),
                  pl.BlockSpec((None, tq, hg * C_ROPE), lambda b, g, i: (b, i, g)),
                  pl.BlockSpec((None, S, hg * C_NOPE), lambda b, g, i: (b, 0, g)),
                  pl.BlockSpec((None, 2, S, LANE), lambda b, g, i: (b, 0, 0, 0)),
                  pl.BlockSpec((None, hg, nk, C_NOPE + ONES_ROWS, tq), lambda b, g, i: (b, g, 0, 0, 0)),
                  pl.BlockSpec((None, S, tq), lambda b, g, i: (b, 0, i))],
        out_specs=pl.BlockSpec((None, tq, hg * C_NOPE), lambda b, g, i: (b, i, g)),
        out_shape=jax.ShapeDtypeStruct((B, S, n_heads * C_NOPE), BF16),
        scratch_shapes=[pltpu.VMEM((hg, tq, tq), F32), pltpu.VMEM((hg, C_NOPE + ONES_ROWS, tq), F32)],
        compiler_params=_params(("parallel", "parallel", "arbitrary"), est),
        name="dsa_attention",
    )(qn, qr, kv, kr2, vt, bias)


def _rope_tables(seq, dim):
    inv_freq = ROPE_THETA ** (-jnp.arange(0, dim, 2, dtype=F32) / dim)
    ang = jnp.arange(seq, dtype=F32)[:, None] * inv_freq[None, :]
    return jnp.cos(ang), jnp.sin(ang)


def _cross_block(x, mem, l, norm_xattn, norm_mem, xa_wq, xa_wk, xa_wv, xa_wo, xa_qnorm, xa_knorm, B, S):
    N, D = x.shape
    X = xa_wq.shape[2]
    n_heads = X // HEAD_DIM
    M = mem.shape[1]
    h = _rmsnorm(x, norm_xattn[l], BF16)
    mn = _rmsnorm(mem.reshape(B * M, D), norm_mem[l], BF16)
    ones = jnp.ones((S, LANE), F32)
    q = _matmul(h, xa_wq[l].astype(BF16), out_dtype=BF16, name="xattn_q",
                epi=dict(flags=[EPI_NORM] * (X // _tile(X, 1024, LANE)), modes=(EPI_NORM,), rope_half=0,
                         tn=_tile(X, 1024, LANE), cos=ones, sin=ones,
                         gain=jnp.tile(xa_qnorm[l] * (HEAD_DIM ** -0.5 * LOG2E), n_heads)))
    wkv = jnp.concatenate([xa_wk[l], xa_wv[l]], axis=1).astype(BF16)
    tn = _tile(X, 1024, LANE)
    kv = _matmul(mn, wkv, out_dtype=BF16, name="xattn_kv",
                 epi=dict(flags=[EPI_NORM] * (X // tn) + [EPI_NONE] * (X // tn), modes=(EPI_NORM,), rope_half=0,
                          tn=tn, cos=jnp.ones((M, LANE), F32), sin=jnp.ones((M, LANE), F32),
                          gain=jnp.concatenate([jnp.tile(xa_knorm[l], n_heads), jnp.ones((X,), F32)])))
    o = _cross_attention(q.reshape(B, S, X), kv.reshape(B, M, 2 * X), n_heads)
    return _matmul(o.reshape(N, X), xa_wo[l].astype(BF16), out_dtype=F32, name="xattn_out", resid=x)


def _even_mixer(x, l, i, B, S, norm_mix, ev_w_in, ev_a_qnorm, ev_a_knorm, lam_params, ev_a_subln,
                ev_b_qnorm, ev_b_knorm, ev_w_out):
    N, D = x.shape
    a_heads = D // (4 * HEAD_DIM)
    b_heads = D // (2 * HEAD_DIM)
    aw = a_heads * 2 * HEAD_DIM
    bw = b_heads * HEAD_DIM
    width = 3 * aw + 3 * bw
    scale = HEAD_DIM ** -0.5 * LOG2E
    cos, sin = _rope_tables(S, HEAD_DIM)
    cos2 = jnp.concatenate([cos, cos], axis=-1)
    sin2 = jnp.concatenate([-sin, sin], axis=-1)
    tn = _tile(math.gcd(aw, bw), 1024, LANE)
    seg = [(aw, EPI_NORM_ROPE, ev_a_qnorm[i] * scale), (aw, EPI_NORM_ROPE, ev_a_knorm[i]), (aw, EPI_NONE, None),
           (bw, EPI_NORM_ROPE, ev_b_qnorm[i] * scale), (bw, EPI_NORM_ROPE, ev_b_knorm[i]), (bw, EPI_NONE, None)]
    flags, gains = [], []
    for w, flag, g in seg:
        flags += [flag] * (w // tn)
        gains.append(jnp.ones((w,), F32) if g is None else jnp.tile(g.astype(F32), w // HEAD_DIM))
    h = _rmsnorm(x, norm_mix[l], BF16)
    qkv = _matmul(h, ev_w_in[i].astype(BF16), out_dtype=BF16, name="even_in",
                  epi=dict(flags=flags, modes=(EPI_NORM_ROPE,), rope_half=HEAD_DIM // 2, tn=tn,
                           cos=cos2, sin=sin2, gain=jnp.concatenate(gains)))
    qkv = qkv.reshape(B, S, width)
    lam_init = 0.8 - 0.6 * math.exp(-0.3 * l)
    tq = _tile(S, 256, LANE)
    ya = _diff_attention(qkv, _value_tiles(qkv[:, :, 2 * aw:3 * aw], a_heads, tq), a_heads, lam_params,
                         ev_a_subln[i], lam_init, tq)
    c0 = 3 * aw // HEAD_DIM
    yb = _moba_attention(qkv, _value_tiles(qkv[:, :, 3 * aw + 2 * bw:], b_heads, MOBA_BLOCK), b_heads,
                         c0, c0 + b_heads)
    y = jnp.concatenate([ya, yb], axis=-1).reshape(N, aw + bw)
    return _matmul(y, ev_w_out[i].astype(BF16), out_dtype=F32, name="even_out", resid=x)


def _odd_mixer(x, l, i, B, S, norm_mix, od_w_in, od_qa_norm, od_w_qb, od_q_norm, od_kv_norm, od_kr_norm,
               od_w_uk, od_w_uv, od_w_iqb, od_ik_norm, od_w_out):
    N, D = x.shape
    c1 = od_qa_norm.shape[1]
    kv_rank = od_kv_norm.shape[1]
    c2 = c1 + kv_rank
    n_heads = od_w_uk.shape[2]
    idx_heads = od_w_iqb.shape[2] // IDX_DIM
    assert od_w_in.shape[2] == c2 + C_ROPE + IDX_DIM + idx_heads and idx_heads == C_ROPE
    scale = (C_NOPE + C_ROPE) ** -0.5 * LOG2E
    cos, sin = _rope_tables(S, C_ROPE)
    one, zero = jnp.ones((S, C_ROPE), F32), jnp.zeros((S, C_ROPE), F32)
    cos_p = jnp.concatenate([cos, cos, one], axis=-1)
    sin_p = jnp.concatenate([-sin, sin, zero], axis=-1)
    cos_q = jnp.concatenate([cos, cos, cos, cos], axis=-1)
    sin_q = jnp.concatenate([-sin, sin, -sin, sin], axis=-1)

    h = _rmsnorm(x, norm_mix[l], BF16)
    proj = _matmul(h, od_w_in[i].astype(BF16), out_dtype=F32, name="odd_in")
    g_kr = jnp.concatenate([od_kr_norm[i], jnp.zeros((LANE - C_ROPE,), F32)])
    qa, ckv, kr2, ik, iw = _odd_prep(proj.reshape(B, S, -1), c1, c2, od_qa_norm[i], od_kv_norm[i], g_kr,
                                     od_ik_norm[i], cos_p, sin_p, idx_heads ** -0.5 * IDX_DIM ** -0.5)
    qa = qa.reshape(N, c1)
    wqb = od_w_qb[i].reshape(c1, n_heads, C_NOPE + C_ROPE)
    wqb = jnp.concatenate([wqb[:, :, :C_NOPE].reshape(c1, -1), wqb[:, :, C_NOPE:].reshape(c1, -1)], axis=1)
    qraw = _matmul(qa, wqb.astype(BF16), out_dtype=F32, name="odd_qb")
    qn, qr = _q_prep(qraw.reshape(B, S, -1), n_heads, od_q_norm[i][:C_NOPE] * scale,
                     jnp.tile(od_q_norm[i][C_NOPE:], 2) * scale, cos_q, sin_q)
    tn = _tile(idx_heads * IDX_DIM, 1024, LANE)
    iq = _matmul(qa, od_w_iqb[i].astype(BF16), out_dtype=BF16, name="odd_iqb",
                 epi=dict(flags=[EPI_ROPE] * (idx_heads * IDX_DIM // tn), modes=(EPI_ROPE,), rope_half=IDX_ROPE // 2,
                          tn=tn, cos=cos_p, sin=sin_p, gain=jnp.ones((idx_heads * IDX_DIM,), F32)))
    wkv = jnp.concatenate([od_w_uk[i].reshape(kv_rank, -1), od_w_uv[i].reshape(kv_rank, -1)], axis=1)
    kv = _matmul(ckv.reshape(N, kv_rank), wkv.astype(BF16), out_dtype=BF16, name="odd_kv")

    n_keep = min(IDX_TOPK, S // 4)
    tq = _tile(S, 256, LANE)
    iwt = jnp.swapaxes(iw[:, :, :idx_heads], 1, 2)
    bias = _indexer(ik, iq.reshape(B, S, -1), iwt, n_keep, tq)
    kv = kv.reshape(B, S, -1)
    vt = _value_tiles(kv[:, :, n_heads * C_NOPE:], n_heads, tq)
    y = _dsa_attention(qn, qr, kv, kr2, vt, bias, n_heads, tq)
    return _matmul(y.reshape(N, -1), od_w_out[i].astype(BF16), out_dtype=F32, name="odd_out", resid=x)


def _dense_ffn(x, g, wg, wu, wd):
    N, D = x.shape
    F = wg.shape[1]
    h = _rmsnorm(x, g, BF16)
    tm = _tile(N, 2048, 16)
    tn = _tile(F, 512, LANE)
    nt = N // tm
    hid = _swiglu_up(h, wg[None], wu[None], jnp.zeros((nt,), I32), jnp.full((1,), nt, I32),
                     tm=tm, tn=tn, name="ffn_up")
    return _matmul(hid, wd.astype(BF16), out_dtype=F32, name="ffn_down", resid=x, tm=512, tn=1024,
                   tk=F if F <= 4096 else _tile(F, F // 2, LANE))


def _moe_ffn(x, g, router, wg, wu, wd, tm=512):
    N, D = x.shape
    E, _, F = wg.shape
    h, gate, sel = _norm_router(x, g, router)
    tm = _tile(N, tm, 16)
    cnt = jnp.sum(sel, axis=0)
    tiles_e = (cnt + tm - 1) // tm
    tile_end = jnp.cumsum(tiles_e)
    start = (tile_end - tiles_e) * tm
    rank = jnp.cumsum(sel, axis=0) - sel
    P = N * TOP_K + E * tm
    n_tiles = P // tm
    slot = (start[None, :] + rank).astype(I32)
    lane = jnp.arange(E, dtype=I32)[None, :]
    e2 = jnp.stack([jnp.min(jnp.where(sel > 0, lane, E), axis=1), jnp.max(jnp.where(sel > 0, lane, -1), axis=1)], 1)
    slots2 = jnp.take_along_axis(slot, e2, axis=1)
    gates2 = jnp.take_along_axis(gate, e2, axis=1)
    tok2 = jnp.broadcast_to(jnp.arange(N, dtype=I32)[:, None], (N, TOP_K))
    tok_of_slot = jnp.zeros((P,), I32).at[slots2.reshape(-1)].set(tok2.reshape(-1), unique_indices=True)
    tile_expert = jnp.minimum(jnp.searchsorted(tile_end, jnp.arange(n_tiles, dtype=I32), side="right"),
                              E - 1).astype(I32)
    n_active = tile_end[-1:].astype(I32)

    xs = _gather_rows(h, tok_of_slot, BF16)
    hid = _swiglu_up(xs, wg, wu, tile_expert, n_active, tm=tm, tn=_tile(F, 512, LANE), name="moe_up")
    y = _grouped_down(hid, wd, tile_expert, n_active, tm=tm, tn=_tile(D, 512, LANE), name="moe_down")
    return _moe_combine(x, y, slots2, gates2)


def kernel(x, mem, norm_mix, norm_xattn, norm_mem, norm_ffn, ev_w_in, ev_a_qnorm, ev_a_knorm, ev_lambda_q1, ev_lambda_k1, ev_lambda_q2, ev_lambda_k2, ev_a_subln, ev_b_qnorm, ev_b_knorm, ev_w_out, od_w_in, od_qa_norm, od_w_qb, od_q_norm, od_kv_norm, od_kr_norm, od_w_uk, od_w_uv, od_w_iqb, od_ik_norm, od_w_out, xa_wq, xa_wk, xa_wv, xa_wo, xa_qnorm, xa_knorm, ffn_wg, ffn_wu, ffn_wd, moe_router, moe_wg, moe_wu, moe_wd):
    B, S, D = x.shape
    depth = norm_mix.shape[0]
    x = x.reshape(B * S, D)
    for l in range(depth):
        i = l // 2
        if l % 2 == 0:
            x = _even_mixer(x, l, i, B, S, norm_mix, ev_w_in, ev_a_qnorm, ev_a_knorm,
                            (ev_lambda_q1[i], ev_lambda_k1[i], ev_lambda_q2[i], ev_lambda_k2[i]),
                            ev_a_subln, ev_b_qnorm, ev_b_knorm, ev_w_out)
        else:
            x = _odd_mixer(x, l, i, B, S, norm_mix, od_w_in, od_qa_norm, od_w_qb, od_q_norm, od_kv_norm,
                           od_kr_norm, od_w_uk, od_w_uv, od_w_iqb, od_ik_norm, od_w_out)
        x = _cross_block(x, mem, l, norm_xattn, norm_mem, xa_wq, xa_wk, xa_wv, xa_wo, xa_qnorm, xa_knorm, B, S)
        if l % 2 == 0:
            x = _dense_ffn(x, norm_ffn[l], ffn_wg[i], ffn_wu[i], ffn_wd[i])
        else:
            x = _moe_ffn(x, norm_ffn[l], moe_router[i], moe_wg[i], moe_wu[i], moe_wd[i])
    return x.reshape(B, S, D)
```

```python
import functools
import math

import jax
import jax.numpy as jnp
import numpy as np
from jax import lax
from jax.experimental import pallas as pl
from jax.experimental.pallas import tpu as pltpu

F32 = jnp.float32
BF16 = jnp.bfloat16
I32 = jnp.int32

LANE = 128
MXU_WIDTH = 256
V7X_VMEM_BYTES = 64 * 1024 * 1024
VMEM_CAP = V7X_VMEM_BYTES - 4 * 1024 * 1024

HEAD_DIM = 128
ROPE_THETA = 10000.0
EPS = 1e-6
MOBA_BLOCK = 256
MOBA_TOPK = 3
C_NOPE = 128
C_ROPE = 64
IDX_DIM = 128
IDX_ROPE = 64
IDX_TOPK = 256
TOP_K = 2
NEG = -1e30
LOG2E = math.log2(math.e)
INT_MIN = -(2 ** 31)


def _tile(dim, pref, align):
    t = min(pref, dim)
    t -= t % align
    while t >= align:
        if dim % t == 0:
            return t
        t -= align
    return dim


def _params(sem, est_bytes):
    limit = int(min(max(est_bytes * 1.3 + (4 << 20), 32 << 20), VMEM_CAP))
    return pltpu.CompilerParams(dimension_semantics=sem, vmem_limit_bytes=limit)


def _nt(a, b):
    return lax.dot_general(a, b, (((1,), (1,)), ((), ())), preferred_element_type=F32)


def _softmax_steps(sts, carry, s_ref, acc_ref, value_tile):
    for c, st in enumerate(sts):
        s_ref[c] = st
    new = []
    for c in range(len(sts)):
        m_new = jnp.maximum(carry[c], jnp.max(s_ref[c], axis=0, keepdims=True))
        alpha = jnp.exp2(carry[c] - m_new)
        p = jnp.exp2(s_ref[c] - m_new)
        new.append(m_new)
        acc_ref[c] = alpha * acc_ref[c] + jnp.dot(value_tile(c), p.astype(BF16), preferred_element_type=F32)
    return tuple(new)


def _normalized(acc_ref, c, dv):
    acc = acc_ref[c]
    return acc[:dv] / acc[dv:dv + 1]


def _init_carry(n_chains, tq):
    return tuple(jnp.full((1, tq), NEG, F32) for _ in range(n_chains))


ONES_ROWS = 16


def _value_tiles(v, n_heads, tk):
    B, S, C = v.shape
    vt = v.reshape(B, S // tk, tk, n_heads, C // n_heads).transpose(0, 3, 1, 4, 2)
    extra = jnp.zeros(vt.shape[:3] + (ONES_ROWS, tk), v.dtype).at[..., 0, :].set(1)
    return jnp.concatenate([vt, extra], axis=3)


def _rmsnorm_body(x_ref, g_ref, o_ref):
    x = x_ref[...].astype(F32)
    ms = jnp.mean(x * x, axis=-1, keepdims=True)
    o_ref[...] = (x * lax.rsqrt(ms + EPS) * g_ref[...]).astype(o_ref.dtype)


def _rmsnorm(x, g, out_dtype, tm=256):
    M, D = x.shape
    tm = _tile(M, tm, 16)
    est = 2 * tm * D * (4 + 4)
    return pl.pallas_call(
        _rmsnorm_body,
        grid=(M // tm,),
        in_specs=[pl.BlockSpec((tm, D), lambda i: (i, 0)), pl.BlockSpec((1, D), lambda i: (0, 0))],
        out_specs=pl.BlockSpec((tm, D), lambda i: (i, 0)),
        out_shape=jax.ShapeDtypeStruct((M, D), out_dtype),
        compiler_params=_params(("parallel",), est),
        name="rmsnorm",
    )(x, g.reshape(1, D).astype(F32))


def _norm_router_body(x_ref, g_ref, rt_ref, h_ref, gate_ref, sel_ref, *, n_exp):
    x = x_ref[...]
    ms = jnp.mean(x * x, axis=-1, keepdims=True)
    h = x * lax.rsqrt(ms + EPS) * g_ref[...]
    h_ref[...] = h
    lane = lax.broadcasted_iota(I32, gate_ref.shape, 1)
    logits = jnp.full(gate_ref.shape, -jnp.inf, F32)
    for e in range(n_exp):
        col = jnp.sum(h * rt_ref[e:e + 1, :], axis=-1, keepdims=True)
        logits = jnp.where(lane == e, col, logits)
    m1 = jnp.max(logits, axis=-1, keepdims=True)
    i1 = jnp.min(jnp.where(logits == m1, lane, LANE), axis=-1, keepdims=True)
    rest = jnp.where(lane == i1, -jnp.inf, logits)
    m2 = jnp.max(rest, axis=-1, keepdims=True)
    i2 = jnp.min(jnp.where(rest == m2, lane, LANE), axis=-1, keepdims=True)
    e2 = jnp.exp(m2 - m1)
    den = 1.0 + e2
    gate_ref[...] = jnp.where(lane == i1, 1.0 / den, 0.0) + jnp.where(lane == i2, e2 / den, 0.0)
    sel_ref[...] = jnp.where((lane == i1) | (lane == i2), 1, 0).astype(I32)


def _norm_router(x, g, router, tm=256):
    M, D = x.shape
    n_exp = router.shape[1]
    tm = _tile(M, tm, 8)
    est = 2 * tm * D * 8 + 2 * 8 * D * 4
    h, gate, sel = pl.pallas_call(
        functools.partial(_norm_router_body, n_exp=n_exp),
        grid=(M // tm,),
        in_specs=[pl.BlockSpec((tm, D), lambda i: (i, 0)), pl.BlockSpec((1, D), lambda i: (0, 0)),
                  pl.BlockSpec((n_exp, D), lambda i: (0, 0))],
        out_specs=[pl.BlockSpec((tm, D), lambda i: (i, 0)), pl.BlockSpec((tm, LANE), lambda i: (i, 0)),
                   pl.BlockSpec((tm, LANE), lambda i: (i, 0))],
        out_shape=[jax.ShapeDtypeStruct((M, D), F32), jax.ShapeDtypeStruct((M, LANE), F32),
                   jax.ShapeDtypeStruct((M, LANE), I32)],
        compiler_params=_params(("parallel",), est),
        name="norm_router",
    )(x, g.reshape(1, D).astype(F32), router.T.astype(F32))
    return h, gate[:, :n_exp], sel[:, :n_exp]


EPI_NONE, EPI_NORM, EPI_NORM_ROPE, EPI_ROPE = 0, 1, 2, 3


def _head_epilogue(x, g, cos, sin, mode, rope_half):
    if mode in (EPI_NORM, EPI_NORM_ROPE):
        ss = jnp.dot((x * x).astype(BF16), jnp.ones((LANE, LANE), BF16), preferred_element_type=F32)
        x = x * lax.rsqrt(ss * (1.0 / LANE) + EPS) * g
    if mode in (EPI_NORM_ROPE, EPI_ROPE):
        if rope_half == LANE // 2:
            r = pltpu.roll(x, LANE // 2, axis=1)
        else:
            lane = lax.broadcasted_iota(I32, x.shape, 1)
            first = (lane % (2 * rope_half)) < rope_half
            r = jnp.where(first, pltpu.roll(x, LANE - rope_half, axis=1), pltpu.roll(x, rope_half, axis=1))
        x = x * cos + r * sin
    return x


def _mm_body(flags_ref, a_ref, w_ref, *rest, nk, has_resid, modes, rope_half, tn, cw):
    rest = list(rest)
    resid_ref = rest.pop(0) if has_resid else None
    gain_ref = cos_ref = sin_ref = None
    if modes:
        gain_ref, cos_ref, sin_ref = rest.pop(0), rest.pop(0), rest.pop(0)
    o_ref = rest.pop(0)
    acc_ref = rest.pop(0) if nk > 1 else None
    j = pl.program_id(1)
    k = pl.program_id(2)

    def emit(cols, acc, mode):
        if mode != EPI_NONE:
            for c in range(cols.start, cols.stop, LANE):
                sl = slice(c, c + LANE)
                y = _head_epilogue(acc[:, sl.start - cols.start:sl.stop - cols.start], gain_ref[:, sl],
                                   cos_ref[...], sin_ref[...], mode, rope_half)
                o_ref[:, sl] = y.astype(o_ref.dtype)
        elif has_resid:
            o_ref[:, cols] = (resid_ref[:, cols] + acc).astype(o_ref.dtype)
        else:
            o_ref[:, cols] = acc.astype(o_ref.dtype)

    if nk == 1:
        def run(mode):
            pending = None
            for c in range(0, tn, cw):
                cols = slice(c, c + cw)
                acc = jnp.dot(a_ref[...], w_ref[:, cols], preferred_element_type=F32)
                if pending is not None:
                    emit(*pending, mode)
                pending = (cols, acc)
            emit(*pending, mode)

        if modes:
            flag = flags_ref[j]
            for mode in (EPI_NONE,) + modes:
                pl.when(flag == mode)(functools.partial(run, mode))
        else:
            run(EPI_NONE)
    else:
        part = jnp.dot(a_ref[...], w_ref[...], preferred_element_type=F32)

        @pl.when(k == 0)
        def _():
            acc_ref[...] = part

        @pl.when((k > 0) & (k < nk - 1))
        def _():
            acc_ref[...] += part

        @pl.when(k == nk - 1)
        def _():
            emit(slice(0, tn), acc_ref[...] + part, EPI_NONE)


def _matmul(a, w, *, out_dtype, name, resid=None, epi=None, tm=1024, tn=1024, tk=4096):
    M, K = a.shape
    N = w.shape[1]
    tm = _tile(epi["cos"].shape[0] if epi else M, tm, 16)
    tn = epi["tn"] if epi else _tile(N, tn, LANE)
    tk = _tile(K, tk, LANE)
    nk = K // tk
    assert a.dtype == BF16 and w.dtype == BF16 and (nk == 1 or not epi)
    a_bytes = a.dtype.itemsize
    o_bytes = jnp.dtype(out_dtype).itemsize
    modes = tuple(epi["modes"]) if epi else ()
    in_specs = [pl.BlockSpec((tm, tk), lambda i, j, k, f: (i, k)),
                pl.BlockSpec((tk, tn), lambda i, j, k, f: (k, j))]
    args = [a, w]
    est = 2 * (tm * tk * a_bytes + tk * tn * 2 + tm * tn * o_bytes) + 3 * tm * tn * 4
    if resid is not None:
        in_specs.append(pl.BlockSpec((tm, tn), lambda i, j, k, f: (i, j)))
        args.append(resid)
        est += 2 * tm * tn * 4
    if epi:
        ns = epi["cos"].shape[0] // tm
        in_specs += [pl.BlockSpec((1, tn), lambda i, j, k, f: (0, j)),
                     pl.BlockSpec((tm, LANE), lambda i, j, k, f: (i % ns, 0)),
                     pl.BlockSpec((tm, LANE), lambda i, j, k, f: (i % ns, 0))]
        args += [epi["gain"].reshape(1, N).astype(F32), epi["cos"], epi["sin"]]
        flags = jnp.asarray(epi["flags"], I32)
        est += 4 * tm * LANE * 4
    else:
        flags = jnp.zeros((N // tn,), I32)
    body = functools.partial(_mm_body, nk=nk, has_resid=resid is not None, modes=modes,
                             rope_half=epi["rope_half"] if epi else 0, tn=tn, cw=math.gcd(tn, MXU_WIDTH))
    return pl.pallas_call(
        body,
        grid_spec=pltpu.PrefetchScalarGridSpec(
            num_scalar_prefetch=1,
            grid=(M // tm, N // tn, nk),
            in_specs=in_specs,
            out_specs=pl.BlockSpec((tm, tn), lambda i, j, k, f: (i, j)),
            scratch_shapes=[pltpu.VMEM((tm, tn), F32)] if nk > 1 else [],
        ),
        out_shape=jax.ShapeDtypeStruct((M, N), out_dtype),
        compiler_params=_params(("parallel", "parallel", "arbitrary"), est),
        name=name,
    )(flags, *args)


def _segments(tile_expert, n_active):
    T = tile_expert.shape[0]
    idx = jnp.arange(T, dtype=I32)
    prev = jnp.concatenate([tile_expert[:1] - 1, tile_expert[:-1]])
    first = (idx < n_active[0]) & (tile_expert != prev)
    first_idx = jnp.where(first, idx, T)
    after = jnp.concatenate([lax.cummin(first_idx[::-1])[::-1][1:], jnp.full((1,), T, I32)])
    nxt = jnp.where(after < T, tile_expert[jnp.minimum(after, T - 1)], -1)
    return first.astype(I32), nxt.astype(I32)


def _stream_weights(te_ref, first_ref, next_ref, w_refs, stage_refs, cast_refs, sem, tn):
    n = pl.program_id(0)
    m = pl.program_id(1)

    def copies(e, nn):
        cols = pl.ds(pl.multiple_of(nn * tn, tn), tn)
        return [pltpu.make_async_copy(w.at[e, :, cols], st, sem.at[i])
                for i, (w, st) in enumerate(zip(w_refs, stage_refs))]

    @pl.when(first_ref[m] == 1)
    def _():
        @pl.when((n == 0) & (m == 0))
        def _():
            for c in copies(te_ref[0], 0):
                c.start()

        for c in copies(te_ref[m], n):
            c.wait()
        for st, cb in zip(stage_refs, cast_refs):
            cb[...] = st[...].astype(BF16)
        last = next_ref[m] < 0
        e2 = jnp.where(last, te_ref[0], next_ref[m])
        n2 = jnp.where(last, n + 1, n)

        @pl.when(n2 < pl.num_programs(0))
        def _():
            for c in copies(e2, n2):
                c.start()


ROW_SPLIT = 2


def _row_parts(nv_ref, o_ref, compute):
    m = pl.program_id(1)
    hm = o_ref.shape[0] // ROW_SPLIT
    for h in range(ROW_SPLIT):
        rows = slice(h * hm, (h + 1) * hm)
        pl.when(nv_ref[m] > h * hm)(functools.partial(compute, rows))

        @pl.when(nv_ref[m] <= h * hm)
        def _(rows=rows):
            o_ref[rows, :] = jnp.zeros((hm, o_ref.shape[1]), o_ref.dtype)


def _swiglu_body(te_ref, na_ref, first_ref, next_ref, nv_ref, a_ref, wg_ref, wu_ref, o_ref,
                 wgf_ref, wuf_ref, wgb_ref, wub_ref, sem):
    tn = o_ref.shape[1]
    _stream_weights(te_ref, first_ref, next_ref, (wg_ref, wu_ref), (wgf_ref, wuf_ref), (wgb_ref, wub_ref), sem, tn)
    cw = math.gcd(tn, MXU_WIDTH)

    def compute(rows):
        def emit(cols, g, u):
            o_ref[rows, cols] = (g * jax.nn.sigmoid(g) * u).astype(o_ref.dtype)

        pending = None
        for c in range(0, tn, cw):
            cols = slice(c, c + cw)
            g = jnp.dot(a_ref[rows, :], wgb_ref[:, cols], preferred_element_type=F32)
            u = jnp.dot(a_ref[rows, :], wub_ref[:, cols], preferred_element_type=F32)
            if pending is not None:
                emit(*pending)
            pending = (cols, g, u)
        emit(*pending)

    _row_parts(nv_ref, o_ref, compute)


def _swiglu_up(a, wg, wu, tile_expert, n_active, valid_rows, *, tm, tn, name):
    M, D = a.shape
    F = wg.shape[2]
    first, nxt = _segments(tile_expert, n_active)
    est = 2 * (tm * D * 2 + tm * tn * 2) + 2 * D * tn * (4 + 2) + 4 * tm * tn * 4
    return pl.pallas_call(
        _swiglu_body,
        grid_spec=pltpu.PrefetchScalarGridSpec(
            num_scalar_prefetch=5,
            grid=(F // tn, M // tm),
            in_specs=[pl.BlockSpec((tm, D), lambda n, m, *_: (m, 0)),
                      pl.BlockSpec(memory_space=pl.ANY), pl.BlockSpec(memory_space=pl.ANY)],
            out_specs=pl.BlockSpec((tm, tn), lambda n, m, *_: (m, n)),
            scratch_shapes=[pltpu.VMEM((D, tn), F32), pltpu.VMEM((D, tn), F32),
                            pltpu.VMEM((D, tn), BF16), pltpu.VMEM((D, tn), BF16),
                            pltpu.SemaphoreType.DMA((2,))],
        ),
        out_shape=jax.ShapeDtypeStruct((M, F), BF16),
        compiler_params=_params(("arbitrary", "arbitrary"), est),
        name=name,
    )(tile_expert, n_active, first, nxt, valid_rows, a, wg, wu)


def _down_body(te_ref, na_ref, first_ref, next_ref, nv_ref, a_ref, w_ref, o_ref, wf_ref, wb_ref, sem):
    _stream_weights(te_ref, first_ref, next_ref, (w_ref,), (wf_ref,), (wb_ref,), sem, o_ref.shape[1])

    def compute(rows):
        o_ref[rows, :] = jnp.dot(a_ref[rows, :], wb_ref[...], preferred_element_type=F32)

    _row_parts(nv_ref, o_ref, compute)


def _grouped_down(a, wd, tile_expert, n_active, valid_rows, *, tm, tn, name):
    M, F = a.shape
    D = wd.shape[2]
    first, nxt = _segments(tile_expert, n_active)
    est = 2 * (tm * F * 2 + tm * tn * 4) + F * tn * (4 + 2) + 2 * tm * tn * 4
    return pl.pallas_call(
        _down_body,
        grid_spec=pltpu.PrefetchScalarGridSpec(
            num_scalar_prefetch=5,
            grid=(D // tn, M // tm),
            in_specs=[pl.BlockSpec((tm, F), lambda n, m, *_: (m, 0)), pl.BlockSpec(memory_space=pl.ANY)],
            out_specs=pl.BlockSpec((tm, tn), lambda n, m, *_: (m, n)),
            scratch_shapes=[pltpu.VMEM((F, tn), F32), pltpu.VMEM((F, tn), BF16), pltpu.SemaphoreType.DMA((1,))],
        ),
        out_shape=jax.ShapeDtypeStruct((M, D), F32),
        compiler_params=_params(("arbitrary", "arbitrary"), est),
        name=name,
    )(tile_expert, n_active, first, nxt, valid_rows, a, wd)


GATHER_UNROLL = 8


def _gather_body(idx_ref, src_ref, o_ref, buf_ref, sem, *, tg):
    step = pl.program_id(0)

    def row_copy(slot, i, row):
        return pltpu.make_async_copy(src_ref.at[pl.ds(row, 1)], buf_ref.at[slot, pl.ds(i, 1)], sem.at[slot])

    def issue(s):
        slot = s % 2

        def start(i, c):
            row_copy(slot, i, idx_ref[s * tg + i]).start()
            return c

        lax.fori_loop(0, tg, start, 0, unroll=GATHER_UNROLL)

    @pl.when(step == 0)
    def _():
        issue(step)

    @pl.when(step + 1 < pl.num_programs(0))
    def _():
        issue(step + 1)

    slot = step % 2

    def wait(i, c):
        row_copy(slot, i, 0).wait()
        return c

    lax.fori_loop(0, tg, wait, 0, unroll=GATHER_UNROLL)
    o_ref[...] = buf_ref[slot].astype(o_ref.dtype)


def _gather_rows(src, idx, out_dtype, tg=256):
    P = idx.shape[0]
    D = src.shape[1]
    tg = _tile(P, tg, 16)
    est = 2 * tg * D * 4 + 2 * tg * D * 2
    return pl.pallas_call(
        functools.partial(_gather_body, tg=tg),
        grid_spec=pltpu.PrefetchScalarGridSpec(
            num_scalar_prefetch=1,
            grid=(P // tg,),
            in_specs=[pl.BlockSpec(memory_space=pl.ANY)],
            out_specs=pl.BlockSpec((tg, D), lambda i, idx: (i, 0)),
            scratch_shapes=[pltpu.VMEM((2, tg, D), src.dtype), pltpu.SemaphoreType.DMA((2,))],
        ),
        out_shape=jax.ShapeDtypeStruct((P, D), out_dtype),
        compiler_params=_params(("arbitrary",), est),
        name="moe_dispatch_gather",
    )(idx, src)


def _combine_body(slot_ref, x_ref, g_ref, y_ref, o_ref, buf_ref, sem, *, tc):
    step = pl.program_id(0)

    def row_copy(slot, i, k, row):
        return pltpu.make_async_copy(y_ref.at[pl.ds(row, 1)], buf_ref.at[slot, k, pl.ds(i, 1)], sem.at[slot])

    def issue(s):
        slot = s % 2

        def start(i, c):
            for k in range(TOP_K):
                row_copy(slot, i, k, slot_ref[(s * tc + i) * TOP_K + k]).start()
            return c

        lax.fori_loop(0, tc, start, 0, unroll=GATHER_UNROLL // TOP_K)

    @pl.when(step == 0)
    def _():
        issue(step)

    @pl.when(step + 1 < pl.num_programs(0))
    def _():
        issue(step + 1)

    slot = step % 2

    def wait(i, c):
        for k in range(TOP_K):
            row_copy(slot, i, k, 0).wait()
        return c

    lax.fori_loop(0, tc, wait, 0, unroll=GATHER_UNROLL // TOP_K)
    g = g_ref[...]
    o_ref[...] = x_ref[...] + (g[:, 0:1] * buf_ref[slot, 0] + g[:, 1:2] * buf_ref[slot, 1])


def _moe_combine(x, y, slots, gates, tc=128):
    M, D = x.shape
    tc = _tile(M, tc, 8)
    est = 4 * tc * D * 4 + 2 * TOP_K * tc * D * 4
    return pl.pallas_call(
        functools.partial(_combine_body, tc=tc),
        grid_spec=pltpu.PrefetchScalarGridSpec(
            num_scalar_prefetch=1,
            grid=(M // tc,),
            in_specs=[pl.BlockSpec((tc, D), lambda i, s: (i, 0)), pl.BlockSpec((tc, TOP_K), lambda i, s: (i, 0)),
                      pl.BlockSpec(memory_space=pl.ANY)],
            out_specs=pl.BlockSpec((tc, D), lambda i, s: (i, 0)),
            scratch_shapes=[pltpu.VMEM((2, TOP_K, tc, D), F32), pltpu.SemaphoreType.DMA((2,))],
        ),
        out_shape=jax.ShapeDtypeStruct((M, D), F32),
        compiler_params=_params(("arbitrary",), est),
        name="moe_combine",
    )(slots.reshape(-1), x, gates, y)


def _diff_attn_body(q_ref, k_ref, vt_ref, lq1_ref, lk1_ref, lq2_ref, lk2_ref, g_ref, o_ref, s_ref, acc_ref,
                    *, tq, hg, lam_init):
    qi = pl.program_id(2)
    w = 2 * HEAD_DIM
    cols = [slice(hh * w + mi * HEAD_DIM, hh * w + (mi + 1) * HEAD_DIM) for hh in range(hg) for mi in range(2)]
    qs = [q_ref[:, c] for c in cols]

    def step(n, carry, mask):
        off = pl.multiple_of(n * tq, tq)
        sts = [_nt(k_ref[pl.ds(off, tq), col_sl], qs[c]) for c, col_sl in enumerate(cols)]
        if mask is not None:
            sts = [jnp.where(mask, st, NEG) for st in sts]
        return _softmax_steps(sts, carry, s_ref, acc_ref, lambda c: vt_ref[c // 2, n])

    acc_ref[...] = jnp.zeros(acc_ref.shape, F32)
    carry = lax.fori_loop(0, qi, lambda n, c: step(n, c, None), _init_carry(len(cols), tq))
    key = lax.broadcasted_iota(I32, (tq, tq), 0)
    qry = lax.broadcasted_iota(I32, (tq, tq), 1)
    step(qi, carry, key <= qry)
    lam = (jnp.exp(jnp.sum(lq1_ref[...] * lk1_ref[...], axis=-1, keepdims=True))
           - jnp.exp(jnp.sum(lq2_ref[...] * lk2_ref[...], axis=-1, keepdims=True)) + lam_init)
    for hh in range(hg):
        y = (_normalized(acc_ref, 2 * hh, w) - lam * _normalized(acc_ref, 2 * hh + 1, w)).T
        ms = jnp.mean(y * y, axis=-1, keepdims=True)
        hs = slice(hh * w, (hh + 1) * w)
        o_ref[:, hs] = (y * lax.rsqrt(ms + EPS) * g_ref[...] * (1.0 - lam_init)).astype(o_ref.dtype)


def _diff_attention(qkv, vt, n_heads, lam_params, subln, lam_init, tq, hg=4):
    B, S, _ = qkv.shape
    w = 2 * HEAD_DIM
    hg = math.gcd(hg, n_heads)
    ng = n_heads // hg
    vec = pl.BlockSpec((1, HEAD_DIM), lambda b, g, i: (0, 0))
    est = 2 * (2 * tq * hg * w * 2 + 2 * S * hg * w * 2) + 2 * hg * (3 * tq * tq * 4 + tq * w * 4)
    return pl.pallas_call(
        functools.partial(_diff_attn_body, tq=tq, hg=hg, lam_init=lam_init),
        grid=(B, ng, S // tq),
        in_specs=[pl.BlockSpec((None, tq, hg * w), lambda b, g, i: (b, i, g)),
                  pl.BlockSpec((None, S, hg * w), lambda b, g, i: (b, 0, ng + g)),
                  pl.BlockSpec((None, hg, S // tq, w + ONES_ROWS, tq), lambda b, g, i: (b, g, 0, 0, 0)),
                  vec, vec, vec, vec,
                  pl.BlockSpec((1, w), lambda b, g, i: (0, 0))],
        out_specs=pl.BlockSpec((None, tq, hg * w), lambda b, g, i: (b, i, g)),
        out_shape=jax.ShapeDtypeStruct((B, S, n_heads * w), BF16),
        scratch_shapes=[pltpu.VMEM((2 * hg, tq, tq), F32), pltpu.VMEM((2 * hg, w + ONES_ROWS, tq), F32)],
        compiler_params=_params(("parallel", "parallel", "arbitrary"), est),
        name="diff_attention",
    )(qkv, qkv, vt, *[p.reshape(1, HEAD_DIM).astype(F32) for p in lam_params],
      subln.reshape(1, w).astype(F32))


def _moba_body(q_ref, k_ref, vt_ref, o_ref, kmean_ref, bias_ref, s_ref, acc_ref, *, nb, hg):
    qi = pl.program_id(2)
    blk = MOBA_BLOCK
    d = HEAD_DIM
    heads = [slice(hh * d, (hh + 1) * d) for hh in range(hg)]

    @pl.when(qi == 0)
    def _():
        for hh, hs in enumerate(heads):
            for n in range(nb):
                kb = k_ref[n * blk:(n + 1) * blk, hs].astype(F32)
                kmean_ref[hh, n:n + 1, :] = jnp.mean(kb, axis=0, keepdims=True)

    blk_id = lax.broadcasted_iota(I32, (nb, blk), 0)
    qs = []
    for hh, hs in enumerate(heads):
        q = q_ref[:, hs]
        km = kmean_ref[hh]
        km_hi = km.astype(BF16)
        km_lo = (km - km_hi.astype(F32)).astype(BF16)
        gate = _nt(km_hi, q) + _nt(km_lo, q)
        gate = jnp.where(blk_id < qi, gate, -jnp.inf)
        sel = jnp.zeros(gate.shape, jnp.bool_)
        for _ in range(MOBA_TOPK):
            mx = jnp.max(gate, axis=0, keepdims=True)
            idx = jnp.min(jnp.where(gate == mx, blk_id, nb), axis=0, keepdims=True)
            sel = sel | ((blk_id == idx) & (mx > -jnp.inf))
            gate = jnp.where(blk_id == idx, -jnp.inf, gate)
        qs.append(q)
        bias_ref[hh] = jnp.where(sel, 0.0, NEG)

    def step(n, carry, mask):
        off = pl.multiple_of(n * blk, blk)
        sts = [_nt(k_ref[pl.ds(off, blk), hs], qs[hh]) for hh, hs in enumerate(heads)]
        if mask is None:
            sts = [st + bias_ref[hh, pl.ds(n, 1), :] for hh, st in enumerate(sts)]
        else:
            sts = [jnp.where(mask, st, NEG) for st in sts]
        return _softmax_steps(sts, carry, s_ref, acc_ref, lambda hh: vt_ref[hh, n])

    acc_ref[...] = jnp.zeros(acc_ref.shape, F32)
    carry = lax.fori_loop(0, qi, lambda n, c: step(n, c, None), _init_carry(hg, blk))
    key = lax.broadcasted_iota(I32, (blk, blk), 0)
    qry = lax.broadcasted_iota(I32, (blk, blk), 1)
    step(qi, carry, key <= qry)
    for hh, hs in enumerate(heads):
        o_ref[:, hs] = _normalized(acc_ref, hh, d).T.astype(o_ref.dtype)


def _moba_attention(qkv, vt, n_heads, q_col, k_col, hg=8):
    B, S, _ = qkv.shape
    assert S % MOBA_BLOCK == 0 and S // MOBA_BLOCK >= MOBA_TOPK
    nb = S // MOBA_BLOCK
    d = HEAD_DIM
    hg = math.gcd(math.gcd(hg, n_heads), math.gcd(q_col, k_col))
    est = 2 * (2 * MOBA_BLOCK * hg * d * 2 + 2 * S * hg * d * 2) + hg * (3 * MOBA_BLOCK * MOBA_BLOCK * 4)
    return pl.pallas_call(
        functools.partial(_moba_body, nb=nb, hg=hg),
        grid=(B, n_heads // hg, nb),
        in_specs=[pl.BlockSpec((None, MOBA_BLOCK, hg * d), lambda b, g, i: (b, i, q_col // hg + g)),
                  pl.BlockSpec((None, S, hg * d), lambda b, g, i: (b, 0, k_col // hg + g)),
                  pl.BlockSpec((None, hg, nb, d + ONES_ROWS, MOBA_BLOCK), lambda b, g, i: (b, g, 0, 0, 0))],
        out_specs=pl.BlockSpec((None, MOBA_BLOCK, hg * d), lambda b, g, i: (b, i, g)),
        out_shape=jax.ShapeDtypeStruct((B, S, n_heads * d), BF16),
        scratch_shapes=[pltpu.VMEM((hg, nb, d), F32), pltpu.VMEM((hg, nb, MOBA_BLOCK), F32),
                        pltpu.VMEM((hg, MOBA_BLOCK, MOBA_BLOCK), F32),
                        pltpu.VMEM((hg, d + ONES_ROWS, MOBA_BLOCK), F32)],
        compiler_params=_params(("parallel", "parallel", "arbitrary"), est),
        name="moba_attention",
    )(qkv, qkv, vt)


def _xattn_body(q_ref, k_ref, v_ref, o_ref, *, n_heads):
    for h in range(n_heads):
        sl = slice(h * HEAD_DIM, (h + 1) * HEAD_DIM)
        s = _nt(q_ref[:, sl], k_ref[:, sl])
        m = jnp.max(s, axis=-1, keepdims=True)
        p = jnp.exp2(s - m)
        l = jnp.sum(p, axis=-1, keepdims=True)
        o = jnp.dot(p.astype(BF16), v_ref[:, sl], preferred_element_type=F32)
        o_ref[:, sl] = (o / l).astype(o_ref.dtype)


def _cross_attention(q, kv, n_heads, tq=512):
    B, S, X = q.shape
    M = kv.shape[1]
    tq = _tile(S, tq, 16)
    est = 2 * (2 * tq * X * 2 + 2 * M * X * 2) + 6 * tq * M * 4
    return pl.pallas_call(
        functools.partial(_xattn_body, n_heads=n_heads),
        grid=(B, S // tq),
        in_specs=[pl.BlockSpec((None, tq, X), lambda b, i: (b, i, 0)),
                  pl.BlockSpec((None, M, X), lambda b, i: (b, 0, 0)),
                  pl.BlockSpec((None, M, X), lambda b, i: (b, 0, 1))],
        out_specs=pl.BlockSpec((None, tq, X), lambda b, i: (b, i, 0)),
        out_shape=jax.ShapeDtypeStruct((B, S, X), BF16),
        compiler_params=_params(("parallel", "parallel"), est),
        name="cross_attention",
    )(q, kv, kv)


def _odd_prep_body(x_ref, gqa_ref, gkv_ref, gkr_ref, gik_ref, cos_ref, sin_ref,
                   qa_ref, ckv_ref, kr_ref, ik_ref, iw_ref, *, c1, c2, iw_scale):
    def norm(x, g):
        ms = jnp.mean(x * x, axis=-1, keepdims=True)
        return x * lax.rsqrt(ms + EPS) * g

    def rope(y):
        lane = lax.broadcasted_iota(I32, y.shape, 1)
        half = C_ROPE // 2
        first = (lane % C_ROPE) < half
        r = jnp.where(first, pltpu.roll(y, LANE - half, axis=1), pltpu.roll(y, half, axis=1))
        return y * cos_ref[...] + r * sin_ref[...]

    qa_ref[...] = norm(x_ref[:, :c1], gqa_ref[...]).astype(qa_ref.dtype)
    ckv_ref[...] = norm(x_ref[:, c1:c2], gkv_ref[...]).astype(ckv_ref.dtype)
    slab_a = x_ref[:, c2:c2 + LANE]
    slab_b = x_ref[:, c2 + LANE:c2 + 2 * LANE]
    lane = lax.broadcasted_iota(I32, slab_a.shape, 1)
    low = lane < C_ROPE
    ms = jnp.sum(jnp.where(low, slab_a * slab_a, 0.0), axis=-1, keepdims=True) / C_ROPE
    kr = rope(slab_a * lax.rsqrt(ms + EPS) * gkr_ref[...])
    kr_ref[0] = kr.astype(kr_ref.dtype)
    kr_ref[1] = pltpu.roll(kr, C_ROPE, axis=1).astype(kr_ref.dtype)
    rot_a = pltpu.roll(slab_a, C_ROPE, axis=1)
    rot_b = pltpu.roll(slab_b, C_ROPE, axis=1)
    ik = jnp.where(low, rot_a, rot_b)
    ik_ref[...] = rope(norm(ik, gik_ref[...])).astype(ik_ref.dtype)
    iw_ref[...] = rot_b * iw_scale


def _odd_prep(x, c1, c2, g_qa, g_kv, g_kr, g_ik, cos_p, sin_p, iw_scale, tm=256):
    B, S, C = x.shape
    assert C == c2 + 2 * LANE
    tm = _tile(S, tm, 16)
    vec = lambda n: pl.BlockSpec((1, n), lambda b, i: (0, 0))
    tab = pl.BlockSpec((tm, LANE), lambda b, i: (i, 0))
    est = 2 * tm * C * 4 * 2
    return pl.pallas_call(
        functools.partial(_odd_prep_body, c1=c1, c2=c2, iw_scale=iw_scale),
        grid=(B, S // tm),
        in_specs=[pl.BlockSpec((None, tm, C), lambda b, i: (b, i, 0)),
                  vec(c1), vec(c2 - c1), vec(LANE), vec(LANE), tab, tab],
        out_specs=[pl.BlockSpec((None, tm, c1), lambda b, i: (b, i, 0)),
                   pl.BlockSpec((None, tm, c2 - c1), lambda b, i: (b, i, 0)),
                   pl.BlockSpec((None, 2, tm, LANE), lambda b, i: (b, 0, i, 0)),
                   pl.BlockSpec((None, tm, LANE), lambda b, i: (b, i, 0)),
                   pl.BlockSpec((None, tm, LANE), lambda b, i: (b, i, 0))],
        out_shape=[jax.ShapeDtypeStruct((B, S, c1), BF16),
                   jax.ShapeDtypeStruct((B, S, c2 - c1), BF16),
                   jax.ShapeDtypeStruct((B, 2, S, LANE), BF16),
                   jax.ShapeDtypeStruct((B, S, LANE), BF16),
                   jax.ShapeDtypeStruct((B, S, LANE), F32)],
        compiler_params=_params(("parallel", "parallel"), est),
        name="odd_prep",
    )(x, g_qa.reshape(1, -1), g_kv.reshape(1, -1), g_kr.reshape(1, -1), g_ik.reshape(1, -1), cos_p, sin_p)


def _q_prep_body(x_ref, gn_ref, gr_ref, cos_ref, sin_ref, qn_ref, qr_ref, *, n_heads):
    nope_w = n_heads * C_NOPE
    width = C_NOPE + C_ROPE
    half = C_ROPE // 2
    lane = lax.broadcasted_iota(I32, (x_ref.shape[0], LANE), 1)
    low = lane < C_ROPE
    first = (lane % C_ROPE) < half
    for p in range(n_heads // 2):
        n0 = x_ref[:, (2 * p) * LANE:(2 * p + 1) * LANE]
        n1 = x_ref[:, (2 * p + 1) * LANE:(2 * p + 2) * LANE]
        r = x_ref[:, nope_w + p * LANE:nope_w + (p + 1) * LANE]
        r2 = r * r
        ss0 = jnp.sum(n0 * n0, axis=-1, keepdims=True) + jnp.sum(jnp.where(low, r2, 0.0), axis=-1, keepdims=True)
        ss1 = jnp.sum(n1 * n1, axis=-1, keepdims=True) + jnp.sum(jnp.where(low, 0.0, r2), axis=-1, keepdims=True)
        inv0 = lax.rsqrt(ss0 / width + EPS)
        inv1 = lax.rsqrt(ss1 / width + EPS)
        qn_ref[:, (2 * p) * LANE:(2 * p + 1) * LANE] = (n0 * inv0 * gn_ref[...]).astype(qn_ref.dtype)
        qn_ref[:, (2 * p + 1) * LANE:(2 * p + 2) * LANE] = (n1 * inv1 * gn_ref[...]).astype(qn_ref.dtype)
        y = r * jnp.where(low, inv0, inv1) * gr_ref[...]
        rot = jnp.where(first, pltpu.roll(y, LANE - half, axis=1), pltpu.roll(y, half, axis=1))
        qr_ref[:, p * LANE:(p + 1) * LANE] = (y * cos_ref[...] + rot * sin_ref[...]).astype(qr_ref.dtype)


def _q_prep(x, n_heads, g_nope, g_rope2, cos_q, sin_q, tm=256):
    B, S, C = x.shape
    tm = _tile(S, tm, 16)
    nope_w, rope_w = n_heads * C_NOPE, n_heads * C_ROPE
    vec = pl.BlockSpec((1, LANE), lambda b, i: (0, 0))
    tab = pl.BlockSpec((tm, LANE), lambda b, i: (i, 0))
    est = 2 * tm * C * 6
    return pl.pallas_call(
        functools.partial(_q_prep_body, n_heads=n_heads),
        grid=(B, S // tm),
        in_specs=[pl.BlockSpec((None, tm, C), lambda b, i: (b, i, 0)), vec, vec, tab, tab],
        out_specs=[pl.BlockSpec((None, tm, nope_w), lambda b, i: (b, i, 0)),
                   pl.BlockSpec((None, tm, rope_w), lambda b, i: (b, i, 0))],
        out_shape=[jax.ShapeDtypeStruct((B, S, nope_w), BF16), jax.ShapeDtypeStruct((B, S, rope_w), BF16)],
        compiler_params=_params(("parallel", "parallel"), est),
        name="dsa_q_prep",
    )(x, g_nope.reshape(1, LANE), g_rope2.reshape(1, LANE), cos_q, sin_q)


def _indexer_body(ik_ref, iq_ref, iwt_ref, o_ref, key_ref, *, tq, n_heads, n_keep):
    S = ik_ref.shape[0]
    qi = pl.program_id(1)
    n_tiles = qi + 1
    t_idx = qi * tq + lax.broadcasted_iota(I32, (tq, tq), 1)
    s_loc = lax.broadcasted_iota(I32, (tq, tq), 0)

    def score_tile(kt, c):
        off = pl.multiple_of(kt * tq, tq)
        ikt = ik_ref[pl.ds(off, tq), :]
        acc = jnp.zeros((tq, tq), F32)
        for h in range(n_heads):
            r = _nt(ikt, iq_ref[:, h * IDX_DIM:(h + 1) * IDX_DIM])
            acc = acc + jnp.maximum(r, 0.0) * iwt_ref[h:h + 1, :]
        acc = jnp.where(off + s_loc <= t_idx, acc, -jnp.inf)
        bits = lax.bitcast_convert_type(acc, I32)
        key_ref[pl.ds(off, tq), :] = bits ^ ((bits >> 31) & 0x7FFFFFFF)
        return c

    lax.fori_loop(0, n_tiles, score_tile, 0)

    def count(hit):
        def body(kt, cnt):
            off = pl.multiple_of(kt * tq, tq)
            one = jnp.where(hit(key_ref[pl.ds(off, tq), :], off + s_loc), 1, 0).astype(I32)
            return cnt + jnp.sum(one.reshape(tq // 8, 8, tq), axis=0)
        cnt = lax.fori_loop(0, n_tiles, body, jnp.zeros((8, tq), I32))
        return jnp.sum(cnt, axis=0, keepdims=True)

    n_pos = count(lambda k, s: k >= 0)
    thr = jnp.where(n_pos >= n_keep, 0, INT_MIN).astype(I32)

    def bit_step(i, thr):
        cand = thr + lax.shift_left(jnp.int32(1), 30 - i)
        return jnp.where(count(lambda k, s: k >= cand) >= n_keep, cand, thr)

    thr = lax.fori_loop(0, 31, bit_step, thr)

    def tie_cut():
        need = n_keep - count(lambda k, s: k > thr)
        bits = S.bit_length()

        def step(i, cut):
            cand = cut + lax.shift_left(jnp.int32(1), bits - 1 - i)
            below = count(lambda k, s: (k == thr) & (s < cand))
            return jnp.where(below < need, cand, cut)

        return lax.fori_loop(0, bits, step, jnp.zeros((1, tq), I32))

    has_ties = jnp.max(count(lambda k, s: k >= thr)) > n_keep
    cut = lax.cond(has_ties, tie_cut, lambda: jnp.full((1, tq), S, I32))

    def out_tile(kt, c):
        off = pl.multiple_of(kt * tq, tq)
        key = key_ref[pl.ds(off, tq), :]
        s_idx = off + s_loc
        ok = ((key > thr) | ((key == thr) & (s_idx <= cut))) & (s_idx <= t_idx)
        o_ref[pl.ds(off, tq), :] = jnp.where(ok, 0.0, NEG).astype(o_ref.dtype)
        return c

    def neg_tile(kt, c):
        off = pl.multiple_of(kt * tq, tq)
        o_ref[pl.ds(off, tq), :] = jnp.full((tq, tq), NEG, o_ref.dtype)
        return c

    lax.fori_loop(0, n_tiles, out_tile, 0)
    lax.fori_loop(n_tiles, S // tq, neg_tile, 0)


def _indexer(ik, iq, iwt, n_keep, tq=256):
    B, S, _ = ik.shape
    n_heads = iwt.shape[1]
    tq = _tile(S, tq, LANE)
    est = 2 * (S * IDX_DIM * 2 + tq * n_heads * IDX_DIM * 2 + n_heads * tq * 4 + S * tq * 2) + S * tq * 4 + 8 * tq * tq * 4
    return pl.pallas_call(
        functools.partial(_indexer_body, tq=tq, n_heads=n_heads, n_keep=n_keep),
        grid=(B, S // tq),
        in_specs=[pl.BlockSpec((None, S, IDX_DIM), lambda b, i: (b, 0, 0)),
                  pl.BlockSpec((None, tq, n_heads * IDX_DIM), lambda b, i: (b, i, 0)),
                  pl.BlockSpec((None, n_heads, tq), lambda b, i: (b, 0, i))],
        out_specs=pl.BlockSpec((None, S, tq), lambda b, i: (b, 0, i)),
        out_shape=jax.ShapeDtypeStruct((B, S, S), BF16),
        scratch_shapes=[pltpu.VMEM((S, tq), I32)],
        compiler_params=_params(("parallel", "arbitrary"), est),
        name="dsa_indexer",
    )(ik, iq, iwt)


def _dsa_attn_body(qn_ref, qr_ref, kn_ref, kr_ref, vt_ref, bias_ref, o_ref, s_ref, acc_ref, *, tq, hg):
    qi = pl.program_id(2)
    heads = [slice(hh * C_NOPE, (hh + 1) * C_NOPE) for hh in range(hg)]
    qs = [jnp.concatenate([qn_ref[:, hs], qr_ref[:, (hh // 2) * LANE:(hh // 2 + 1) * LANE]], axis=1)
          for hh, hs in enumerate(heads)]

    def scores(n):
        off = pl.multiple_of(n * tq, tq)
        bias = bias_ref[pl.ds(off, tq), :].astype(F32)
        kr = [kr_ref[par, pl.ds(off, tq), :] for par in range(2)]
        return tuple(bias + _nt(jnp.concatenate([kn_ref[pl.ds(off, tq), hs], kr[hh % 2]], axis=1), qs[hh])
                     for hh, hs in enumerate(heads))

    def body(n, carry):
        return _softmax_steps(scores(n), carry, s_ref, acc_ref, lambda hh: vt_ref[hh, n])

    acc_ref[...] = jnp.zeros(acc_ref.shape, F32)
    lax.fori_loop(0, qi + 1, body, _init_carry(hg, tq))
    for hh, hs in enumerate(heads):
        o_ref[:, hs] = _normalized(acc_ref, hh, C_NOPE).T.astype(o_ref.dtype)


def _dsa_attention(qn, qr, kv, kr2, vt, bias, n_heads, tq, hg=8):
    B, S, _ = qn.shape
    hg = min(hg, n_heads)
    assert hg % 2 == 0 and n_heads % hg == 0
    ng = n_heads // hg
    nk = S // tq
    est = 2 * (tq * hg * 192 * 2 + 2 * S * hg * LANE * 2 + 2 * S * LANE * 2 + S * tq * 2 + tq * hg * LANE * 2) \
        + hg * 3 * tq * tq * 4
    return pl.pallas_call(
        functools.partial(_dsa_attn_body, tq=tq, hg=hg),
        grid=(B, ng, S // tq),
        in_specs=[pl.BlockSpec((None, tq, hg * C_NOPE), lambda b, g, i: (b, i, g)),
                  pl.BlockSpec((None, tq, hg * C_ROPE), lambda b, g, i: (b, i, g)),
                  pl.BlockSpec((None, S, hg * C_NOPE), lambda b, g, i: (b, 0, g)),
                  pl.BlockSpec((None, 2, S, LANE), lambda b, g, i: (b, 0, 0, 0)),
                  pl.BlockSpec((None, hg, nk, C_NOPE + ONES_ROWS, tq), lambda b, g, i: (b, g, 0, 0, 0)),
                  pl.BlockSpec((None, S, tq), lambda b, g, i: (b, 0, i))],
        out_specs=pl.BlockSpec((None, tq, hg * C_NOPE), lambda b, g, i: (b, i, g)),
        out_shape=jax.ShapeDtypeStruct((B, S, n_heads * C_NOPE), BF16),
        scratch_shapes=[pltpu.VMEM((hg, tq, tq), F32), pltpu.VMEM((hg, C_NOPE + ONES_ROWS, tq), F32)],
        compiler_params=_params(("parallel", "parallel", "arbitrary"), est),
        name="dsa_attention",
    )(qn, qr, kv, kr2, vt, bias)


def _rope_tables(seq, dim):
    inv_freq = ROPE_THETA ** (-jnp.arange(0, dim, 2, dtype=F32) / dim)
    ang = jnp.arange(seq, dtype=F32)[:, None] * inv_freq[None, :]
    return jnp.cos(ang), jnp.sin(ang)


def _cross_block(x, mem, l, norm_xattn, norm_mem, xa_wq, xa_wk, xa_wv, xa_wo, xa_qnorm, xa_knorm, B, S):
    N, D = x.shape
    X = xa_wq.shape[2]
    n_heads = X // HEAD_DIM
    M = mem.shape[1]
    h = _rmsnorm(x, norm_xattn[l], BF16)
    mn = _rmsnorm(mem.reshape(B * M, D), norm_mem[l], BF16)
    ones = jnp.ones((S, LANE), F32)
    q = _matmul(h, xa_wq[l].astype(BF16), out_dtype=BF16, name="xattn_q",
                epi=dict(flags=[EPI_NORM] * (X // _tile(X, 1024, LANE)), modes=(EPI_NORM,), rope_half=0,
                         tn=_tile(X, 1024, LANE), cos=ones, sin=ones,
                         gain=jnp.tile(xa_qnorm[l] * (HEAD_DIM ** -0.5 * LOG2E), n_heads)))
    wkv = jnp.concatenate([xa_wk[l], xa_wv[l]], axis=1).astype(BF16)
    tn = _tile(X, 1024, LANE)
    kv = _matmul(mn, wkv, out_dtype=BF16, name="xattn_kv",
                 epi=dict(flags=[EPI_NORM] * (X // tn) + [EPI_NONE] * (X // tn), modes=(EPI_NORM,), rope_half=0,
                          tn=tn, cos=jnp.ones((M, LANE), F32), sin=jnp.ones((M, LANE), F32),
                          gain=jnp.concatenate([jnp.tile(xa_knorm[l], n_heads), jnp.ones((X,), F32)])))
    o = _cross_attention(q.reshape(B, S, X), kv.reshape(B, M, 2 * X), n_heads)
    return _matmul(o.reshape(N, X), xa_wo[l].astype(BF16), out_dtype=F32, name="xattn_out", resid=x)


def _even_mixer(x, l, i, B, S, norm_mix, ev_w_in, ev_a_qnorm, ev_a_knorm, lam_params, ev_a_subln,
                ev_b_qnorm, ev_b_knorm, ev_w_out):
    N, D = x.shape
    a_heads = D // (4 * HEAD_DIM)
    b_heads = D // (2 * HEAD_DIM)
    aw = a_heads * 2 * HEAD_DIM
    bw = b_heads * HEAD_DIM
    width = 3 * aw + 3 * bw
    scale = HEAD_DIM ** -0.5 * LOG2E
    cos, sin = _rope_tables(S, HEAD_DIM)
    cos2 = jnp.concatenate([cos, cos], axis=-1)
    sin2 = jnp.concatenate([-sin, sin], axis=-1)
    tn = _tile(math.gcd(aw, bw), 1024, LANE)
    seg = [(aw, EPI_NORM_ROPE, ev_a_qnorm[i] * scale), (aw, EPI_NORM_ROPE, ev_a_knorm[i]), (aw, EPI_NONE, None),
           (bw, EPI_NORM_ROPE, ev_b_qnorm[i] * scale), (bw, EPI_NORM_ROPE, ev_b_knorm[i]), (bw, EPI_NONE, None)]
    flags, gains = [], []
    for w, flag, g in seg:
        flags += [flag] * (w // tn)
        gains.append(jnp.ones((w,), F32) if g is None else jnp.tile(g.astype(F32), w // HEAD_DIM))
    h = _rmsnorm(x, norm_mix[l], BF16)
    qkv = _matmul(h, ev_w_in[i].astype(BF16), out_dtype=BF16, name="even_in",
                  epi=dict(flags=flags, modes=(EPI_NORM_ROPE,), rope_half=HEAD_DIM // 2, tn=tn,
                           cos=cos2, sin=sin2, gain=jnp.concatenate(gains)))
    qkv = qkv.reshape(B, S, width)
    lam_init = 0.8 - 0.6 * math.exp(-0.3 * l)
    tq = _tile(S, 256, LANE)
    ya = _diff_attention(qkv, _value_tiles(qkv[:, :, 2 * aw:3 * aw], a_heads, tq), a_heads, lam_params,
                         ev_a_subln[i], lam_init, tq)
    c0 = 3 * aw // HEAD_DIM
    yb = _moba_attention(qkv, _value_tiles(qkv[:, :, 3 * aw + 2 * bw:], b_heads, MOBA_BLOCK), b_heads,
                         c0, c0 + b_heads)
    y = jnp.concatenate([ya, yb], axis=-1).reshape(N, aw + bw)
    return _matmul(y, ev_w_out[i].astype(BF16), out_dtype=F32, name="even_out", resid=x)


def _odd_mixer(x, l, i, B, S, norm_mix, od_w_in, od_qa_norm, od_w_qb, od_q_norm, od_kv_norm, od_kr_norm,
               od_w_uk, od_w_uv, od_w_iqb, od_ik_norm, od_w_out):
    N, D = x.shape
    c1 = od_qa_norm.shape[1]
    kv_rank = od_kv_norm.shape[1]
    c2 = c1 + kv_rank
    n_heads = od_w_uk.shape[2]
    idx_heads = od_w_iqb.shape[2] // IDX_DIM
    assert od_w_in.shape[2] == c2 + C_ROPE + IDX_DIM + idx_heads and idx_heads == C_ROPE
    scale = (C_NOPE + C_ROPE) ** -0.5 * LOG2E
    cos, sin = _rope_tables(S, C_ROPE)
    one, zero = jnp.ones((S, C_ROPE), F32), jnp.zeros((S, C_ROPE), F32)
    cos_p = jnp.concatenate([cos, cos, one], axis=-1)
    sin_p = jnp.concatenate([-sin, sin, zero], axis=-1)
    cos_q = jnp.concatenate([cos, cos, cos, cos], axis=-1)
    sin_q = jnp.concatenate([-sin, sin, -sin, sin], axis=-1)

    h = _rmsnorm(x, norm_mix[l], BF16)
    proj = _matmul(h, od_w_in[i].astype(BF16), out_dtype=F32, name="odd_in")
    g_kr = jnp.concatenate([od_kr_norm[i], jnp.zeros((LANE - C_ROPE,), F32)])
    qa, ckv, kr2, ik, iw = _odd_prep(proj.reshape(B, S, -1), c1, c2, od_qa_norm[i], od_kv_norm[i], g_kr,
                                     od_ik_norm[i], cos_p, sin_p, idx_heads ** -0.5 * IDX_DIM ** -0.5)
    qa = qa.reshape(N, c1)
    wqb = od_w_qb[i].reshape(c1, n_heads, C_NOPE + C_ROPE)
    wqb = jnp.concatenate([wqb[:, :, :C_NOPE].reshape(c1, -1), wqb[:, :, C_NOPE:].reshape(c1, -1)], axis=1)
    qraw = _matmul(qa, wqb.astype(BF16), out_dtype=F32, name="odd_qb")
    qn, qr = _q_prep(qraw.reshape(B, S, -1), n_heads, od_q_norm[i][:C_NOPE] * scale,
                     jnp.tile(od_q_norm[i][C_NOPE:], 2) * scale, cos_q, sin_q)
    tn = _tile(idx_heads * IDX_DIM, 1024, LANE)
    iq = _matmul(qa, od_w_iqb[i].astype(BF16), out_dtype=BF16, name="odd_iqb",
                 epi=dict(flags=[EPI_ROPE] * (idx_heads * IDX_DIM // tn), modes=(EPI_ROPE,), rope_half=IDX_ROPE // 2,
                          tn=tn, cos=cos_p, sin=sin_p, gain=jnp.ones((idx_heads * IDX_DIM,), F32)))
    wkv = jnp.concatenate([od_w_uk[i].reshape(kv_rank, -1), od_w_uv[i].reshape(kv_rank, -1)], axis=1)
    kv = _matmul(ckv.reshape(N, kv_rank), wkv.astype(BF16), out_dtype=BF16, name="odd_kv")

    n_keep = min(IDX_TOPK, S // 4)
    tq = _tile(S, 256, LANE)
    iwt = jnp.swapaxes(iw[:, :, :idx_heads], 1, 2)
    bias = _indexer(ik, iq.reshape(B, S, -1), iwt, n_keep, tq)
    kv = kv.reshape(B, S, -1)
    vt = _value_tiles(kv[:, :, n_heads * C_NOPE:], n_heads, tq)
    y = _dsa_attention(qn, qr, kv, kr2, vt, bias, n_heads, tq)
    return _matmul(y.reshape(N, -1), od_w_out[i].astype(BF16), out_dtype=F32, name="odd_out", resid=x)


def _dense_ffn(x, g, wg, wu, wd):
    N, D = x.shape
    F = wg.shape[1]
    h = _rmsnorm(x, g, BF16)
    tm = _tile(N, 2048, 16)
    tn = _tile(F, 512, LANE)
    nt = N // tm
    hid = _swiglu_up(h, wg[None], wu[None], jnp.zeros((nt,), I32), jnp.full((1,), nt, I32),
                     jnp.full((nt,), tm, I32), tm=tm, tn=tn, name="ffn_up")
    return _matmul(hid, wd.astype(BF16), out_dtype=F32, name="ffn_down", resid=x, tm=512, tn=1024,
                   tk=F if F <= 4096 else _tile(F, F // 2, LANE))


def _moe_ffn(x, g, router, wg, wu, wd, tm=512):
    N, D = x.shape
    E, _, F = wg.shape
    h, gate, sel = _norm_router(x, g, router)
    tm = _tile(N, tm, 16)
    cnt = jnp.sum(sel, axis=0)
    tiles_e = (cnt + tm - 1) // tm
    tile_end = jnp.cumsum(tiles_e)
    start = (tile_end - tiles_e) * tm
    rank = jnp.cumsum(sel, axis=0) - sel
    P = N * TOP_K + E * tm
    n_tiles = P // tm
    slot = (start[None, :] + rank).astype(I32)
    lane = jnp.arange(E, dtype=I32)[None, :]
    e2 = jnp.stack([jnp.min(jnp.where(sel > 0, lane, E), axis=1), jnp.max(jnp.where(sel > 0, lane, -1), axis=1)], 1)
    slots2 = jnp.take_along_axis(slot, e2, axis=1)
    gates2 = jnp.take_along_axis(gate, e2, axis=1)
    tok2 = jnp.broadcast_to(jnp.arange(N, dtype=I32)[:, None], (N, TOP_K))
    tok_of_slot = jnp.zeros((P,), I32).at[slots2.reshape(-1)].set(tok2.reshape(-1), unique_indices=True)
    tile_expert = jnp.minimum(jnp.searchsorted(tile_end, jnp.arange(n_tiles, dtype=I32), side="right"),
                              E - 1).astype(I32)
    n_active = tile_end[-1:].astype(I32)
    tile_row = jnp.arange(n_tiles, dtype=I32) * tm
    valid_rows = jnp.where(tile_row < n_active[0] * tm,
                           jnp.clip((start + cnt)[tile_expert] - tile_row, 0, tm), 0).astype(I32)

    xs = _gather_rows(h, tok_of_slot, BF16)
    hid = _swiglu_up(xs, wg, wu, tile_expert, n_active, valid_rows, tm=tm, tn=_tile(F, 512, LANE), name="moe_up")
    y = _grouped_down(hid, wd, tile_expert, n_active, valid_rows, tm=tm, tn=_tile(D, 1024, LANE), name="moe_down")
    return _moe_combine(x, y, slots2, gates2)


def kernel(x, mem, norm_mix, norm_xattn, norm_mem, norm_ffn, ev_w_in, ev_a_qnorm, ev_a_knorm, ev_lambda_q1, ev_lambda_k1, ev_lambda_q2, ev_lambda_k2, ev_a_subln, ev_b_qnorm, ev_b_knorm, ev_w_out, od_w_in, od_qa_norm, od_w_qb, od_q_norm, od_kv_norm, od_kr_norm, od_w_uk, od_w_uv, od_w_iqb, od_ik_norm, od_w_out, xa_wq, xa_wk, xa_wv, xa_wo, xa_qnorm, xa_knorm, ffn_wg, ffn_wu, ffn_wd, moe_router, moe_wg, moe_wu, moe_wd):
    B, S, D = x.shape
    depth = norm_mix.shape[0]
    x = x.reshape(B * S, D)
    for l in range(depth):
        i = l // 2
        if l % 2 == 0:
            x = _even_mixer(x, l, i, B, S, norm_mix, ev_w_in, ev_a_qnorm, ev_a_knorm,
                            (ev_lambda_q1[i], ev_lambda_k1[i], ev_lambda_q2[i], ev_lambda_k2[i]),
                            ev_a_subln, ev_b_qnorm, ev_b_knorm, ev_w_out)
        else:
            x = _odd_mixer(x, l, i, B, S, norm_mix, od_w_in, od_qa_norm, od_w_qb, od_q_norm, od_kv_norm,
                           od_kr_norm, od_w_uk, od_w_uv, od_w_iqb, od_ik_norm, od_w_out)
        x = _cross_block(x, mem, l, norm_xattn, norm_mem, xa_wq, xa_wk, xa_wv, xa_wo, xa_qnorm, xa_knorm, B, S)
        if l % 2 == 0:
            x = _dense_ffn(x, norm_ffn[l], ffn_wg[i], ffn_wu[i], ffn_wd[i])
        else:
            x = _moe_ffn(x, norm_ffn[l], moe_router[i], moe_wg[i], moe_wu[i], moe_wd[i])
    return x.reshape(B, S, D)
```

```python
import functools
import math

import jax
import jax.numpy as jnp
import numpy as np
from jax import lax
from jax.experimental import pallas as pl
from jax.experimental.pallas import tpu as pltpu

F32 = jnp.float32
BF16 = jnp.bfloat16
I32 = jnp.int32

LANE = 128
MXU_WIDTH = 256
V7X_VMEM_BYTES = 64 * 1024 * 1024
VMEM_CAP = V7X_VMEM_BYTES - 4 * 1024 * 1024

HEAD_DIM = 128
ROPE_THETA = 10000.0
EPS = 1e-6
MOBA_BLOCK = 256
MOBA_TOPK = 3
C_NOPE = 128
C_ROPE = 64
IDX_DIM = 128
IDX_ROPE = 64
IDX_TOPK = 256
TOP_K = 2
NEG = -1e30
LOG2E = math.log2(math.e)
INT_MIN = -(2 ** 31)


def _tile(dim, pref, align):
    t = min(pref, dim)
    t -= t % align
    while t >= align:
        if dim % t == 0:
            return t
        t -= align
    return dim


def _params(sem, est_bytes):
    limit = int(min(max(est_bytes * 1.3 + (4 << 20), 32 << 20), VMEM_CAP))
    return pltpu.CompilerParams(dimension_semantics=sem, vmem_limit_bytes=limit)


def _nt(a, b):
    return lax.dot_general(a, b, (((1,), (1,)), ((), ())), preferred_element_type=F32)


def _softmax_steps(sts, carry, s_ref, acc_ref, value_tile):
    for c, st in enumerate(sts):
        s_ref[c] = st
    new = []
    for c in range(len(sts)):
        m_new = jnp.maximum(carry[c], jnp.max(s_ref[c], axis=0, keepdims=True))
        alpha = jnp.exp2(carry[c] - m_new)
        p = jnp.exp2(s_ref[c] - m_new)
        new.append(m_new)
        acc_ref[c] = alpha * acc_ref[c] + jnp.dot(value_tile(c), p.astype(BF16), preferred_element_type=F32)
    return tuple(new)


def _normalized(acc_ref, c, dv):
    acc = acc_ref[c]
    return acc[:dv] / acc[dv:dv + 1]


def _init_carry(n_chains, tq):
    return tuple(jnp.full((1, tq), NEG, F32) for _ in range(n_chains))


ONES_ROWS = 16


def _value_tiles(v, n_heads, tk):
    B, S, C = v.shape
    vt = v.reshape(B, S // tk, tk, n_heads, C // n_heads).transpose(0, 3, 1, 4, 2)
    extra = jnp.zeros(vt.shape[:3] + (ONES_ROWS, tk), v.dtype).at[..., 0, :].set(1)
    return jnp.concatenate([vt, extra], axis=3)


def _rmsnorm_body(x_ref, g_ref, o_ref):
    x = x_ref[...].astype(F32)
    ms = jnp.mean(x * x, axis=-1, keepdims=True)
    o_ref[...] = (x * lax.rsqrt(ms + EPS) * g_ref[...]).astype(o_ref.dtype)


def _rmsnorm(x, g, out_dtype, tm=256):
    M, D = x.shape
    tm = _tile(M, tm, 16)
    est = 2 * tm * D * (4 + 4)
    return pl.pallas_call(
        _rmsnorm_body,
        grid=(M // tm,),
        in_specs=[pl.BlockSpec((tm, D), lambda i: (i, 0)), pl.BlockSpec((1, D), lambda i: (0, 0))],
        out_specs=pl.BlockSpec((tm, D), lambda i: (i, 0)),
        out_shape=jax.ShapeDtypeStruct((M, D), out_dtype),
        compiler_params=_params(("parallel",), est),
        name="rmsnorm",
    )(x, g.reshape(1, D).astype(F32))


def _norm_router_body(x_ref, g_ref, rt_ref, h_ref, gate_ref, sel_ref, *, n_exp):
    x = x_ref[...]
    ms = jnp.mean(x * x, axis=-1, keepdims=True)
    h = x * lax.rsqrt(ms + EPS) * g_ref[...]
    h_ref[...] = h
    lane = lax.broadcasted_iota(I32, gate_ref.shape, 1)
    logits = jnp.full(gate_ref.shape, -jnp.inf, F32)
    for e in range(n_exp):
        col = jnp.sum(h * rt_ref[e:e + 1, :], axis=-1, keepdims=True)
        logits = jnp.where(lane == e, col, logits)
    m1 = jnp.max(logits, axis=-1, keepdims=True)
    i1 = jnp.min(jnp.where(logits == m1, lane, LANE), axis=-1, keepdims=True)
    rest = jnp.where(lane == i1, -jnp.inf, logits)
    m2 = jnp.max(rest, axis=-1, keepdims=True)
    i2 = jnp.min(jnp.where(rest == m2, lane, LANE), axis=-1, keepdims=True)
    e2 = jnp.exp(m2 - m1)
    den = 1.0 + e2
    gate_ref[...] = jnp.where(lane == i1, 1.0 / den, 0.0) + jnp.where(lane == i2, e2 / den, 0.0)
    sel_ref[...] = jnp.where((lane == i1) | (lane == i2), 1, 0).astype(I32)


def _norm_router(x, g, router, tm=256):
    M, D = x.shape
    n_exp = router.shape[1]
    tm = _tile(M, tm, 8)
    est = 2 * tm * D * 8 + 2 * 8 * D * 4
    h, gate, sel = pl.pallas_call(
        functools.partial(_norm_router_body, n_exp=n_exp),
        grid=(M // tm,),
        in_specs=[pl.BlockSpec((tm, D), lambda i: (i, 0)), pl.BlockSpec((1, D), lambda i: (0, 0)),
                  pl.BlockSpec((n_exp, D), lambda i: (0, 0))],
        out_specs=[pl.BlockSpec((tm, D), lambda i: (i, 0)), pl.BlockSpec((tm, LANE), lambda i: (i, 0)),
                   pl.BlockSpec((tm, LANE), lambda i: (i, 0))],
        out_shape=[jax.ShapeDtypeStruct((M, D), F32), jax.ShapeDtypeStruct((M, LANE), F32),
                   jax.ShapeDtypeStruct((M, LANE), I32)],
        compiler_params=_params(("parallel",), est),
        name="norm_router",
    )(x, g.reshape(1, D).astype(F32), router.T.astype(F32))
    return h, gate[:, :n_exp], sel[:, :n_exp]


EPI_NONE, EPI_NORM, EPI_NORM_ROPE, EPI_ROPE = 0, 1, 2, 3


def _head_epilogue(x, g, cos, sin, mode, rope_half):
    if mode in (EPI_NORM, EPI_NORM_ROPE):
        ss = jnp.dot((x * x).astype(BF16), jnp.ones((LANE, LANE), BF16), preferred_element_type=F32)
        x = x * lax.rsqrt(ss * (1.0 / LANE) + EPS) * g
    if mode in (EPI_NORM_ROPE, EPI_ROPE):
        if rope_half == LANE // 2:
            r = pltpu.roll(x, LANE // 2, axis=1)
        else:
            lane = lax.broadcasted_iota(I32, x.shape, 1)
            first = (lane % (2 * rope_half)) < rope_half
            r = jnp.where(first, pltpu.roll(x, LANE - rope_half, axis=1), pltpu.roll(x, rope_half, axis=1))
        x = x * cos + r * sin
    return x


def _mm_body(flags_ref, a_ref, w_ref, *rest, nk, has_resid, modes, rope_half, tn, cw):
    rest = list(rest)
    resid_ref = rest.pop(0) if has_resid else None
    gain_ref = cos_ref = sin_ref = None
    if modes:
        gain_ref, cos_ref, sin_ref = rest.pop(0), rest.pop(0), rest.pop(0)
    o_ref = rest.pop(0)
    acc_ref = rest.pop(0) if nk > 1 else None
    j = pl.program_id(1)
    k = pl.program_id(2)

    def emit(cols, acc, mode):
        if mode != EPI_NONE:
            for c in range(cols.start, cols.stop, LANE):
                sl = slice(c, c + LANE)
                y = _head_epilogue(acc[:, sl.start - cols.start:sl.stop - cols.start], gain_ref[:, sl],
                                   cos_ref[...], sin_ref[...], mode, rope_half)
                o_ref[:, sl] = y.astype(o_ref.dtype)
        elif has_resid:
            o_ref[:, cols] = (resid_ref[:, cols] + acc).astype(o_ref.dtype)
        else:
            o_ref[:, cols] = acc.astype(o_ref.dtype)

    if nk == 1:
        def run(mode):
            pending = None
            for c in range(0, tn, cw):
                cols = slice(c, c + cw)
                acc = jnp.dot(a_ref[...], w_ref[:, cols].astype(BF16), preferred_element_type=F32)
                if pending is not None:
                    emit(*pending, mode)
                pending = (cols, acc)
            emit(*pending, mode)

        if modes:
            flag = flags_ref[j]
            for mode in (EPI_NONE,) + modes:
                pl.when(flag == mode)(functools.partial(run, mode))
        else:
            run(EPI_NONE)
    else:
        part = jnp.dot(a_ref[...], w_ref[...], preferred_element_type=F32)

        @pl.when(k == 0)
        def _():
            acc_ref[...] = part

        @pl.when((k > 0) & (k < nk - 1))
        def _():
            acc_ref[...] += part

        @pl.when(k == nk - 1)
        def _():
            emit(slice(0, tn), acc_ref[...] + part, EPI_NONE)


def _matmul(a, w, *, out_dtype, name, resid=None, epi=None, tm=1024, tn=1024, tk=4096):
    M, K = a.shape
    N = w.shape[1]
    tm = _tile(epi["cos"].shape[0] if epi else M, tm, 16)
    tn = epi["tn"] if epi else _tile(N, tn, LANE)
    tk = _tile(K, tk, LANE)
    nk = K // tk
    assert a.dtype == BF16 and (w.dtype == BF16 or nk == 1) and (nk == 1 or not epi)
    a_bytes = a.dtype.itemsize
    o_bytes = jnp.dtype(out_dtype).itemsize
    modes = tuple(epi["modes"]) if epi else ()
    in_specs = [pl.BlockSpec((tm, tk), lambda i, j, k, f: (i, k)),
                pl.BlockSpec((tk, tn), lambda i, j, k, f: (k, j))]
    args = [a, w]
    est = 2 * (tm * tk * a_bytes + tk * tn * w.dtype.itemsize + tm * tn * o_bytes) + 3 * tm * tn * 4
    if resid is not None:
        in_specs.append(pl.BlockSpec((tm, tn), lambda i, j, k, f: (i, j)))
        args.append(resid)
        est += 2 * tm * tn * 4
    if epi:
        ns = epi["cos"].shape[0] // tm
        in_specs += [pl.BlockSpec((1, tn), lambda i, j, k, f: (0, j)),
                     pl.BlockSpec((tm, LANE), lambda i, j, k, f: (i % ns, 0)),
                     pl.BlockSpec((tm, LANE), lambda i, j, k, f: (i % ns, 0))]
        args += [epi["gain"].reshape(1, N).astype(F32), epi["cos"], epi["sin"]]
        flags = jnp.asarray(epi["flags"], I32)
        est += 4 * tm * LANE * 4
    else:
        flags = jnp.zeros((N // tn,), I32)
    body = functools.partial(_mm_body, nk=nk, has_resid=resid is not None, modes=modes,
                             rope_half=epi["rope_half"] if epi else 0, tn=tn, cw=math.gcd(tn, MXU_WIDTH))
    return pl.pallas_call(
        body,
        grid_spec=pltpu.PrefetchScalarGridSpec(
            num_scalar_prefetch=1,
            grid=(M // tm, N // tn, nk),
            in_specs=in_specs,
            out_specs=pl.BlockSpec((tm, tn), lambda i, j, k, f: (i, j)),
            scratch_shapes=[pltpu.VMEM((tm, tn), F32)] if nk > 1 else [],
        ),
        out_shape=jax.ShapeDtypeStruct((M, N), out_dtype),
        compiler_params=_params(("parallel", "parallel", "arbitrary"), est),
        name=name,
    )(flags, *args)


def _segments(tile_expert, n_active):
    T = tile_expert.shape[0]
    idx = jnp.arange(T, dtype=I32)
    prev = jnp.concatenate([tile_expert[:1] - 1, tile_expert[:-1]])
    first = (idx < n_active[0]) & (tile_expert != prev)
    first_idx = jnp.where(first, idx, T)
    after = jnp.concatenate([lax.cummin(first_idx[::-1])[::-1][1:], jnp.full((1,), T, I32)])
    nxt = jnp.where(after < T, tile_expert[jnp.minimum(after, T - 1)], -1)
    return first.astype(I32), nxt.astype(I32)


def _stream_weights(te_ref, first_ref, next_ref, w_refs, stage_refs, cast_refs, sem, tn):
    n = pl.program_id(0)
    m = pl.program_id(1)

    def copies(e, nn):
        cols = pl.ds(pl.multiple_of(nn * tn, tn), tn)
        return [pltpu.make_async_copy(w.at[e, :, cols], st, sem.at[i])
                for i, (w, st) in enumerate(zip(w_refs, stage_refs))]

    @pl.when(first_ref[m] == 1)
    def _():
        @pl.when((n == 0) & (m == 0))
        def _():
            for c in copies(te_ref[0], 0):
                c.start()

        for c in copies(te_ref[m], n):
            c.wait()
        for st, cb in zip(stage_refs, cast_refs):
            cb[...] = st[...].astype(BF16)
        last = next_ref[m] < 0
        e2 = jnp.where(last, te_ref[0], next_ref[m])
        n2 = jnp.where(last, n + 1, n)

        @pl.when(n2 < pl.num_programs(0))
        def _():
            for c in copies(e2, n2):
                c.start()


ROW_SPLIT = 2


def _row_parts(nv_ref, o_ref, compute):
    m = pl.program_id(1)
    hm = o_ref.shape[0] // ROW_SPLIT
    for h in range(ROW_SPLIT):
        rows = slice(h * hm, (h + 1) * hm)
        pl.when(nv_ref[m] > h * hm)(functools.partial(compute, rows))

        @pl.when(nv_ref[m] <= h * hm)
        def _(rows=rows):
            o_ref[rows, :] = jnp.zeros((hm, o_ref.shape[1]), o_ref.dtype)


def _swiglu_body(te_ref, na_ref, first_ref, next_ref, nv_ref, a_ref, wg_ref, wu_ref, o_ref,
                 wgf_ref, wuf_ref, wgb_ref, wub_ref, sem):
    tn = o_ref.shape[1]
    _stream_weights(te_ref, first_ref, next_ref, (wg_ref, wu_ref), (wgf_ref, wuf_ref), (wgb_ref, wub_ref), sem, tn)
    cw = math.gcd(tn, MXU_WIDTH)

    def compute(rows):
        def emit(cols, g, u):
            o_ref[rows, cols] = (g * jax.nn.sigmoid(g) * u).astype(o_ref.dtype)

        pending = None
        for c in range(0, tn, cw):
            cols = slice(c, c + cw)
            g = jnp.dot(a_ref[rows, :], wgb_ref[:, cols], preferred_element_type=F32)
            u = jnp.dot(a_ref[rows, :], wub_ref[:, cols], preferred_element_type=F32)
            if pending is not None:
                emit(*pending)
            pending = (cols, g, u)
        emit(*pending)

    _row_parts(nv_ref, o_ref, compute)


def _swiglu_up(a, wg, wu, tile_expert, n_active, valid_rows, *, tm, tn, name):
    M, D = a.shape
    F = wg.shape[2]
    first, nxt = _segments(tile_expert, n_active)
    est = 2 * (tm * D * 2 + tm * tn * 2) + 2 * D * tn * (4 + 2) + 4 * tm * tn * 4
    return pl.pallas_call(
        _swiglu_body,
        grid_spec=pltpu.PrefetchScalarGridSpec(
            num_scalar_prefetch=5,
            grid=(F // tn, M // tm),
            in_specs=[pl.BlockSpec((tm, D), lambda n, m, *_: (m, 0)),
                      pl.BlockSpec(memory_space=pl.ANY), pl.BlockSpec(memory_space=pl.ANY)],
            out_specs=pl.BlockSpec((tm, tn), lambda n, m, *_: (m, n)),
            scratch_shapes=[pltpu.VMEM((D, tn), F32), pltpu.VMEM((D, tn), F32),
                            pltpu.VMEM((D, tn), BF16), pltpu.VMEM((D, tn), BF16),
                            pltpu.SemaphoreType.DMA((2,))],
        ),
        out_shape=jax.ShapeDtypeStruct((M, F), BF16),
        compiler_params=_params(("arbitrary", "arbitrary"), est),
        name=name,
    )(tile_expert, n_active, first, nxt, valid_rows, a, wg, wu)


def _down_body(te_ref, na_ref, first_ref, next_ref, nv_ref, a_ref, w_ref, o_ref, wf_ref, wb_ref, sem):
    _stream_weights(te_ref, first_ref, next_ref, (w_ref,), (wf_ref,), (wb_ref,), sem, o_ref.shape[1])

    def compute(rows):
        o_ref[rows, :] = jnp.dot(a_ref[rows, :], wb_ref[...], preferred_element_type=F32)

    _row_parts(nv_ref, o_ref, compute)


def _grouped_down(a, wd, tile_expert, n_active, valid_rows, *, tm, tn, name):
    M, F = a.shape
    D = wd.shape[2]
    first, nxt = _segments(tile_expert, n_active)
    est = 2 * (tm * F * 2 + tm * tn * 4) + F * tn * (4 + 2) + 2 * tm * tn * 4
    return pl.pallas_call(
        _down_body,
        grid_spec=pltpu.PrefetchScalarGridSpec(
            num_scalar_prefetch=5,
            grid=(D // tn, M // tm),
            in_specs=[pl.BlockSpec((tm, F), lambda n, m, *_: (m, 0)), pl.BlockSpec(memory_space=pl.ANY)],
            out_specs=pl.BlockSpec((tm, tn), lambda n, m, *_: (m, n)),
            scratch_shapes=[pltpu.VMEM((F, tn), F32), pltpu.VMEM((F, tn), BF16), pltpu.SemaphoreType.DMA((1,))],
        ),
        out_shape=jax.ShapeDtypeStruct((M, D), F32),
        compiler_params=_params(("arbitrary", "arbitrary"), est),
        name=name,
    )(tile_expert, n_active, first, nxt, valid_rows, a, wd)


GATHER_UNROLL = 8


def _gather_body(idx_ref, src_ref, o_ref, buf_ref, sem, *, tg):
    step = pl.program_id(0)

    def row_copy(slot, i, row):
        return pltpu.make_async_copy(src_ref.at[pl.ds(row, 1)], buf_ref.at[slot, pl.ds(i, 1)], sem.at[slot])

    def issue(s):
        slot = s % 2

        def start(i, c):
            row_copy(slot, i, idx_ref[s * tg + i]).start()
            return c

        lax.fori_loop(0, tg, start, 0, unroll=GATHER_UNROLL)

    @pl.when(step == 0)
    def _():
        issue(step)

    @pl.when(step + 1 < pl.num_programs(0))
    def _():
        issue(step + 1)

    slot = step % 2

    def wait(i, c):
        row_copy(slot, i, 0).wait()
        return c

    lax.fori_loop(0, tg, wait, 0, unroll=GATHER_UNROLL)
    o_ref[...] = buf_ref[slot].astype(o_ref.dtype)


def _gather_rows(src, idx, out_dtype, tg=256):
    P = idx.shape[0]
    D = src.shape[1]
    tg = _tile(P, tg, 16)
    est = 2 * tg * D * 4 + 2 * tg * D * 2
    return pl.pallas_call(
        functools.partial(_gather_body, tg=tg),
        grid_spec=pltpu.PrefetchScalarGridSpec(
            num_scalar_prefetch=1,
            grid=(P // tg,),
            in_specs=[pl.BlockSpec(memory_space=pl.ANY)],
            out_specs=pl.BlockSpec((tg, D), lambda i, idx: (i, 0)),
            scratch_shapes=[pltpu.VMEM((2, tg, D), src.dtype), pltpu.SemaphoreType.DMA((2,))],
        ),
        out_shape=jax.ShapeDtypeStruct((P, D), out_dtype),
        compiler_params=_params(("arbitrary",), est),
        name="moe_dispatch_gather",
    )(idx, src)


def _combine_body(slot_ref, x_ref, g_ref, y_ref, o_ref, buf_ref, sem, *, tc):
    step = pl.program_id(0)

    def row_copy(slot, i, k, row):
        return pltpu.make_async_copy(y_ref.at[pl.ds(row, 1)], buf_ref.at[slot, k, pl.ds(i, 1)], sem.at[slot])

    def issue(s):
        slot = s % 2

        def start(i, c):
            for k in range(TOP_K):
                row_copy(slot, i, k, slot_ref[(s * tc + i) * TOP_K + k]).start()
            return c

        lax.fori_loop(0, tc, start, 0, unroll=GATHER_UNROLL // TOP_K)

    @pl.when(step == 0)
    def _():
        issue(step)

    @pl.when(step + 1 < pl.num_programs(0))
    def _():
        issue(step + 1)

    slot = step % 2

    def wait(i, c):
        for k in range(TOP_K):
            row_copy(slot, i, k, 0).wait()
        return c

    lax.fori_loop(0, tc, wait, 0, unroll=GATHER_UNROLL // TOP_K)
    g = g_ref[...]
    o_ref[...] = x_ref[...] + (g[:, 0:1] * buf_ref[slot, 0] + g[:, 1:2] * buf_ref[slot, 1])


def _moe_combine(x, y, slots, gates, tc=128):
    M, D = x.shape
    tc = _tile(M, tc, 8)
    est = 4 * tc * D * 4 + 2 * TOP_K * tc * D * 4
    return pl.pallas_call(
        functools.partial(_combine_body, tc=tc),
        grid_spec=pltpu.PrefetchScalarGridSpec(
            num_scalar_prefetch=1,
            grid=(M // tc,),
            in_specs=[pl.BlockSpec((tc, D), lambda i, s: (i, 0)), pl.BlockSpec((tc, TOP_K), lambda i, s: (i, 0)),
                      pl.BlockSpec(memory_space=pl.ANY)],
            out_specs=pl.BlockSpec((tc, D), lambda i, s: (i, 0)),
            scratch_shapes=[pltpu.VMEM((2, TOP_K, tc, D), F32), pltpu.SemaphoreType.DMA((2,))],
        ),
        out_shape=jax.ShapeDtypeStruct((M, D), F32),
        compiler_params=_params(("arbitrary",), est),
        name="moe_combine",
    )(slots.reshape(-1), x, gates, y)


def _diff_attn_body(q_ref, k_ref, vt_ref, lq1_ref, lk1_ref, lq2_ref, lk2_ref, g_ref, o_ref, s_ref, acc_ref,
                    *, tq, hg, lam_init):
    qi = pl.program_id(2)
    w = 2 * HEAD_DIM
    cols = [slice(hh * w + mi * HEAD_DIM, hh * w + (mi + 1) * HEAD_DIM) for hh in range(hg) for mi in range(2)]
    qs = [q_ref[:, c] for c in cols]

    def step(n, carry, mask):
        off = pl.multiple_of(n * tq, tq)
        sts = [_nt(k_ref[pl.ds(off, tq), col_sl], qs[c]) for c, col_sl in enumerate(cols)]
        if mask is not None:
            sts = [jnp.where(mask, st, NEG) for st in sts]
        return _softmax_steps(sts, carry, s_ref, acc_ref, lambda c: vt_ref[c // 2, n])

    acc_ref[...] = jnp.zeros(acc_ref.shape, F32)
    carry = lax.fori_loop(0, qi, lambda n, c: step(n, c, None), _init_carry(len(cols), tq))
    key = lax.broadcasted_iota(I32, (tq, tq), 0)
    qry = lax.broadcasted_iota(I32, (tq, tq), 1)
    step(qi, carry, key <= qry)
    lam = (jnp.exp(jnp.sum(lq1_ref[...] * lk1_ref[...], axis=-1, keepdims=True))
           - jnp.exp(jnp.sum(lq2_ref[...] * lk2_ref[...], axis=-1, keepdims=True)) + lam_init)
    for hh in range(hg):
        y = (_normalized(acc_ref, 2 * hh, w) - lam * _normalized(acc_ref, 2 * hh + 1, w)).T
        ms = jnp.mean(y * y, axis=-1, keepdims=True)
        hs = slice(hh * w, (hh + 1) * w)
        o_ref[:, hs] = (y * lax.rsqrt(ms + EPS) * g_ref[...] * (1.0 - lam_init)).astype(o_ref.dtype)


def _diff_attention(qkv, vt, n_heads, lam_params, subln, lam_init, tq, hg=4):
    B, S, _ = qkv.shape
    w = 2 * HEAD_DIM
    hg = math.gcd(hg, n_heads)
    ng = n_heads // hg
    vec = pl.BlockSpec((1, HEAD_DIM), lambda b, g, i: (0, 0))
    est = 2 * (2 * tq * hg * w * 2 + 2 * S * hg * w * 2) + 2 * hg * (3 * tq * tq * 4 + tq * w * 4)
    return pl.pallas_call(
        functools.partial(_diff_attn_body, tq=tq, hg=hg, lam_init=lam_init),
        grid=(B, ng, S // tq),
        in_specs=[pl.BlockSpec((None, tq, hg * w), lambda b, g, i: (b, i, g)),
                  pl.BlockSpec((None, S, hg * w), lambda b, g, i: (b, 0, ng + g)),
                  pl.BlockSpec((None, hg, S // tq, w + ONES_ROWS, tq), lambda b, g, i: (b, g, 0, 0, 0)),
                  vec, vec, vec, vec,
                  pl.BlockSpec((1, w), lambda b, g, i: (0, 0))],
        out_specs=pl.BlockSpec((None, tq, hg * w), lambda b, g, i: (b, i, g)),
        out_shape=jax.ShapeDtypeStruct((B, S, n_heads * w), BF16),
        scratch_shapes=[pltpu.VMEM((2 * hg, tq, tq), F32), pltpu.VMEM((2 * hg, w + ONES_ROWS, tq), F32)],
        compiler_params=_params(("parallel", "parallel", "arbitrary"), est),
        name="diff_attention",
    )(qkv, qkv, vt, *[p.reshape(1, HEAD_DIM).astype(F32) for p in lam_params],
      subln.reshape(1, w).astype(F32))


def _moba_body(q_ref, k_ref, vt_ref, o_ref, kmean_ref, bias_ref, s_ref, acc_ref, *, nb, hg):
    qi = pl.program_id(2)
    blk = MOBA_BLOCK
    d = HEAD_DIM
    heads = [slice(hh * d, (hh + 1) * d) for hh in range(hg)]

    @pl.when(qi == 0)
    def _():
        for hh, hs in enumerate(heads):
            for n in range(nb):
                kb = k_ref[n * blk:(n + 1) * blk, hs].astype(F32)
                kmean_ref[hh, n:n + 1, :] = jnp.mean(kb, axis=0, keepdims=True)

    blk_id = lax.broadcasted_iota(I32, (nb, blk), 0)
    qs = []
    for hh, hs in enumerate(heads):
        q = q_ref[:, hs]
        km = kmean_ref[hh]
        km_hi = km.astype(BF16)
        km_lo = (km - km_hi.astype(F32)).astype(BF16)
        gate = _nt(km_hi, q) + _nt(km_lo, q)
        gate = jnp.where(blk_id < qi, gate, -jnp.inf)
        sel = jnp.zeros(gate.shape, jnp.bool_)
        for _ in range(MOBA_TOPK):
            mx = jnp.max(gate, axis=0, keepdims=True)
            idx = jnp.min(jnp.where(gate == mx, blk_id, nb), axis=0, keepdims=True)
            sel = sel | ((blk_id == idx) & (mx > -jnp.inf))
            gate = jnp.where(blk_id == idx, -jnp.inf, gate)
        qs.append(q)
        bias_ref[hh] = jnp.where(sel, 0.0, NEG)

    def step(n, carry, mask):
        off = pl.multiple_of(n * blk, blk)
        sts = [_nt(k_ref[pl.ds(off, blk), hs], qs[hh]) for hh, hs in enumerate(heads)]
        if mask is None:
            sts = [st + bias_ref[hh, pl.ds(n, 1), :] for hh, st in enumerate(sts)]
        else:
            sts = [jnp.where(mask, st, NEG) for st in sts]
        return _softmax_steps(sts, carry, s_ref, acc_ref, lambda hh: vt_ref[hh, n])

    acc_ref[...] = jnp.zeros(acc_ref.shape, F32)
    carry = lax.fori_loop(0, qi, lambda n, c: step(n, c, None), _init_carry(hg, blk))
    key = lax.broadcasted_iota(I32, (blk, blk), 0)
    qry = lax.broadcasted_iota(I32, (blk, blk), 1)
    step(qi, carry, key <= qry)
    for hh, hs in enumerate(heads):
        o_ref[:, hs] = _normalized(acc_ref, hh, d).T.astype(o_ref.dtype)


def _moba_attention(qkv, vt, n_heads, q_col, k_col, hg=8):
    B, S, _ = qkv.shape
    assert S % MOBA_BLOCK == 0 and S // MOBA_BLOCK >= MOBA_TOPK
    nb = S // MOBA_BLOCK
    d = HEAD_DIM
    hg = math.gcd(math.gcd(hg, n_heads), math.gcd(q_col, k_col))
    est = 2 * (2 * MOBA_BLOCK * hg * d * 2 + 2 * S * hg * d * 2) + hg * (3 * MOBA_BLOCK * MOBA_BLOCK * 4)
    return pl.pallas_call(
        functools.partial(_moba_body, nb=nb, hg=hg),
        grid=(B, n_heads // hg, nb),
        in_specs=[pl.BlockSpec((None, MOBA_BLOCK, hg * d), lambda b, g, i: (b, i, q_col // hg + g)),
                  pl.BlockSpec((None, S, hg * d), lambda b, g, i: (b, 0, k_col // hg + g)),
                  pl.BlockSpec((None, hg, nb, d + ONES_ROWS, MOBA_BLOCK), lambda b, g, i: (b, g, 0, 0, 0))],
        out_specs=pl.BlockSpec((None, MOBA_BLOCK, hg * d), lambda b, g, i: (b, i, g)),
        out_shape=jax.ShapeDtypeStruct((B, S, n_heads * d), BF16),
        scratch_shapes=[pltpu.VMEM((hg, nb, d), F32), pltpu.VMEM((hg, nb, MOBA_BLOCK), F32),
                        pltpu.VMEM((hg, MOBA_BLOCK, MOBA_BLOCK), F32),
                        pltpu.VMEM((hg, d + ONES_ROWS, MOBA_BLOCK), F32)],
        compiler_params=_params(("parallel", "parallel", "arbitrary"), est),
        name="moba_attention",
    )(qkv, qkv, vt)


def _xattn_body(q_ref, k_ref, v_ref, o_ref, *, n_heads):
    for h in range(n_heads):
        sl = slice(h * HEAD_DIM, (h + 1) * HEAD_DIM)
        s = _nt(q_ref[:, sl], k_ref[:, sl])
        m = jnp.max(s, axis=-1, keepdims=True)
        p = jnp.exp2(s - m)
        l = jnp.sum(p, axis=-1, keepdims=True)
        o = jnp.dot(p.astype(BF16), v_ref[:, sl], preferred_element_type=F32)
        o_ref[:, sl] = (o / l).astype(o_ref.dtype)


def _cross_attention(q, kv, n_heads, tq=512):
    B, S, X = q.shape
    M = kv.shape[1]
    tq = _tile(S, tq, 16)
    est = 2 * (2 * tq * X * 2 + 2 * M * X * 2) + 6 * tq * M * 4
    return pl.pallas_call(
        functools.partial(_xattn_body, n_heads=n_heads),
        grid=(B, S // tq),
        in_specs=[pl.BlockSpec((None, tq, X), lambda b, i: (b, i, 0)),
                  pl.BlockSpec((None, M, X), lambda b, i: (b, 0, 0)),
                  pl.BlockSpec((None, M, X), lambda b, i: (b, 0, 1))],
        out_specs=pl.BlockSpec((None, tq, X), lambda b, i: (b, i, 0)),
        out_shape=jax.ShapeDtypeStruct((B, S, X), BF16),
        compiler_params=_params(("parallel", "parallel"), est),
        name="cross_attention",
    )(q, kv, kv)


def _odd_prep_body(x_ref, gqa_ref, gkv_ref, gkr_ref, gik_ref, cos_ref, sin_ref,
                   qa_ref, ckv_ref, kr_ref, ik_ref, iw_ref, *, c1, c2, iw_scale):
    def norm(x, g):
        ms = jnp.mean(x * x, axis=-1, keepdims=True)
        return x * lax.rsqrt(ms + EPS) * g

    def rope(y):
        lane = lax.broadcasted_iota(I32, y.shape, 1)
        half = C_ROPE // 2
        first = (lane % C_ROPE) < half
        r = jnp.where(first, pltpu.roll(y, LANE - half, axis=1), pltpu.roll(y, half, axis=1))
        return y * cos_ref[...] + r * sin_ref[...]

    qa_ref[...] = norm(x_ref[:, :c1], gqa_ref[...]).astype(qa_ref.dtype)
    ckv_ref[...] = norm(x_ref[:, c1:c2], gkv_ref[...]).astype(ckv_ref.dtype)
    slab_a = x_ref[:, c2:c2 + LANE]
    slab_b = x_ref[:, c2 + LANE:c2 + 2 * LANE]
    lane = lax.broadcasted_iota(I32, slab_a.shape, 1)
    low = lane < C_ROPE
    ms = jnp.sum(jnp.where(low, slab_a * slab_a, 0.0), axis=-1, keepdims=True) / C_ROPE
    kr = rope(slab_a * lax.rsqrt(ms + EPS) * gkr_ref[...])
    kr_ref[0] = kr.astype(kr_ref.dtype)
    kr_ref[1] = pltpu.roll(kr, C_ROPE, axis=1).astype(kr_ref.dtype)
    rot_a = pltpu.roll(slab_a, C_ROPE, axis=1)
    rot_b = pltpu.roll(slab_b, C_ROPE, axis=1)
    ik = jnp.where(low, rot_a, rot_b)
    ik_ref[...] = rope(norm(ik, gik_ref[...])).astype(ik_ref.dtype)
    iw_ref[...] = rot_b * iw_scale


def _odd_prep(x, c1, c2, g_qa, g_kv, g_kr, g_ik, cos_p, sin_p, iw_scale, tm=256):
    B, S, C = x.shape
    assert C == c2 + 2 * LANE
    tm = _tile(S, tm, 16)
    vec = lambda n: pl.BlockSpec((1, n), lambda b, i: (0, 0))
    tab = pl.BlockSpec((tm, LANE), lambda b, i: (i, 0))
    est = 2 * tm * C * 4 * 2
    return pl.pallas_call(
        functools.partial(_odd_prep_body, c1=c1, c2=c2, iw_scale=iw_scale),
        grid=(B, S // tm),
        in_specs=[pl.BlockSpec((None, tm, C), lambda b, i: (b, i, 0)),
                  vec(c1), vec(c2 - c1), vec(LANE), vec(LANE), tab, tab],
        out_specs=[pl.BlockSpec((None, tm, c1), lambda b, i: (b, i, 0)),
                   pl.BlockSpec((None, tm, c2 - c1), lambda b, i: (b, i, 0)),
                   pl.BlockSpec((None, 2, tm, LANE), lambda b, i: (b, 0, i, 0)),
                   pl.BlockSpec((None, tm, LANE), lambda b, i: (b, i, 0)),
                   pl.BlockSpec((None, tm, LANE), lambda b, i: (b, i, 0))],
        out_shape=[jax.ShapeDtypeStruct((B, S, c1), BF16),
                   jax.ShapeDtypeStruct((B, S, c2 - c1), BF16),
                   jax.ShapeDtypeStruct((B, 2, S, LANE), BF16),
                   jax.ShapeDtypeStruct((B, S, LANE), BF16),
                   jax.ShapeDtypeStruct((B, S, LANE), F32)],
        compiler_params=_params(("parallel", "parallel"), est),
        name="odd_prep",
    )(x, g_qa.reshape(1, -1), g_kv.reshape(1, -1), g_kr.reshape(1, -1), g_ik.reshape(1, -1), cos_p, sin_p)


def _q_prep_body(x_ref, gn_ref, gr_ref, cos_ref, sin_ref, qn_ref, qr_ref, *, n_heads):
    nope_w = n_heads * C_NOPE
    width = C_NOPE + C_ROPE
    half = C_ROPE // 2
    lane = lax.broadcasted_iota(I32, (x_ref.shape[0], LANE), 1)
    low = lane < C_ROPE
    first = (lane % C_ROPE) < half
    for p in range(n_heads // 2):
        n0 = x_ref[:, (2 * p) * LANE:(2 * p + 1) * LANE]
        n1 = x_ref[:, (2 * p + 1) * LANE:(2 * p + 2) * LANE]
        r = x_ref[:, nope_w + p * LANE:nope_w + (p + 1) * LANE]
        r2 = r * r
        ss0 = jnp.sum(n0 * n0, axis=-1, keepdims=True) + jnp.sum(jnp.where(low, r2, 0.0), axis=-1, keepdims=True)
        ss1 = jnp.sum(n1 * n1, axis=-1, keepdims=True) + jnp.sum(jnp.where(low, 0.0, r2), axis=-1, keepdims=True)
        inv0 = lax.rsqrt(ss0 / width + EPS)
        inv1 = lax.rsqrt(ss1 / width + EPS)
        qn_ref[:, (2 * p) * LANE:(2 * p + 1) * LANE] = (n0 * inv0 * gn_ref[...]).astype(qn_ref.dtype)
        qn_ref[:, (2 * p + 1) * LANE:(2 * p + 2) * LANE] = (n1 * inv1 * gn_ref[...]).astype(qn_ref.dtype)
        y = r * jnp.where(low, inv0, inv1) * gr_ref[...]
        rot = jnp.where(first, pltpu.roll(y, LANE - half, axis=1), pltpu.roll(y, half, axis=1))
        qr_ref[:, p * LANE:(p + 1) * LANE] = (y * cos_ref[...] + rot * sin_ref[...]).astype(qr_ref.dtype)


def _q_prep(x, n_heads, g_nope, g_rope2, cos_q, sin_q, tm=256):
    B, S, C = x.shape
    tm = _tile(S, tm, 16)
    nope_w, rope_w = n_heads * C_NOPE, n_heads * C_ROPE
    vec = pl.BlockSpec((1, LANE), lambda b, i: (0, 0))
    tab = pl.BlockSpec((tm, LANE), lambda b, i: (i, 0))
    est = 2 * tm * C * 6
    return pl.pallas_call(
        functools.partial(_q_prep_body, n_heads=n_heads),
        grid=(B, S // tm),
        in_specs=[pl.BlockSpec((None, tm, C), lambda b, i: (b, i, 0)), vec, vec, tab, tab],
        out_specs=[pl.BlockSpec((None, tm, nope_w), lambda b, i: (b, i, 0)),
                   pl.BlockSpec((None, tm, rope_w), lambda b, i: (b, i, 0))],
        out_shape=[jax.ShapeDtypeStruct((B, S, nope_w), BF16), jax.ShapeDtypeStruct((B, S, rope_w), BF16)],
        compiler_params=_params(("parallel", "parallel"), est),
        name="dsa_q_prep",
    )(x, g_nope.reshape(1, LANE), g_rope2.reshape(1, LANE), cos_q, sin_q)


def _indexer_body(ik_ref, iq_ref, iwt_ref, o_ref, key_ref, *, tq, n_heads, n_keep):
    S = ik_ref.shape[0]
    qi = pl.program_id(1)
    n_tiles = qi + 1
    t_idx = qi * tq + lax.broadcasted_iota(I32, (tq, tq), 1)
    s_loc = lax.broadcasted_iota(I32, (tq, tq), 0)

    def score_tile(kt, c):
        off = pl.multiple_of(kt * tq, tq)
        ikt = ik_ref[pl.ds(off, tq), :]
        acc = jnp.zeros((tq, tq), F32)
        for h in range(n_heads):
            r = _nt(ikt, iq_ref[:, h * IDX_DIM:(h + 1) * IDX_DIM])
            acc = acc + jnp.maximum(r, 0.0) * iwt_ref[h:h + 1, :]
        acc = jnp.where(off + s_loc <= t_idx, acc, -jnp.inf)
        bits = lax.bitcast_convert_type(acc, I32)
        key_ref[pl.ds(off, tq), :] = bits ^ ((bits >> 31) & 0x7FFFFFFF)
        return c

    lax.fori_loop(0, n_tiles, score_tile, 0)

    def count(hit):
        def body(kt, cnt):
            off = pl.multiple_of(kt * tq, tq)
            one = jnp.where(hit(key_ref[pl.ds(off, tq), :], off + s_loc), 1, 0).astype(I32)
            return cnt + jnp.sum(one.reshape(tq // 8, 8, tq), axis=0)
        cnt = lax.fori_loop(0, n_tiles, body, jnp.zeros((8, tq), I32))
        return jnp.sum(cnt, axis=0, keepdims=True)

    n_pos = count(lambda k, s: k >= 0)
    thr = jnp.where(n_pos >= n_keep, 0, INT_MIN).astype(I32)

    def bit_step(i, thr):
        cand = thr + lax.shift_left(jnp.int32(1), 30 - i)
        return jnp.where(count(lambda k, s: k >= cand) >= n_keep, cand, thr)

    thr = lax.fori_loop(0, 31, bit_step, thr)

    def tie_cut():
        need = n_keep - count(lambda k, s: k > thr)
        bits = S.bit_length()

        def step(i, cut):
            cand = cut + lax.shift_left(jnp.int32(1), bits - 1 - i)
            below = count(lambda k, s: (k == thr) & (s < cand))
            return jnp.where(below < need, cand, cut)

        return lax.fori_loop(0, bits, step, jnp.zeros((1, tq), I32))

    has_ties = jnp.max(count(lambda k, s: k >= thr)) > n_keep
    cut = lax.cond(has_ties, tie_cut, lambda: jnp.full((1, tq), S, I32))

    def out_tile(kt, c):
        off = pl.multiple_of(kt * tq, tq)
        key = key_ref[pl.ds(off, tq), :]
        s_idx = off + s_loc
        ok = ((key > thr) | ((key == thr) & (s_idx <= cut))) & (s_idx <= t_idx)
        o_ref[pl.ds(off, tq), :] = jnp.where(ok, 0.0, NEG).astype(o_ref.dtype)
        return c

    def neg_tile(kt, c):
        off = pl.multiple_of(kt * tq, tq)
        o_ref[pl.ds(off, tq), :] = jnp.full((tq, tq), NEG, o_ref.dtype)
        return c

    lax.fori_loop(0, n_tiles, out_tile, 0)
    lax.fori_loop(n_tiles, S // tq, neg_tile, 0)


def _indexer(ik, iq, iwt, n_keep, tq=256):
    B, S, _ = ik.shape
    n_heads = iwt.shape[1]
    tq = _tile(S, tq, LANE)
    est = 2 * (S * IDX_DIM * 2 + tq * n_heads * IDX_DIM * 2 + n_heads * tq * 4 + S * tq * 2) + S * tq * 4 + 8 * tq * tq * 4
    return pl.pallas_call(
        functools.partial(_indexer_body, tq=tq, n_heads=n_heads, n_keep=n_keep),
        grid=(B, S // tq),
        in_specs=[pl.BlockSpec((None, S, IDX_DIM), lambda b, i: (b, 0, 0)),
                  pl.BlockSpec((None, tq, n_heads * IDX_DIM), lambda b, i: (b, i, 0)),
                  pl.BlockSpec((None, n_heads, tq), lambda b, i: (b, 0, i))],
        out_specs=pl.BlockSpec((None, S, tq), lambda b, i: (b, 0, i)),
        out_shape=jax.ShapeDtypeStruct((B, S, S), BF16),
        scratch_shapes=[pltpu.VMEM((S, tq), I32)],
        compiler_params=_params(("parallel", "arbitrary"), est),
        name="dsa_indexer",
    )(ik, iq, iwt)


def _dsa_attn_body(qn_ref, qr_ref, kn_ref, kr_ref, vt_ref, bias_ref, o_ref, s_ref, acc_ref, *, tq, hg):
    qi = pl.program_id(2)
    heads = [slice(hh * C_NOPE, (hh + 1) * C_NOPE) for hh in range(hg)]
    qs = [jnp.concatenate([qn_ref[:, hs], qr_ref[:, (hh // 2) * LANE:(hh // 2 + 1) * LANE]], axis=1)
          for hh, hs in enumerate(heads)]

    def scores(n):
        off = pl.multiple_of(n * tq, tq)
        bias = bias_ref[pl.ds(off, tq), :].astype(F32)
        kr = [kr_ref[par, pl.ds(off, tq), :] for par in range(2)]
        return tuple(bias + _nt(jnp.concatenate([kn_ref[pl.ds(off, tq), hs], kr[hh % 2]], axis=1), qs[hh])
                     for hh, hs in enumerate(heads))

    def body(n, carry):
        return _softmax_steps(scores(n), carry, s_ref, acc_ref, lambda hh: vt_ref[hh, n])

    acc_ref[...] = jnp.zeros(acc_ref.shape, F32)
    lax.fori_loop(0, qi + 1, body, _init_carry(hg, tq))
    for hh, hs in enumerate(heads):
        o_ref[:, hs] = _normalized(acc_ref, hh, C_NOPE).T.astype(o_ref.dtype)


def _dsa_attention(qn, qr, kv, kr2, vt, bias, n_heads, tq, hg=8):
    B, S, _ = qn.shape
    hg = min(hg, n_heads)
    assert hg % 2 == 0 and n_heads % hg == 0
    ng = n_heads // hg
    nk = S // tq
    est = 2 * (tq * hg * 192 * 2 + 2 * S * hg * LANE * 2 + 2 * S * LANE * 2 + S * tq * 2 + tq * hg * LANE * 2) \
        + hg * 3 * tq * tq * 4
    return pl.pallas_call(
        functools.partial(_dsa_attn_body, tq=tq, hg=hg),
        grid=(B, ng, S // tq),
        in_specs=[pl.BlockSpec((None, tq, hg * C_NOPE), lambda b, g, i: (b, i, g)),
                  pl.BlockSpec((None, tq, hg * C_ROPE), lambda b, g, i: (b, i, g)),
                  pl.BlockSpec((None, S, hg * C_NOPE), lambda b, g, i: (b, 0, g)),
                  pl.BlockSpec((None, 2, S, LANE), lambda b, g, i: (b, 0, 0, 0)),
                  pl.BlockSpec((None, hg, nk, C_NOPE + ONES_ROWS, tq), lambda b, g, i: (b, g, 0, 0, 0)),
                  pl.BlockSpec((None, S, tq), lambda b, g, i: (b, 0, i))],
        out_specs=pl.BlockSpec((None, tq, hg * C_NOPE), lambda b, g, i: (b, i, g)),
        out_shape=jax.ShapeDtypeStruct((B, S, n_heads * C_NOPE), BF16),
        scratch_shapes=[pltpu.VMEM((hg, tq, tq), F32), pltpu.VMEM((hg, C_NOPE + ONES_ROWS, tq), F32)],
        compiler_params=_params(("parallel", "parallel", "arbitrary"), est),
        name="dsa_attention",
    )(qn, qr, kv, kr2, vt, bias)


def _rope_tables(seq, dim):
    inv_freq = ROPE_THETA ** (-jnp.arange(0, dim, 2, dtype=F32) / dim)
    ang = jnp.arange(seq, dtype=F32)[:, None] * inv_freq[None, :]
    return jnp.cos(ang), jnp.sin(ang)


def _cross_block(x, mem, l, norm_xattn, norm_mem, xa_wq, xa_wk, xa_wv, xa_wo, xa_qnorm, xa_knorm, B, S):
    N, D = x.shape
    X = xa_wq.shape[2]
    n_heads = X // HEAD_DIM
    M = mem.shape[1]
    h = _rmsnorm(x, norm_xattn[l], BF16)
    mn = _rmsnorm(mem.reshape(B * M, D), norm_mem[l], BF16)
    ones = jnp.ones((S, LANE), F32)
    q = _matmul(h, xa_wq[l].astype(BF16), out_dtype=BF16, name="xattn_q",
                epi=dict(flags=[EPI_NORM] * (X // _tile(X, 1024, LANE)), modes=(EPI_NORM,), rope_half=0,
                         tn=_tile(X, 1024, LANE), cos=ones, sin=ones,
                         gain=jnp.tile(xa_qnorm[l] * (HEAD_DIM ** -0.5 * LOG2E), n_heads)))
    wkv = jnp.concatenate([xa_wk[l], xa_wv[l]], axis=1).astype(BF16)
    tn = _tile(X, 1024, LANE)
    kv = _matmul(mn, wkv, out_dtype=BF16, name="xattn_kv",
                 epi=dict(flags=[EPI_NORM] * (X // tn) + [EPI_NONE] * (X // tn), modes=(EPI_NORM,), rope_half=0,
                          tn=tn, cos=jnp.ones((M, LANE), F32), sin=jnp.ones((M, LANE), F32),
                          gain=jnp.concatenate([jnp.tile(xa_knorm[l], n_heads), jnp.ones((X,), F32)])))
    o = _cross_attention(q.reshape(B, S, X), kv.reshape(B, M, 2 * X), n_heads)
    return _matmul(o.reshape(N, X), xa_wo[l].astype(BF16), out_dtype=F32, name="xattn_out", resid=x)


def _even_mixer(x, l, i, B, S, norm_mix, ev_w_in, ev_a_qnorm, ev_a_knorm, lam_params, ev_a_subln,
                ev_b_qnorm, ev_b_knorm, ev_w_out):
    N, D = x.shape
    a_heads = D // (4 * HEAD_DIM)
    b_heads = D // (2 * HEAD_DIM)
    aw = a_heads * 2 * HEAD_DIM
    bw = b_heads * HEAD_DIM
    width = 3 * aw + 3 * bw
    scale = HEAD_DIM ** -0.5 * LOG2E
    cos, sin = _rope_tables(S, HEAD_DIM)
    cos2 = jnp.concatenate([cos, cos], axis=-1)
    sin2 = jnp.concatenate([-sin, sin], axis=-1)
    tn = _tile(math.gcd(aw, bw), 1024, LANE)
    seg = [(aw, EPI_NORM_ROPE, ev_a_qnorm[i] * scale), (aw, EPI_NORM_ROPE, ev_a_knorm[i]), (aw, EPI_NONE, None),
           (bw, EPI_NORM_ROPE, ev_b_qnorm[i] * scale), (bw, EPI_NORM_ROPE, ev_b_knorm[i]), (bw, EPI_NONE, None)]
    flags, gains = [], []
    for w, flag, g in seg:
        flags += [flag] * (w // tn)
        gains.append(jnp.ones((w,), F32) if g is None else jnp.tile(g.astype(F32), w // HEAD_DIM))
    h = _rmsnorm(x, norm_mix[l], BF16)
    qkv = _matmul(h, ev_w_in[i], out_dtype=BF16, name="even_in",
                  epi=dict(flags=flags, modes=(EPI_NORM_ROPE,), rope_half=HEAD_DIM // 2, tn=tn,
                           cos=cos2, sin=sin2, gain=jnp.concatenate(gains)))
    qkv = qkv.reshape(B, S, width)
    lam_init = 0.8 - 0.6 * math.exp(-0.3 * l)
    tq = _tile(S, 256, LANE)
    ya = _diff_attention(qkv, _value_tiles(qkv[:, :, 2 * aw:3 * aw], a_heads, tq), a_heads, lam_params,
                         ev_a_subln[i], lam_init, tq)
    c0 = 3 * aw // HEAD_DIM
    yb = _moba_attention(qkv, _value_tiles(qkv[:, :, 3 * aw + 2 * bw:], b_heads, MOBA_BLOCK), b_heads,
                         c0, c0 + b_heads)
    y = jnp.concatenate([ya, yb], axis=-1).reshape(N, aw + bw)
    return _matmul(y, ev_w_out[i].astype(BF16), out_dtype=F32, name="even_out", resid=x)


def _odd_mixer(x, l, i, B, S, norm_mix, od_w_in, od_qa_norm, od_w_qb, od_q_norm, od_kv_norm, od_kr_norm,
               od_w_uk, od_w_uv, od_w_iqb, od_ik_norm, od_w_out):
    N, D = x.shape
    c1 = od_qa_norm.shape[1]
    kv_rank = od_kv_norm.shape[1]
    c2 = c1 + kv_rank
    n_heads = od_w_uk.shape[2]
    idx_heads = od_w_iqb.shape[2] // IDX_DIM
    assert od_w_in.shape[2] == c2 + C_ROPE + IDX_DIM + idx_heads and idx_heads == C_ROPE
    scale = (C_NOPE + C_ROPE) ** -0.5 * LOG2E
    cos, sin = _rope_tables(S, C_ROPE)
    one, zero = jnp.ones((S, C_ROPE), F32), jnp.zeros((S, C_ROPE), F32)
    cos_p = jnp.concatenate([cos, cos, one], axis=-1)
    sin_p = jnp.concatenate([-sin, sin, zero], axis=-1)
    cos_q = jnp.concatenate([cos, cos, cos, cos], axis=-1)
    sin_q = jnp.concatenate([-sin, sin, -sin, sin], axis=-1)

    h = _rmsnorm(x, norm_mix[l], BF16)
    proj = _matmul(h, od_w_in[i], out_dtype=F32, name="odd_in")
    g_kr = jnp.concatenate([od_kr_norm[i], jnp.zeros((LANE - C_ROPE,), F32)])
    qa, ckv, kr2, ik, iw = _odd_prep(proj.reshape(B, S, -1), c1, c2, od_qa_norm[i], od_kv_norm[i], g_kr,
                                     od_ik_norm[i], cos_p, sin_p, idx_heads ** -0.5 * IDX_DIM ** -0.5)
    qa = qa.reshape(N, c1)
    wqb = od_w_qb[i].reshape(c1, n_heads, C_NOPE + C_ROPE)
    wqb = jnp.concatenate([wqb[:, :, :C_NOPE].reshape(c1, -1), wqb[:, :, C_NOPE:].reshape(c1, -1)], axis=1)
    qraw = _matmul(qa, wqb.astype(BF16), out_dtype=F32, name="odd_qb")
    qn, qr = _q_prep(qraw.reshape(B, S, -1), n_heads, od_q_norm[i][:C_NOPE] * scale,
                     jnp.tile(od_q_norm[i][C_NOPE:], 2) * scale, cos_q, sin_q)
    tn = _tile(idx_heads * IDX_DIM, 1024, LANE)
    iq = _matmul(qa, od_w_iqb[i], out_dtype=BF16, name="odd_iqb",
                 epi=dict(flags=[EPI_ROPE] * (idx_heads * IDX_DIM // tn), modes=(EPI_ROPE,), rope_half=IDX_ROPE // 2,
                          tn=tn, cos=cos_p, sin=sin_p, gain=jnp.ones((idx_heads * IDX_DIM,), F32)))
    wkv = jnp.concatenate([od_w_uk[i].reshape(kv_rank, -1), od_w_uv[i].reshape(kv_rank, -1)], axis=1)
    kv = _matmul(ckv.reshape(N, kv_rank), wkv.astype(BF16), out_dtype=BF16, name="odd_kv")

    n_keep = min(IDX_TOPK, S // 4)
    tq = _tile(S, 256, LANE)
    iwt = jnp.swapaxes(iw[:, :, :idx_heads], 1, 2)
    bias = _indexer(ik, iq.reshape(B, S, -1), iwt, n_keep, tq)
    kv = kv.reshape(B, S, -1)
    vt = _value_tiles(kv[:, :, n_heads * C_NOPE:], n_heads, tq)
    y = _dsa_attention(qn, qr, kv, kr2, vt, bias, n_heads, tq)
    return _matmul(y.reshape(N, -1), od_w_out[i].astype(BF16), out_dtype=F32, name="odd_out", resid=x)


def _dense_ffn(x, g, wg, wu, wd):
    N, D = x.shape
    F = wg.shape[1]
    h = _rmsnorm(x, g, BF16)
    tm = _tile(N, 2048, 16)
    tn = _tile(F, 512, LANE)
    nt = N // tm
    hid = _swiglu_up(h, wg[None], wu[None], jnp.zeros((nt,), I32), jnp.full((1,), nt, I32),
                     jnp.full((nt,), tm, I32), tm=tm, tn=tn, name="ffn_up")
    return _matmul(hid, wd.astype(BF16), out_dtype=F32, name="ffn_down", resid=x, tm=512, tn=1024,
                   tk=F if F <= 4096 else _tile(F, F // 2, LANE))


def _moe_ffn(x, g, router, wg, wu, wd, tm=512):
    N, D = x.shape
    E, _, F = wg.shape
    h, gate, sel = _norm_router(x, g, router)
    tm = _tile(N, tm, 16)
    cnt = jnp.sum(sel, axis=0)
    tiles_e = (cnt + tm - 1) // tm
    tile_end = jnp.cumsum(tiles_e)
    start = (tile_end - tiles_e) * tm
    rank = jnp.cumsum(sel, axis=0) - sel
    P = N * TOP_K + E * tm
    n_tiles = P // tm
    slot = (start[None, :] + rank).astype(I32)
    lane = jnp.arange(E, dtype=I32)[None, :]
    e2 = jnp.stack([jnp.min(jnp.where(sel > 0, lane, E), axis=1), jnp.max(jnp.where(sel > 0, lane, -1), axis=1)], 1)
    slots2 = jnp.take_along_axis(slot, e2, axis=1)
    gates2 = jnp.take_along_axis(gate, e2, axis=1)
    tok2 = jnp.broadcast_to(jnp.arange(N, dtype=I32)[:, None], (N, TOP_K))
    tok_of_slot = jnp.zeros((P,), I32).at[slots2.reshape(-1)].set(tok2.reshape(-1), unique_indices=True)
    tile_expert = jnp.minimum(jnp.searchsorted(tile_end, jnp.arange(n_tiles, dtype=I32), side="right"),
                              E - 1).astype(I32)
    n_active = tile_end[-1:].astype(I32)
    tile_row = jnp.arange(n_tiles, dtype=I32) * tm
    valid_rows = jnp.where(tile_row < n_active[0] * tm,
                           jnp.clip((start + cnt)[tile_expert] - tile_row, 0, tm), 0).astype(I32)

    xs = _gather_rows(h, tok_of_slot, BF16)
    hid = _swiglu_up(xs, wg, wu, tile_expert, n_active, valid_rows, tm=tm, tn=_tile(F, 512, LANE), name="moe_up")
    y = _grouped_down(hid, wd, tile_expert, n_active, valid_rows, tm=tm, tn=_tile(D, 1024, LANE), name="moe_down")
    return _moe_combine(x, y, slots2, gates2)


def kernel(x, mem, norm_mix, norm_xattn, norm_mem, norm_ffn, ev_w_in, ev_a_qnorm, ev_a_knorm, ev_lambda_q1, ev_lambda_k1, ev_lambda_q2, ev_lambda_k2, ev_a_subln, ev_b_qnorm, ev_b_knorm, ev_w_out, od_w_in, od_qa_norm, od_w_qb, od_q_norm, od_kv_norm, od_kr_norm, od_w_uk, od_w_uv, od_w_iqb, od_ik_norm, od_w_out, xa_wq, xa_wk, xa_wv, xa_wo, xa_qnorm, xa_knorm, ffn_wg, ffn_wu, ffn_wd, moe_router, moe_wg, moe_wu, moe_wd):
    B, S, D = x.shape
    depth = norm_mix.shape[0]
    x = x.reshape(B * S, D)
    for l in range(depth):
        i = l // 2
        if l % 2 == 0:
            x = _even_mixer(x, l, i, B, S, norm_mix, ev_w_in, ev_a_qnorm, ev_a_knorm,
                            (ev_lambda_q1[i], ev_lambda_k1[i], ev_lambda_q2[i], ev_lambda_k2[i]),
                            ev_a_subln, ev_b_qnorm, ev_b_knorm, ev_w_out)
        else:
            x = _odd_mixer(x, l, i, B, S, norm_mix, od_w_in, od_qa_norm, od_w_qb, od_q_norm, od_kv_norm,
                           od_kr_norm, od_w_uk, od_w_uv, od_w_iqb, od_ik_norm, od_w_out)
        x = _cross_block(x, mem, l, norm_xattn, norm_mem, xa_wq, xa_wk, xa_wv, xa_wo, xa_qnorm, xa_knorm, B, S)
        if l % 2 == 0:
            x = _dense_ffn(x, norm_ffn[l], ffn_wg[i], ffn_wu[i], ffn_wd[i])
        else:
            x = _moe_ffn(x, norm_ffn[l], moe_router[i], moe_wg[i], moe_wu[i], moe_wd[i])
    return x.reshape(B, S, D)
```

```python
import functools
import math

import jax
import jax.numpy as jnp
import numpy as np
from jax import lax
from jax.experimental import pallas as pl
from jax.experimental.pallas import tpu as pltpu

F32 = jnp.float32
BF16 = jnp.bfloat16
I32 = jnp.int32

LANE = 128
MXU_WIDTH = 256
V7X_VMEM_BYTES = 64 * 1024 * 1024
VMEM_CAP = V7X_VMEM_BYTES - 4 * 1024 * 1024

HEAD_DIM = 128
ROPE_THETA = 10000.0
EPS = 1e-6
MOBA_BLOCK = 256
MOBA_TOPK = 3
C_NOPE = 128
C_ROPE = 64
IDX_DIM = 128
IDX_ROPE = 64
IDX_TOPK = 256
TOP_K = 2
NEG = -1e30
LOG2E = math.log2(math.e)
INT_MIN = -(2 ** 31)


def _tile(dim, pref, align):
    t = min(pref, dim)
    t -= t % align
    while t >= align:
        if dim % t == 0:
            return t
        t -= align
    return dim


def _params(sem, est_bytes):
    limit = int(min(max(est_bytes * 1.3 + (4 << 20), 32 << 20), VMEM_CAP))
    return pltpu.CompilerParams(dimension_semantics=sem, vmem_limit_bytes=limit)


def _nt(a, b):
    return lax.dot_general(a, b, (((1,), (1,)), ((), ())), preferred_element_type=F32)


def _softmax_steps(sts, carry, s_ref, acc_ref, value_tile):
    for c, st in enumerate(sts):
        s_ref[c] = st
    new = []
    for c in range(len(sts)):
        m_new = jnp.maximum(carry[c], jnp.max(s_ref[c], axis=0, keepdims=True))
        alpha = jnp.exp2(carry[c] - m_new)
        p = jnp.exp2(s_ref[c] - m_new)
        new.append(m_new)
        acc_ref[c] = alpha * acc_ref[c] + jnp.dot(value_tile(c), p.astype(BF16), preferred_element_type=F32)
    return tuple(new)


def _normalized(acc_ref, c, dv):
    acc = acc_ref[c]
    return acc[:dv] / acc[dv:dv + 1]


def _init_carry(n_chains, tq):
    return tuple(jnp.full((1, tq), NEG, F32) for _ in range(n_chains))


ONES_ROWS = 16


def _value_tiles(v, n_heads, tk):
    B, S, C = v.shape
    vt = v.reshape(B, S // tk, tk, n_heads, C // n_heads).transpose(0, 3, 1, 4, 2)
    extra = jnp.zeros(vt.shape[:3] + (ONES_ROWS, tk), v.dtype).at[..., 0, :].set(1)
    return jnp.concatenate([vt, extra], axis=3)


def _rmsnorm_body(x_ref, g_ref, o_ref):
    x = x_ref[...].astype(F32)
    ms = jnp.mean(x * x, axis=-1, keepdims=True)
    o_ref[...] = (x * lax.rsqrt(ms + EPS) * g_ref[...]).astype(o_ref.dtype)


def _rmsnorm(x, g, out_dtype, tm=256):
    M, D = x.shape
    tm = _tile(M, tm, 16)
    est = 2 * tm * D * (4 + 4)
    return pl.pallas_call(
        _rmsnorm_body,
        grid=(M // tm,),
        in_specs=[pl.BlockSpec((tm, D), lambda i: (i, 0)), pl.BlockSpec((1, D), lambda i: (0, 0))],
        out_specs=pl.BlockSpec((tm, D), lambda i: (i, 0)),
        out_shape=jax.ShapeDtypeStruct((M, D), out_dtype),
        compiler_params=_params(("parallel",), est),
        name="rmsnorm",
    )(x, g.reshape(1, D).astype(F32))


def _norm_router_body(x_ref, g_ref, rt_ref, h_ref, gate_ref, sel_ref, *, n_exp):
    x = x_ref[...]
    ms = jnp.mean(x * x, axis=-1, keepdims=True)
    h = x * lax.rsqrt(ms + EPS) * g_ref[...]
    h_ref[...] = h
    lane = lax.broadcasted_iota(I32, gate_ref.shape, 1)
    logits = jnp.full(gate_ref.shape, -jnp.inf, F32)
    for e in range(n_exp):
        col = jnp.sum(h * rt_ref[e:e + 1, :], axis=-1, keepdims=True)
        logits = jnp.where(lane == e, col, logits)
    m1 = jnp.max(logits, axis=-1, keepdims=True)
    i1 = jnp.min(jnp.where(logits == m1, lane, LANE), axis=-1, keepdims=True)
    rest = jnp.where(lane == i1, -jnp.inf, logits)
    m2 = jnp.max(rest, axis=-1, keepdims=True)
    i2 = jnp.min(jnp.where(rest == m2, lane, LANE), axis=-1, keepdims=True)
    e2 = jnp.exp(m2 - m1)
    den = 1.0 + e2
    gate_ref[...] = jnp.where(lane == i1, 1.0 / den, 0.0) + jnp.where(lane == i2, e2 / den, 0.0)
    sel_ref[...] = jnp.where((lane == i1) | (lane == i2), 1, 0).astype(I32)


def _norm_router(x, g, router, tm=256):
    M, D = x.shape
    n_exp = router.shape[1]
    tm = _tile(M, tm, 8)
    est = 2 * tm * D * 8 + 2 * 8 * D * 4
    h, gate, sel = pl.pallas_call(
        functools.partial(_norm_router_body, n_exp=n_exp),
        grid=(M // tm,),
        in_specs=[pl.BlockSpec((tm, D), lambda i: (i, 0)), pl.BlockSpec((1, D), lambda i: (0, 0)),
                  pl.BlockSpec((n_exp, D), lambda i: (0, 0))],
        out_specs=[pl.BlockSpec((tm, D), lambda i: (i, 0)), pl.BlockSpec((tm, LANE), lambda i: (i, 0)),
                   pl.BlockSpec((tm, LANE), lambda i: (i, 0))],
        out_shape=[jax.ShapeDtypeStruct((M, D), F32), jax.ShapeDtypeStruct((M, LANE), F32),
                   jax.ShapeDtypeStruct((M, LANE), I32)],
        compiler_params=_params(("parallel",), est),
        name="norm_router",
    )(x, g.reshape(1, D).astype(F32), router.T.astype(F32))
    return h, gate[:, :n_exp], sel[:, :n_exp]


EPI_NONE, EPI_NORM, EPI_NORM_ROPE, EPI_ROPE = 0, 1, 2, 3


def _head_epilogue(x, g, cos, sin, mode, rope_half):
    if mode in (EPI_NORM, EPI_NORM_ROPE):
        ss = jnp.dot((x * x).astype(BF16), jnp.ones((LANE, LANE), BF16), preferred_element_type=F32)
        x = x * lax.rsqrt(ss * (1.0 / LANE) + EPS) * g
    if mode in (EPI_NORM_ROPE, EPI_ROPE):
        if rope_half == LANE // 2:
            r = pltpu.roll(x, LANE // 2, axis=1)
        else:
            lane = lax.broadcasted_iota(I32, x.shape, 1)
            first = (lane % (2 * rope_half)) < rope_half
            r = jnp.where(first, pltpu.roll(x, LANE - rope_half, axis=1), pltpu.roll(x, rope_half, axis=1))
        x = x * cos + r * sin
    return x


def _mm_body(flags_ref, a_ref, w_ref, *rest, nk, has_resid, modes, rope_half, tn, cw):
    rest = list(rest)
    resid_ref = rest.pop(0) if has_resid else None
    gain_ref = cos_ref = sin_ref = None
    if modes:
        gain_ref, cos_ref, sin_ref = rest.pop(0), rest.pop(0), rest.pop(0)
    o_ref = rest.pop(0)
    acc_ref = rest.pop(0) if nk > 1 else None
    j = pl.program_id(1)
    k = pl.program_id(2)

    def emit(cols, acc, mode):
        if mode != EPI_NONE:
            for c in range(cols.start, cols.stop, LANE):
                sl = slice(c, c + LANE)
                y = _head_epilogue(acc[:, sl.start - cols.start:sl.stop - cols.start], gain_ref[:, sl],
                                   cos_ref[...], sin_ref[...], mode, rope_half)
                o_ref[:, sl] = y.astype(o_ref.dtype)
        elif has_resid:
            o_ref[:, cols] = (resid_ref[:, cols] + acc).astype(o_ref.dtype)
        else:
            o_ref[:, cols] = acc.astype(o_ref.dtype)

    if nk == 1:
        def run(mode):
            pending = None
            for c in range(0, tn, cw):
                cols = slice(c, c + cw)
                acc = jnp.dot(a_ref[...], w_ref[:, cols].astype(BF16), preferred_element_type=F32)
                if pending is not None:
                    emit(*pending, mode)
                pending = (cols, acc)
            emit(*pending, mode)

        if modes:
            flag = flags_ref[j]
            for mode in (EPI_NONE,) + modes:
                pl.when(flag == mode)(functools.partial(run, mode))
        else:
            run(EPI_NONE)
    else:
        part = jnp.dot(a_ref[...], w_ref[...], preferred_element_type=F32)

        @pl.when(k == 0)
        def _():
            acc_ref[...] = part

        @pl.when((k > 0) & (k < nk - 1))
        def _():
            acc_ref[...] += part

        @pl.when(k == nk - 1)
        def _():
            emit(slice(0, tn), acc_ref[...] + part, EPI_NONE)


def _matmul(a, w, *, out_dtype, name, resid=None, epi=None, tm=1024, tn=1024, tk=4096):
    M, K = a.shape
    N = w.shape[1]
    tm = _tile(epi["cos"].shape[0] if epi else M, tm, 16)
    tn = epi["tn"] if epi else _tile(N, tn, LANE)
    tk = _tile(K, tk, LANE)
    nk = K // tk
    assert a.dtype == BF16 and (w.dtype == BF16 or nk == 1) and (nk == 1 or not epi)
    a_bytes = a.dtype.itemsize
    o_bytes = jnp.dtype(out_dtype).itemsize
    modes = tuple(epi["modes"]) if epi else ()
    in_specs = [pl.BlockSpec((tm, tk), lambda i, j, k, f: (i, k)),
                pl.BlockSpec((tk, tn), lambda i, j, k, f: (k, j))]
    args = [a, w]
    est = 2 * (tm * tk * a_bytes + tk * tn * w.dtype.itemsize + tm * tn * o_bytes) + 3 * tm * tn * 4
    if resid is not None:
        in_specs.append(pl.BlockSpec((tm, tn), lambda i, j, k, f: (i, j)))
        args.append(resid)
        est += 2 * tm * tn * 4
    if epi:
        ns = epi["cos"].shape[0] // tm
        in_specs += [pl.BlockSpec((1, tn), lambda i, j, k, f: (0, j)),
                     pl.BlockSpec((tm, LANE), lambda i, j, k, f: (i % ns, 0)),
                     pl.BlockSpec((tm, LANE), lambda i, j, k, f: (i % ns, 0))]
        args += [epi["gain"].reshape(1, N).astype(F32), epi["cos"], epi["sin"]]
        flags = jnp.asarray(epi["flags"], I32)
        est += 4 * tm * LANE * 4
    else:
        flags = jnp.zeros((N // tn,), I32)
    body = functools.partial(_mm_body, nk=nk, has_resid=resid is not None, modes=modes,
                             rope_half=epi["rope_half"] if epi else 0, tn=tn, cw=math.gcd(tn, MXU_WIDTH))
    return pl.pallas_call(
        body,
        grid_spec=pltpu.PrefetchScalarGridSpec(
            num_scalar_prefetch=1,
            grid=(M // tm, N // tn, nk),
            in_specs=in_specs,
            out_specs=pl.BlockSpec((tm, tn), lambda i, j, k, f: (i, j)),
            scratch_shapes=[pltpu.VMEM((tm, tn), F32)] if nk > 1 else [],
        ),
        out_shape=jax.ShapeDtypeStruct((M, N), out_dtype),
        compiler_params=_params(("parallel", "parallel", "arbitrary"), est),
        name=name,
    )(flags, *args)


def _segments(tile_expert, n_active):
    T = tile_expert.shape[0]
    idx = jnp.arange(T, dtype=I32)
    prev = jnp.concatenate([tile_expert[:1] - 1, tile_expert[:-1]])
    first = (idx < n_active[0]) & (tile_expert != prev)
    first_idx = jnp.where(first, idx, T)
    after = jnp.concatenate([lax.cummin(first_idx[::-1])[::-1][1:], jnp.full((1,), T, I32)])
    nxt = jnp.where(after < T, tile_expert[jnp.minimum(after, T - 1)], -1)
    return first.astype(I32), nxt.astype(I32)


def _stream_weights(te_ref, first_ref, next_ref, w_refs, stage_refs, cast_refs, sem, tn):
    n = pl.program_id(0)
    m = pl.program_id(1)

    def copies(e, nn):
        cols = pl.ds(pl.multiple_of(nn * tn, tn), tn)
        return [pltpu.make_async_copy(w.at[e, :, cols], st, sem.at[i])
                for i, (w, st) in enumerate(zip(w_refs, stage_refs))]

    @pl.when(first_ref[m] == 1)
    def _():
        @pl.when((n == 0) & (m == 0))
        def _():
            for c in copies(te_ref[0], 0):
                c.start()

        for c in copies(te_ref[m], n):
            c.wait()
        for st, cb in zip(stage_refs, cast_refs):
            cb[...] = st[...].astype(BF16)
        last = next_ref[m] < 0
        e2 = jnp.where(last, te_ref[0], next_ref[m])
        n2 = jnp.where(last, n + 1, n)

        @pl.when(n2 < pl.num_programs(0))
        def _():
            for c in copies(e2, n2):
                c.start()


def _row_parts(nv_ref, o_ref, compute, parts):
    m = pl.program_id(1)
    hm = o_ref.shape[0] // parts
    for h in range(parts):
        rows = slice(h * hm, (h + 1) * hm)
        pl.when(nv_ref[m] > h * hm)(functools.partial(compute, rows))

        @pl.when(nv_ref[m] <= h * hm)
        def _(rows=rows):
            o_ref[rows, :] = jnp.zeros((hm, o_ref.shape[1]), o_ref.dtype)


def _swiglu_body(te_ref, na_ref, first_ref, next_ref, nv_ref, a_ref, wg_ref, wu_ref, o_ref,
                 wgf_ref, wuf_ref, wgb_ref, wub_ref, sem):
    tn = o_ref.shape[1]
    _stream_weights(te_ref, first_ref, next_ref, (wg_ref, wu_ref), (wgf_ref, wuf_ref), (wgb_ref, wub_ref), sem, tn)
    cw = math.gcd(tn, MXU_WIDTH)

    def compute(rows):
        def emit(cols, g, u):
            o_ref[rows, cols] = (g * jax.nn.sigmoid(g) * u).astype(o_ref.dtype)

        pending = None
        for c in range(0, tn, cw):
            cols = slice(c, c + cw)
            g = jnp.dot(a_ref[rows, :], wgb_ref[:, cols], preferred_element_type=F32)
            u = jnp.dot(a_ref[rows, :], wub_ref[:, cols], preferred_element_type=F32)
            if pending is not None:
                emit(*pending)
            pending = (cols, g, u)
        emit(*pending)

    _row_parts(nv_ref, o_ref, compute, parts=1)


def _swiglu_up(a, wg, wu, tile_expert, n_active, valid_rows, *, tm, tn, name):
    M, D = a.shape
    F = wg.shape[2]
    first, nxt = _segments(tile_expert, n_active)
    est = 2 * (tm * D * 2 + tm * tn * 2) + 2 * D * tn * (4 + 2) + 4 * tm * tn * 4
    return pl.pallas_call(
        _swiglu_body,
        grid_spec=pltpu.PrefetchScalarGridSpec(
            num_scalar_prefetch=5,
            grid=(F // tn, M // tm),
            in_specs=[pl.BlockSpec((tm, D), lambda n, m, *_: (m, 0)),
                      pl.BlockSpec(memory_space=pl.ANY), pl.BlockSpec(memory_space=pl.ANY)],
            out_specs=pl.BlockSpec((tm, tn), lambda n, m, *_: (m, n)),
            scratch_shapes=[pltpu.VMEM((D, tn), F32), pltpu.VMEM((D, tn), F32),
                            pltpu.VMEM((D, tn), BF16), pltpu.VMEM((D, tn), BF16),
                            pltpu.SemaphoreType.DMA((2,))],
        ),
        out_shape=jax.ShapeDtypeStruct((M, F), BF16),
        compiler_params=_params(("arbitrary", "arbitrary"), est),
        name=name,
    )(tile_expert, n_active, first, nxt, valid_rows, a, wg, wu)


def _down_body(te_ref, na_ref, first_ref, next_ref, nv_ref, a_ref, w_ref, o_ref, wf_ref, wb_ref, sem):
    _stream_weights(te_ref, first_ref, next_ref, (w_ref,), (wf_ref,), (wb_ref,), sem, o_ref.shape[1])

    def compute(rows):
        o_ref[rows, :] = jnp.dot(a_ref[rows, :], wb_ref[...], preferred_element_type=F32)

    _row_parts(nv_ref, o_ref, compute, parts=2)


def _grouped_down(a, wd, tile_expert, n_active, valid_rows, *, tm, tn, name):
    M, F = a.shape
    D = wd.shape[2]
    first, nxt = _segments(tile_expert, n_active)
    est = 2 * (tm * F * 2 + tm * tn * 4) + F * tn * (4 + 2) + 2 * tm * tn * 4
    return pl.pallas_call(
        _down_body,
        grid_spec=pltpu.PrefetchScalarGridSpec(
            num_scalar_prefetch=5,
            grid=(D // tn, M // tm),
            in_specs=[pl.BlockSpec((tm, F), lambda n, m, *_: (m, 0)), pl.BlockSpec(memory_space=pl.ANY)],
            out_specs=pl.BlockSpec((tm, tn), lambda n, m, *_: (m, n)),
            scratch_shapes=[pltpu.VMEM((F, tn), F32), pltpu.VMEM((F, tn), BF16), pltpu.SemaphoreType.DMA((1,))],
        ),
        out_shape=jax.ShapeDtypeStruct((M, D), F32),
        compiler_params=_params(("arbitrary", "arbitrary"), est),
        name=name,
    )(tile_expert, n_active, first, nxt, valid_rows, a, wd)


GATHER_UNROLL = 8


def _gather_body(idx_ref, src_ref, o_ref, buf_ref, sem, *, tg):
    step = pl.program_id(0)

    def row_copy(slot, i, row):
        return pltpu.make_async_copy(src_ref.at[pl.ds(row, 1)], buf_ref.at[slot, pl.ds(i, 1)], sem.at[slot])

    def issue(s):
        slot = s % 2

        def start(i, c):
            row_copy(slot, i, idx_ref[s * tg + i]).start()
            return c

        lax.fori_loop(0, tg, start, 0, unroll=GATHER_UNROLL)

    @pl.when(step == 0)
    def _():
        issue(step)

    @pl.when(step + 1 < pl.num_programs(0))
    def _():
        issue(step + 1)

    slot = step % 2

    def wait(i, c):
        row_copy(slot, i, 0).wait()
        return c

    lax.fori_loop(0, tg, wait, 0, unroll=GATHER_UNROLL)
    o_ref[...] = buf_ref[slot].astype(o_ref.dtype)


def _gather_rows(src, idx, out_dtype, tg=256):
    P = idx.shape[0]
    D = src.shape[1]
    tg = _tile(P, tg, 16)
    est = 2 * tg * D * 4 + 2 * tg * D * 2
    return pl.pallas_call(
        functools.partial(_gather_body, tg=tg),
        grid_spec=pltpu.PrefetchScalarGridSpec(
            num_scalar_prefetch=1,
            grid=(P // tg,),
            in_specs=[pl.BlockSpec(memory_space=pl.ANY)],
            out_specs=pl.BlockSpec((tg, D), lambda i, idx: (i, 0)),
            scratch_shapes=[pltpu.VMEM((2, tg, D), src.dtype), pltpu.SemaphoreType.DMA((2,))],
        ),
        out_shape=jax.ShapeDtypeStruct((P, D), out_dtype),
        compiler_params=_params(("arbitrary",), est),
        name="moe_dispatch_gather",
    )(idx, src)


def _combine_body(slot_ref, x_ref, g_ref, y_ref, o_ref, buf_ref, sem, *, tc):
    step = pl.program_id(0)

    def row_copy(slot, i, k, row):
        return pltpu.make_async_copy(y_ref.at[pl.ds(row, 1)], buf_ref.at[slot, k, pl.ds(i, 1)], sem.at[slot])

    def issue(s):
        slot = s % 2

        def start(i, c):
            for k in range(TOP_K):
                row_copy(slot, i, k, slot_ref[(s * tc + i) * TOP_K + k]).start()
            return c

        lax.fori_loop(0, tc, start, 0, unroll=GATHER_UNROLL // TOP_K)

    @pl.when(step == 0)
    def _():
        issue(step)

    @pl.when(step + 1 < pl.num_programs(0))
    def _():
        issue(step + 1)

    slot = step % 2

    def wait(i, c):
        for k in range(TOP_K):
            row_copy(slot, i, k, 0).wait()
        return c

    lax.fori_loop(0, tc, wait, 0, unroll=GATHER_UNROLL // TOP_K)
    g = g_ref[...]
    o_ref[...] = x_ref[...] + (g[:, 0:1] * buf_ref[slot, 0] + g[:, 1:2] * buf_ref[slot, 1])


def _moe_combine(x, y, slots, gates, tc=128):
    M, D = x.shape
    tc = _tile(M, tc, 8)
    est = 4 * tc * D * 4 + 2 * TOP_K * tc * D * 4
    return pl.pallas_call(
        functools.partial(_combine_body, tc=tc),
        grid_spec=pltpu.PrefetchScalarGridSpec(
            num_scalar_prefetch=1,
            grid=(M // tc,),
            in_specs=[pl.BlockSpec((tc, D), lambda i, s: (i, 0)), pl.BlockSpec((tc, TOP_K), lambda i, s: (i, 0)),
                      pl.BlockSpec(memory_space=pl.ANY)],
            out_specs=pl.BlockSpec((tc, D), lambda i, s: (i, 0)),
            scratch_shapes=[pltpu.VMEM((2, TOP_K, tc, D), F32), pltpu.SemaphoreType.DMA((2,))],
        ),
        out_shape=jax.ShapeDtypeStruct((M, D), F32),
        compiler_params=_params(("arbitrary",), est),
        name="moe_combine",
    )(slots.reshape(-1), x, gates, y)


def _diff_attn_body(q_ref, k_ref, vt_ref, lq1_ref, lk1_ref, lq2_ref, lk2_ref, g_ref, o_ref, s_ref, acc_ref,
                    *, tq, hg, lam_init):
    qi = pl.program_id(2)
    w = 2 * HEAD_DIM
    cols = [slice(hh * w + mi * HEAD_DIM, hh * w + (mi + 1) * HEAD_DIM) for hh in range(hg) for mi in range(2)]
    qs = [q_ref[:, c] for c in cols]

    def step(n, carry, mask):
        off = pl.multiple_of(n * tq, tq)
        sts = [_nt(k_ref[pl.ds(off, tq), col_sl], qs[c]) for c, col_sl in enumerate(cols)]
        if mask is not None:
            sts = [jnp.where(mask, st, NEG) for st in sts]
        return _softmax_steps(sts, carry, s_ref, acc_ref, lambda c: vt_ref[c // 2, n])

    acc_ref[...] = jnp.zeros(acc_ref.shape, F32)
    carry = lax.fori_loop(0, qi, lambda n, c: step(n, c, None), _init_carry(len(cols), tq))
    key = lax.broadcasted_iota(I32, (tq, tq), 0)
    qry = lax.broadcasted_iota(I32, (tq, tq), 1)
    step(qi, carry, key <= qry)
    lam = (jnp.exp(jnp.sum(lq1_ref[...] * lk1_ref[...], axis=-1, keepdims=True))
           - jnp.exp(jnp.sum(lq2_ref[...] * lk2_ref[...], axis=-1, keepdims=True)) + lam_init)
    for hh in range(hg):
        y = (_normalized(acc_ref, 2 * hh, w) - lam * _normalized(acc_ref, 2 * hh + 1, w)).T
        ms = jnp.mean(y * y, axis=-1, keepdims=True)
        hs = slice(hh * w, (hh + 1) * w)
        o_ref[:, hs] = (y * lax.rsqrt(ms + EPS) * g_ref[...] * (1.0 - lam_init)).astype(o_ref.dtype)


def _diff_attention(qkv, vt, n_heads, lam_params, subln, lam_init, tq, hg=4):
    B, S, _ = qkv.shape
    w = 2 * HEAD_DIM
    hg = math.gcd(hg, n_heads)
    ng = n_heads // hg
    vec = pl.BlockSpec((1, HEAD_DIM), lambda b, g, i: (0, 0))
    est = 2 * (2 * tq * hg * w * 2 + 2 * S * hg * w * 2) + 2 * hg * (3 * tq * tq * 4 + tq * w * 4)
    return pl.pallas_call(
        functools.partial(_diff_attn_body, tq=tq, hg=hg, lam_init=lam_init),
        grid=(B, ng, S // tq),
        in_specs=[pl.BlockSpec((None, tq, hg * w), lambda b, g, i: (b, i, g)),
                  pl.BlockSpec((None, S, hg * w), lambda b, g, i: (b, 0, ng + g)),
                  pl.BlockSpec((None, hg, S // tq, w + ONES_ROWS, tq), lambda b, g, i: (b, g, 0, 0, 0)),
                  vec, vec, vec, vec,
                  pl.BlockSpec((1, w), lambda b, g, i: (0, 0))],
        out_specs=pl.BlockSpec((None, tq, hg * w), lambda b, g, i: (b, i, g)),
        out_shape=jax.ShapeDtypeStruct((B, S, n_heads * w), BF16),
        scratch_shapes=[pltpu.VMEM((2 * hg, tq, tq), F32), pltpu.VMEM((2 * hg, w + ONES_ROWS, tq), F32)],
        compiler_params=_params(("parallel", "parallel", "arbitrary"), est),
        name="diff_attention",
    )(qkv, qkv, vt, *[p.reshape(1, HEAD_DIM).astype(F32) for p in lam_params],
      subln.reshape(1, w).astype(F32))


def _moba_body(q_ref, k_ref, vt_ref, o_ref, kmean_ref, bias_ref, s_ref, acc_ref, *, nb, hg):
    qi = pl.program_id(2)
    blk = MOBA_BLOCK
    d = HEAD_DIM
    heads = [slice(hh * d, (hh + 1) * d) for hh in range(hg)]

    @pl.when(qi == 0)
    def _():
        for hh, hs in enumerate(heads):
            for n in range(nb):
                kb = k_ref[n * blk:(n + 1) * blk, hs].astype(F32)
                kmean_ref[hh, n:n + 1, :] = jnp.mean(kb, axis=0, keepdims=True)

    blk_id = lax.broadcasted_iota(I32, (nb, blk), 0)
    qs = []
    for hh, hs in enumerate(heads):
        q = q_ref[:, hs]
        km = kmean_ref[hh]
        km_hi = km.astype(BF16)
        km_lo = (km - km_hi.astype(F32)).astype(BF16)
        gate = _nt(km_hi, q) + _nt(km_lo, q)
        gate = jnp.where(blk_id < qi, gate, -jnp.inf)
        sel = jnp.zeros(gate.shape, jnp.bool_)
        for _ in range(MOBA_TOPK):
            mx = jnp.max(gate, axis=0, keepdims=True)
            idx = jnp.min(jnp.where(gate == mx, blk_id, nb), axis=0, keepdims=True)
            sel = sel | ((blk_id == idx) & (mx > -jnp.inf))
            gate = jnp.where(blk_id == idx, -jnp.inf, gate)
        qs.append(q)
        bias_ref[hh] = jnp.where(sel, 0.0, NEG)

    def step(n, carry, mask):
        off = pl.multiple_of(n * blk, blk)
        sts = [_nt(k_ref[pl.ds(off, blk), hs], qs[hh]) for hh, hs in enumerate(heads)]
        if mask is None:
            sts = [st + bias_ref[hh, pl.ds(n, 1), :] for hh, st in enumerate(sts)]
        else:
            sts = [jnp.where(mask, st, NEG) for st in sts]
        return _softmax_steps(sts, carry, s_ref, acc_ref, lambda hh: vt_ref[hh, n])

    acc_ref[...] = jnp.zeros(acc_ref.shape, F32)
    carry = lax.fori_loop(0, qi, lambda n, c: step(n, c, None), _init_carry(hg, blk))
    key = lax.broadcasted_iota(I32, (blk, blk), 0)
    qry = lax.broadcasted_iota(I32, (blk, blk), 1)
    step(qi, carry, key <= qry)
    for hh, hs in enumerate(heads):
        o_ref[:, hs] = _normalized(acc_ref, hh, d).T.astype(o_ref.dtype)


def _moba_attention(qkv, vt, n_heads, q_col, k_col, hg=8):
    B, S, _ = qkv.shape
    assert S % MOBA_BLOCK == 0 and S // MOBA_BLOCK >= MOBA_TOPK
    nb = S // MOBA_BLOCK
    d = HEAD_DIM
    hg = math.gcd(math.gcd(hg, n_heads), math.gcd(q_col, k_col))
    est = 2 * (2 * MOBA_BLOCK * hg * d * 2 + 2 * S * hg * d * 2) + hg * (3 * MOBA_BLOCK * MOBA_BLOCK * 4)
    return pl.pallas_call(
        functools.partial(_moba_body, nb=nb, hg=hg),
        grid=(B, n_heads // hg, nb),
        in_specs=[pl.BlockSpec((None, MOBA_BLOCK, hg * d), lambda b, g, i: (b, i, q_col // hg + g)),
                  pl.BlockSpec((None, S, hg * d), lambda b, g, i: (b, 0, k_col // hg + g)),
                  pl.BlockSpec((None, hg, nb, d + ONES_ROWS, MOBA_BLOCK), lambda b, g, i: (b, g, 0, 0, 0))],
        out_specs=pl.BlockSpec((None, MOBA_BLOCK, hg * d), lambda b, g, i: (b, i, g)),
        out_shape=jax.ShapeDtypeStruct((B, S, n_heads * d), BF16),
        scratch_shapes=[pltpu.VMEM((hg, nb, d), F32), pltpu.VMEM((hg, nb, MOBA_BLOCK), F32),
                        pltpu.VMEM((hg, MOBA_BLOCK, MOBA_BLOCK), F32),
                        pltpu.VMEM((hg, d + ONES_ROWS, MOBA_BLOCK), F32)],
        compiler_params=_params(("parallel", "parallel", "arbitrary"), est),
        name="moba_attention",
    )(qkv, qkv, vt)


def _xattn_body(q_ref, k_ref, v_ref, o_ref, *, n_heads):
    for h in range(n_heads):
        sl = slice(h * HEAD_DIM, (h + 1) * HEAD_DIM)
        s = _nt(q_ref[:, sl], k_ref[:, sl])
        m = jnp.max(s, axis=-1, keepdims=True)
        p = jnp.exp2(s - m)
        l = jnp.sum(p, axis=-1, keepdims=True)
        o = jnp.dot(p.astype(BF16), v_ref[:, sl], preferred_element_type=F32)
        o_ref[:, sl] = (o / l).astype(o_ref.dtype)


def _cross_attention(q, kv, n_heads, tq=512):
    B, S, X = q.shape
    M = kv.shape[1]
    tq = _tile(S, tq, 16)
    est = 2 * (2 * tq * X * 2 + 2 * M * X * 2) + 6 * tq * M * 4
    return pl.pallas_call(
        functools.partial(_xattn_body, n_heads=n_heads),
        grid=(B, S // tq),
        in_specs=[pl.BlockSpec((None, tq, X), lambda b, i: (b, i, 0)),
                  pl.BlockSpec((None, M, X), lambda b, i: (b, 0, 0)),
                  pl.BlockSpec((None, M, X), lambda b, i: (b, 0, 1))],
        out_specs=pl.BlockSpec((None, tq, X), lambda b, i: (b, i, 0)),
        out_shape=jax.ShapeDtypeStruct((B, S, X), BF16),
        compiler_params=_params(("parallel", "parallel"), est),
        name="cross_attention",
    )(q, kv, kv)


def _odd_prep_body(x_ref, gqa_ref, gkv_ref, gkr_ref, gik_ref, cos_ref, sin_ref,
                   qa_ref, ckv_ref, kr_ref, ik_ref, iw_ref, *, c1, c2, iw_scale):
    def norm(x, g):
        ms = jnp.mean(x * x, axis=-1, keepdims=True)
        return x * lax.rsqrt(ms + EPS) * g

    def rope(y):
        lane = lax.broadcasted_iota(I32, y.shape, 1)
        half = C_ROPE // 2
        first = (lane % C_ROPE) < half
        r = jnp.where(first, pltpu.roll(y, LANE - half, axis=1), pltpu.roll(y, half, axis=1))
        return y * cos_ref[...] + r * sin_ref[...]

    qa_ref[...] = norm(x_ref[:, :c1], gqa_ref[...]).astype(qa_ref.dtype)
    ckv_ref[...] = norm(x_ref[:, c1:c2], gkv_ref[...]).astype(ckv_ref.dtype)
    slab_a = x_ref[:, c2:c2 + LANE]
    slab_b = x_ref[:, c2 + LANE:c2 + 2 * LANE]
    lane = lax.broadcasted_iota(I32, slab_a.shape, 1)
    low = lane < C_ROPE
    ms = jnp.sum(jnp.where(low, slab_a * slab_a, 0.0), axis=-1, keepdims=True) / C_ROPE
    kr = rope(slab_a * lax.rsqrt(ms + EPS) * gkr_ref[...])
    kr_ref[0] = kr.astype(kr_ref.dtype)
    kr_ref[1] = pltpu.roll(kr, C_ROPE, axis=1).astype(kr_ref.dtype)
    rot_a = pltpu.roll(slab_a, C_ROPE, axis=1)
    rot_b = pltpu.roll(slab_b, C_ROPE, axis=1)
    ik = jnp.where(low, rot_a, rot_b)
    ik_ref[...] = rope(norm(ik, gik_ref[...])).astype(ik_ref.dtype)
    iw_ref[...] = rot_b * iw_scale


def _odd_prep(x, c1, c2, g_qa, g_kv, g_kr, g_ik, cos_p, sin_p, iw_scale, tm=256):
    B, S, C = x.shape
    assert C == c2 + 2 * LANE
    tm = _tile(S, tm, 16)
    vec = lambda n: pl.BlockSpec((1, n), lambda b, i: (0, 0))
    tab = pl.BlockSpec((tm, LANE), lambda b, i: (i, 0))
    est = 2 * tm * C * 4 * 2
    return pl.pallas_call(
        functools.partial(_odd_prep_body, c1=c1, c2=c2, iw_scale=iw_scale),
        grid=(B, S // tm),
        in_specs=[pl.BlockSpec((None, tm, C), lambda b, i: (b, i, 0)),
                  vec(c1), vec(c2 - c1), vec(LANE), vec(LANE), tab, tab],
        out_specs=[pl.BlockSpec((None, tm, c1), lambda b, i: (b, i, 0)),
                   pl.BlockSpec((None, tm, c2 - c1), lambda b, i: (b, i, 0)),
                   pl.BlockSpec((None, 2, tm, LANE), lambda b, i: (b, 0, i, 0)),
                   pl.BlockSpec((None, tm, LANE), lambda b, i: (b, i, 0)),
                   pl.BlockSpec((None, tm, LANE), lambda b, i: (b, i, 0))],
        out_shape=[jax.ShapeDtypeStruct((B, S, c1), BF16),
                   jax.ShapeDtypeStruct((B, S, c2 - c1), BF16),
                   jax.ShapeDtypeStruct((B, 2, S, LANE), BF16),
                   jax.ShapeDtypeStruct((B, S, LANE), BF16),
                   jax.ShapeDtypeStruct((B, S, LANE), F32)],
        compiler_params=_params(("parallel", "parallel"), est),
        name="odd_prep",
    )(x, g_qa.reshape(1, -1), g_kv.reshape(1, -1), g_kr.reshape(1, -1), g_ik.reshape(1, -1), cos_p, sin_p)


def _q_prep_body(x_ref, gn_ref, gr_ref, cos_ref, sin_ref, qn_ref, qr_ref, *, n_heads):
    nope_w = n_heads * C_NOPE
    width = C_NOPE + C_ROPE
    half = C_ROPE // 2
    lane = lax.broadcasted_iota(I32, (x_ref.shape[0], LANE), 1)
    low = lane < C_ROPE
    first = (lane % C_ROPE) < half
    for p in range(n_heads // 2):
        n0 = x_ref[:, (2 * p) * LANE:(2 * p + 1) * LANE]
        n1 = x_ref[:, (2 * p + 1) * LANE:(2 * p + 2) * LANE]
        r = x_ref[:, nope_w + p * LANE:nope_w + (p + 1) * LANE]
        r2 = r * r
        ss0 = jnp.sum(n0 * n0, axis=-1, keepdims=True) + jnp.sum(jnp.where(low, r2, 0.0), axis=-1, keepdims=True)
        ss1 = jnp.sum(n1 * n1, axis=-1, keepdims=True) + jnp.sum(jnp.where(low, 0.0, r2), axis=-1, keepdims=True)
        inv0 = lax.rsqrt(ss0 / width + EPS)
        inv1 = lax.rsqrt(ss1 / width + EPS)
        qn_ref[:, (2 * p) * LANE:(2 * p + 1) * LANE] = (n0 * inv0 * gn_ref[...]).astype(qn_ref.dtype)
        qn_ref[:, (2 * p + 1) * LANE:(2 * p + 2) * LANE] = (n1 * inv1 * gn_ref[...]).astype(qn_ref.dtype)
        y = r * jnp.where(low, inv0, inv1) * gr_ref[...]
        rot = jnp.where(first, pltpu.roll(y, LANE - half, axis=1), pltpu.roll(y, half, axis=1))
        qr_ref[:, p * LANE:(p + 1) * LANE] = (y * cos_ref[...] + rot * sin_ref[...]).astype(qr_ref.dtype)


def _q_prep(x, n_heads, g_nope, g_rope2, cos_q, sin_q, tm=256):
    B, S, C = x.shape
    tm = _tile(S, tm, 16)
    nope_w, rope_w = n_heads * C_NOPE, n_heads * C_ROPE
    vec = pl.BlockSpec((1, LANE), lambda b, i: (0, 0))
    tab = pl.BlockSpec((tm, LANE), lambda b, i: (i, 0))
    est = 2 * tm * C * 6
    return pl.pallas_call(
        functools.partial(_q_prep_body, n_heads=n_heads),
        grid=(B, S // tm),
        in_specs=[pl.BlockSpec((None, tm, C), lambda b, i: (b, i, 0)), vec, vec, tab, tab],
        out_specs=[pl.BlockSpec((None, tm, nope_w), lambda b, i: (b, i, 0)),
                   pl.BlockSpec((None, tm, rope_w), lambda b, i: (b, i, 0))],
        out_shape=[jax.ShapeDtypeStruct((B, S, nope_w), BF16), jax.ShapeDtypeStruct((B, S, rope_w), BF16)],
        compiler_params=_params(("parallel", "parallel"), est),
        name="dsa_q_prep",
    )(x, g_nope.reshape(1, LANE), g_rope2.reshape(1, LANE), cos_q, sin_q)


def _indexer_body(ik_ref, iq_ref, iwt_ref, o_ref, key_ref, *, tq, n_heads, n_keep):
    S = ik_ref.shape[0]
    qi = pl.program_id(1)
    n_tiles = qi + 1
    t_idx = qi * tq + lax.broadcasted_iota(I32, (tq, tq), 1)
    s_loc = lax.broadcasted_iota(I32, (tq, tq), 0)

    def score_tile(kt, c):
        off = pl.multiple_of(kt * tq, tq)
        ikt = ik_ref[pl.ds(off, tq), :]
        acc = jnp.zeros((tq, tq), F32)
        for h in range(n_heads):
            r = _nt(ikt, iq_ref[:, h * IDX_DIM:(h + 1) * IDX_DIM])
            acc = acc + jnp.maximum(r, 0.0) * iwt_ref[h:h + 1, :]
        acc = jnp.where(off + s_loc <= t_idx, acc, -jnp.inf)
        bits = lax.bitcast_convert_type(acc, I32)
        key_ref[pl.ds(off, tq), :] = bits ^ ((bits >> 31) & 0x7FFFFFFF)
        return c

    lax.fori_loop(0, n_tiles, score_tile, 0)

    def count(hit):
        def body(kt, cnt):
            off = pl.multiple_of(kt * tq, tq)
            one = jnp.where(hit(key_ref[pl.ds(off, tq), :], off + s_loc), 1, 0).astype(I32)
            return cnt + jnp.sum(one.reshape(tq // 8, 8, tq), axis=0)
        cnt = lax.fori_loop(0, n_tiles, body, jnp.zeros((8, tq), I32))
        return jnp.sum(cnt, axis=0, keepdims=True)

    n_pos = count(lambda k, s: k >= 0)
    thr = jnp.where(n_pos >= n_keep, 0, INT_MIN).astype(I32)

    def bit_step(i, thr):
        cand = thr + lax.shift_left(jnp.int32(1), 30 - i)
        return jnp.where(count(lambda k, s: k >= cand) >= n_keep, cand, thr)

    thr = lax.fori_loop(0, 31, bit_step, thr)

    def tie_cut():
        need = n_keep - count(lambda k, s: k > thr)
        bits = S.bit_length()

        def step(i, cut):
            cand = cut + lax.shift_left(jnp.int32(1), bits - 1 - i)
            below = count(lambda k, s: (k == thr) & (s < cand))
            return jnp.where(below < need, cand, cut)

        return lax.fori_loop(0, bits, step, jnp.zeros((1, tq), I32))

    has_ties = jnp.max(count(lambda k, s: k >= thr)) > n_keep
    cut = lax.cond(has_ties, tie_cut, lambda: jnp.full((1, tq), S, I32))

    def out_tile(kt, c):
        off = pl.multiple_of(kt * tq, tq)
        key = key_ref[pl.ds(off, tq), :]
        s_idx = off + s_loc
        ok = ((key > thr) | ((key == thr) & (s_idx <= cut))) & (s_idx <= t_idx)
        o_ref[pl.ds(off, tq), :] = jnp.where(ok, 0.0, NEG).astype(o_ref.dtype)
        return c

    def neg_tile(kt, c):
        off = pl.multiple_of(kt * tq, tq)
        o_ref[pl.ds(off, tq), :] = jnp.full((tq, tq), NEG, o_ref.dtype)
        return c

    lax.fori_loop(0, n_tiles, out_tile, 0)
    lax.fori_loop(n_tiles, S // tq, neg_tile, 0)


def _indexer(ik, iq, iwt, n_keep, tq=256):
    B, S, _ = ik.shape
    n_heads = iwt.shape[1]
    tq = _tile(S, tq, LANE)
    est = 2 * (S * IDX_DIM * 2 + tq * n_heads * IDX_DIM * 2 + n_heads * tq * 4 + S * tq * 2) + S * tq * 4 + 8 * tq * tq * 4
    return pl.pallas_call(
        functools.partial(_indexer_body, tq=tq, n_heads=n_heads, n_keep=n_keep),
        grid=(B, S // tq),
        in_specs=[pl.BlockSpec((None, S, IDX_DIM), lambda b, i: (b, 0, 0)),
                  pl.BlockSpec((None, tq, n_heads * IDX_DIM), lambda b, i: (b, i, 0)),
                  pl.BlockSpec((None, n_heads, tq), lambda b, i: (b, 0, i))],
        out_specs=pl.BlockSpec((None, S, tq), lambda b, i: (b, 0, i)),
        out_shape=jax.ShapeDtypeStruct((B, S, S), BF16),
        scratch_shapes=[pltpu.VMEM((S, tq), I32)],
        compiler_params=_params(("parallel", "arbitrary"), est),
        name="dsa_indexer",
    )(ik, iq, iwt)


def _dsa_attn_body(qn_ref, qr_ref, kn_ref, kr_ref, vt_ref, bias_ref, o_ref, s_ref, acc_ref, *, tq, hg):
    qi = pl.program_id(2)
    heads = [slice(hh * C_NOPE, (hh + 1) * C_NOPE) for hh in range(hg)]
    qs = [jnp.concatenate([qn_ref[:, hs], qr_ref[:, (hh // 2) * LANE:(hh // 2 + 1) * LANE]], axis=1)
          for hh, hs in enumerate(heads)]

    def scores(n):
        off = pl.multiple_of(n * tq, tq)
        bias = bias_ref[pl.ds(off, tq), :].astype(F32)
        kr = [kr_ref[par, pl.ds(off, tq), :] for par in range(2)]
        return tuple(bias + _nt(jnp.concatenate([kn_ref[pl.ds(off, tq), hs], kr[hh % 2]], axis=1), qs[hh])
                     for hh, hs in enumerate(heads))

    def body(n, carry):
        return _softmax_steps(scores(n), carry, s_ref, acc_ref, lambda hh: vt_ref[hh, n])

    acc_ref[...] = jnp.zeros(acc_ref.shape, F32)
    lax.fori_loop(0, qi + 1, body, _init_carry(hg, tq))
    for hh, hs in enumerate(heads):
        o_ref[:, hs] = _normalized(acc_ref, hh, C_NOPE).T.astype(o_ref.dtype)


def _dsa_attention(qn, qr, kv, kr2, vt, bias, n_heads, tq, hg=8):
    B, S, _ = qn.shape
    hg = min(hg, n_heads)
    assert hg % 2 == 0 and n_heads % hg == 0
    ng = n_heads // hg
    nk = S // tq
    est = 2 * (tq * hg * 192 * 2 + 2 * S * hg * LANE * 2 + 2 * S * LANE * 2 + S * tq * 2 + tq * hg * LANE * 2) \
        + hg * 3 * tq * tq * 4
    return pl.pallas_call(
        functools.partial(_dsa_attn_body, tq=tq, hg=hg),
        grid=(B, ng, S // tq),
        in_specs=[pl.BlockSpec((None, tq, hg * C_NOPE), lambda b, g, i: (b, i, g)),
                  pl.BlockSpec((None, tq, hg * C_ROPE), lambda b, g, i: (b, i, g)),
                  pl.BlockSpec((None, S, hg * C_NOPE), lambda b, g, i: (b, 0, g)),
                  pl.BlockSpec((None, 2, S, LANE), lambda b, g, i: (b, 0, 0, 0)),
                  pl.BlockSpec((None, hg, nk, C_NOPE + ONES_ROWS, tq), lambda b, g, i: (b, g, 0, 0, 0)),
                  pl.BlockSpec((None, S, tq), lambda b, g, i: (b, 0, i))],
        out_specs=pl.BlockSpec((None, tq, hg * C_NOPE), lambda b, g, i: (b, i, g)),
        out_shape=jax.ShapeDtypeStruct((B, S, n_heads * C_NOPE), BF16),
        scratch_shapes=[pltpu.VMEM((hg, tq, tq), F32), pltpu.VMEM((hg, C_NOPE + ONES_ROWS, tq), F32)],
        compiler_params=_params(("parallel", "parallel", "arbitrary"), est),
        name="dsa_attention",
    )(qn, qr, kv, kr2, vt, bias)


def _rope_tables(seq, dim):
    inv_freq = ROPE_THETA ** (-jnp.arange(0, dim, 2, dtype=F32) / dim)
    ang = jnp.arange(seq, dtype=F32)[:, None] * inv_freq[None, :]
    return jnp.cos(ang), jnp.sin(ang)


def _cross_block(x, mem, l, norm_xattn, norm_mem, xa_wq, xa_wk, xa_wv, xa_wo, xa_qnorm, xa_knorm, B, S):
    N, D = x.shape
    X = xa_wq.shape[2]
    n_heads = X // HEAD_DIM
    M = mem.shape[1]
    h = _rmsnorm(x, norm_xattn[l], BF16)
    mn = _rmsnorm(mem.reshape(B * M, D), norm_mem[l], BF16)
    ones = jnp.ones((S, LANE), F32)
    q = _matmul(h, xa_wq[l], out_dtype=BF16, name="xattn_q",
                epi=dict(flags=[EPI_NORM] * (X // _tile(X, 1024, LANE)), modes=(EPI_NORM,), rope_half=0,
                         tn=_tile(X, 1024, LANE), cos=ones, sin=ones,
                         gain=jnp.tile(xa_qnorm[l] * (HEAD_DIM ** -0.5 * LOG2E), n_heads)))
    wkv = jnp.concatenate([xa_wk[l], xa_wv[l]], axis=1).astype(BF16)
    tn = _tile(X, 1024, LANE)
    kv = _matmul(mn, wkv, out_dtype=BF16, name="xattn_kv",
                 epi=dict(flags=[EPI_NORM] * (X // tn) + [EPI_NONE] * (X // tn), modes=(EPI_NORM,), rope_half=0,
                          tn=tn, cos=jnp.ones((M, LANE), F32), sin=jnp.ones((M, LANE), F32),
                          gain=jnp.concatenate([jnp.tile(xa_knorm[l], n_heads), jnp.ones((X,), F32)])))
    o = _cross_attention(q.reshape(B, S, X), kv.reshape(B, M, 2 * X), n_heads)
    return _matmul(o.reshape(N, X), xa_wo[l], out_dtype=F32, name="xattn_out", resid=x)


def _even_mixer(x, l, i, B, S, norm_mix, ev_w_in, ev_a_qnorm, ev_a_knorm, lam_params, ev_a_subln,
                ev_b_qnorm, ev_b_knorm, ev_w_out):
    N, D = x.shape
    a_heads = D // (4 * HEAD_DIM)
    b_heads = D // (2 * HEAD_DIM)
    aw = a_heads * 2 * HEAD_DIM
    bw = b_heads * HEAD_DIM
    width = 3 * aw + 3 * bw
    scale = HEAD_DIM ** -0.5 * LOG2E
    cos, sin = _rope_tables(S, HEAD_DIM)
    cos2 = jnp.concatenate([cos, cos], axis=-1)
    sin2 = jnp.concatenate([-sin, sin], axis=-1)
    tn = _tile(math.gcd(aw, bw), 1024, LANE)
    seg = [(aw, EPI_NORM_ROPE, ev_a_qnorm[i] * scale), (aw, EPI_NORM_ROPE, ev_a_knorm[i]), (aw, EPI_NONE, None),
           (bw, EPI_NORM_ROPE, ev_b_qnorm[i] * scale), (bw, EPI_NORM_ROPE, ev_b_knorm[i]), (bw, EPI_NONE, None)]
    flags, gains = [], []
    for w, flag, g in seg:
        flags += [flag] * (w // tn)
        gains.append(jnp.ones((w,), F32) if g is None else jnp.tile(g.astype(F32), w // HEAD_DIM))
    h = _rmsnorm(x, norm_mix[l], BF16)
    qkv = _matmul(h, ev_w_in[i], out_dtype=BF16, name="even_in",
                  epi=dict(flags=flags, modes=(EPI_NORM_ROPE,), rope_half=HEAD_DIM // 2, tn=tn,
                           cos=cos2, sin=sin2, gain=jnp.concatenate(gains)))
    qkv = qkv.reshape(B, S, width)
    lam_init = 0.8 - 0.6 * math.exp(-0.3 * l)
    tq = _tile(S, 256, LANE)
    ya = _diff_attention(qkv, _value_tiles(qkv[:, :, 2 * aw:3 * aw], a_heads, tq), a_heads, lam_params,
                         ev_a_subln[i], lam_init, tq)
    c0 = 3 * aw // HEAD_DIM
    yb = _moba_attention(qkv, _value_tiles(qkv[:, :, 3 * aw + 2 * bw:], b_heads, MOBA_BLOCK), b_heads,
                         c0, c0 + b_heads)
    y = jnp.concatenate([ya, yb], axis=-1).reshape(N, aw + bw)
    return _matmul(y, ev_w_out[i], out_dtype=F32, name="even_out", resid=x, tn=512)


def _odd_mixer(x, l, i, B, S, norm_mix, od_w_in, od_qa_norm, od_w_qb, od_q_norm, od_kv_norm, od_kr_norm,
               od_w_uk, od_w_uv, od_w_iqb, od_ik_norm, od_w_out):
    N, D = x.shape
    c1 = od_qa_norm.shape[1]
    kv_rank = od_kv_norm.shape[1]
    c2 = c1 + kv_rank
    n_heads = od_w_uk.shape[2]
    idx_heads = od_w_iqb.shape[2] // IDX_DIM
    assert od_w_in.shape[2] == c2 + C_ROPE + IDX_DIM + idx_heads and idx_heads == C_ROPE
    scale = (C_NOPE + C_ROPE) ** -0.5 * LOG2E
    cos, sin = _rope_tables(S, C_ROPE)
    one, zero = jnp.ones((S, C_ROPE), F32), jnp.zeros((S, C_ROPE), F32)
    cos_p = jnp.concatenate([cos, cos, one], axis=-1)
    sin_p = jnp.concatenate([-sin, sin, zero], axis=-1)
    cos_q = jnp.concatenate([cos, cos, cos, cos], axis=-1)
    sin_q = jnp.concatenate([-sin, sin, -sin, sin], axis=-1)

    h = _rmsnorm(x, norm_mix[l], BF16)
    proj = _matmul(h, od_w_in[i], out_dtype=F32, name="odd_in")
    g_kr = jnp.concatenate([od_kr_norm[i], jnp.zeros((LANE - C_ROPE,), F32)])
    qa, ckv, kr2, ik, iw = _odd_prep(proj.reshape(B, S, -1), c1, c2, od_qa_norm[i], od_kv_norm[i], g_kr,
                                     od_ik_norm[i], cos_p, sin_p, idx_heads ** -0.5 * IDX_DIM ** -0.5)
    qa = qa.reshape(N, c1)
    wqb = od_w_qb[i].reshape(c1, n_heads, C_NOPE + C_ROPE)
    wqb = jnp.concatenate([wqb[:, :, :C_NOPE].reshape(c1, -1), wqb[:, :, C_NOPE:].reshape(c1, -1)], axis=1)
    qraw = _matmul(qa, wqb.astype(BF16), out_dtype=F32, name="odd_qb")
    qn, qr = _q_prep(qraw.reshape(B, S, -1), n_heads, od_q_norm[i][:C_NOPE] * scale,
                     jnp.tile(od_q_norm[i][C_NOPE:], 2) * scale, cos_q, sin_q)
    tn = _tile(idx_heads * IDX_DIM, 1024, LANE)
    iq = _matmul(qa, od_w_iqb[i], out_dtype=BF16, name="odd_iqb",
                 epi=dict(flags=[EPI_ROPE] * (idx_heads * IDX_DIM // tn), modes=(EPI_ROPE,), rope_half=IDX_ROPE // 2,
                          tn=tn, cos=cos_p, sin=sin_p, gain=jnp.ones((idx_heads * IDX_DIM,), F32)))
    wkv = jnp.concatenate([od_w_uk[i].reshape(kv_rank, -1), od_w_uv[i].reshape(kv_rank, -1)], axis=1)
    kv = _matmul(ckv.reshape(N, kv_rank), wkv.astype(BF16), out_dtype=BF16, name="odd_kv")

    n_keep = min(IDX_TOPK, S // 4)
    tq = _tile(S, 256, LANE)
    iwt = jnp.swapaxes(iw[:, :, :idx_heads], 1, 2)
    bias = _indexer(ik, iq.reshape(B, S, -1), iwt, n_keep, tq)
    kv = kv.reshape(B, S, -1)
    vt = _value_tiles(kv[:, :, n_heads * C_NOPE:], n_heads, tq)
    y = _dsa_attention(qn, qr, kv, kr2, vt, bias, n_heads, tq)
    return _matmul(y.reshape(N, -1), od_w_out[i], out_dtype=F32, name="odd_out", resid=x, tn=512)


def _dense_ffn(x, g, wg, wu, wd):
    N, D = x.shape
    F = wg.shape[1]
    h = _rmsnorm(x, g, BF16)
    tm = _tile(N, 2048, 16)
    tn = _tile(F, 512, LANE)
    nt = N // tm
    hid = _swiglu_up(h, wg[None], wu[None], jnp.zeros((nt,), I32), jnp.full((1,), nt, I32),
                     jnp.full((nt,), tm, I32), tm=tm, tn=tn, name="ffn_up")
    return _matmul(hid, wd.astype(BF16), out_dtype=F32, name="ffn_down", resid=x, tm=512, tn=1024,
                   tk=F if F <= 4096 else _tile(F, F // 2, LANE))


def _moe_ffn(x, g, router, wg, wu, wd, tm=512):
    N, D = x.shape
    E, _, F = wg.shape
    h, gate, sel = _norm_router(x, g, router)
    tm = _tile(N, tm, 16)
    cnt = jnp.sum(sel, axis=0)
    tiles_e = (cnt + tm - 1) // tm
    tile_end = jnp.cumsum(tiles_e)
    start = (tile_end - tiles_e) * tm
    rank = jnp.cumsum(sel, axis=0) - sel
    P = N * TOP_K + E * tm
    n_tiles = P // tm
    slot = (start[None, :] + rank).astype(I32)
    lane = jnp.arange(E, dtype=I32)[None, :]
    e2 = jnp.stack([jnp.min(jnp.where(sel > 0, lane, E), axis=1), jnp.max(jnp.where(sel > 0, lane, -1), axis=1)], 1)
    slots2 = jnp.take_along_axis(slot, e2, axis=1)
    gates2 = jnp.take_along_axis(gate, e2, axis=1)
    tok2 = jnp.broadcast_to(jnp.arange(N, dtype=I32)[:, None], (N, TOP_K))
    tok_of_slot = jnp.zeros((P,), I32).at[slots2.reshape(-1)].set(tok2.reshape(-1), unique_indices=True)
    tile_expert = jnp.minimum(jnp.searchsorted(tile_end, jnp.arange(n_tiles, dtype=I32), side="right"),
                              E - 1).astype(I32)
    n_active = tile_end[-1:].astype(I32)
    tile_row = jnp.arange(n_tiles, dtype=I32) * tm
    valid_rows = jnp.where(tile_row < n_active[0] * tm,
                           jnp.clip((start + cnt)[tile_expert] - tile_row, 0, tm), 0).astype(I32)

    xs = _gather_rows(h, tok_of_slot, BF16)
    hid = _swiglu_up(xs, wg, wu, tile_expert, n_active, valid_rows, tm=tm, tn=_tile(F, 512, LANE), name="moe_up")
    y = _grouped_down(hid, wd, tile_expert, n_active, valid_rows, tm=tm, tn=_tile(D, 1024, LANE), name="moe_down")
    return _moe_combine(x, y, slots2, gates2)


def kernel(x, mem, norm_mix, norm_xattn, norm_mem, norm_ffn, ev_w_in, ev_a_qnorm, ev_a_knorm, ev_lambda_q1, ev_lambda_k1, ev_lambda_q2, ev_lambda_k2, ev_a_subln, ev_b_qnorm, ev_b_knorm, ev_w_out, od_w_in, od_qa_norm, od_w_qb, od_q_norm, od_kv_norm, od_kr_norm, od_w_uk, od_w_uv, od_w_iqb, od_ik_norm, od_w_out, xa_wq, xa_wk, xa_wv, xa_wo, xa_qnorm, xa_knorm, ffn_wg, ffn_wu, ffn_wd, moe_router, moe_wg, moe_wu, moe_wd):
    B, S, D = x.shape
    depth = norm_mix.shape[0]
    x = x.reshape(B * S, D)
    for l in range(depth):
        i = l // 2
        if l % 2 == 0:
            x = _even_mixer(x, l, i, B, S, norm_mix, ev_w_in, ev_a_qnorm, ev_a_knorm,
                            (ev_lambda_q1[i], ev_lambda_k1[i], ev_lambda_q2[i], ev_lambda_k2[i]),
                            ev_a_subln, ev_b_qnorm, ev_b_knorm, ev_w_out)
        else:
            x = _odd_mixer(x, l, i, B, S, norm_mix, od_w_in, od_qa_norm, od_w_qb, od_q_norm, od_kv_norm,
                           od_kr_norm, od_w_uk, od_w_uv, od_w_iqb, od_ik_norm, od_w_out)
        x = _cross_block(x, mem, l, norm_xattn, norm_mem, xa_wq, xa_wk, xa_wv, xa_wo, xa_qnorm, xa_knorm, B, S)
        if l % 2 == 0:
            x = _dense_ffn(x, norm_ffn[l], ffn_wg[i], ffn_wu[i], ffn_wd[i])
        else:
            x = _moe_ffn(x, norm_ffn[l], moe_router[i], moe_wg[i], moe_wu[i], moe_wd[i])
    return x.reshape(B, S, D)
```

```python
import functools
import math

import jax
import jax.numpy as jnp
import numpy as np
from jax import lax
from jax.experimental import pallas as pl
from jax.experimental.pallas import tpu as pltpu

F32 = jnp.float32
BF16 = jnp.bfloat16
I32 = jnp.int32

LANE = 128
MXU_WIDTH = 256
V7X_VMEM_BYTES = 64 * 1024 * 1024
VMEM_CAP = V7X_VMEM_BYTES - 4 * 1024 * 1024

HEAD_DIM = 128
ROPE_THETA = 10000.0
EPS = 1e-6
MOBA_BLOCK = 256
MOBA_TOPK = 3
C_NOPE = 128
C_ROPE = 64
IDX_DIM = 128
IDX_ROPE = 64
IDX_TOPK = 256
TOP_K = 2
NEG = -1e30
LOG2E = math.log2(math.e)
INT_MIN = -(2 ** 31)


def _tile(dim, pref, align):
    t = min(pref, dim)
    t -= t % align
    while t >= align:
        if dim % t == 0:
            return t
        t -= align
    return dim


def _params(sem, est_bytes):
    limit = int(min(max(est_bytes * 1.3 + (4 << 20), 32 << 20), VMEM_CAP))
    return pltpu.CompilerParams(dimension_semantics=sem, vmem_limit_bytes=limit)


def _nt(a, b):
    return lax.dot_general(a, b, (((1,), (1,)), ((), ())), preferred_element_type=F32)


def _softmax_steps(sts, carry, s_ref, acc_ref, value_tile):
    for c, st in enumerate(sts):
        s_ref[c] = st
    new = []
    for c in range(len(sts)):
        m_new = jnp.maximum(carry[c], jnp.max(s_ref[c], axis=0, keepdims=True))
        alpha = jnp.exp2(carry[c] - m_new)
        p = jnp.exp2(s_ref[c] - m_new)
        new.append(m_new)
        acc_ref[c] = alpha * acc_ref[c] + jnp.dot(value_tile(c), p.astype(BF16), preferred_element_type=F32)
    return tuple(new)


def _normalized(acc_ref, c, dv):
    acc = acc_ref[c]
    return acc[:dv] / acc[dv:dv + 1]


def _init_carry(n_chains, tq):
    return tuple(jnp.full((1, tq), NEG, F32) for _ in range(n_chains))


ONES_ROWS = 16


def _value_tiles(v, n_heads, tk):
    B, S, C = v.shape
    vt = v.reshape(B, S // tk, tk, n_heads, C // n_heads).transpose(0, 3, 1, 4, 2)
    extra = jnp.zeros(vt.shape[:3] + (ONES_ROWS, tk), v.dtype).at[..., 0, :].set(1)
    return jnp.concatenate([vt, extra], axis=3)


def _rmsnorm_body(x_ref, g_ref, o_ref):
    x = x_ref[...].astype(F32)
    ms = jnp.mean(x * x, axis=-1, keepdims=True)
    o_ref[...] = (x * lax.rsqrt(ms + EPS) * g_ref[...]).astype(o_ref.dtype)


def _rmsnorm(x, g, out_dtype, tm=256):
    M, D = x.shape
    tm = _tile(M, tm, 16)
    est = 2 * tm * D * (4 + 4)
    return pl.pallas_call(
        _rmsnorm_body,
        grid=(M // tm,),
        in_specs=[pl.BlockSpec((tm, D), lambda i: (i, 0)), pl.BlockSpec((1, D), lambda i: (0, 0))],
        out_specs=pl.BlockSpec((tm, D), lambda i: (i, 0)),
        out_shape=jax.ShapeDtypeStruct((M, D), out_dtype),
        compiler_params=_params(("parallel",), est),
        name="rmsnorm",
    )(x, g.reshape(1, D).astype(F32))


def _norm_router_body(x_ref, g_ref, rt_ref, h_ref, gate_ref, sel_ref, *, n_exp):
    x = x_ref[...]
    ms = jnp.mean(x * x, axis=-1, keepdims=True)
    h = x * lax.rsqrt(ms + EPS) * g_ref[...]
    h_ref[...] = h
    lane = lax.broadcasted_iota(I32, gate_ref.shape, 1)
    logits = jnp.full(gate_ref.shape, -jnp.inf, F32)
    for e in range(n_exp):
        col = jnp.sum(h * rt_ref[e:e + 1, :], axis=-1, keepdims=True)
        logits = jnp.where(lane == e, col, logits)
    m1 = jnp.max(logits, axis=-1, keepdims=True)
    i1 = jnp.min(jnp.where(logits == m1, lane, LANE), axis=-1, keepdims=True)
    rest = jnp.where(lane == i1, -jnp.inf, logits)
    m2 = jnp.max(rest, axis=-1, keepdims=True)
    i2 = jnp.min(jnp.where(rest == m2, lane, LANE), axis=-1, keepdims=True)
    e2 = jnp.exp(m2 - m1)
    den = 1.0 + e2
    gate_ref[...] = jnp.where(lane == i1, 1.0 / den, 0.0) + jnp.where(lane == i2, e2 / den, 0.0)
    sel_ref[...] = jnp.where((lane == i1) | (lane == i2), 1, 0).astype(I32)


def _norm_router(x, g, router, tm=256):
    M, D = x.shape
    n_exp = router.shape[1]
    tm = _tile(M, tm, 8)
    est = 2 * tm * D * 8 + 2 * 8 * D * 4
    h, gate, sel = pl.pallas_call(
        functools.partial(_norm_router_body, n_exp=n_exp),
        grid=(M // tm,),
        in_specs=[pl.BlockSpec((tm, D), lambda i: (i, 0)), pl.BlockSpec((1, D), lambda i: (0, 0)),
                  pl.BlockSpec((n_exp, D), lambda i: (0, 0))],
        out_specs=[pl.BlockSpec((tm, D), lambda i: (i, 0)), pl.BlockSpec((tm, LANE), lambda i: (i, 0)),
                   pl.BlockSpec((tm, LANE), lambda i: (i, 0))],
        out_shape=[jax.ShapeDtypeStruct((M, D), F32), jax.ShapeDtypeStruct((M, LANE), F32),
                   jax.ShapeDtypeStruct((M, LANE), I32)],
        compiler_params=_params(("parallel",), est),
        name="norm_router",
    )(x, g.reshape(1, D).astype(F32), router.T.astype(F32))
    return h, gate[:, :n_exp], sel[:, :n_exp]


EPI_NONE, EPI_NORM, EPI_NORM_ROPE, EPI_ROPE = 0, 1, 2, 3


def _head_epilogue(x, g, cos, sin, mode, rope_half):
    if mode in (EPI_NORM, EPI_NORM_ROPE):
        ss = jnp.dot((x * x).astype(BF16), jnp.ones((LANE, LANE), BF16), preferred_element_type=F32)
        x = x * lax.rsqrt(ss * (1.0 / LANE) + EPS) * g
    if mode in (EPI_NORM_ROPE, EPI_ROPE):
        if rope_half == LANE // 2:
            r = pltpu.roll(x, LANE // 2, axis=1)
        else:
            lane = lax.broadcasted_iota(I32, x.shape, 1)
            first = (lane % (2 * rope_half)) < rope_half
            r = jnp.where(first, pltpu.roll(x, LANE - rope_half, axis=1), pltpu.roll(x, rope_half, axis=1))
        x = x * cos + r * sin
    return x


def _mm_body(flags_ref, a_ref, w_ref, *rest, nk, has_resid, modes, rope_half, tn, cw):
    rest = list(rest)
    resid_ref = rest.pop(0) if has_resid else None
    gain_ref = cos_ref = sin_ref = None
    if modes:
        gain_ref, cos_ref, sin_ref = rest.pop(0), rest.pop(0), rest.pop(0)
    o_ref = rest.pop(0)
    acc_ref = rest.pop(0) if nk > 1 else None
    j = pl.program_id(1)
    k = pl.program_id(2)

    def emit(cols, acc, mode):
        if mode != EPI_NONE:
            for c in range(cols.start, cols.stop, LANE):
                sl = slice(c, c + LANE)
                y = _head_epilogue(acc[:, sl.start - cols.start:sl.stop - cols.start], gain_ref[:, sl],
                                   cos_ref[...], sin_ref[...], mode, rope_half)
                o_ref[:, sl] = y.astype(o_ref.dtype)
        elif has_resid:
            o_ref[:, cols] = (resid_ref[:, cols] + acc).astype(o_ref.dtype)
        else:
            o_ref[:, cols] = acc.astype(o_ref.dtype)

    if nk == 1:
        def run(mode):
            pending = None
            for c in range(0, tn, cw):
                cols = slice(c, c + cw)
                acc = jnp.dot(a_ref[...], w_ref[:, cols].astype(BF16), preferred_element_type=F32)
                if pending is not None:
                    emit(*pending, mode)
                pending = (cols, acc)
            emit(*pending, mode)

        if modes:
            flag = flags_ref[j]
            for mode in (EPI_NONE,) + modes:
                pl.when(flag == mode)(functools.partial(run, mode))
        else:
            run(EPI_NONE)
    else:
        part = jnp.dot(a_ref[...], w_ref[...], preferred_element_type=F32)

        @pl.when(k == 0)
        def _():
            acc_ref[...] = part

        @pl.when((k > 0) & (k < nk - 1))
        def _():
            acc_ref[...] += part

        @pl.when(k == nk - 1)
        def _():
            emit(slice(0, tn), acc_ref[...] + part, EPI_NONE)


def _matmul(a, w, *, out_dtype, name, resid=None, epi=None, tm=1024, tn=1024, tk=4096):
    M, K = a.shape
    N = w.shape[1]
    tm = _tile(epi["cos"].shape[0] if epi else M, tm, 16)
    tn = epi["tn"] if epi else _tile(N, tn, LANE)
    tk = _tile(K, tk, LANE)
    nk = K // tk
    assert a.dtype == BF16 and (w.dtype == BF16 or nk == 1) and (nk == 1 or not epi)
    a_bytes = a.dtype.itemsize
    o_bytes = jnp.dtype(out_dtype).itemsize
    modes = tuple(epi["modes"]) if epi else ()
    in_specs = [pl.BlockSpec((tm, tk), lambda i, j, k, f: (i, k)),
                pl.BlockSpec((tk, tn), lambda i, j, k, f: (k, j))]
    args = [a, w]
    est = 2 * (tm * tk * a_bytes + tk * tn * w.dtype.itemsize + tm * tn * o_bytes) + 3 * tm * tn * 4
    if resid is not None:
        in_specs.append(pl.BlockSpec((tm, tn), lambda i, j, k, f: (i, j)))
        args.append(resid)
        est += 2 * tm * tn * 4
    if epi:
        ns = epi["cos"].shape[0] // tm
        in_specs += [pl.BlockSpec((1, tn), lambda i, j, k, f: (0, j)),
                     pl.BlockSpec((tm, LANE), lambda i, j, k, f: (i % ns, 0)),
                     pl.BlockSpec((tm, LANE), lambda i, j, k, f: (i % ns, 0))]
        args += [epi["gain"].reshape(1, N).astype(F32), epi["cos"], epi["sin"]]
        flags = jnp.asarray(epi["flags"], I32)
        est += 4 * tm * LANE * 4
    else:
        flags = jnp.zeros((N // tn,), I32)
    body = functools.partial(_mm_body, nk=nk, has_resid=resid is not None, modes=modes,
                             rope_half=epi["rope_half"] if epi else 0, tn=tn, cw=math.gcd(tn, MXU_WIDTH))
    return pl.pallas_call(
        body,
        grid_spec=pltpu.PrefetchScalarGridSpec(
            num_scalar_prefetch=1,
            grid=(M // tm, N // tn, nk),
            in_specs=in_specs,
            out_specs=pl.BlockSpec((tm, tn), lambda i, j, k, f: (i, j)),
            scratch_shapes=[pltpu.VMEM((tm, tn), F32)] if nk > 1 else [],
        ),
        out_shape=jax.ShapeDtypeStruct((M, N), out_dtype),
        compiler_params=_params(("parallel", "parallel", "arbitrary"), est),
        name=name,
    )(flags, *args)


def _segments(tile_expert, n_active):
    T = tile_expert.shape[0]
    idx = jnp.arange(T, dtype=I32)
    prev = jnp.concatenate([tile_expert[:1] - 1, tile_expert[:-1]])
    first = (idx < n_active[0]) & (tile_expert != prev)
    first_idx = jnp.where(first, idx, T)
    after = jnp.concatenate([lax.cummin(first_idx[::-1])[::-1][1:], jnp.full((1,), T, I32)])
    nxt = jnp.where(after < T, tile_expert[jnp.minimum(after, T - 1)], -1)
    return first.astype(I32), nxt.astype(I32)


def _stream_weights(te_ref, first_ref, next_ref, w_refs, stage_refs, cast_refs, sem, tn):
    n = pl.program_id(0)
    m = pl.program_id(1)

    def copies(e, nn):
        cols = pl.ds(pl.multiple_of(nn * tn, tn), tn)
        return [pltpu.make_async_copy(w.at[e, :, cols], st, sem.at[i])
                for i, (w, st) in enumerate(zip(w_refs, stage_refs))]

    @pl.when(first_ref[m] == 1)
    def _():
        @pl.when((n == 0) & (m == 0))
        def _():
            for c in copies(te_ref[0], 0):
                c.start()

        for c in copies(te_ref[m], n):
            c.wait()
        for st, cb in zip(stage_refs, cast_refs):
            cb[...] = st[...].astype(BF16)
        last = next_ref[m] < 0
        e2 = jnp.where(last, te_ref[0], next_ref[m])
        n2 = jnp.where(last, n + 1, n)

        @pl.when(n2 < pl.num_programs(0))
        def _():
            for c in copies(e2, n2):
                c.start()


def _row_parts(nv_ref, o_ref, compute, parts):
    m = pl.program_id(1)
    hm = o_ref.shape[0] // parts
    for h in range(parts):
        rows = slice(h * hm, (h + 1) * hm)
        pl.when(nv_ref[m] > h * hm)(functools.partial(compute, rows))

        @pl.when(nv_ref[m] <= h * hm)
        def _(rows=rows):
            o_ref[rows, :] = jnp.zeros((hm, o_ref.shape[1]), o_ref.dtype)


def _swiglu_body(te_ref, first_ref, next_ref, nv_ref, a_ref, wg_ref, wu_ref, o_ref,
                 wgf_ref, wuf_ref, wgb_ref, wub_ref, sem):
    tn = o_ref.shape[1]
    _stream_weights(te_ref, first_ref, next_ref, (wg_ref, wu_ref), (wgf_ref, wuf_ref), (wgb_ref, wub_ref), sem, tn)
    cw = math.gcd(tn, MXU_WIDTH)

    def compute(rows):
        def emit(cols, g, u):
            o_ref[rows, cols] = (g * jax.nn.sigmoid(g) * u).astype(o_ref.dtype)

        pending = None
        for c in range(0, tn, cw):
            cols = slice(c, c + cw)
            g = jnp.dot(a_ref[rows, :], wgb_ref[:, cols], preferred_element_type=F32)
            u = jnp.dot(a_ref[rows, :], wub_ref[:, cols], preferred_element_type=F32)
            if pending is not None:
                emit(*pending)
            pending = (cols, g, u)
        emit(*pending)

    _row_parts(nv_ref, o_ref, compute, parts=1)


def _swiglu_up(a, wg, wu, tile_expert, n_active, valid_rows, *, tm, tn, name):
    M, D = a.shape
    F = wg.shape[2]
    first, nxt = _segments(tile_expert, n_active)
    est = 2 * (tm * D * 2 + tm * tn * 2) + 2 * D * tn * (4 + 2) + 4 * tm * tn * 4
    return pl.pallas_call(
        _swiglu_body,
        grid_spec=pltpu.PrefetchScalarGridSpec(
            num_scalar_prefetch=4,
            grid=(F // tn, M // tm),
            in_specs=[pl.BlockSpec((tm, D), lambda n, m, *_: (m, 0)),
                      pl.BlockSpec(memory_space=pl.ANY), pl.BlockSpec(memory_space=pl.ANY)],
            out_specs=pl.BlockSpec((tm, tn), lambda n, m, *_: (m, n)),
            scratch_shapes=[pltpu.VMEM((D, tn), F32), pltpu.VMEM((D, tn), F32),
                            pltpu.VMEM((D, tn), BF16), pltpu.VMEM((D, tn), BF16),
                            pltpu.SemaphoreType.DMA((2,))],
        ),
        out_shape=jax.ShapeDtypeStruct((M, F), BF16),
        compiler_params=_params(("arbitrary", "arbitrary"), est),
        name=name,
    )(tile_expert, first, nxt, valid_rows, a, wg, wu)


def _down_body(te_ref, first_ref, next_ref, nv_ref, a_ref, w_ref, o_ref, wf_ref, wb_ref, sem):
    _stream_weights(te_ref, first_ref, next_ref, (w_ref,), (wf_ref,), (wb_ref,), sem, o_ref.shape[1])

    def compute(rows):
        o_ref[rows, :] = jnp.dot(a_ref[rows, :], wb_ref[...], preferred_element_type=F32)

    _row_parts(nv_ref, o_ref, compute, parts=2)


def _grouped_down(a, wd, tile_expert, n_active, valid_rows, *, tm, tn, name):
    M, F = a.shape
    D = wd.shape[2]
    first, nxt = _segments(tile_expert, n_active)
    est = 2 * (tm * F * 2 + tm * tn * 4) + F * tn * (4 + 2) + 2 * tm * tn * 4
    return pl.pallas_call(
        _down_body,
        grid_spec=pltpu.PrefetchScalarGridSpec(
            num_scalar_prefetch=4,
            grid=(D // tn, M // tm),
            in_specs=[pl.BlockSpec((tm, F), lambda n, m, *_: (m, 0)), pl.BlockSpec(memory_space=pl.ANY)],
            out_specs=pl.BlockSpec((tm, tn), lambda n, m, *_: (m, n)),
            scratch_shapes=[pltpu.VMEM((F, tn), F32), pltpu.VMEM((F, tn), BF16), pltpu.SemaphoreType.DMA((1,))],
        ),
        out_shape=jax.ShapeDtypeStruct((M, D), F32),
        compiler_params=_params(("arbitrary", "arbitrary"), est),
        name=name,
    )(tile_expert, first, nxt, valid_rows, a, wd)


GATHER_UNROLL = 8


def _gather_body(idx_ref, src_ref, o_ref, buf_ref, sem, *, tg):
    step = pl.program_id(0)

    def row_copy(slot, i, row):
        return pltpu.make_async_copy(src_ref.at[pl.ds(row, 1)], buf_ref.at[slot, pl.ds(i, 1)], sem.at[slot])

    def issue(s):
        slot = s % 2

        def start(i, c):
            row_copy(slot, i, idx_ref[s * tg + i]).start()
            return c

        lax.fori_loop(0, tg, start, 0, unroll=GATHER_UNROLL)

    @pl.when(step == 0)
    def _():
        issue(step)

    @pl.when(step + 1 < pl.num_programs(0))
    def _():
        issue(step + 1)

    slot = step % 2

    def wait(i, c):
        row_copy(slot, i, 0).wait()
        return c

    lax.fori_loop(0, tg, wait, 0, unroll=GATHER_UNROLL)
    o_ref[...] = buf_ref[slot].astype(o_ref.dtype)


def _gather_rows(src, idx, out_dtype, tg=512):
    P = idx.shape[0]
    D = src.shape[1]
    tg = _tile(P, tg, 16)
    est = 2 * tg * D * 4 + 2 * tg * D * 2
    return pl.pallas_call(
        functools.partial(_gather_body, tg=tg),
        grid_spec=pltpu.PrefetchScalarGridSpec(
            num_scalar_prefetch=1,
            grid=(P // tg,),
            in_specs=[pl.BlockSpec(memory_space=pl.ANY)],
            out_specs=pl.BlockSpec((tg, D), lambda i, idx: (i, 0)),
            scratch_shapes=[pltpu.VMEM((2, tg, D), src.dtype), pltpu.SemaphoreType.DMA((2,))],
        ),
        out_shape=jax.ShapeDtypeStruct((P, D), out_dtype),
        compiler_params=_params(("arbitrary",), est),
        name="moe_dispatch_gather",
    )(idx, src)


def _combine_body(slot_ref, x_ref, g_ref, y_ref, o_ref, buf_ref, sem, *, tc):
    step = pl.program_id(0)

    def row_copy(slot, i, k, row):
        return pltpu.make_async_copy(y_ref.at[pl.ds(row, 1)], buf_ref.at[slot, k, pl.ds(i, 1)], sem.at[slot])

    def issue(s):
        slot = s % 2

        def start(i, c):
            for k in range(TOP_K):
                row_copy(slot, i, k, slot_ref[(s * tc + i) * TOP_K + k]).start()
            return c

        lax.fori_loop(0, tc, start, 0, unroll=GATHER_UNROLL // TOP_K)

    @pl.when(step == 0)
    def _():
        issue(step)

    @pl.when(step + 1 < pl.num_programs(0))
    def _():
        issue(step + 1)

    slot = step % 2

    def wait(i, c):
        for k in range(TOP_K):
            row_copy(slot, i, k, 0).wait()
        return c

    lax.fori_loop(0, tc, wait, 0, unroll=GATHER_UNROLL // TOP_K)
    g = g_ref[...]
    o_ref[...] = x_ref[...] + (g[:, 0:1] * buf_ref[slot, 0] + g[:, 1:2] * buf_ref[slot, 1])


def _moe_combine(x, y, slots, gates, tc=128):
    M, D = x.shape
    tc = _tile(M, tc, 8)
    est = 4 * tc * D * 4 + 2 * TOP_K * tc * D * 4
    return pl.pallas_call(
        functools.partial(_combine_body, tc=tc),
        grid_spec=pltpu.PrefetchScalarGridSpec(
            num_scalar_prefetch=1,
            grid=(M // tc,),
            in_specs=[pl.BlockSpec((tc, D), lambda i, s: (i, 0)), pl.BlockSpec((tc, TOP_K), lambda i, s: (i, 0)),
                      pl.BlockSpec(memory_space=pl.ANY)],
            out_specs=pl.BlockSpec((tc, D), lambda i, s: (i, 0)),
            scratch_shapes=[pltpu.VMEM((2, TOP_K, tc, D), F32), pltpu.SemaphoreType.DMA((2,))],
        ),
        out_shape=jax.ShapeDtypeStruct((M, D), F32),
        compiler_params=_params(("arbitrary",), est),
        name="moe_combine",
    )(slots.reshape(-1), x, gates, y)


def _diff_attn_body(q_ref, k_ref, vt_ref, lq1_ref, lk1_ref, lq2_ref, lk2_ref, g_ref, o_ref, s_ref, acc_ref,
                    *, tq, hg, lam_init):
    qi = pl.program_id(2)
    w = 2 * HEAD_DIM
    cols = [slice(hh * w + mi * HEAD_DIM, hh * w + (mi + 1) * HEAD_DIM) for hh in range(hg) for mi in range(2)]
    qs = [q_ref[:, c] for c in cols]

    def step(n, carry, mask):
        off = pl.multiple_of(n * tq, tq)
        sts = [_nt(k_ref[pl.ds(off, tq), col_sl], qs[c]) for c, col_sl in enumerate(cols)]
        if mask is not None:
            sts = [jnp.where(mask, st, NEG) for st in sts]
        return _softmax_steps(sts, carry, s_ref, acc_ref, lambda c: vt_ref[c // 2, n])

    acc_ref[...] = jnp.zeros(acc_ref.shape, F32)
    carry = lax.fori_loop(0, qi, lambda n, c: step(n, c, None), _init_carry(len(cols), tq))
    key = lax.broadcasted_iota(I32, (tq, tq), 0)
    qry = lax.broadcasted_iota(I32, (tq, tq), 1)
    step(qi, carry, key <= qry)
    lam = (jnp.exp(jnp.sum(lq1_ref[...] * lk1_ref[...], axis=-1, keepdims=True))
           - jnp.exp(jnp.sum(lq2_ref[...] * lk2_ref[...], axis=-1, keepdims=True)) + lam_init)
    for hh in range(hg):
        y = (_normalized(acc_ref, 2 * hh, w) - lam * _normalized(acc_ref, 2 * hh + 1, w)).T
        ms = jnp.mean(y * y, axis=-1, keepdims=True)
        hs = slice(hh * w, (hh + 1) * w)
        o_ref[:, hs] = (y * lax.rsqrt(ms + EPS) * g_ref[...] * (1.0 - lam_init)).astype(o_ref.dtype)


def _diff_attention(qkv, vt, n_heads, lam_params, subln, lam_init, tq, hg=4):
    B, S, _ = qkv.shape
    w = 2 * HEAD_DIM
    hg = math.gcd(hg, n_heads)
    ng = n_heads // hg
    vec = pl.BlockSpec((1, HEAD_DIM), lambda b, g, i: (0, 0))
    est = 2 * (2 * tq * hg * w * 2 + 2 * S * hg * w * 2) + 2 * hg * (3 * tq * tq * 4 + tq * w * 4)
    return pl.pallas_call(
        functools.partial(_diff_attn_body, tq=tq, hg=hg, lam_init=lam_init),
        grid=(B, ng, S // tq),
        in_specs=[pl.BlockSpec((None, tq, hg * w), lambda b, g, i: (b, i, g)),
                  pl.BlockSpec((None, S, hg * w), lambda b, g, i: (b, 0, ng + g)),
                  pl.BlockSpec((None, hg, S // tq, w + ONES_ROWS, tq), lambda b, g, i: (b, g, 0, 0, 0)),
                  vec, vec, vec, vec,
                  pl.BlockSpec((1, w), lambda b, g, i: (0, 0))],
        out_specs=pl.BlockSpec((None, tq, hg * w), lambda b, g, i: (b, i, g)),
        out_shape=jax.ShapeDtypeStruct((B, S, n_heads * w), BF16),
        scratch_shapes=[pltpu.VMEM((2 * hg, tq, tq), F32), pltpu.VMEM((2 * hg, w + ONES_ROWS, tq), F32)],
        compiler_params=_params(("parallel", "parallel", "arbitrary"), est),
        name="diff_attention",
    )(qkv, qkv, vt, *[p.reshape(1, HEAD_DIM).astype(F32) for p in lam_params],
      subln.reshape(1, w).astype(F32))


def _moba_body(q_ref, k_ref, vt_ref, o_ref, kmean_ref, bias_ref, s_ref, acc_ref, *, nb, hg):
    qi = pl.program_id(2)
    blk = MOBA_BLOCK
    d = HEAD_DIM
    heads = [slice(hh * d, (hh + 1) * d) for hh in range(hg)]

    @pl.when(qi == 0)
    def _():
        for hh, hs in enumerate(heads):
            for n in range(nb):
                kb = k_ref[n * blk:(n + 1) * blk, hs].astype(F32)
                kmean_ref[hh, n:n + 1, :] = jnp.mean(kb, axis=0, keepdims=True)

    blk_id = lax.broadcasted_iota(I32, (nb, blk), 0)
    qs = []
    for hh, hs in enumerate(heads):
        q = q_ref[:, hs]
        km = kmean_ref[hh]
        km_hi = km.astype(BF16)
        km_lo = (km - km_hi.astype(F32)).astype(BF16)
        gate = _nt(km_hi, q) + _nt(km_lo, q)
        gate = jnp.where(blk_id < qi, gate, -jnp.inf)
        sel = jnp.zeros(gate.shape, jnp.bool_)
        for _ in range(MOBA_TOPK):
            mx = jnp.max(gate, axis=0, keepdims=True)
            idx = jnp.min(jnp.where(gate == mx, blk_id, nb), axis=0, keepdims=True)
            sel = sel | ((blk_id == idx) & (mx > -jnp.inf))
            gate = jnp.where(blk_id == idx, -jnp.inf, gate)
        qs.append(q)
        bias_ref[hh] = jnp.where(sel, 0.0, NEG)

    def step(n, carry, mask):
        off = pl.multiple_of(n * blk, blk)
        sts = [_nt(k_ref[pl.ds(off, blk), hs], qs[hh]) for hh, hs in enumerate(heads)]
        if mask is None:
            sts = [st + bias_ref[hh, pl.ds(n, 1), :] for hh, st in enumerate(sts)]
        else:
            sts = [jnp.where(mask, st, NEG) for st in sts]
        return _softmax_steps(sts, carry, s_ref, acc_ref, lambda hh: vt_ref[hh, n])

    acc_ref[...] = jnp.zeros(acc_ref.shape, F32)
    carry = lax.fori_loop(0, qi, lambda n, c: step(n, c, None), _init_carry(hg, blk))
    key = lax.broadcasted_iota(I32, (blk, blk), 0)
    qry = lax.broadcasted_iota(I32, (blk, blk), 1)
    step(qi, carry, key <= qry)
    for hh, hs in enumerate(heads):
        o_ref[:, hs] = _normalized(acc_ref, hh, d).T.astype(o_ref.dtype)


def _moba_attention(qkv, vt, n_heads, q_col, k_col, hg=8):
    B, S, _ = qkv.shape
    assert S % MOBA_BLOCK == 0 and S // MOBA_BLOCK >= MOBA_TOPK
    nb = S // MOBA_BLOCK
    d = HEAD_DIM
    hg = math.gcd(math.gcd(hg, n_heads), math.gcd(q_col, k_col))
    est = 2 * (2 * MOBA_BLOCK * hg * d * 2 + 2 * S * hg * d * 2) + hg * (3 * MOBA_BLOCK * MOBA_BLOCK * 4)
    return pl.pallas_call(
        functools.partial(_moba_body, nb=nb, hg=hg),
        grid=(B, n_heads // hg, nb),
        in_specs=[pl.BlockSpec((None, MOBA_BLOCK, hg * d), lambda b, g, i: (b, i, q_col // hg + g)),
                  pl.BlockSpec((None, S, hg * d), lambda b, g, i: (b, 0, k_col // hg + g)),
                  pl.BlockSpec((None, hg, nb, d + ONES_ROWS, MOBA_BLOCK), lambda b, g, i: (b, g, 0, 0, 0))],
        out_specs=pl.BlockSpec((None, MOBA_BLOCK, hg * d), lambda b, g, i: (b, i, g)),
        out_shape=jax.ShapeDtypeStruct((B, S, n_heads * d), BF16),
        scratch_shapes=[pltpu.VMEM((hg, nb, d), F32), pltpu.VMEM((hg, nb, MOBA_BLOCK), F32),
                        pltpu.VMEM((hg, MOBA_BLOCK, MOBA_BLOCK), F32),
                        pltpu.VMEM((hg, d + ONES_ROWS, MOBA_BLOCK), F32)],
        compiler_params=_params(("parallel", "parallel", "arbitrary"), est),
        name="moba_attention",
    )(qkv, qkv, vt)


def _xattn_body(q_ref, k_ref, v_ref, o_ref, *, n_heads):
    for h in range(n_heads):
        sl = slice(h * HEAD_DIM, (h + 1) * HEAD_DIM)
        s = _nt(q_ref[:, sl], k_ref[:, sl])
        m = jnp.max(s, axis=-1, keepdims=True)
        p = jnp.exp2(s - m)
        l = jnp.sum(p, axis=-1, keepdims=True)
        o = jnp.dot(p.astype(BF16), v_ref[:, sl], preferred_element_type=F32)
        o_ref[:, sl] = (o / l).astype(o_ref.dtype)


def _cross_attention(q, kv, n_heads, tq=512):
    B, S, X = q.shape
    M = kv.shape[1]
    tq = _tile(S, tq, 16)
    est = 2 * (2 * tq * X * 2 + 2 * M * X * 2) + 6 * tq * M * 4
    return pl.pallas_call(
        functools.partial(_xattn_body, n_heads=n_heads),
        grid=(B, S // tq),
        in_specs=[pl.BlockSpec((None, tq, X), lambda b, i: (b, i, 0)),
                  pl.BlockSpec((None, M, X), lambda b, i: (b, 0, 0)),
                  pl.BlockSpec((None, M, X), lambda b, i: (b, 0, 1))],
        out_specs=pl.BlockSpec((None, tq, X), lambda b, i: (b, i, 0)),
        out_shape=jax.ShapeDtypeStruct((B, S, X), BF16),
        compiler_params=_params(("parallel", "parallel"), est),
        name="cross_attention",
    )(q, kv, kv)


def _odd_prep_body(x_ref, gqa_ref, gkv_ref, gkr_ref, gik_ref, cos_ref, sin_ref,
                   qa_ref, ckv_ref, kr_ref, ik_ref, iw_ref, *, c1, c2, iw_scale):
    def norm(x, g):
        ms = jnp.mean(x * x, axis=-1, keepdims=True)
        return x * lax.rsqrt(ms + EPS) * g

    def rope(y):
        lane = lax.broadcasted_iota(I32, y.shape, 1)
        half = C_ROPE // 2
        first = (lane % C_ROPE) < half
        r = jnp.where(first, pltpu.roll(y, LANE - half, axis=1), pltpu.roll(y, half, axis=1))
        return y * cos_ref[...] + r * sin_ref[...]

    qa_ref[...] = norm(x_ref[:, :c1], gqa_ref[...]).astype(qa_ref.dtype)
    ckv_ref[...] = norm(x_ref[:, c1:c2], gkv_ref[...]).astype(ckv_ref.dtype)
    slab_a = x_ref[:, c2:c2 + LANE]
    slab_b = x_ref[:, c2 + LANE:c2 + 2 * LANE]
    lane = lax.broadcasted_iota(I32, slab_a.shape, 1)
    low = lane < C_ROPE
    ms = jnp.sum(jnp.where(low, slab_a * slab_a, 0.0), axis=-1, keepdims=True) / C_ROPE
    kr = rope(slab_a * lax.rsqrt(ms + EPS) * gkr_ref[...])
    kr_ref[0] = kr.astype(kr_ref.dtype)
    kr_ref[1] = pltpu.roll(kr, C_ROPE, axis=1).astype(kr_ref.dtype)
    rot_a = pltpu.roll(slab_a, C_ROPE, axis=1)
    rot_b = pltpu.roll(slab_b, C_ROPE, axis=1)
    ik = jnp.where(low, rot_a, rot_b)
    ik_ref[...] = rope(norm(ik, gik_ref[...])).astype(ik_ref.dtype)
    iw_ref[...] = rot_b * iw_scale


def _odd_prep(x, c1, c2, g_qa, g_kv, g_kr, g_ik, cos_p, sin_p, iw_scale, tm=256):
    B, S, C = x.shape
    assert C == c2 + 2 * LANE
    tm = _tile(S, tm, 16)
    vec = lambda n: pl.BlockSpec((1, n), lambda b, i: (0, 0))
    tab = pl.BlockSpec((tm, LANE), lambda b, i: (i, 0))
    est = 2 * tm * C * 4 * 2
    return pl.pallas_call(
        functools.partial(_odd_prep_body, c1=c1, c2=c2, iw_scale=iw_scale),
        grid=(B, S // tm),
        in_specs=[pl.BlockSpec((None, tm, C), lambda b, i: (b, i, 0)),
                  vec(c1), vec(c2 - c1), vec(LANE), vec(LANE), tab, tab],
        out_specs=[pl.BlockSpec((None, tm, c1), lambda b, i: (b, i, 0)),
                   pl.BlockSpec((None, tm, c2 - c1), lambda b, i: (b, i, 0)),
                   pl.BlockSpec((None, 2, tm, LANE), lambda b, i: (b, 0, i, 0)),
                   pl.BlockSpec((None, tm, LANE), lambda b, i: (b, i, 0)),
                   pl.BlockSpec((None, tm, LANE), lambda b, i: (b, i, 0))],
        out_shape=[jax.ShapeDtypeStruct((B, S, c1), BF16),
                   jax.ShapeDtypeStruct((B, S, c2 - c1), BF16),
                   jax.ShapeDtypeStruct((B, 2, S, LANE), BF16),
                   jax.ShapeDtypeStruct((B, S, LANE), BF16),
                   jax.ShapeDtypeStruct((B, S, LANE), F32)],
        compiler_params=_params(("parallel", "parallel"), est),
        name="odd_prep",
    )(x, g_qa.reshape(1, -1), g_kv.reshape(1, -1), g_kr.reshape(1, -1), g_ik.reshape(1, -1), cos_p, sin_p)


def _q_prep_body(x_ref, gn_ref, gr_ref, cos_ref, sin_ref, qn_ref, qr_ref, *, n_heads):
    nope_w = n_heads * C_NOPE
    width = C_NOPE + C_ROPE
    half = C_ROPE // 2
    lane = lax.broadcasted_iota(I32, (x_ref.shape[0], LANE), 1)
    low = lane < C_ROPE
    first = (lane % C_ROPE) < half
    for p in range(n_heads // 2):
        n0 = x_ref[:, (2 * p) * LANE:(2 * p + 1) * LANE]
        n1 = x_ref[:, (2 * p + 1) * LANE:(2 * p + 2) * LANE]
        r = x_ref[:, nope_w + p * LANE:nope_w + (p + 1) * LANE]
        r2 = r * r
        ss0 = jnp.sum(n0 * n0, axis=-1, keepdims=True) + jnp.sum(jnp.where(low, r2, 0.0), axis=-1, keepdims=True)
        ss1 = jnp.sum(n1 * n1, axis=-1, keepdims=True) + jnp.sum(jnp.where(low, 0.0, r2), axis=-1, keepdims=True)
        inv0 = lax.rsqrt(ss0 / width + EPS)
        inv1 = lax.rsqrt(ss1 / width + EPS)
        qn_ref[:, (2 * p) * LANE:(2 * p + 1) * LANE] = (n0 * inv0 * gn_ref[...]).astype(qn_ref.dtype)
        qn_ref[:, (2 * p + 1) * LANE:(2 * p + 2) * LANE] = (n1 * inv1 * gn_ref[...]).astype(qn_ref.dtype)
        y = r * jnp.where(low, inv0, inv1) * gr_ref[...]
        rot = jnp.where(first, pltpu.roll(y, LANE - half, axis=1), pltpu.roll(y, half, axis=1))
        qr_ref[:, p * LANE:(p + 1) * LANE] = (y * cos_ref[...] + rot * sin_ref[...]).astype(qr_ref.dtype)


def _q_prep(x, n_heads, g_nope, g_rope2, cos_q, sin_q, tm=256):
    B, S, C = x.shape
    tm = _tile(S, tm, 16)
    nope_w, rope_w = n_heads * C_NOPE, n_heads * C_ROPE
    vec = pl.BlockSpec((1, LANE), lambda b, i: (0, 0))
    tab = pl.BlockSpec((tm, LANE), lambda b, i: (i, 0))
    est = 2 * tm * C * 6
    return pl.pallas_call(
        functools.partial(_q_prep_body, n_heads=n_heads),
        grid=(B, S // tm),
        in_specs=[pl.BlockSpec((None, tm, C), lambda b, i: (b, i, 0)), vec, vec, tab, tab],
        out_specs=[pl.BlockSpec((None, tm, nope_w), lambda b, i: (b, i, 0)),
                   pl.BlockSpec((None, tm, rope_w), lambda b, i: (b, i, 0))],
        out_shape=[jax.ShapeDtypeStruct((B, S, nope_w), BF16), jax.ShapeDtypeStruct((B, S, rope_w), BF16)],
        compiler_params=_params(("parallel", "parallel"), est),
        name="dsa_q_prep",
    )(x, g_nope.reshape(1, LANE), g_rope2.reshape(1, LANE), cos_q, sin_q)


def _indexer_body(ik_ref, iq_ref, iwt_ref, o_ref, key_ref, *, tq, n_heads, n_keep):
    S = ik_ref.shape[0]
    qi = pl.program_id(1)
    n_tiles = qi + 1
    t_idx = qi * tq + lax.broadcasted_iota(I32, (tq, tq), 1)
    s_loc = lax.broadcasted_iota(I32, (tq, tq), 0)

    def score_tile(kt, c):
        off = pl.multiple_of(kt * tq, tq)
        ikt = ik_ref[pl.ds(off, tq), :]
        acc = jnp.zeros((tq, tq), F32)
        for h in range(n_heads):
            r = _nt(ikt, iq_ref[:, h * IDX_DIM:(h + 1) * IDX_DIM])
            acc = acc + jnp.maximum(r, 0.0) * iwt_ref[h:h + 1, :]
        acc = jnp.where(off + s_loc <= t_idx, acc, -jnp.inf)
        bits = lax.bitcast_convert_type(acc, I32)
        key_ref[pl.ds(off, tq), :] = bits ^ ((bits >> 31) & 0x7FFFFFFF)
        return c

    lax.fori_loop(0, n_tiles, score_tile, 0)

    def count(hit):
        def body(kt, cnt):
            off = pl.multiple_of(kt * tq, tq)
            one = jnp.where(hit(key_ref[pl.ds(off, tq), :], off + s_loc), 1, 0).astype(I32)
            return cnt + jnp.sum(one.reshape(tq // 8, 8, tq), axis=0)
        cnt = lax.fori_loop(0, n_tiles, body, jnp.zeros((8, tq), I32))
        return jnp.sum(cnt, axis=0, keepdims=True)

    n_pos = count(lambda k, s: k >= 0)
    thr = jnp.where(n_pos >= n_keep, 0, INT_MIN).astype(I32)

    def bit_step(i, thr):
        cand = thr + lax.shift_left(jnp.int32(1), 30 - i)
        return jnp.where(count(lambda k, s: k >= cand) >= n_keep, cand, thr)

    thr = lax.fori_loop(0, 31, bit_step, thr)

    def tie_cut():
        need = n_keep - count(lambda k, s: k > thr)
        bits = S.bit_length()

        def step(i, cut):
            cand = cut + lax.shift_left(jnp.int32(1), bits - 1 - i)
            below = count(lambda k, s: (k == thr) & (s < cand))
            return jnp.where(below < need, cand, cut)

        return lax.fori_loop(0, bits, step, jnp.zeros((1, tq), I32))

    has_ties = jnp.max(count(lambda k, s: k >= thr)) > n_keep
    cut = lax.cond(has_ties, tie_cut, lambda: jnp.full((1, tq), S, I32))

    def out_tile(kt, c):
        off = pl.multiple_of(kt * tq, tq)
        key = key_ref[pl.ds(off, tq), :]
        s_idx = off + s_loc
        ok = ((key > thr) | ((key == thr) & (s_idx <= cut))) & (s_idx <= t_idx)
        o_ref[pl.ds(off, tq), :] = jnp.where(ok, 0.0, NEG).astype(o_ref.dtype)
        return c

    def neg_tile(kt, c):
        off = pl.multiple_of(kt * tq, tq)
        o_ref[pl.ds(off, tq), :] = jnp.full((tq, tq), NEG, o_ref.dtype)
        return c

    lax.fori_loop(0, n_tiles, out_tile, 0)
    lax.fori_loop(n_tiles, S // tq, neg_tile, 0)


def _indexer(ik, iq, iwt, n_keep, tq=256):
    B, S, _ = ik.shape
    n_heads = iwt.shape[1]
    tq = _tile(S, tq, LANE)
    est = 2 * (S * IDX_DIM * 2 + tq * n_heads * IDX_DIM * 2 + n_heads * tq * 4 + S * tq * 2) + S * tq * 4 + 8 * tq * tq * 4
    return pl.pallas_call(
        functools.partial(_indexer_body, tq=tq, n_heads=n_heads, n_keep=n_keep),
        grid=(B, S // tq),
        in_specs=[pl.BlockSpec((None, S, IDX_DIM), lambda b, i: (b, 0, 0)),
                  pl.BlockSpec((None, tq, n_heads * IDX_DIM), lambda b, i: (b, i, 0)),
                  pl.BlockSpec((None, n_heads, tq), lambda b, i: (b, 0, i))],
        out_specs=pl.BlockSpec((None, S, tq), lambda b, i: (b, 0, i)),
        out_shape=jax.ShapeDtypeStruct((B, S, S), BF16),
        scratch_shapes=[pltpu.VMEM((S, tq), I32)],
        compiler_params=_params(("parallel", "arbitrary"), est),
        name="dsa_indexer",
    )(ik, iq, iwt)


def _dsa_attn_body(qn_ref, qr_ref, kn_ref, kr_ref, vt_ref, bias_ref, o_ref, s_ref, acc_ref, *, tq, hg):
    qi = pl.program_id(2)
    heads = [slice(hh * C_NOPE, (hh + 1) * C_NOPE) for hh in range(hg)]
    qs = [jnp.concatenate([qn_ref[:, hs], qr_ref[:, (hh // 2) * LANE:(hh // 2 + 1) * LANE]], axis=1)
          for hh, hs in enumerate(heads)]

    def scores(n):
        off = pl.multiple_of(n * tq, tq)
        bias = bias_ref[pl.ds(off, tq), :].astype(F32)
        kr = [kr_ref[par, pl.ds(off, tq), :] for par in range(2)]
        return tuple(bias + _nt(jnp.concatenate([kn_ref[pl.ds(off, tq), hs], kr[hh % 2]], axis=1), qs[hh])
                     for hh, hs in enumerate(heads))

    def body(n, carry):
        return _softmax_steps(scores(n), carry, s_ref, acc_ref, lambda hh: vt_ref[hh, n])

    acc_ref[...] = jnp.zeros(acc_ref.shape, F32)
    lax.fori_loop(0, qi + 1, body, _init_carry(hg, tq))
    for hh, hs in enumerate(heads):
        o_ref[:, hs] = _normalized(acc_ref, hh, C_NOPE).T.astype(o_ref.dtype)


def _dsa_attention(qn, qr, kv, kr2, vt, bias, n_heads, tq, hg=8):
    B, S, _ = qn.shape
    hg = min(hg, n_heads)
    assert hg % 2 == 0 and n_heads % hg == 0
    ng = n_heads // hg
    nk = S // tq
    est = 2 * (tq * hg * 192 * 2 + 2 * S * hg * LANE * 2 + 2 * S * LANE * 2 + S * tq * 2 + tq * hg * LANE * 2) \
        + hg * 3 * tq * tq * 4
    return pl.pallas_call(
        functools.partial(_dsa_attn_body, tq=tq, hg=hg),
        grid=(B, ng, S // tq),
        in_specs=[pl.BlockSpec((None, tq, hg * C_NOPE), lambda b, g, i: (b, i, g)),
                  pl.BlockSpec((None, tq, hg * C_ROPE), lambda b, g, i: (b, i, g)),
                  pl.BlockSpec((None, S, hg * C_NOPE), lambda b, g, i: (b, 0, g)),
                  pl.BlockSpec((None, 2, S, LANE), lambda b, g, i: (b, 0, 0, 0)),
                  pl.BlockSpec((None, hg, nk, C_NOPE + ONES_ROWS, tq), lambda b, g, i: (b, g, 0, 0, 0)),
                  pl.BlockSpec((None, S, tq), lambda b, g, i: (b, 0, i))],
        out_specs=pl.BlockSpec((None, tq, hg * C_NOPE), lambda b, g, i: (b, i, g)),
        out_shape=jax.ShapeDtypeStruct((B, S, n_heads * C_NOPE), BF16),
        scratch_shapes=[pltpu.VMEM((hg, tq, tq), F32), pltpu.VMEM((hg, C_NOPE + ONES_ROWS, tq), F32)],
        compiler_params=_params(("parallel", "parallel", "arbitrary"), est),
        name="dsa_attention",
    )(qn, qr, kv, kr2, vt, bias)


def _rope_tables(seq, dim):
    inv_freq = ROPE_THETA ** (-jnp.arange(0, dim, 2, dtype=F32) / dim)
    ang = jnp.arange(seq, dtype=F32)[:, None] * inv_freq[None, :]
    return jnp.cos(ang), jnp.sin(ang)


def _cross_block(x, mem, l, norm_xattn, norm_mem, xa_wq, xa_wk, xa_wv, xa_wo, xa_qnorm, xa_knorm, B, S):
    N, D = x.shape
    X = xa_wq.shape[2]
    n_heads = X // HEAD_DIM
    M = mem.shape[1]
    h = _rmsnorm(x, norm_xattn[l], BF16)
    mn = _rmsnorm(mem.reshape(B * M, D), norm_mem[l], BF16)
    ones = jnp.ones((S, LANE), F32)
    q = _matmul(h, xa_wq[l].astype(BF16), out_dtype=BF16, name="xattn_q",
                epi=dict(flags=[EPI_NORM] * (X // _tile(X, 1024, LANE)), modes=(EPI_NORM,), rope_half=0,
                         tn=_tile(X, 1024, LANE), cos=ones, sin=ones,
                         gain=jnp.tile(xa_qnorm[l] * (HEAD_DIM ** -0.5 * LOG2E), n_heads)))
    wkv = jnp.concatenate([xa_wk[l], xa_wv[l]], axis=1).astype(BF16)
    tn = _tile(X, 1024, LANE)
    kv = _matmul(mn, wkv, out_dtype=BF16, name="xattn_kv",
                 epi=dict(flags=[EPI_NORM] * (X // tn) + [EPI_NONE] * (X // tn), modes=(EPI_NORM,), rope_half=0,
                          tn=tn, cos=jnp.ones((M, LANE), F32), sin=jnp.ones((M, LANE), F32),
                          gain=jnp.concatenate([jnp.tile(xa_knorm[l], n_heads), jnp.ones((X,), F32)])))
    o = _cross_attention(q.reshape(B, S, X), kv.reshape(B, M, 2 * X), n_heads)
    return _matmul(o.reshape(N, X), xa_wo[l].astype(BF16), out_dtype=F32, name="xattn_out", resid=x)


def _even_mixer(x, l, i, B, S, norm_mix, ev_w_in, ev_a_qnorm, ev_a_knorm, lam_params, ev_a_subln,
                ev_b_qnorm, ev_b_knorm, ev_w_out):
    N, D = x.shape
    a_heads = D // (4 * HEAD_DIM)
    b_heads = D // (2 * HEAD_DIM)
    aw = a_heads * 2 * HEAD_DIM
    bw = b_heads * HEAD_DIM
    width = 3 * aw + 3 * bw
    scale = HEAD_DIM ** -0.5 * LOG2E
    cos, sin = _rope_tables(S, HEAD_DIM)
    cos2 = jnp.concatenate([cos, cos], axis=-1)
    sin2 = jnp.concatenate([-sin, sin], axis=-1)
    tn = _tile(math.gcd(aw, bw), 1024, LANE)
    seg = [(aw, EPI_NORM_ROPE, ev_a_qnorm[i] * scale), (aw, EPI_NORM_ROPE, ev_a_knorm[i]), (aw, EPI_NONE, None),
           (bw, EPI_NORM_ROPE, ev_b_qnorm[i] * scale), (bw, EPI_NORM_ROPE, ev_b_knorm[i]), (bw, EPI_NONE, None)]
    flags, gains = [], []
    for w, flag, g in seg:
        flags += [flag] * (w // tn)
        gains.append(jnp.ones((w,), F32) if g is None else jnp.tile(g.astype(F32), w // HEAD_DIM))
    h = _rmsnorm(x, norm_mix[l], BF16)
    qkv = _matmul(h, ev_w_in[i], out_dtype=BF16, name="even_in",
                  epi=dict(flags=flags, modes=(EPI_NORM_ROPE,), rope_half=HEAD_DIM // 2, tn=tn,
                           cos=cos2, sin=sin2, gain=jnp.concatenate(gains)))
    qkv = qkv.reshape(B, S, width)
    lam_init = 0.8 - 0.6 * math.exp(-0.3 * l)
    tq = _tile(S, 256, LANE)
    ya = _diff_attention(qkv, _value_tiles(qkv[:, :, 2 * aw:3 * aw], a_heads, tq), a_heads, lam_params,
                         ev_a_subln[i], lam_init, tq)
    c0 = 3 * aw // HEAD_DIM
    yb = _moba_attention(qkv, _value_tiles(qkv[:, :, 3 * aw + 2 * bw:], b_heads, MOBA_BLOCK), b_heads,
                         c0, c0 + b_heads)
    y = jnp.concatenate([ya, yb], axis=-1).reshape(N, aw + bw)
    return _matmul(y, ev_w_out[i].astype(BF16), out_dtype=F32, name="even_out", resid=x)


def _odd_mixer(x, l, i, B, S, norm_mix, od_w_in, od_qa_norm, od_w_qb, od_q_norm, od_kv_norm, od_kr_norm,
               od_w_uk, od_w_uv, od_w_iqb, od_ik_norm, od_w_out):
    N, D = x.shape
    c1 = od_qa_norm.shape[1]
    kv_rank = od_kv_norm.shape[1]
    c2 = c1 + kv_rank
    n_heads = od_w_uk.shape[2]
    idx_heads = od_w_iqb.shape[2] // IDX_DIM
    assert od_w_in.shape[2] == c2 + C_ROPE + IDX_DIM + idx_heads and idx_heads == C_ROPE
    scale = (C_NOPE + C_ROPE) ** -0.5 * LOG2E
    cos, sin = _rope_tables(S, C_ROPE)
    one, zero = jnp.ones((S, C_ROPE), F32), jnp.zeros((S, C_ROPE), F32)
    cos_p = jnp.concatenate([cos, cos, one], axis=-1)
    sin_p = jnp.concatenate([-sin, sin, zero], axis=-1)
    cos_q = jnp.concatenate([cos, cos, cos, cos], axis=-1)
    sin_q = jnp.concatenate([-sin, sin, -sin, sin], axis=-1)

    h = _rmsnorm(x, norm_mix[l], BF16)
    proj = _matmul(h, od_w_in[i], out_dtype=F32, name="odd_in")
    g_kr = jnp.concatenate([od_kr_norm[i], jnp.zeros((LANE - C_ROPE,), F32)])
    qa, ckv, kr2, ik, iw = _odd_prep(proj.reshape(B, S, -1), c1, c2, od_qa_norm[i], od_kv_norm[i], g_kr,
                                     od_ik_norm[i], cos_p, sin_p, idx_heads ** -0.5 * IDX_DIM ** -0.5)
    qa = qa.reshape(N, c1)
    wqb = od_w_qb[i].reshape(c1, n_heads, C_NOPE + C_ROPE)
    wqb = jnp.concatenate([wqb[:, :, :C_NOPE].reshape(c1, -1), wqb[:, :, C_NOPE:].reshape(c1, -1)], axis=1)
    qraw = _matmul(qa, wqb.astype(BF16), out_dtype=F32, name="odd_qb")
    qn, qr = _q_prep(qraw.reshape(B, S, -1), n_heads, od_q_norm[i][:C_NOPE] * scale,
                     jnp.tile(od_q_norm[i][C_NOPE:], 2) * scale, cos_q, sin_q)
    tn = _tile(idx_heads * IDX_DIM, 1024, LANE)
    iq = _matmul(qa, od_w_iqb[i], out_dtype=BF16, name="odd_iqb",
                 epi=dict(flags=[EPI_ROPE] * (idx_heads * IDX_DIM // tn), modes=(EPI_ROPE,), rope_half=IDX_ROPE // 2,
                          tn=tn, cos=cos_p, sin=sin_p, gain=jnp.ones((idx_heads * IDX_DIM,), F32)))
    wkv = jnp.concatenate([od_w_uk[i].reshape(kv_rank, -1), od_w_uv[i].reshape(kv_rank, -1)], axis=1)
    kv = _matmul(ckv.reshape(N, kv_rank), wkv.astype(BF16), out_dtype=BF16, name="odd_kv")

    n_keep = min(IDX_TOPK, S // 4)
    tq = _tile(S, 256, LANE)
    iwt = jnp.swapaxes(iw[:, :, :idx_heads], 1, 2)
    bias = _indexer(ik, iq.reshape(B, S, -1), iwt, n_keep, tq)
    kv = kv.reshape(B, S, -1)
    vt = _value_tiles(kv[:, :, n_heads * C_NOPE:], n_heads, tq)
    y = _dsa_attention(qn, qr, kv, kr2, vt, bias, n_heads, tq)
    return _matmul(y.reshape(N, -1), od_w_out[i].astype(BF16), out_dtype=F32, name="odd_out", resid=x)


def _dense_ffn(x, g, wg, wu, wd):
    N, D = x.shape
    F = wg.shape[1]
    h = _rmsnorm(x, g, BF16)
    tm = _tile(N, 2048, 16)
    tn = _tile(F, 512, LANE)
    nt = N // tm
    hid = _swiglu_up(h, wg[None], wu[None], jnp.zeros((nt,), I32), jnp.full((1,), nt, I32),
                     jnp.full((nt,), tm, I32), tm=tm, tn=tn, name="ffn_up")
    return _matmul(hid, wd.astype(BF16), out_dtype=F32, name="ffn_down", resid=x, tm=512, tn=1024,
                   tk=F if F <= 4096 else _tile(F, F // 2, LANE))


def _moe_ffn(x, g, router, wg, wu, wd, tm=512):
    N, D = x.shape
    E, _, F = wg.shape
    h, gate, sel = _norm_router(x, g, router)
    tm = _tile(N, tm, 16)
    cnt = jnp.sum(sel, axis=0)
    tiles_e = (cnt + tm - 1) // tm
    tile_end = jnp.cumsum(tiles_e)
    start = (tile_end - tiles_e) * tm
    rank = jnp.cumsum(sel, axis=0) - sel
    P = N * TOP_K + E * tm
    n_tiles = P // tm
    slot = (start[None, :] + rank).astype(I32)
    lane = jnp.arange(E, dtype=I32)[None, :]
    e2 = jnp.stack([jnp.min(jnp.where(sel > 0, lane, E), axis=1), jnp.max(jnp.where(sel > 0, lane, -1), axis=1)], 1)
    slots2 = jnp.take_along_axis(slot, e2, axis=1)
    gates2 = jnp.take_along_axis(gate, e2, axis=1)
    tok2 = jnp.broadcast_to(jnp.arange(N, dtype=I32)[:, None], (N, TOP_K))
    tok_of_slot = jnp.zeros((P,), I32).at[slots2.reshape(-1)].set(tok2.reshape(-1), unique_indices=True)
    tile_expert = jnp.minimum(jnp.searchsorted(tile_end, jnp.arange(n_tiles, dtype=I32), side="right"),
                              E - 1).astype(I32)
    n_active = tile_end[-1:].astype(I32)
    tile_row = jnp.arange(n_tiles, dtype=I32) * tm
    valid_rows = jnp.where(tile_row < n_active[0] * tm,
                           jnp.clip((start + cnt)[tile_expert] - tile_row, 0, tm), 0).astype(I32)

    xs = _gather_rows(h, tok_of_slot, BF16)
    hid = _swiglu_up(xs, wg, wu, tile_expert, n_active, valid_rows, tm=tm, tn=_tile(F, 512, LANE), name="moe_up")
    y = _grouped_down(hid, wd, tile_expert, n_active, valid_rows, tm=tm, tn=_tile(D, 1024, LANE), name="moe_down")
    return _moe_combine(x, y, slots2, gates2)


def kernel(x, mem, norm_mix, norm_xattn, norm_mem, norm_ffn, ev_w_in, ev_a_qnorm, ev_a_knorm, ev_lambda_q1, ev_lambda_k1, ev_lambda_q2, ev_lambda_k2, ev_a_subln, ev_b_qnorm, ev_b_knorm, ev_w_out, od_w_in, od_qa_norm, od_w_qb, od_q_norm, od_kv_norm, od_kr_norm, od_w_uk, od_w_uv, od_w_iqb, od_ik_norm, od_w_out, xa_wq, xa_wk, xa_wv, xa_wo, xa_qnorm, xa_knorm, ffn_wg, ffn_wu, ffn_wd, moe_router, moe_wg, moe_wu, moe_wd):
    B, S, D = x.shape
    depth = norm_mix.shape[0]
    x = x.reshape(B * S, D)
    for l in range(depth):
        i = l // 2
        if l % 2 == 0:
            x = _even_mixer(x, l, i, B, S, norm_mix, ev_w_in, ev_a_qnorm, ev_a_knorm,
                            (ev_lambda_q1[i], ev_lambda_k1[i], ev_lambda_q2[i], ev_lambda_k2[i]),
                            ev_a_subln, ev_b_qnorm, ev_b_knorm, ev_w_out)
        else:
            x = _odd_mixer(x, l, i, B, S, norm_mix, od_w_in, od_qa_norm, od_w_qb, od_q_norm, od_kv_norm,
                           od_kr_norm, od_w_uk, od_w_uv, od_w_iqb, od_ik_norm, od_w_out)
        x = _cross_block(x, mem, l, norm_xattn, norm_mem, xa_wq, xa_wk, xa_wv, xa_wo, xa_qnorm, xa_knorm, B, S)
        if l % 2 == 0:
            x = _dense_ffn(x, norm_ffn[l], ffn_wg[i], ffn_wu[i], ffn_wd[i])
        else:
            x = _moe_ffn(x, norm_ffn[l], moe_router[i], moe_wg[i], moe_wu[i], moe_wd[i])
    return x.reshape(B, S, D)
```

```python
import functools
import math

import jax
import jax.numpy as jnp
import numpy as np
from jax import lax
from jax.experimental import pallas as pl
from jax.experimental.pallas import tpu as pltpu

F32 = jnp.float32
BF16 = jnp.bfloat16
I32 = jnp.int32

LANE = 128
MXU_WIDTH = 256
V7X_VMEM_BYTES = 64 * 1024 * 1024
VMEM_CAP = V7X_VMEM_BYTES - 4 * 1024 * 1024

HEAD_DIM = 128
ROPE_THETA = 10000.0
EPS = 1e-6
MOBA_BLOCK = 256
MOBA_TOPK = 3
C_NOPE = 128
C_ROPE = 64
IDX_DIM = 128
IDX_ROPE = 64
IDX_TOPK = 256
TOP_K = 2
NEG = -1e30
LOG2E = math.log2(math.e)
INT_MIN = -(2 ** 31)


def _tile(dim, pref, align):
    t = min(pref, dim)
    t -= t % align
    while t >= align:
        if dim % t == 0:
            return t
        t -= align
    return dim


def _params(sem, est_bytes):
    limit = int(min(max(est_bytes * 1.3 + (4 << 20), 32 << 20), VMEM_CAP))
    return pltpu.CompilerParams(dimension_semantics=sem, vmem_limit_bytes=limit)


def _nt(a, b):
    return lax.dot_general(a, b, (((1,), (1,)), ((), ())), preferred_element_type=F32)


def _softmax_steps(sts, carry, s_ref, acc_ref, value_tile):
    for c, st in enumerate(sts):
        s_ref[c] = st
    new = []
    for c in range(len(sts)):
        m_new = jnp.maximum(carry[c], jnp.max(s_ref[c], axis=0, keepdims=True))
        alpha = jnp.exp2(carry[c] - m_new)
        p = jnp.exp2(s_ref[c] - m_new)
        new.append(m_new)
        acc_ref[c] = alpha * acc_ref[c] + jnp.dot(value_tile(c), p.astype(BF16), preferred_element_type=F32)
    return tuple(new)


def _normalized(acc_ref, c, dv):
    acc = acc_ref[c]
    return acc[:dv] / acc[dv:dv + 1]


def _init_carry(n_chains, tq):
    return tuple(jnp.full((1, tq), NEG, F32) for _ in range(n_chains))


ONES_ROWS = 16


def _value_tiles(v, n_heads, tk):
    B, S, C = v.shape
    vt = v.reshape(B, S // tk, tk, n_heads, C // n_heads).transpose(0, 3, 1, 4, 2)
    extra = jnp.zeros(vt.shape[:3] + (ONES_ROWS, tk), v.dtype).at[..., 0, :].set(1)
    return jnp.concatenate([vt, extra], axis=3)


def _rmsnorm_body(x_ref, g_ref, o_ref):
    x = x_ref[...].astype(F32)
    ms = jnp.mean(x * x, axis=-1, keepdims=True)
    o_ref[...] = (x * lax.rsqrt(ms + EPS) * g_ref[...]).astype(o_ref.dtype)


def _rmsnorm(x, g, out_dtype, tm=256):
    M, D = x.shape
    tm = _tile(M, tm, 16)
    est = 2 * tm * D * (4 + 4)
    return pl.pallas_call(
        _rmsnorm_body,
        grid=(M // tm,),
        in_specs=[pl.BlockSpec((tm, D), lambda i: (i, 0)), pl.BlockSpec((1, D), lambda i: (0, 0))],
        out_specs=pl.BlockSpec((tm, D), lambda i: (i, 0)),
        out_shape=jax.ShapeDtypeStruct((M, D), out_dtype),
        compiler_params=_params(("parallel",), est),
        name="rmsnorm",
    )(x, g.reshape(1, D).astype(F32))


def _norm_router_body(x_ref, g_ref, rt_ref, h_ref, gate_ref, sel_ref, *, n_exp):
    x = x_ref[...]
    ms = jnp.mean(x * x, axis=-1, keepdims=True)
    h = x * lax.rsqrt(ms + EPS) * g_ref[...]
    h_ref[...] = h
    lane = lax.broadcasted_iota(I32, gate_ref.shape, 1)
    logits = jnp.full(gate_ref.shape, -jnp.inf, F32)
    for e in range(n_exp):
        col = jnp.sum(h * rt_ref[e:e + 1, :], axis=-1, keepdims=True)
        logits = jnp.where(lane == e, col, logits)
    m1 = jnp.max(logits, axis=-1, keepdims=True)
    i1 = jnp.min(jnp.where(logits == m1, lane, LANE), axis=-1, keepdims=True)
    rest = jnp.where(lane == i1, -jnp.inf, logits)
    m2 = jnp.max(rest, axis=-1, keepdims=True)
    i2 = jnp.min(jnp.where(rest == m2, lane, LANE), axis=-1, keepdims=True)
    e2 = jnp.exp(m2 - m1)
    den = 1.0 + e2
    gate_ref[...] = jnp.where(lane == i1, 1.0 / den, 0.0) + jnp.where(lane == i2, e2 / den, 0.0)
    sel_ref[...] = jnp.where((lane == i1) | (lane == i2), 1, 0).astype(I32)


def _norm_router(x, g, router, tm=256):
    M, D = x.shape
    n_exp = router.shape[1]
    tm = _tile(M, tm, 8)
    est = 2 * tm * D * 8 + 2 * 8 * D * 4
    h, gate, sel = pl.pallas_call(
        functools.partial(_norm_router_body, n_exp=n_exp),
        grid=(M // tm,),
        in_specs=[pl.BlockSpec((tm, D), lambda i: (i, 0)), pl.BlockSpec((1, D), lambda i: (0, 0)),
                  pl.BlockSpec((n_exp, D), lambda i: (0, 0))],
        out_specs=[pl.BlockSpec((tm, D), lambda i: (i, 0)), pl.BlockSpec((tm, LANE), lambda i: (i, 0)),
                   pl.BlockSpec((tm, LANE), lambda i: (i, 0))],
        out_shape=[jax.ShapeDtypeStruct((M, D), F32), jax.ShapeDtypeStruct((M, LANE), F32),
                   jax.ShapeDtypeStruct((M, LANE), I32)],
        compiler_params=_params(("parallel",), est),
        name="norm_router",
    )(x, g.reshape(1, D).astype(F32), router.T.astype(F32))
    return h, gate[:, :n_exp], sel[:, :n_exp]


EPI_NONE, EPI_NORM, EPI_NORM_ROPE, EPI_ROPE = 0, 1, 2, 3


def _head_epilogue(x, g, cos, sin, mode, rope_half):
    if mode in (EPI_NORM, EPI_NORM_ROPE):
        ss = jnp.dot((x * x).astype(BF16), jnp.ones((LANE, LANE), BF16), preferred_element_type=F32)
        x = x * lax.rsqrt(ss * (1.0 / LANE) + EPS) * g
    if mode in (EPI_NORM_ROPE, EPI_ROPE):
        if rope_half == LANE // 2:
            r = pltpu.roll(x, LANE // 2, axis=1)
        else:
            lane = lax.broadcasted_iota(I32, x.shape, 1)
            first = (lane % (2 * rope_half)) < rope_half
            r = jnp.where(first, pltpu.roll(x, LANE - rope_half, axis=1), pltpu.roll(x, rope_half, axis=1))
        x = x * cos + r * sin
    return x


def _mm_body(flags_ref, a_ref, w_ref, *rest, nk, has_resid, modes, rope_half, tn, cw):
    rest = list(rest)
    resid_ref = rest.pop(0) if has_resid else None
    gain_ref = cos_ref = sin_ref = None
    if modes:
        gain_ref, cos_ref, sin_ref = rest.pop(0), rest.pop(0), rest.pop(0)
    o_ref = rest.pop(0)
    acc_ref = rest.pop(0) if nk > 1 else None
    j = pl.program_id(1)
    k = pl.program_id(2)

    def emit(cols, acc, mode):
        if mode != EPI_NONE:
            for c in range(cols.start, cols.stop, LANE):
                sl = slice(c, c + LANE)
                y = _head_epilogue(acc[:, sl.start - cols.start:sl.stop - cols.start], gain_ref[:, sl],
                                   cos_ref[...], sin_ref[...], mode, rope_half)
                o_ref[:, sl] = y.astype(o_ref.dtype)
        elif has_resid:
            o_ref[:, cols] = (resid_ref[:, cols] + acc).astype(o_ref.dtype)
        else:
            o_ref[:, cols] = acc.astype(o_ref.dtype)

    if nk == 1:
        def run(mode):
            pending = None
            for c in range(0, tn, cw):
                cols = slice(c, c + cw)
                acc = jnp.dot(a_ref[...], w_ref[:, cols].astype(BF16), preferred_element_type=F32)
                if pending is not None:
                    emit(*pending, mode)
                pending = (cols, acc)
            emit(*pending, mode)

        if modes:
            flag = flags_ref[j]
            for mode in (EPI_NONE,) + modes:
                pl.when(flag == mode)(functools.partial(run, mode))
        else:
            run(EPI_NONE)
    else:
        part = jnp.dot(a_ref[...], w_ref[...], preferred_element_type=F32)

        @pl.when(k == 0)
        def _():
            acc_ref[...] = part

        @pl.when((k > 0) & (k < nk - 1))
        def _():
            acc_ref[...] += part

        @pl.when(k == nk - 1)
        def _():
            emit(slice(0, tn), acc_ref[...] + part, EPI_NONE)


def _matmul(a, w, *, out_dtype, name, resid=None, epi=None, tm=1024, tn=1024, tk=4096):
    M, K = a.shape
    N = w.shape[1]
    tm = _tile(epi["cos"].shape[0] if epi else M, tm, 16)
    tn = epi["tn"] if epi else _tile(N, tn, LANE)
    tk = _tile(K, tk, LANE)
    nk = K // tk
    assert a.dtype == BF16 and (w.dtype == BF16 or nk == 1) and (nk == 1 or not epi)
    a_bytes = a.dtype.itemsize
    o_bytes = jnp.dtype(out_dtype).itemsize
    modes = tuple(epi["modes"]) if epi else ()
    in_specs = [pl.BlockSpec((tm, tk), lambda i, j, k, f: (i, k)),
                pl.BlockSpec((tk, tn), lambda i, j, k, f: (k, j))]
    args = [a, w]
    est = 2 * (tm * tk * a_bytes + tk * tn * w.dtype.itemsize + tm * tn * o_bytes) + 3 * tm * tn * 4
    if resid is not None:
        in_specs.append(pl.BlockSpec((tm, tn), lambda i, j, k, f: (i, j)))
        args.append(resid)
        est += 2 * tm * tn * 4
    if epi:
        ns = epi["cos"].shape[0] // tm
        in_specs += [pl.BlockSpec((1, tn), lambda i, j, k, f: (0, j)),
                     pl.BlockSpec((tm, LANE), lambda i, j, k, f: (i % ns, 0)),
                     pl.BlockSpec((tm, LANE), lambda i, j, k, f: (i % ns, 0))]
        args += [epi["gain"].reshape(1, N).astype(F32), epi["cos"], epi["sin"]]
        flags = jnp.asarray(epi["flags"], I32)
        est += 4 * tm * LANE * 4
    else:
        flags = jnp.zeros((N // tn,), I32)
    body = functools.partial(_mm_body, nk=nk, has_resid=resid is not None, modes=modes,
                             rope_half=epi["rope_half"] if epi else 0, tn=tn, cw=math.gcd(tn, MXU_WIDTH))
    return pl.pallas_call(
        body,
        grid_spec=pltpu.PrefetchScalarGridSpec(
            num_scalar_prefetch=1,
            grid=(M // tm, N // tn, nk),
            in_specs=in_specs,
            out_specs=pl.BlockSpec((tm, tn), lambda i, j, k, f: (i, j)),
            scratch_shapes=[pltpu.VMEM((tm, tn), F32)] if nk > 1 else [],
        ),
        out_shape=jax.ShapeDtypeStruct((M, N), out_dtype),
        compiler_params=_params(("parallel", "parallel", "arbitrary"), est),
        name=name,
    )(flags, *args)


def _segments(tile_expert, n_active):
    T = tile_expert.shape[0]
    idx = jnp.arange(T, dtype=I32)
    prev = jnp.concatenate([tile_expert[:1] - 1, tile_expert[:-1]])
    first = (idx < n_active[0]) & (tile_expert != prev)
    first_idx = jnp.where(first, idx, T)
    after = jnp.concatenate([lax.cummin(first_idx[::-1])[::-1][1:], jnp.full((1,), T, I32)])
    nxt = jnp.where(after < T, tile_expert[jnp.minimum(after, T - 1)], -1)
    return first.astype(I32), nxt.astype(I32)


def _stream_weights(te_ref, first_ref, next_ref, w_refs, stage_refs, cast_refs, sem, tn):
    n = pl.program_id(0)
    m = pl.program_id(1)

    def copies(e, nn):
        cols = pl.ds(pl.multiple_of(nn * tn, tn), tn)
        return [pltpu.make_async_copy(w.at[e, :, cols], st, sem.at[i])
                for i, (w, st) in enumerate(zip(w_refs, stage_refs))]

    @pl.when(first_ref[m] == 1)
    def _():
        @pl.when((n == 0) & (m == 0))
        def _():
            for c in copies(te_ref[0], 0):
                c.start()

        for c in copies(te_ref[m], n):
            c.wait()
        for st, cb in zip(stage_refs, cast_refs):
            cb[...] = st[...].astype(BF16)
        last = next_ref[m] < 0
        e2 = jnp.where(last, te_ref[0], next_ref[m])
        n2 = jnp.where(last, n + 1, n)

        @pl.when(n2 < pl.num_programs(0))
        def _():
            for c in copies(e2, n2):
                c.start()


def _row_parts(nv_ref, o_ref, compute, parts):
    m = pl.program_id(1)
    hm = o_ref.shape[0] // parts
    for h in range(parts):
        rows = slice(h * hm, (h + 1) * hm)
        pl.when(nv_ref[m] > h * hm)(functools.partial(compute, rows))

        @pl.when(nv_ref[m] <= h * hm)
        def _(rows=rows):
            o_ref[rows, :] = jnp.zeros((hm, o_ref.shape[1]), o_ref.dtype)


def _swiglu_body(te_ref, na_ref, first_ref, next_ref, nv_ref, a_ref, wg_ref, wu_ref, o_ref,
                 wgf_ref, wuf_ref, wgb_ref, wub_ref, sem):
    tn = o_ref.shape[1]
    _stream_weights(te_ref, first_ref, next_ref, (wg_ref, wu_ref), (wgf_ref, wuf_ref), (wgb_ref, wub_ref), sem, tn)
    cw = math.gcd(tn, MXU_WIDTH)

    def compute(rows):
        def emit(cols, g, u):
            o_ref[rows, cols] = (g * jax.nn.sigmoid(g) * u).astype(o_ref.dtype)

        pending = None
        for c in range(0, tn, cw):
            cols = slice(c, c + cw)
            g = jnp.dot(a_ref[rows, :], wgb_ref[:, cols], preferred_element_type=F32)
            u = jnp.dot(a_ref[rows, :], wub_ref[:, cols], preferred_element_type=F32)
            if pending is not None:
                emit(*pending)
            pending = (cols, g, u)
        emit(*pending)

    _row_parts(nv_ref, o_ref, compute, parts=1)


def _swiglu_up(a, wg, wu, tile_expert, n_active, valid_rows, *, tm, tn, name):
    M, D = a.shape
    F = wg.shape[2]
    first, nxt = _segments(tile_expert, n_active)
    est = 2 * (tm * D * 2 + tm * tn * 2) + 2 * D * tn * (4 + 2) + 4 * tm * tn * 4
    return pl.pallas_call(
        _swiglu_body,
        grid_spec=pltpu.PrefetchScalarGridSpec(
            num_scalar_prefetch=5,
            grid=(F // tn, M // tm),
            in_specs=[pl.BlockSpec((tm, D), lambda n, m, *_: (m, 0)),
                      pl.BlockSpec(memory_space=pl.ANY), pl.BlockSpec(memory_space=pl.ANY)],
            out_specs=pl.BlockSpec((tm, tn), lambda n, m, *_: (m, n)),
            scratch_shapes=[pltpu.VMEM((D, tn), F32), pltpu.VMEM((D, tn), F32),
                            pltpu.VMEM((D, tn), BF16), pltpu.VMEM((D, tn), BF16),
                            pltpu.SemaphoreType.DMA((2,))],
        ),
        out_shape=jax.ShapeDtypeStruct((M, F), BF16),
        compiler_params=_params(("arbitrary", "arbitrary"), est),
        name=name,
    )(tile_expert, n_active, first, nxt, valid_rows, a, wg, wu)


def _down_body(te_ref, na_ref, first_ref, next_ref, nv_ref, a_ref, w_ref, o_ref, wf_ref, wb_ref, sem):
    _stream_weights(te_ref, first_ref, next_ref, (w_ref,), (wf_ref,), (wb_ref,), sem, o_ref.shape[1])

    def compute(rows):
        o_ref[rows, :] = jnp.dot(a_ref[rows, :], wb_ref[...], preferred_element_type=F32)

    _row_parts(nv_ref, o_ref, compute, parts=2)


def _grouped_down(a, wd, tile_expert, n_active, valid_rows, *, tm, tn, name):
    M, F = a.shape
    D = wd.shape[2]
    first, nxt = _segments(tile_expert, n_active)
    est = 2 * (tm * F * 2 + tm * tn * 4) + F * tn * (4 + 2) + 2 * tm * tn * 4
    return pl.pallas_call(
        _down_body,
        grid_spec=pltpu.PrefetchScalarGridSpec(
            num_scalar_prefetch=5,
            grid=(D // tn, M // tm),
            in_specs=[pl.BlockSpec((tm, F), lambda n, m, *_: (m, 0)), pl.BlockSpec(memory_space=pl.ANY)],
            out_specs=pl.BlockSpec((tm, tn), lambda n, m, *_: (m, n)),
            scratch_shapes=[pltpu.VMEM((F, tn), F32), pltpu.VMEM((F, tn), BF16), pltpu.SemaphoreType.DMA((1,))],
        ),
        out_shape=jax.ShapeDtypeStruct((M, D), F32),
        compiler_params=_params(("arbitrary", "arbitrary"), est),
        name=name,
    )(tile_expert, n_active, first, nxt, valid_rows, a, wd)


GATHER_UNROLL = 8


def _gather_body(idx_ref, src_ref, o_ref, buf_ref, sem, *, tg):
    step = pl.program_id(0)

    def row_copy(slot, i, row):
        return pltpu.make_async_copy(src_ref.at[pl.ds(row, 1)], buf_ref.at[slot, pl.ds(i, 1)], sem.at[slot])

    def issue(s):
        slot = s % 2

        def start(i, c):
            row_copy(slot, i, idx_ref[s * tg + i]).start()
            return c

        lax.fori_loop(0, tg, start, 0, unroll=GATHER_UNROLL)

    @pl.when(step == 0)
    def _():
        issue(step)

    @pl.when(step + 1 < pl.num_programs(0))
    def _():
        issue(step + 1)

    slot = step % 2

    def wait(i, c):
        row_copy(slot, i, 0).wait()
        return c

    lax.fori_loop(0, tg, wait, 0, unroll=GATHER_UNROLL)
    o_ref[...] = buf_ref[slot].astype(o_ref.dtype)


def _gather_rows(src, idx, out_dtype, tg=256):
    P = idx.shape[0]
    D = src.shape[1]
    tg = _tile(P, tg, 16)
    est = 2 * tg * D * 4 + 2 * tg * D * 2
    return pl.pallas_call(
        functools.partial(_gather_body, tg=tg),
        grid_spec=pltpu.PrefetchScalarGridSpec(
            num_scalar_prefetch=1,
            grid=(P // tg,),
            in_specs=[pl.BlockSpec(memory_space=pl.ANY)],
            out_specs=pl.BlockSpec((tg, D), lambda i, idx: (i, 0)),
            scratch_shapes=[pltpu.VMEM((2, tg, D), src.dtype), pltpu.SemaphoreType.DMA((2,))],
        ),
        out_shape=jax.ShapeDtypeStruct((P, D), out_dtype),
        compiler_params=_params(("arbitrary",), est),
        name="moe_dispatch_gather",
    )(idx, src)


def _combine_body(slot_ref, x_ref, g_ref, y_ref, o_ref, buf_ref, sem, *, tc):
    step = pl.program_id(0)

    def row_copy(slot, i, k, row):
        return pltpu.make_async_copy(y_ref.at[pl.ds(row, 1)], buf_ref.at[slot, k, pl.ds(i, 1)], sem.at[slot])

    def issue(s):
        slot = s % 2

        def start(i, c):
            for k in range(TOP_K):
                row_copy(slot, i, k, slot_ref[(s * tc + i) * TOP_K + k]).start()
            return c

        lax.fori_loop(0, tc, start, 0, unroll=GATHER_UNROLL // TOP_K)

    @pl.when(step == 0)
    def _():
        issue(step)

    @pl.when(step + 1 < pl.num_programs(0))
    def _():
        issue(step + 1)

    slot = step % 2

    def wait(i, c):
        for k in range(TOP_K):
            row_copy(slot, i, k, 0).wait()
        return c

    lax.fori_loop(0, tc, wait, 0, unroll=GATHER_UNROLL // TOP_K)
    g = g_ref[...]
    o_ref[...] = x_ref[...] + (g[:, 0:1] * buf_ref[slot, 0] + g[:, 1:2] * buf_ref[slot, 1])


def _moe_combine(x, y, slots, gates, tc=128):
    M, D = x.shape
    tc = _tile(M, tc, 8)
    est = 4 * tc * D * 4 + 2 * TOP_K * tc * D * 4
    return pl.pallas_call(
        functools.partial(_combine_body, tc=tc),
        grid_spec=pltpu.PrefetchScalarGridSpec(
            num_scalar_prefetch=1,
            grid=(M // tc,),
            in_specs=[pl.BlockSpec((tc, D), lambda i, s: (i, 0)), pl.BlockSpec((tc, TOP_K), lambda i, s: (i, 0)),
                      pl.BlockSpec(memory_space=pl.ANY)],
            out_specs=pl.BlockSpec((tc, D), lambda i, s: (i, 0)),
            scratch_shapes=[pltpu.VMEM((2, TOP_K, tc, D), F32), pltpu.SemaphoreType.DMA((2,))],
        ),
        out_shape=jax.ShapeDtypeStruct((M, D), F32),
        compiler_params=_params(("arbitrary",), est),
        name="moe_combine",
    )(slots.reshape(-1), x, gates, y)


def _diff_attn_body(q_ref, k_ref, vt_ref, lq1_ref, lk1_ref, lq2_ref, lk2_ref, g_ref, o_ref, s_ref, acc_ref,
                    *, tq, hg, lam_init):
    qi = pl.program_id(2)
    w = 2 * HEAD_DIM
    cols = [slice(hh * w + mi * HEAD_DIM, hh * w + (mi + 1) * HEAD_DIM) for hh in range(hg) for mi in range(2)]
    qs = [q_ref[:, c] for c in cols]

    def step(n, carry, mask):
        off = pl.multiple_of(n * tq, tq)
        sts = [_nt(k_ref[pl.ds(off, tq), col_sl], qs[c]) for c, col_sl in enumerate(cols)]
        if mask is not None:
            sts = [jnp.where(mask, st, NEG) for st in sts]
        return _softmax_steps(sts, carry, s_ref, acc_ref, lambda c: vt_ref[c // 2, n])

    acc_ref[...] = jnp.zeros(acc_ref.shape, F32)
    carry = lax.fori_loop(0, qi, lambda n, c: step(n, c, None), _init_carry(len(cols), tq))
    key = lax.broadcasted_iota(I32, (tq, tq), 0)
    qry = lax.broadcasted_iota(I32, (tq, tq), 1)
    step(qi, carry, key <= qry)
    lam = (jnp.exp(jnp.sum(lq1_ref[...] * lk1_ref[...], axis=-1, keepdims=True))
           - jnp.exp(jnp.sum(lq2_ref[...] * lk2_ref[...], axis=-1, keepdims=True)) + lam_init)
    for hh in range(hg):
        y = (_normalized(acc_ref, 2 * hh, w) - lam * _normalized(acc_ref, 2 * hh + 1, w)).T
        ms = jnp.mean(y * y, axis=-1, keepdims=True)
        hs = slice(hh * w, (hh + 1) * w)
        o_ref[:, hs] = (y * lax.rsqrt(ms + EPS) * g_ref[...] * (1.0 - lam_init)).astype(o_ref.dtype)


def _diff_attention(qkv, vt, n_heads, lam_params, subln, lam_init, tq, hg=4):
    B, S, _ = qkv.shape
    w = 2 * HEAD_DIM
    hg = math.gcd(hg, n_heads)
    ng = n_heads // hg
    vec = pl.BlockSpec((1, HEAD_DIM), lambda b, g, i: (0, 0))
    est = 2 * (2 * tq * hg * w * 2 + 2 * S * hg * w * 2) + 2 * hg * (3 * tq * tq * 4 + tq * w * 4)
    return pl.pallas_call(
        functools.partial(_diff_attn_body, tq=tq, hg=hg, lam_init=lam_init),
        grid=(B, ng, S // tq),
        in_specs=[pl.BlockSpec((None, tq, hg * w), lambda b, g, i: (b, i, g)),
                  pl.BlockSpec((None, S, hg * w), lambda b, g, i: (b, 0, ng + g)),
                  pl.BlockSpec((None, hg, S // tq, w + ONES_ROWS, tq), lambda b, g, i: (b, g, 0, 0, 0)),
                  vec, vec, vec, vec,
                  pl.BlockSpec((1, w), lambda b, g, i: (0, 0))],
        out_specs=pl.BlockSpec((None, tq, hg * w), lambda b, g, i: (b, i, g)),
        out_shape=jax.ShapeDtypeStruct((B, S, n_heads * w), BF16),
        scratch_shapes=[pltpu.VMEM((2 * hg, tq, tq), F32), pltpu.VMEM((2 * hg, w + ONES_ROWS, tq), F32)],
        compiler_params=_params(("parallel", "parallel", "arbitrary"), est),
        name="diff_attention",
    )(qkv, qkv, vt, *[p.reshape(1, HEAD_DIM).astype(F32) for p in lam_params],
      subln.reshape(1, w).astype(F32))


def _moba_body(q_ref, k_ref, vt_ref, o_ref, kmean_ref, bias_ref, s_ref, acc_ref, *, nb, hg):
    qi = pl.program_id(2)
    blk = MOBA_BLOCK
    d = HEAD_DIM
    heads = [slice(hh * d, (hh + 1) * d) for hh in range(hg)]

    @pl.when(qi == 0)
    def _():
        for hh, hs in enumerate(heads):
            for n in range(nb):
                kb = k_ref[n * blk:(n + 1) * blk, hs].astype(F32)
                kmean_ref[hh, n:n + 1, :] = jnp.mean(kb, axis=0, keepdims=True)

    blk_id = lax.broadcasted_iota(I32, (nb, blk), 0)
    qs = []
    for hh, hs in enumerate(heads):
        q = q_ref[:, hs]
        km = kmean_ref[hh]
        km_hi = km.astype(BF16)
        km_lo = (km - km_hi.astype(F32)).astype(BF16)
        gate = _nt(km_hi, q) + _nt(km_lo, q)
        gate = jnp.where(blk_id < qi, gate, -jnp.inf)
        sel = jnp.zeros(gate.shape, jnp.bool_)
        for _ in range(MOBA_TOPK):
            mx = jnp.max(gate, axis=0, keepdims=True)
            idx = jnp.min(jnp.where(gate == mx, blk_id, nb), axis=0, keepdims=True)
            sel = sel | ((blk_id == idx) & (mx > -jnp.inf))
            gate = jnp.where(blk_id == idx, -jnp.inf, gate)
        qs.append(q)
        bias_ref[hh] = jnp.where(sel, 0.0, NEG)

    def step(n, carry, mask):
        off = pl.multiple_of(n * blk, blk)
        sts = [_nt(k_ref[pl.ds(off, blk), hs], qs[hh]) for hh, hs in enumerate(heads)]
        if mask is None:
            sts = [st + bias_ref[hh, pl.ds(n, 1), :] for hh, st in enumerate(sts)]
        else:
            sts = [jnp.where(mask, st, NEG) for st in sts]
        return _softmax_steps(sts, carry, s_ref, acc_ref, lambda hh: vt_ref[hh, n])

    acc_ref[...] = jnp.zeros(acc_ref.shape, F32)
    carry = lax.fori_loop(0, qi, lambda n, c: step(n, c, None), _init_carry(hg, blk))
    key = lax.broadcasted_iota(I32, (blk, blk), 0)
    qry = lax.broadcasted_iota(I32, (blk, blk), 1)
    step(qi, carry, key <= qry)
    for hh, hs in enumerate(heads):
        o_ref[:, hs] = _normalized(acc_ref, hh, d).T.astype(o_ref.dtype)


def _moba_attention(qkv, vt, n_heads, q_col, k_col, hg=8):
    B, S, _ = qkv.shape
    assert S % MOBA_BLOCK == 0 and S // MOBA_BLOCK >= MOBA_TOPK
    nb = S // MOBA_BLOCK
    d = HEAD_DIM
    hg = math.gcd(math.gcd(hg, n_heads), math.gcd(q_col, k_col))
    est = 2 * (2 * MOBA_BLOCK * hg * d * 2 + 2 * S * hg * d * 2) + hg * (3 * MOBA_BLOCK * MOBA_BLOCK * 4)
    return pl.pallas_call(
        functools.partial(_moba_body, nb=nb, hg=hg),
        grid=(B, n_heads // hg, nb),
        in_specs=[pl.BlockSpec((None, MOBA_BLOCK, hg * d), lambda b, g, i: (b, i, q_col // hg + g)),
                  pl.BlockSpec((None, S, hg * d), lambda b, g, i: (b, 0, k_col // hg + g)),
                  pl.BlockSpec((None, hg, nb, d + ONES_ROWS, MOBA_BLOCK), lambda b, g, i: (b, g, 0, 0, 0))],
        out_specs=pl.BlockSpec((None, MOBA_BLOCK, hg * d), lambda b, g, i: (b, i, g)),
        out_shape=jax.ShapeDtypeStruct((B, S, n_heads * d), BF16),
        scratch_shapes=[pltpu.VMEM((hg, nb, d), F32), pltpu.VMEM((hg, nb, MOBA_BLOCK), F32),
                        pltpu.VMEM((hg, MOBA_BLOCK, MOBA_BLOCK), F32),
                        pltpu.VMEM((hg, d + ONES_ROWS, MOBA_BLOCK), F32)],
        compiler_params=_params(("parallel", "parallel", "arbitrary"), est),
        name="moba_attention",
    )(qkv, qkv, vt)


def _xattn_body(q_ref, k_ref, v_ref, o_ref, *, n_heads):
    for h in range(n_heads):
        sl = slice(h * HEAD_DIM, (h + 1) * HEAD_DIM)
        s = _nt(q_ref[:, sl], k_ref[:, sl])
        m = jnp.max(s, axis=-1, keepdims=True)
        p = jnp.exp2(s - m)
        l = jnp.sum(p, axis=-1, keepdims=True)
        o = jnp.dot(p.astype(BF16), v_ref[:, sl], preferred_element_type=F32)
        o_ref[:, sl] = (o / l).astype(o_ref.dtype)


def _cross_attention(q, kv, n_heads, tq=512):
    B, S, X = q.shape
    M = kv.shape[1]
    tq = _tile(S, tq, 16)
    est = 2 * (2 * tq * X * 2 + 2 * M * X * 2) + 6 * tq * M * 4
    return pl.pallas_call(
        functools.partial(_xattn_body, n_heads=n_heads),
        grid=(B, S // tq),
        in_specs=[pl.BlockSpec((None, tq, X), lambda b, i: (b, i, 0)),
                  pl.BlockSpec((None, M, X), lambda b, i: (b, 0, 0)),
                  pl.BlockSpec((None, M, X), lambda b, i: (b, 0, 1))],
        out_specs=pl.BlockSpec((None, tq, X), lambda b, i: (b, i, 0)),
        out_shape=jax.ShapeDtypeStruct((B, S, X), BF16),
        compiler_params=_params(("parallel", "parallel"), est),
        name="cross_attention",
    )(q, kv, kv)


def _odd_prep_body(x_ref, gqa_ref, gkv_ref, gkr_ref, gik_ref, cos_ref, sin_ref,
                   qa_ref, ckv_ref, kr_ref, ik_ref, iw_ref, *, c1, c2, iw_scale):
    def norm(x, g):
        ms = jnp.mean(x * x, axis=-1, keepdims=True)
        return x * lax.rsqrt(ms + EPS) * g

    def rope(y):
        lane = lax.broadcasted_iota(I32, y.shape, 1)
        half = C_ROPE // 2
        first = (lane % C_ROPE) < half
        r = jnp.where(first, pltpu.roll(y, LANE - half, axis=1), pltpu.roll(y, half, axis=1))
        return y * cos_ref[...] + r * sin_ref[...]

    qa_ref[...] = norm(x_ref[:, :c1], gqa_ref[...]).astype(qa_ref.dtype)
    ckv_ref[...] = norm(x_ref[:, c1:c2], gkv_ref[...]).astype(ckv_ref.dtype)
    slab_a = x_ref[:, c2:c2 + LANE]
    slab_b = x_ref[:, c2 + LANE:c2 + 2 * LANE]
    lane = lax.broadcasted_iota(I32, slab_a.shape, 1)
    low = lane < C_ROPE
    ms = jnp.sum(jnp.where(low, slab_a * slab_a, 0.0), axis=-1, keepdims=True) / C_ROPE
    kr = rope(slab_a * lax.rsqrt(ms + EPS) * gkr_ref[...])
    kr_ref[0] = kr.astype(kr_ref.dtype)
    kr_ref[1] = pltpu.roll(kr, C_ROPE, axis=1).astype(kr_ref.dtype)
    rot_a = pltpu.roll(slab_a, C_ROPE, axis=1)
    rot_b = pltpu.roll(slab_b, C_ROPE, axis=1)
    ik = jnp.where(low, rot_a, rot_b)
    ik_ref[...] = rope(norm(ik, gik_ref[...])).astype(ik_ref.dtype)
    iw_ref[...] = rot_b * iw_scale


def _odd_prep(x, c1, c2, g_qa, g_kv, g_kr, g_ik, cos_p, sin_p, iw_scale, tm=256):
    B, S, C = x.shape
    assert C == c2 + 2 * LANE
    tm = _tile(S, tm, 16)
    vec = lambda n: pl.BlockSpec((1, n), lambda b, i: (0, 0))
    tab = pl.BlockSpec((tm, LANE), lambda b, i: (i, 0))
    est = 2 * tm * C * 4 * 2
    return pl.pallas_call(
        functools.partial(_odd_prep_body, c1=c1, c2=c2, iw_scale=iw_scale),
        grid=(B, S // tm),
        in_specs=[pl.BlockSpec((None, tm, C), lambda b, i: (b, i, 0)),
                  vec(c1), vec(c2 - c1), vec(LANE), vec(LANE), tab, tab],
        out_specs=[pl.BlockSpec((None, tm, c1), lambda b, i: (b, i, 0)),
                   pl.BlockSpec((None, tm, c2 - c1), lambda b, i: (b, i, 0)),
                   pl.BlockSpec((None, 2, tm, LANE), lambda b, i: (b, 0, i, 0)),
                   pl.BlockSpec((None, tm, LANE), lambda b, i: (b, i, 0)),
                   pl.BlockSpec((None, tm, LANE), lambda b, i: (b, i, 0))],
        out_shape=[jax.ShapeDtypeStruct((B, S, c1), BF16),
                   jax.ShapeDtypeStruct((B, S, c2 - c1), BF16),
                   jax.ShapeDtypeStruct((B, 2, S, LANE), BF16),
                   jax.ShapeDtypeStruct((B, S, LANE), BF16),
                   jax.ShapeDtypeStruct((B, S, LANE), F32)],
        compiler_params=_params(("parallel", "parallel"), est),
        name="odd_prep",
    )(x, g_qa.reshape(1, -1), g_kv.reshape(1, -1), g_kr.reshape(1, -1), g_ik.reshape(1, -1), cos_p, sin_p)


def _q_prep_body(x_ref, gn_ref, gr_ref, cos_ref, sin_ref, qn_ref, qr_ref, *, n_heads):
    nope_w = n_heads * C_NOPE
    width = C_NOPE + C_ROPE
    half = C_ROPE // 2
    lane = lax.broadcasted_iota(I32, (x_ref.shape[0], LANE), 1)
    low = lane < C_ROPE
    first = (lane % C_ROPE) < half
    for p in range(n_heads // 2):
        n0 = x_ref[:, (2 * p) * LANE:(2 * p + 1) * LANE]
        n1 = x_ref[:, (2 * p + 1) * LANE:(2 * p + 2) * LANE]
        r = x_ref[:, nope_w + p * LANE:nope_w + (p + 1) * LANE]
        r2 = r * r
        ss0 = jnp.sum(n0 * n0, axis=-1, keepdims=True) + jnp.sum(jnp.where(low, r2, 0.0), axis=-1, keepdims=True)
        ss1 = jnp.sum(n1 * n1, axis=-1, keepdims=True) + jnp.sum(jnp.where(low, 0.0, r2), axis=-1, keepdims=True)
        inv0 = lax.rsqrt(ss0 / width + EPS)
        inv1 = lax.rsqrt(ss1 / width + EPS)
        qn_ref[:, (2 * p) * LANE:(2 * p + 1) * LANE] = (n0 * inv0 * gn_ref[...]).astype(qn_ref.dtype)
        qn_ref[:, (2 * p + 1) * LANE:(2 * p + 2) * LANE] = (n1 * inv1 * gn_ref[...]).astype(qn_ref.dtype)
        y = r * jnp.where(low, inv0, inv1) * gr_ref[...]
        rot = jnp.where(first, pltpu.roll(y, LANE - half, axis=1), pltpu.roll(y, half, axis=1))
        qr_ref[:, p * LANE:(p + 1) * LANE] = (y * cos_ref[...] + rot * sin_ref[...]).astype(qr_ref.dtype)


def _q_prep(x, n_heads, g_nope, g_rope2, cos_q, sin_q, tm=256):
    B, S, C = x.shape
    tm = _tile(S, tm, 16)
    nope_w, rope_w = n_heads * C_NOPE, n_heads * C_ROPE
    vec = pl.BlockSpec((1, LANE), lambda b, i: (0, 0))
    tab = pl.BlockSpec((tm, LANE), lambda b, i: (i, 0))
    est = 2 * tm * C * 6
    return pl.pallas_call(
        functools.partial(_q_prep_body, n_heads=n_heads),
        grid=(B, S // tm),
        in_specs=[pl.BlockSpec((None, tm, C), lambda b, i: (b, i, 0)), vec, vec, tab, tab],
        out_specs=[pl.BlockSpec((None, tm, nope_w), lambda b, i: (b, i, 0)),
                   pl.BlockSpec((None, tm, rope_w), lambda b, i: (b, i, 0))],
        out_shape=[jax.ShapeDtypeStruct((B, S, nope_w), BF16), jax.ShapeDtypeStruct((B, S, rope_w), BF16)],
        compiler_params=_params(("parallel", "parallel"), est),
        name="dsa_q_prep",
    )(x, g_nope.reshape(1, LANE), g_rope2.reshape(1, LANE), cos_q, sin_q)


def _indexer_body(ik_ref, iq_ref, iwt_ref, o_ref, key_ref, *, tq, n_heads, n_keep):
    S = ik_ref.shape[0]
    qi = pl.program_id(1)
    n_tiles = qi + 1
    t_idx = qi * tq + lax.broadcasted_iota(I32, (tq, tq), 1)
    s_loc = lax.broadcasted_iota(I32, (tq, tq), 0)

    def score_tile(kt, c):
        off = pl.multiple_of(kt * tq, tq)
        ikt = ik_ref[pl.ds(off, tq), :]
        acc = jnp.zeros((tq, tq), F32)
        for h in range(n_heads):
            r = _nt(ikt, iq_ref[:, h * IDX_DIM:(h + 1) * IDX_DIM])
            acc = acc + jnp.maximum(r, 0.0) * iwt_ref[h:h + 1, :]
        acc = jnp.where(off + s_loc <= t_idx, acc, -jnp.inf)
        bits = lax.bitcast_convert_type(acc, I32)
        key_ref[pl.ds(off, tq), :] = bits ^ ((bits >> 31) & 0x7FFFFFFF)
        return c

    lax.fori_loop(0, n_tiles, score_tile, 0)

    def count(hit):
        def body(kt, cnt):
            off = pl.multiple_of(kt * tq, tq)
            one = jnp.where(hit(key_ref[pl.ds(off, tq), :], off + s_loc), 1, 0).astype(I32)
            return cnt + jnp.sum(one.reshape(tq // 8, 8, tq), axis=0)
        cnt = lax.fori_loop(0, n_tiles, body, jnp.zeros((8, tq), I32))
        return jnp.sum(cnt, axis=0, keepdims=True)

    n_pos = count(lambda k, s: k >= 0)
    thr = jnp.where(n_pos >= n_keep, 0, INT_MIN).astype(I32)

    def bit_step(i, thr):
        cand = thr + lax.shift_left(jnp.int32(1), 30 - i)
        return jnp.where(count(lambda k, s: k >= cand) >= n_keep, cand, thr)

    thr = lax.fori_loop(0, 31, bit_step, thr)

    def tie_cut():
        need = n_keep - count(lambda k, s: k > thr)
        bits = S.bit_length()

        def step(i, cut):
            cand = cut + lax.shift_left(jnp.int32(1), bits - 1 - i)
            below = count(lambda k, s: (k == thr) & (s < cand))
            return jnp.where(below < need, cand, cut)

        return lax.fori_loop(0, bits, step, jnp.zeros((1, tq), I32))

    has_ties = jnp.max(count(lambda k, s: k >= thr)) > n_keep
    cut = lax.cond(has_ties, tie_cut, lambda: jnp.full((1, tq), S, I32))

    def out_tile(kt, c):
        off = pl.multiple_of(kt * tq, tq)
        key = key_ref[pl.ds(off, tq), :]
        s_idx = off + s_loc
        ok = ((key > thr) | ((key == thr) & (s_idx <= cut))) & (s_idx <= t_idx)
        o_ref[pl.ds(off, tq), :] = jnp.where(ok, 0.0, NEG).astype(o_ref.dtype)
        return c

    def neg_tile(kt, c):
        off = pl.multiple_of(kt * tq, tq)
        o_ref[pl.ds(off, tq), :] = jnp.full((tq, tq), NEG, o_ref.dtype)
        return c

    lax.fori_loop(0, n_tiles, out_tile, 0)
    lax.fori_loop(n_tiles, S // tq, neg_tile, 0)


def _indexer(ik, iq, iwt, n_keep, tq=256):
    B, S, _ = ik.shape
    n_heads = iwt.shape[1]
    tq = _tile(S, tq, LANE)
    est = 2 * (S * IDX_DIM * 2 + tq * n_heads * IDX_DIM * 2 + n_heads * tq * 4 + S * tq * 2) + S * tq * 4 + 8 * tq * tq * 4
    return pl.pallas_call(
        functools.partial(_indexer_body, tq=tq, n_heads=n_heads, n_keep=n_keep),
        grid=(B, S // tq),
        in_specs=[pl.BlockSpec((None, S, IDX_DIM), lambda b, i: (b, 0, 0)),
                  pl.BlockSpec((None, tq, n_heads * IDX_DIM), lambda b, i: (b, i, 0)),
                  pl.BlockSpec((None, n_heads, tq), lambda b, i: (b, 0, i))],
        out_specs=pl.BlockSpec((None, S, tq), lambda b, i: (b, 0, i)),
        out_shape=jax.ShapeDtypeStruct((B, S, S), BF16),
        scratch_shapes=[pltpu.VMEM((S, tq), I32)],
        compiler_params=_params(("parallel", "arbitrary"), est),
        name="dsa_indexer",
    )(ik, iq, iwt)


def _dsa_attn_body(qn_ref, qr_ref, kn_ref, kr_ref, vt_ref, bias_ref, o_ref, s_ref, acc_ref, *, tq, tk, hg):
    qi = pl.program_id(2)
    heads = [slice(hh * C_NOPE, (hh + 1) * C_NOPE) for hh in range(hg)]
    qs = [jnp.concatenate([qn_ref[:, hs], qr_ref[:, (hh // 2) * LANE:(hh // 2 + 1) * LANE]], axis=1)
          for hh, hs in enumerate(heads)]

    def scores(n):
        off = pl.multiple_of(n * tk, tk)
        bias = bias_ref[pl.ds(off, tk), :].astype(F32)
        kr = [kr_ref[par, pl.ds(off, tk), :] for par in range(2)]
        return tuple(bias + _nt(jnp.concatenate([kn_ref[pl.ds(off, tk), hs], kr[hh % 2]], axis=1), qs[hh])
                     for hh, hs in enumerate(heads))

    def body(n, carry):
        return _softmax_steps(scores(n), carry, s_ref, acc_ref, lambda hh: vt_ref[hh, n])

    acc_ref[...] = jnp.zeros(acc_ref.shape, F32)
    lax.fori_loop(0, ((qi + 1) * tq + tk - 1) // tk, body, _init_carry(hg, tq))
    for hh, hs in enumerate(heads):
        o_ref[:, hs] = _normalized(acc_ref, hh, C_NOPE).T.astype(o_ref.dtype)


def _dsa_attention(qn, qr, kv, kr2, vt, bias, n_heads, tq, hg=8):
    B, S, _ = qn.shape
    hg = min(hg, n_heads)
    assert hg % 2 == 0 and n_heads % hg == 0
    ng = n_heads // hg
    nk, tk = vt.shape[2], vt.shape[4]
    est = 2 * (tq * hg * 192 * 2 + 2 * S * hg * LANE * 2 + 2 * S * LANE * 2 + S * tq * 2 + tq * hg * LANE * 2) \
        + hg * 3 * tk * tq * 4
    return pl.pallas_call(
        functools.partial(_dsa_attn_body, tq=tq, tk=tk, hg=hg),
        grid=(B, ng, S // tq),
        in_specs=[pl.BlockSpec((None, tq, hg * C_NOPE), lambda b, g, i: (b, i, g)),
                  pl.BlockSpec((None, tq, hg * C_ROPE), lambda b, g, i: (b, i, g)),
                  pl.BlockSpec((None, S, hg * C_NOPE), lambda b, g, i: (b, 0, g)),
                  pl.BlockSpec((None, 2, S, LANE), lambda b, g, i: (b, 0, 0, 0)),
                  pl.BlockSpec((None, hg, nk, C_NOPE + ONES_ROWS, tk), lambda b, g, i: (b, g, 0, 0, 0)),
                  pl.BlockSpec((None, S, tq), lambda b, g, i: (b, 0, i))],
        out_specs=pl.BlockSpec((None, tq, hg * C_NOPE), lambda b, g, i: (b, i, g)),
        out_shape=jax.ShapeDtypeStruct((B, S, n_heads * C_NOPE), BF16),
        scratch_shapes=[pltpu.VMEM((hg, tk, tq), F32), pltpu.VMEM((hg, C_NOPE + ONES_ROWS, tq), F32)],
        compiler_params=_params(("parallel", "parallel", "arbitrary"), est),
        name="dsa_attention",
    )(qn, qr, kv, kr2, vt, bias)


def _rope_tables(seq, dim):
    inv_freq = ROPE_THETA ** (-jnp.arange(0, dim, 2, dtype=F32) / dim)
    ang = jnp.arange(seq, dtype=F32)[:, None] * inv_freq[None, :]
    return jnp.cos(ang), jnp.sin(ang)


def _cross_block(x, mem, l, norm_xattn, norm_mem, xa_wq, xa_wk, xa_wv, xa_wo, xa_qnorm, xa_knorm, B, S):
    N, D = x.shape
    X = xa_wq.shape[2]
    n_heads = X // HEAD_DIM
    M = mem.shape[1]
    h = _rmsnorm(x, norm_xattn[l], BF16)
    mn = _rmsnorm(mem.reshape(B * M, D), norm_mem[l], BF16)
    ones = jnp.ones((S, LANE), F32)
    q = _matmul(h, xa_wq[l].astype(BF16), out_dtype=BF16, name="xattn_q",
                epi=dict(flags=[EPI_NORM] * (X // _tile(X, 1024, LANE)), modes=(EPI_NORM,), rope_half=0,
                         tn=_tile(X, 1024, LANE), cos=ones, sin=ones,
                         gain=jnp.tile(xa_qnorm[l] * (HEAD_DIM ** -0.5 * LOG2E), n_heads)))
    wkv = jnp.concatenate([xa_wk[l], xa_wv[l]], axis=1).astype(BF16)
    tn = _tile(X, 1024, LANE)
    kv = _matmul(mn, wkv, out_dtype=BF16, name="xattn_kv",
                 epi=dict(flags=[EPI_NORM] * (X // tn) + [EPI_NONE] * (X // tn), modes=(EPI_NORM,), rope_half=0,
                          tn=tn, cos=jnp.ones((M, LANE), F32), sin=jnp.ones((M, LANE), F32),
                          gain=jnp.concatenate([jnp.tile(xa_knorm[l], n_heads), jnp.ones((X,), F32)])))
    o = _cross_attention(q.reshape(B, S, X), kv.reshape(B, M, 2 * X), n_heads)
    return _matmul(o.reshape(N, X), xa_wo[l].astype(BF16), out_dtype=F32, name="xattn_out", resid=x)


def _even_mixer(x, l, i, B, S, norm_mix, ev_w_in, ev_a_qnorm, ev_a_knorm, lam_params, ev_a_subln,
                ev_b_qnorm, ev_b_knorm, ev_w_out):
    N, D = x.shape
    a_heads = D // (4 * HEAD_DIM)
    b_heads = D // (2 * HEAD_DIM)
    aw = a_heads * 2 * HEAD_DIM
    bw = b_heads * HEAD_DIM
    width = 3 * aw + 3 * bw
    scale = HEAD_DIM ** -0.5 * LOG2E
    cos, sin = _rope_tables(S, HEAD_DIM)
    cos2 = jnp.concatenate([cos, cos], axis=-1)
    sin2 = jnp.concatenate([-sin, sin], axis=-1)
    tn = _tile(math.gcd(aw, bw), 1024, LANE)
    seg = [(aw, EPI_NORM_ROPE, ev_a_qnorm[i] * scale), (aw, EPI_NORM_ROPE, ev_a_knorm[i]), (aw, EPI_NONE, None),
           (bw, EPI_NORM_ROPE, ev_b_qnorm[i] * scale), (bw, EPI_NORM_ROPE, ev_b_knorm[i]), (bw, EPI_NONE, None)]
    flags, gains = [], []
    for w, flag, g in seg:
        flags += [flag] * (w // tn)
        gains.append(jnp.ones((w,), F32) if g is None else jnp.tile(g.astype(F32), w // HEAD_DIM))
    h = _rmsnorm(x, norm_mix[l], BF16)
    qkv = _matmul(h, ev_w_in[i], out_dtype=BF16, name="even_in",
                  epi=dict(flags=flags, modes=(EPI_NORM_ROPE,), rope_half=HEAD_DIM // 2, tn=tn,
                           cos=cos2, sin=sin2, gain=jnp.concatenate(gains)))
    qkv = qkv.reshape(B, S, width)
    lam_init = 0.8 - 0.6 * math.exp(-0.3 * l)
    tq = _tile(S, 256, LANE)
    ya = _diff_attention(qkv, _value_tiles(qkv[:, :, 2 * aw:3 * aw], a_heads, tq), a_heads, lam_params,
                         ev_a_subln[i], lam_init, tq)
    c0 = 3 * aw // HEAD_DIM
    yb = _moba_attention(qkv, _value_tiles(qkv[:, :, 3 * aw + 2 * bw:], b_heads, MOBA_BLOCK), b_heads,
                         c0, c0 + b_heads)
    y = jnp.concatenate([ya, yb], axis=-1).reshape(N, aw + bw)
    return _matmul(y, ev_w_out[i].astype(BF16), out_dtype=F32, name="even_out", resid=x)


def _odd_mixer(x, l, i, B, S, norm_mix, od_w_in, od_qa_norm, od_w_qb, od_q_norm, od_kv_norm, od_kr_norm,
               od_w_uk, od_w_uv, od_w_iqb, od_ik_norm, od_w_out):
    N, D = x.shape
    c1 = od_qa_norm.shape[1]
    kv_rank = od_kv_norm.shape[1]
    c2 = c1 + kv_rank
    n_heads = od_w_uk.shape[2]
    idx_heads = od_w_iqb.shape[2] // IDX_DIM
    assert od_w_in.shape[2] == c2 + C_ROPE + IDX_DIM + idx_heads and idx_heads == C_ROPE
    scale = (C_NOPE + C_ROPE) ** -0.5 * LOG2E
    cos, sin = _rope_tables(S, C_ROPE)
    one, zero = jnp.ones((S, C_ROPE), F32), jnp.zeros((S, C_ROPE), F32)
    cos_p = jnp.concatenate([cos, cos, one], axis=-1)
    sin_p = jnp.concatenate([-sin, sin, zero], axis=-1)
    cos_q = jnp.concatenate([cos, cos, cos, cos], axis=-1)
    sin_q = jnp.concatenate([-sin, sin, -sin, sin], axis=-1)

    h = _rmsnorm(x, norm_mix[l], BF16)
    proj = _matmul(h, od_w_in[i], out_dtype=F32, name="odd_in")
    g_kr = jnp.concatenate([od_kr_norm[i], jnp.zeros((LANE - C_ROPE,), F32)])
    qa, ckv, kr2, ik, iw = _odd_prep(proj.reshape(B, S, -1), c1, c2, od_qa_norm[i], od_kv_norm[i], g_kr,
                                     od_ik_norm[i], cos_p, sin_p, idx_heads ** -0.5 * IDX_DIM ** -0.5)
    qa = qa.reshape(N, c1)
    wqb = od_w_qb[i].reshape(c1, n_heads, C_NOPE + C_ROPE)
    wqb = jnp.concatenate([wqb[:, :, :C_NOPE].reshape(c1, -1), wqb[:, :, C_NOPE:].reshape(c1, -1)], axis=1)
    qraw = _matmul(qa, wqb.astype(BF16), out_dtype=F32, name="odd_qb")
    qn, qr = _q_prep(qraw.reshape(B, S, -1), n_heads, od_q_norm[i][:C_NOPE] * scale,
                     jnp.tile(od_q_norm[i][C_NOPE:], 2) * scale, cos_q, sin_q)
    tn = _tile(idx_heads * IDX_DIM, 1024, LANE)
    iq = _matmul(qa, od_w_iqb[i], out_dtype=BF16, name="odd_iqb",
                 epi=dict(flags=[EPI_ROPE] * (idx_heads * IDX_DIM // tn), modes=(EPI_ROPE,), rope_half=IDX_ROPE // 2,
                          tn=tn, cos=cos_p, sin=sin_p, gain=jnp.ones((idx_heads * IDX_DIM,), F32)))
    wkv = jnp.concatenate([od_w_uk[i].reshape(kv_rank, -1), od_w_uv[i].reshape(kv_rank, -1)], axis=1)
    kv = _matmul(ckv.reshape(N, kv_rank), wkv.astype(BF16), out_dtype=BF16, name="odd_kv")

    n_keep = min(IDX_TOPK, S // 4)
    tq = _tile(S, 256, LANE)
    iwt = jnp.swapaxes(iw[:, :, :idx_heads], 1, 2)
    bias = _indexer(ik, iq.reshape(B, S, -1), iwt, n_keep, tq)
    kv = kv.reshape(B, S, -1)
    vt = _value_tiles(kv[:, :, n_heads * C_NOPE:], n_heads, _tile(S, 2 * tq, tq))
    y = _dsa_attention(qn, qr, kv, kr2, vt, bias, n_heads, tq)
    return _matmul(y.reshape(N, -1), od_w_out[i].astype(BF16), out_dtype=F32, name="odd_out", resid=x)


def _dense_ffn(x, g, wg, wu, wd):
    N, D = x.shape
    F = wg.shape[1]
    h = _rmsnorm(x, g, BF16)
    tm = _tile(N, 2048, 16)
    tn = _tile(F, 512, LANE)
    nt = N // tm
    hid = _swiglu_up(h, wg[None], wu[None], jnp.zeros((nt,), I32), jnp.full((1,), nt, I32),
                     jnp.full((nt,), tm, I32), tm=tm, tn=tn, name="ffn_up")
    return _matmul(hid, wd.astype(BF16), out_dtype=F32, name="ffn_down", resid=x, tm=512, tn=1024,
                   tk=F if F <= 4096 else _tile(F, F // 2, LANE))


def _moe_ffn(x, g, router, wg, wu, wd, tm=512):
    N, D = x.shape
    E, _, F = wg.shape
    h, gate, sel = _norm_router(x, g, router)
    tm = _tile(N, tm, 16)
    cnt = jnp.sum(sel, axis=0)
    tiles_e = (cnt + tm - 1) // tm
    tile_end = jnp.cumsum(tiles_e)
    start = (tile_end - tiles_e) * tm
    rank = jnp.cumsum(sel, axis=0) - sel
    P = N * TOP_K + E * tm
    n_tiles = P // tm
    slot = (start[None, :] + rank).astype(I32)
    lane = jnp.arange(E, dtype=I32)[None, :]
    e2 = jnp.stack([jnp.min(jnp.where(sel > 0, lane, E), axis=1), jnp.max(jnp.where(sel > 0, lane, -1), axis=1)], 1)
    slots2 = jnp.take_along_axis(slot, e2, axis=1)
    gates2 = jnp.take_along_axis(gate, e2, axis=1)
    tok2 = jnp.broadcast_to(jnp.arange(N, dtype=I32)[:, None], (N, TOP_K))
    tok_of_slot = jnp.zeros((P,), I32).at[slots2.reshape(-1)].set(tok2.reshape(-1), unique_indices=True)
    tile_expert = jnp.minimum(jnp.searchsorted(tile_end, jnp.arange(n_tiles, dtype=I32), side="right"),
                              E - 1).astype(I32)
    n_active = tile_end[-1:].astype(I32)
    tile_row = jnp.arange(n_tiles, dtype=I32) * tm
    valid_rows = jnp.where(tile_row < n_active[0] * tm,
                           jnp.clip((start + cnt)[tile_expert] - tile_row, 0, tm), 0).astype(I32)

    xs = _gather_rows(h, tok_of_slot, BF16)
    hid = _swiglu_up(xs, wg, wu, tile_expert, n_active, valid_rows, tm=tm, tn=_tile(F, 512, LANE), name="moe_up")
    y = _grouped_down(hid, wd, tile_expert, n_active, valid_rows, tm=tm, tn=_tile(D, 1024, LANE), name="moe_down")
    return _moe_combine(x, y, slots2, gates2)


def kernel(x, mem, norm_mix, norm_xattn, norm_mem, norm_ffn, ev_w_in, ev_a_qnorm, ev_a_knorm, ev_lambda_q1, ev_lambda_k1, ev_lambda_q2, ev_lambda_k2, ev_a_subln, ev_b_qnorm, ev_b_knorm, ev_w_out, od_w_in, od_qa_norm, od_w_qb, od_q_norm, od_kv_norm, od_kr_norm, od_w_uk, od_w_uv, od_w_iqb, od_ik_norm, od_w_out, xa_wq, xa_wk, xa_wv, xa_wo, xa_qnorm, xa_knorm, ffn_wg, ffn_wu, ffn_wd, moe_router, moe_wg, moe_wu, moe_wd):
    B, S, D = x.shape
    depth = norm_mix.shape[0]
    x = x.reshape(B * S, D)
    for l in range(depth):
        i = l // 2
        if l % 2 == 0:
            x = _even_mixer(x, l, i, B, S, norm_mix, ev_w_in, ev_a_qnorm, ev_a_knorm,
                            (ev_lambda_q1[i], ev_lambda_k1[i], ev_lambda_q2[i], ev_lambda_k2[i]),
                            ev_a_subln, ev_b_qnorm, ev_b_knorm, ev_w_out)
        else:
            x = _odd_mixer(x, l, i, B, S, norm_mix, od_w_in, od_qa_norm, od_w_qb, od_q_norm, od_kv_norm,
                           od_kr_norm, od_w_uk, od_w_uv, od_w_iqb, od_ik_norm, od_w_out)
        x = _cross_block(x, mem, l, norm_xattn, norm_mem, xa_wq, xa_wk, xa_wv, xa_wo, xa_qnorm, xa_knorm, B, S)
        if l % 2 == 0:
            x = _dense_ffn(x, norm_ffn[l], ffn_wg[i], ffn_wu[i], ffn_wd[i])
        else:
            x = _moe_ffn(x, norm_ffn[l], moe_router[i], moe_wg[i], moe_wu[i], moe_wd[i])
    return x.reshape(B, S, D)
```

```python
import functools
import math

import jax
import jax.numpy as jnp
import numpy as np
from jax import lax
from jax.experimental import pallas as pl
from jax.experimental.pallas import tpu as pltpu

F32 = jnp.float32
BF16 = jnp.bfloat16
I32 = jnp.int32

LANE = 128
MXU_WIDTH = 256
V7X_VMEM_BYTES = 64 * 1024 * 1024
VMEM_CAP = V7X_VMEM_BYTES - 4 * 1024 * 1024

HEAD_DIM = 128
ROPE_THETA = 10000.0
EPS = 1e-6
MOBA_BLOCK = 256
MOBA_TOPK = 3
C_NOPE = 128
C_ROPE = 64
IDX_DIM = 128
IDX_ROPE = 64
IDX_TOPK = 256
TOP_K = 2
NEG = -1e30
LOG2E = math.log2(math.e)
INT_MIN = -(2 ** 31)


def _tile(dim, pref, align):
    t = min(pref, dim)
    t -= t % align
    while t >= align:
        if dim % t == 0:
            return t
        t -= align
    return dim


def _params(sem, est_bytes):
    limit = int(min(max(est_bytes * 1.3 + (4 << 20), 32 << 20), VMEM_CAP))
    return pltpu.CompilerParams(dimension_semantics=sem, vmem_limit_bytes=limit)


def _nt(a, b):
    return lax.dot_general(a, b, (((1,), (1,)), ((), ())), preferred_element_type=F32)


def _softmax_steps(sts, carry, s_ref, acc_ref, value_tile):
    for c, st in enumerate(sts):
        s_ref[c] = st
    new = []
    for c in range(len(sts)):
        m_new = jnp.maximum(carry[c], jnp.max(s_ref[c], axis=0, keepdims=True))
        alpha = jnp.exp2(carry[c] - m_new)
        p = jnp.exp2(s_ref[c] - m_new)
        new.append(m_new)
        acc_ref[c] = alpha * acc_ref[c] + jnp.dot(value_tile(c), p.astype(BF16), preferred_element_type=F32)
    return tuple(new)


def _normalized(acc_ref, c, dv):
    acc = acc_ref[c]
    return acc[:dv] / acc[dv:dv + 1]


def _init_carry(n_chains, tq):
    return tuple(jnp.full((1, tq), NEG, F32) for _ in range(n_chains))


ONES_ROWS = 16


def _value_tiles(v, n_heads, tk):
    B, S, C = v.shape
    vt = v.reshape(B, S // tk, tk, n_heads, C // n_heads).transpose(0, 3, 1, 4, 2)
    extra = jnp.zeros(vt.shape[:3] + (ONES_ROWS, tk), v.dtype).at[..., 0, :].set(1)
    return jnp.concatenate([vt, extra], axis=3)


def _rmsnorm_body(x_ref, g_ref, o_ref):
    x = x_ref[...].astype(F32)
    ms = jnp.mean(x * x, axis=-1, keepdims=True)
    o_ref[...] = (x * lax.rsqrt(ms + EPS) * g_ref[...]).astype(o_ref.dtype)


def _rmsnorm(x, g, out_dtype, tm=256):
    M, D = x.shape
    tm = _tile(M, tm, 16)
    est = 2 * tm * D * (4 + 4)
    return pl.pallas_call(
        _rmsnorm_body,
        grid=(M // tm,),
        in_specs=[pl.BlockSpec((tm, D), lambda i: (i, 0)), pl.BlockSpec((1, D), lambda i: (0, 0))],
        out_specs=pl.BlockSpec((tm, D), lambda i: (i, 0)),
        out_shape=jax.ShapeDtypeStruct((M, D), out_dtype),
        compiler_params=_params(("parallel",), est),
        name="rmsnorm",
    )(x, g.reshape(1, D).astype(F32))


def _norm_router_body(x_ref, g_ref, rt_ref, h_ref, gate_ref, sel_ref, *, n_exp):
    x = x_ref[...]
    ms = jnp.mean(x * x, axis=-1, keepdims=True)
    h = x * lax.rsqrt(ms + EPS) * g_ref[...]
    h_ref[...] = h
    lane = lax.broadcasted_iota(I32, gate_ref.shape, 1)
    logits = jnp.full(gate_ref.shape, -jnp.inf, F32)
    for e in range(n_exp):
        col = jnp.sum(h * rt_ref[e:e + 1, :], axis=-1, keepdims=True)
        logits = jnp.where(lane == e, col, logits)
    m1 = jnp.max(logits, axis=-1, keepdims=True)
    i1 = jnp.min(jnp.where(logits == m1, lane, LANE), axis=-1, keepdims=True)
    rest = jnp.where(lane == i1, -jnp.inf, logits)
    m2 = jnp.max(rest, axis=-1, keepdims=True)
    i2 = jnp.min(jnp.where(rest == m2, lane, LANE), axis=-1, keepdims=True)
    e2 = jnp.exp(m2 - m1)
    den = 1.0 + e2
    gate_ref[...] = jnp.where(lane == i1, 1.0 / den, 0.0) + jnp.where(lane == i2, e2 / den, 0.0)
    sel_ref[...] = jnp.where((lane == i1) | (lane == i2), 1, 0).astype(I32)


def _norm_router(x, g, router, tm=256):
    M, D = x.shape
    n_exp = router.shape[1]
    tm = _tile(M, tm, 8)
    est = 2 * tm * D * 8 + 2 * 8 * D * 4
    h, gate, sel = pl.pallas_call(
        functools.partial(_norm_router_body, n_exp=n_exp),
        grid=(M // tm,),
        in_specs=[pl.BlockSpec((tm, D), lambda i: (i, 0)), pl.BlockSpec((1, D), lambda i: (0, 0)),
                  pl.BlockSpec((n_exp, D), lambda i: (0, 0))],
        out_specs=[pl.BlockSpec((tm, D), lambda i: (i, 0)), pl.BlockSpec((tm, LANE), lambda i: (i, 0)),
                   pl.BlockSpec((tm, LANE), lambda i: (i, 0))],
        out_shape=[jax.ShapeDtypeStruct((M, D), F32), jax.ShapeDtypeStruct((M, LANE), F32),
                   jax.ShapeDtypeStruct((M, LANE), I32)],
        compiler_params=_params(("parallel",), est),
        name="norm_router",
    )(x, g.reshape(1, D).astype(F32), router.T.astype(F32))
    return h, gate[:, :n_exp], sel[:, :n_exp]


EPI_NONE, EPI_NORM, EPI_NORM_ROPE, EPI_ROPE = 0, 1, 2, 3


def _head_epilogue(x, g, cos, sin, mode, rope_half):
    if mode in (EPI_NORM, EPI_NORM_ROPE):
        ss = jnp.dot((x * x).astype(BF16), jnp.ones((LANE, LANE), BF16), preferred_element_type=F32)
        x = x * lax.rsqrt(ss * (1.0 / LANE) + EPS) * g
    if mode in (EPI_NORM_ROPE, EPI_ROPE):
        if rope_half == LANE // 2:
            r = pltpu.roll(x, LANE // 2, axis=1)
        else:
            lane = lax.broadcasted_iota(I32, x.shape, 1)
            first = (lane % (2 * rope_half)) < rope_half
            r = jnp.where(first, pltpu.roll(x, LANE - rope_half, axis=1), pltpu.roll(x, rope_half, axis=1))
        x = x * cos + r * sin
    return x


def _mm_body(flags_ref, a_ref, w_ref, *rest, nk, has_resid, modes, rope_half, tn, cw):
    rest = list(rest)
    resid_ref = rest.pop(0) if has_resid else None
    gain_ref = cos_ref = sin_ref = None
    if modes:
        gain_ref, cos_ref, sin_ref = rest.pop(0), rest.pop(0), rest.pop(0)
    o_ref = rest.pop(0)
    acc_ref = rest.pop(0) if nk > 1 else None
    j = pl.program_id(1)
    k = pl.program_id(2)

    def emit(cols, acc, mode):
        if mode != EPI_NONE:
            for c in range(cols.start, cols.stop, LANE):
                sl = slice(c, c + LANE)
                y = _head_epilogue(acc[:, sl.start - cols.start:sl.stop - cols.start], gain_ref[:, sl],
                                   cos_ref[...], sin_ref[...], mode, rope_half)
                o_ref[:, sl] = y.astype(o_ref.dtype)
        elif has_resid:
            o_ref[:, cols] = (resid_ref[:, cols] + acc).astype(o_ref.dtype)
        else:
            o_ref[:, cols] = acc.astype(o_ref.dtype)

    if nk == 1:
        def run(mode):
            pending = None
            for c in range(0, tn, cw):
                cols = slice(c, c + cw)
                acc = jnp.dot(a_ref[...], w_ref[:, cols].astype(BF16), preferred_element_type=F32)
                if pending is not None:
                    emit(*pending, mode)
                pending = (cols, acc)
            emit(*pending, mode)

        if modes:
            flag = flags_ref[j]
            for mode in (EPI_NONE,) + modes:
                pl.when(flag == mode)(functools.partial(run, mode))
        else:
            run(EPI_NONE)
    else:
        part = jnp.dot(a_ref[...], w_ref[...], preferred_element_type=F32)

        @pl.when(k == 0)
        def _():
            acc_ref[...] = part

        @pl.when((k > 0) & (k < nk - 1))
        def _():
            acc_ref[...] += part

        @pl.when(k == nk - 1)
        def _():
            emit(slice(0, tn), acc_ref[...] + part, EPI_NONE)


def _matmul(a, w, *, out_dtype, name, resid=None, epi=None, tm=1024, tn=1024, tk=4096):
    M, K = a.shape
    N = w.shape[1]
    tm = _tile(epi["cos"].shape[0] if epi else M, tm, 16)
    tn = epi["tn"] if epi else _tile(N, tn, LANE)
    tk = _tile(K, tk, LANE)
    nk = K // tk
    assert a.dtype == BF16 and (w.dtype == BF16 or nk == 1) and (nk == 1 or not epi)
    a_bytes = a.dtype.itemsize
    o_bytes = jnp.dtype(out_dtype).itemsize
    modes = tuple(epi["modes"]) if epi else ()
    in_specs = [pl.BlockSpec((tm, tk), lambda i, j, k, f: (i, k)),
                pl.BlockSpec((tk, tn), lambda i, j, k, f: (k, j))]
    args = [a, w]
    est = 2 * (tm * tk * a_bytes + tk * tn * w.dtype.itemsize + tm * tn * o_bytes) + 3 * tm * tn * 4
    if resid is not None:
        in_specs.append(pl.BlockSpec((tm, tn), lambda i, j, k, f: (i, j)))
        args.append(resid)
        est += 2 * tm * tn * 4
    if epi:
        ns = epi["cos"].shape[0] // tm
        in_specs += [pl.BlockSpec((1, tn), lambda i, j, k, f: (0, j)),
                     pl.BlockSpec((tm, LANE), lambda i, j, k, f: (i % ns, 0)),
                     pl.BlockSpec((tm, LANE), lambda i, j, k, f: (i % ns, 0))]
        args += [epi["gain"].reshape(1, N).astype(F32), epi["cos"], epi["sin"]]
        flags = jnp.asarray(epi["flags"], I32)
        est += 4 * tm * LANE * 4
    else:
        flags = jnp.zeros((N // tn,), I32)
    body = functools.partial(_mm_body, nk=nk, has_resid=resid is not None, modes=modes,
                             rope_half=epi["rope_half"] if epi else 0, tn=tn, cw=math.gcd(tn, MXU_WIDTH))
    return pl.pallas_call(
        body,
        grid_spec=pltpu.PrefetchScalarGridSpec(
            num_scalar_prefetch=1,
            grid=(M // tm, N // tn, nk),
            in_specs=in_specs,
            out_specs=pl.BlockSpec((tm, tn), lambda i, j, k, f: (i, j)),
            scratch_shapes=[pltpu.VMEM((tm, tn), F32)] if nk > 1 else [],
        ),
        out_shape=jax.ShapeDtypeStruct((M, N), out_dtype),
        compiler_params=_params(("parallel", "parallel", "arbitrary"), est),
        name=name,
    )(flags, *args)


def _segments(tile_expert, n_active):
    T = tile_expert.shape[0]
    idx = jnp.arange(T, dtype=I32)
    prev = jnp.concatenate([tile_expert[:1] - 1, tile_expert[:-1]])
    first = (idx < n_active[0]) & (tile_expert != prev)
    first_idx = jnp.where(first, idx, T)
    after = jnp.concatenate([lax.cummin(first_idx[::-1])[::-1][1:], jnp.full((1,), T, I32)])
    nxt = jnp.where(after < T, tile_expert[jnp.minimum(after, T - 1)], -1)
    return first.astype(I32), nxt.astype(I32)


def _stream_weights(te_ref, first_ref, next_ref, w_refs, stage_refs, cast_refs, sem, tn):
    n = pl.program_id(0)
    m = pl.program_id(1)

    def copies(e, nn):
        cols = pl.ds(pl.multiple_of(nn * tn, tn), tn)
        return [pltpu.make_async_copy(w.at[e, :, cols], st, sem.at[i])
                for i, (w, st) in enumerate(zip(w_refs, stage_refs))]

    @pl.when(first_ref[m] == 1)
    def _():
        @pl.when((n == 0) & (m == 0))
        def _():
            for c in copies(te_ref[0], 0):
                c.start()

        for c in copies(te_ref[m], n):
            c.wait()
        for st, cb in zip(stage_refs, cast_refs):
            cb[...] = st[...].astype(BF16)
        last = next_ref[m] < 0
        e2 = jnp.where(last, te_ref[0], next_ref[m])
        n2 = jnp.where(last, n + 1, n)

        @pl.when(n2 < pl.num_programs(0))
        def _():
            for c in copies(e2, n2):
                c.start()


def _row_parts(nv_ref, o_ref, compute, parts):
    m = pl.program_id(1)
    hm = o_ref.shape[0] // parts
    for h in range(parts):
        rows = slice(h * hm, (h + 1) * hm)
        pl.when(nv_ref[m] > h * hm)(functools.partial(compute, rows))

        @pl.when(nv_ref[m] <= h * hm)
        def _(rows=rows):
            o_ref[rows, :] = jnp.zeros((hm, o_ref.shape[1]), o_ref.dtype)


def _swiglu_body(te_ref, na_ref, first_ref, next_ref, nv_ref, a_ref, wg_ref, wu_ref, o_ref,
                 wgf_ref, wuf_ref, wgb_ref, wub_ref, sem):
    tn = o_ref.shape[1]
    _stream_weights(te_ref, first_ref, next_ref, (wg_ref, wu_ref), (wgf_ref, wuf_ref), (wgb_ref, wub_ref), sem, tn)
    cw = math.gcd(tn, MXU_WIDTH)

    def compute(rows):
        def emit(cols, g, u):
            o_ref[rows, cols] = (g * jax.nn.sigmoid(g) * u).astype(o_ref.dtype)

        pending = None
        for c in range(0, tn, cw):
            cols = slice(c, c + cw)
            g = jnp.dot(a_ref[rows, :], wgb_ref[:, cols], preferred_element_type=F32)
            u = jnp.dot(a_ref[rows, :], wub_ref[:, cols], preferred_element_type=F32)
            if pending is not None:
                emit(*pending)
            pending = (cols, g, u)
        emit(*pending)

    _row_parts(nv_ref, o_ref, compute, parts=1)


def _swiglu_up(a, wg, wu, tile_expert, n_active, valid_rows, *, tm, tn, name):
    M, D = a.shape
    F = wg.shape[2]
    first, nxt = _segments(tile_expert, n_active)
    est = 2 * (tm * D * 2 + tm * tn * 2) + 2 * D * tn * (4 + 2) + 4 * tm * tn * 4
    return pl.pallas_call(
        _swiglu_body,
        grid_spec=pltpu.PrefetchScalarGridSpec(
            num_scalar_prefetch=5,
            grid=(F // tn, M // tm),
            in_specs=[pl.BlockSpec((tm, D), lambda n, m, *_: (m, 0)),
                      pl.BlockSpec(memory_space=pl.ANY), pl.BlockSpec(memory_space=pl.ANY)],
            out_specs=pl.BlockSpec((tm, tn), lambda n, m, *_: (m, n)),
            scratch_shapes=[pltpu.VMEM((D, tn), F32), pltpu.VMEM((D, tn), F32),
                            pltpu.VMEM((D, tn), BF16), pltpu.VMEM((D, tn), BF16),
                            pltpu.SemaphoreType.DMA((2,))],
        ),
        out_shape=jax.ShapeDtypeStruct((M, F), BF16),
        compiler_params=_params(("arbitrary", "arbitrary"), est),
        name=name,
    )(tile_expert, n_active, first, nxt, valid_rows, a, wg, wu)


def _down_body(te_ref, na_ref, first_ref, next_ref, nv_ref, a_ref, w_ref, o_ref, wf_ref, wb_ref, sem):
    _stream_weights(te_ref, first_ref, next_ref, (w_ref,), (wf_ref,), (wb_ref,), sem, o_ref.shape[1])

    def compute(rows):
        o_ref[rows, :] = jnp.dot(a_ref[rows, :], wb_ref[...], preferred_element_type=F32)

    _row_parts(nv_ref, o_ref, compute, parts=2)


def _grouped_down(a, wd, tile_expert, n_active, valid_rows, *, tm, tn, name):
    M, F = a.shape
    D = wd.shape[2]
    first, nxt = _segments(tile_expert, n_active)
    est = 2 * (tm * F * 2 + tm * tn * 4) + F * tn * (4 + 2) + 2 * tm * tn * 4
    return pl.pallas_call(
        _down_body,
        grid_spec=pltpu.PrefetchScalarGridSpec(
            num_scalar_prefetch=5,
            grid=(D // tn, M // tm),
            in_specs=[pl.BlockSpec((tm, F), lambda n, m, *_: (m, 0)), pl.BlockSpec(memory_space=pl.ANY)],
            out_specs=pl.BlockSpec((tm, tn), lambda n, m, *_: (m, n)),
            scratch_shapes=[pltpu.VMEM((F, tn), F32), pltpu.VMEM((F, tn), BF16), pltpu.SemaphoreType.DMA((1,))],
        ),
        out_shape=jax.ShapeDtypeStruct((M, D), F32),
        compiler_params=_params(("arbitrary", "arbitrary"), est),
        name=name,
    )(tile_expert, n_active, first, nxt, valid_rows, a, wd)


GATHER_UNROLL = 8


def _gather_body(idx_ref, src_ref, o_ref, buf_ref, sem, *, tg):
    step = pl.program_id(0)

    def row_copy(slot, i, row):
        return pltpu.make_async_copy(src_ref.at[pl.ds(row, 1)], buf_ref.at[slot, pl.ds(i, 1)], sem.at[slot])

    def issue(s):
        slot = s % 2

        def start(i8, c):
            for j in range(GATHER_UNROLL):
                i = i8 * GATHER_UNROLL + j
                row_copy(slot, i, idx_ref[s * tg + i]).start(priority=j % 2)
            return c

        lax.fori_loop(0, tg // GATHER_UNROLL, start, 0)

    @pl.when(step == 0)
    def _():
        issue(step)

    @pl.when(step + 1 < pl.num_programs(0))
    def _():
        issue(step + 1)

    slot = step % 2

    def wait(i, c):
        row_copy(slot, i, 0).wait()
        return c

    lax.fori_loop(0, tg, wait, 0, unroll=GATHER_UNROLL)
    o_ref[...] = buf_ref[slot].astype(o_ref.dtype)


def _gather_rows(src, idx, out_dtype, tg=256):
    P = idx.shape[0]
    D = src.shape[1]
    tg = _tile(P, tg, 16)
    est = 2 * tg * D * 4 + 2 * tg * D * 2
    return pl.pallas_call(
        functools.partial(_gather_body, tg=tg),
        grid_spec=pltpu.PrefetchScalarGridSpec(
            num_scalar_prefetch=1,
            grid=(P // tg,),
            in_specs=[pl.BlockSpec(memory_space=pl.ANY)],
            out_specs=pl.BlockSpec((tg, D), lambda i, idx: (i, 0)),
            scratch_shapes=[pltpu.VMEM((2, tg, D), src.dtype), pltpu.SemaphoreType.DMA((2,))],
        ),
        out_shape=jax.ShapeDtypeStruct((P, D), out_dtype),
        compiler_params=_params(("arbitrary",), est),
        name="moe_dispatch_gather",
    )(idx, src)


def _combine_body(slot_ref, x_ref, g_ref, y_ref, o_ref, buf_ref, sem, *, tc):
    step = pl.program_id(0)

    def row_copy(slot, i, k, row):
        return pltpu.make_async_copy(y_ref.at[pl.ds(row, 1)], buf_ref.at[slot, k, pl.ds(i, 1)], sem.at[slot])

    def issue(s):
        slot = s % 2

        def start(i, c):
            for k in range(TOP_K):
                row_copy(slot, i, k, slot_ref[(s * tc + i) * TOP_K + k]).start(priority=k % 2)
            return c

        lax.fori_loop(0, tc, start, 0, unroll=GATHER_UNROLL // TOP_K)

    @pl.when(step == 0)
    def _():
        issue(step)

    @pl.when(step + 1 < pl.num_programs(0))
    def _():
        issue(step + 1)

    slot = step % 2

    def wait(i, c):
        for k in range(TOP_K):
            row_copy(slot, i, k, 0).wait()
        return c

    lax.fori_loop(0, tc, wait, 0, unroll=GATHER_UNROLL // TOP_K)
    g = g_ref[...]
    o_ref[...] = x_ref[...] + (g[:, 0:1] * buf_ref[slot, 0] + g[:, 1:2] * buf_ref[slot, 1])


def _moe_combine(x, y, slots, gates, tc=128):
    M, D = x.shape
    tc = _tile(M, tc, 8)
    est = 4 * tc * D * 4 + 2 * TOP_K * tc * D * 4
    return pl.pallas_call(
        functools.partial(_combine_body, tc=tc),
        grid_spec=pltpu.PrefetchScalarGridSpec(
            num_scalar_prefetch=1,
            grid=(M // tc,),
            in_specs=[pl.BlockSpec((tc, D), lambda i, s: (i, 0)), pl.BlockSpec((tc, TOP_K), lambda i, s: (i, 0)),
                      pl.BlockSpec(memory_space=pl.ANY)],
            out_specs=pl.BlockSpec((tc, D), lambda i, s: (i, 0)),
            scratch_shapes=[pltpu.VMEM((2, TOP_K, tc, D), F32), pltpu.SemaphoreType.DMA((2,))],
        ),
        out_shape=jax.ShapeDtypeStruct((M, D), F32),
        compiler_params=_params(("arbitrary",), est),
        name="moe_combine",
    )(slots.reshape(-1), x, gates, y)


def _diff_attn_body(q_ref, k_ref, vt_ref, lq1_ref, lk1_ref, lq2_ref, lk2_ref, g_ref, o_ref, s_ref, acc_ref,
                    *, tq, hg, lam_init):
    qi = pl.program_id(2)
    w = 2 * HEAD_DIM
    cols = [slice(hh * w + mi * HEAD_DIM, hh * w + (mi + 1) * HEAD_DIM) for hh in range(hg) for mi in range(2)]
    qs = [q_ref[:, c] for c in cols]

    def step(n, carry, mask):
        off = pl.multiple_of(n * tq, tq)
        sts = [_nt(k_ref[pl.ds(off, tq), col_sl], qs[c]) for c, col_sl in enumerate(cols)]
        if mask is not None:
            sts = [jnp.where(mask, st, NEG) for st in sts]
        return _softmax_steps(sts, carry, s_ref, acc_ref, lambda c: vt_ref[c // 2, n])

    acc_ref[...] = jnp.zeros(acc_ref.shape, F32)
    carry = lax.fori_loop(0, qi, lambda n, c: step(n, c, None), _init_carry(len(cols), tq))
    key = lax.broadcasted_iota(I32, (tq, tq), 0)
    qry = lax.broadcasted_iota(I32, (tq, tq), 1)
    step(qi, carry, key <= qry)
    lam = (jnp.exp(jnp.sum(lq1_ref[...] * lk1_ref[...], axis=-1, keepdims=True))
           - jnp.exp(jnp.sum(lq2_ref[...] * lk2_ref[...], axis=-1, keepdims=True)) + lam_init)
    for hh in range(hg):
        y = (_normalized(acc_ref, 2 * hh, w) - lam * _normalized(acc_ref, 2 * hh + 1, w)).T
        ms = jnp.mean(y * y, axis=-1, keepdims=True)
        hs = slice(hh * w, (hh + 1) * w)
        o_ref[:, hs] = (y * lax.rsqrt(ms + EPS) * g_ref[...] * (1.0 - lam_init)).astype(o_ref.dtype)


def _diff_attention(qkv, vt, n_heads, lam_params, subln, lam_init, tq, hg=4):
    B, S, _ = qkv.shape
    w = 2 * HEAD_DIM
    hg = math.gcd(hg, n_heads)
    ng = n_heads // hg
    vec = pl.BlockSpec((1, HEAD_DIM), lambda b, g, i: (0, 0))
    est = 2 * (2 * tq * hg * w * 2 + 2 * S * hg * w * 2) + 2 * hg * (3 * tq * tq * 4 + tq * w * 4)
    return pl.pallas_call(
        functools.partial(_diff_attn_body, tq=tq, hg=hg, lam_init=lam_init),
        grid=(B, ng, S // tq),
        in_specs=[pl.BlockSpec((None, tq, hg * w), lambda b, g, i: (b, i, g)),
                  pl.BlockSpec((None, S, hg * w), lambda b, g, i: (b, 0, ng + g)),
                  pl.BlockSpec((None, hg, S // tq, w + ONES_ROWS, tq), lambda b, g, i: (b, g, 0, 0, 0)),
                  vec, vec, vec, vec,
                  pl.BlockSpec((1, w), lambda b, g, i: (0, 0))],
        out_specs=pl.BlockSpec((None, tq, hg * w), lambda b, g, i: (b, i, g)),
        out_shape=jax.ShapeDtypeStruct((B, S, n_heads * w), BF16),
        scratch_shapes=[pltpu.VMEM((2 * hg, tq, tq), F32), pltpu.VMEM((2 * hg, w + ONES_ROWS, tq), F32)],
        compiler_params=_params(("parallel", "parallel", "arbitrary"), est),
        name="diff_attention",
    )(qkv, qkv, vt, *[p.reshape(1, HEAD_DIM).astype(F32) for p in lam_params],
      subln.reshape(1, w).astype(F32))


def _moba_body(q_ref, k_ref, vt_ref, o_ref, kmean_ref, bias_ref, s_ref, acc_ref, *, nb, hg):
    qi = pl.program_id(2)
    blk = MOBA_BLOCK
    d = HEAD_DIM
    heads = [slice(hh * d, (hh + 1) * d) for hh in range(hg)]

    @pl.when(qi == 0)
    def _():
        for hh, hs in enumerate(heads):
            for n in range(nb):
                kb = k_ref[n * blk:(n + 1) * blk, hs].astype(F32)
                kmean_ref[hh, n:n + 1, :] = jnp.mean(kb, axis=0, keepdims=True)

    blk_id = lax.broadcasted_iota(I32, (nb, blk), 0)
    qs = []
    for hh, hs in enumerate(heads):
        q = q_ref[:, hs]
        km = kmean_ref[hh]
        km_hi = km.astype(BF16)
        km_lo = (km - km_hi.astype(F32)).astype(BF16)
        gate = _nt(km_hi, q) + _nt(km_lo, q)
        gate = jnp.where(blk_id < qi, gate, -jnp.inf)
        sel = jnp.zeros(gate.shape, jnp.bool_)
        for _ in range(MOBA_TOPK):
            mx = jnp.max(gate, axis=0, keepdims=True)
            idx = jnp.min(jnp.where(gate == mx, blk_id, nb), axis=0, keepdims=True)
            sel = sel | ((blk_id == idx) & (mx > -jnp.inf))
            gate = jnp.where(blk_id == idx, -jnp.inf, gate)
        qs.append(q)
        bias_ref[hh] = jnp.where(sel, 0.0, NEG)

    def step(n, carry, mask):
        off = pl.multiple_of(n * blk, blk)
        sts = [_nt(k_ref[pl.ds(off, blk), hs], qs[hh]) for hh, hs in enumerate(heads)]
        if mask is None:
            sts = [st + bias_ref[hh, pl.ds(n, 1), :] for hh, st in enumerate(sts)]
        else:
            sts = [jnp.where(mask, st, NEG) for st in sts]
        return _softmax_steps(sts, carry, s_ref, acc_ref, lambda hh: vt_ref[hh, n])

    acc_ref[...] = jnp.zeros(acc_ref.shape, F32)
    carry = lax.fori_loop(0, qi, lambda n, c: step(n, c, None), _init_carry(hg, blk))
    key = lax.broadcasted_iota(I32, (blk, blk), 0)
    qry = lax.broadcasted_iota(I32, (blk, blk), 1)
    step(qi, carry, key <= qry)
    for hh, hs in enumerate(heads):
        o_ref[:, hs] = _normalized(acc_ref, hh, d).T.astype(o_ref.dtype)


def _moba_attention(qkv, vt, n_heads, q_col, k_col, hg=8):
    B, S, _ = qkv.shape
    assert S % MOBA_BLOCK == 0 and S // MOBA_BLOCK >= MOBA_TOPK
    nb = S // MOBA_BLOCK
    d = HEAD_DIM
    hg = math.gcd(math.gcd(hg, n_heads), math.gcd(q_col, k_col))
    est = 2 * (2 * MOBA_BLOCK * hg * d * 2 + 2 * S * hg * d * 2) + hg * (3 * MOBA_BLOCK * MOBA_BLOCK * 4)
    return pl.pallas_call(
        functools.partial(_moba_body, nb=nb, hg=hg),
        grid=(B, n_heads // hg, nb),
        in_specs=[pl.BlockSpec((None, MOBA_BLOCK, hg * d), lambda b, g, i: (b, i, q_col // hg + g)),
                  pl.BlockSpec((None, S, hg * d), lambda b, g, i: (b, 0, k_col // hg + g)),
                  pl.BlockSpec((None, hg, nb, d + ONES_ROWS, MOBA_BLOCK), lambda b, g, i: (b, g, 0, 0, 0))],
        out_specs=pl.BlockSpec((None, MOBA_BLOCK, hg * d), lambda b, g, i: (b, i, g)),
        out_shape=jax.ShapeDtypeStruct((B, S, n_heads * d), BF16),
        scratch_shapes=[pltpu.VMEM((hg, nb, d), F32), pltpu.VMEM((hg, nb, MOBA_BLOCK), F32),
                        pltpu.VMEM((hg, MOBA_BLOCK, MOBA_BLOCK), F32),
                        pltpu.VMEM((hg, d + ONES_ROWS, MOBA_BLOCK), F32)],
        compiler_params=_params(("parallel", "parallel", "arbitrary"), est),
        name="moba_attention",
    )(qkv, qkv, vt)


def _xattn_body(q_ref, k_ref, v_ref, o_ref, *, n_heads):
    for h in range(n_heads):
        sl = slice(h * HEAD_DIM, (h + 1) * HEAD_DIM)
        s = _nt(q_ref[:, sl], k_ref[:, sl])
        m = jnp.max(s, axis=-1, keepdims=True)
        p = jnp.exp2(s - m)
        l = jnp.sum(p, axis=-1, keepdims=True)
        o = jnp.dot(p.astype(BF16), v_ref[:, sl], preferred_element_type=F32)
        o_ref[:, sl] = (o / l).astype(o_ref.dtype)


def _cross_attention(q, kv, n_heads, tq=512):
    B, S, X = q.shape
    M = kv.shape[1]
    tq = _tile(S, tq, 16)
    est = 2 * (2 * tq * X * 2 + 2 * M * X * 2) + 6 * tq * M * 4
    return pl.pallas_call(
        functools.partial(_xattn_body, n_heads=n_heads),
        grid=(B, S // tq),
        in_specs=[pl.BlockSpec((None, tq, X), lambda b, i: (b, i, 0)),
                  pl.BlockSpec((None, M, X), lambda b, i: (b, 0, 0)),
                  pl.BlockSpec((None, M, X), lambda b, i: (b, 0, 1))],
        out_specs=pl.BlockSpec((None, tq, X), lambda b, i: (b, i, 0)),
        out_shape=jax.ShapeDtypeStruct((B, S, X), BF16),
        compiler_params=_params(("parallel", "parallel"), est),
        name="cross_attention",
    )(q, kv, kv)


def _odd_prep_body(x_ref, gqa_ref, gkv_ref, gkr_ref, gik_ref, cos_ref, sin_ref,
                   qa_ref, ckv_ref, kr_ref, ik_ref, iw_ref, *, c1, c2, iw_scale):
    def norm(x, g):
        ms = jnp.mean(x * x, axis=-1, keepdims=True)
        return x * lax.rsqrt(ms + EPS) * g

    def rope(y):
        lane = lax.broadcasted_iota(I32, y.shape, 1)
        half = C_ROPE // 2
        first = (lane % C_ROPE) < half
        r = jnp.where(first, pltpu.roll(y, LANE - half, axis=1), pltpu.roll(y, half, axis=1))
        return y * cos_ref[...] + r * sin_ref[...]

    qa_ref[...] = norm(x_ref[:, :c1], gqa_ref[...]).astype(qa_ref.dtype)
    ckv_ref[...] = norm(x_ref[:, c1:c2], gkv_ref[...]).astype(ckv_ref.dtype)
    slab_a = x_ref[:, c2:c2 + LANE]
    slab_b = x_ref[:, c2 + LANE:c2 + 2 * LANE]
    lane = lax.broadcasted_iota(I32, slab_a.shape, 1)
    low = lane < C_ROPE
    ms = jnp.sum(jnp.where(low, slab_a * slab_a, 0.0), axis=-1, keepdims=True) / C_ROPE
    kr = rope(slab_a * lax.rsqrt(ms + EPS) * gkr_ref[...])
    kr_ref[0] = kr.astype(kr_ref.dtype)
    kr_ref[1] = pltpu.roll(kr, C_ROPE, axis=1).astype(kr_ref.dtype)
    rot_a = pltpu.roll(slab_a, C_ROPE, axis=1)
    rot_b = pltpu.roll(slab_b, C_ROPE, axis=1)
    ik = jnp.where(low, rot_a, rot_b)
    ik_ref[...] = rope(norm(ik, gik_ref[...])).astype(ik_ref.dtype)
    iw_ref[...] = rot_b * iw_scale


def _odd_prep(x, c1, c2, g_qa, g_kv, g_kr, g_ik, cos_p, sin_p, iw_scale, tm=256):
    B, S, C = x.shape
    assert C == c2 + 2 * LANE
    tm = _tile(S, tm, 16)
    vec = lambda n: pl.BlockSpec((1, n), lambda b, i: (0, 0))
    tab = pl.BlockSpec((tm, LANE), lambda b, i: (i, 0))
    est = 2 * tm * C * 4 * 2
    return pl.pallas_call(
        functools.partial(_odd_prep_body, c1=c1, c2=c2, iw_scale=iw_scale),
        grid=(B, S // tm),
        in_specs=[pl.BlockSpec((None, tm, C), lambda b, i: (b, i, 0)),
                  vec(c1), vec(c2 - c1), vec(LANE), vec(LANE), tab, tab],
        out_specs=[pl.BlockSpec((None, tm, c1), lambda b, i: (b, i, 0)),
                   pl.BlockSpec((None, tm, c2 - c1), lambda b, i: (b, i, 0)),
                   pl.BlockSpec((None, 2, tm, LANE), lambda b, i: (b, 0, i, 0)),
                   pl.BlockSpec((None, tm, LANE), lambda b, i: (b, i, 0)),
                   pl.BlockSpec((None, tm, LANE), lambda b, i: (b, i, 0))],
        out_shape=[jax.ShapeDtypeStruct((B, S, c1), BF16),
                   jax.ShapeDtypeStruct((B, S, c2 - c1), BF16),
                   jax.ShapeDtypeStruct((B, 2, S, LANE), BF16),
                   jax.ShapeDtypeStruct((B, S, LANE), BF16),
                   jax.ShapeDtypeStruct((B, S, LANE), F32)],
        compiler_params=_params(("parallel", "parallel"), est),
        name="odd_prep",
    )(x, g_qa.reshape(1, -1), g_kv.reshape(1, -1), g_kr.reshape(1, -1), g_ik.reshape(1, -1), cos_p, sin_p)


def _q_prep_body(x_ref, gn_ref, gr_ref, cos_ref, sin_ref, qn_ref, qr_ref, *, n_heads):
    nope_w = n_heads * C_NOPE
    width = C_NOPE + C_ROPE
    half = C_ROPE // 2
    lane = lax.broadcasted_iota(I32, (x_ref.shape[0], LANE), 1)
    low = lane < C_ROPE
    first = (lane % C_ROPE) < half
    for p in range(n_heads // 2):
        n0 = x_ref[:, (2 * p) * LANE:(2 * p + 1) * LANE]
        n1 = x_ref[:, (2 * p + 1) * LANE:(2 * p + 2) * LANE]
        r = x_ref[:, nope_w + p * LANE:nope_w + (p + 1) * LANE]
        r2 = r * r
        ss0 = jnp.sum(n0 * n0, axis=-1, keepdims=True) + jnp.sum(jnp.where(low, r2, 0.0), axis=-1, keepdims=True)
        ss1 = jnp.sum(n1 * n1, axis=-1, keepdims=True) + jnp.sum(jnp.where(low, 0.0, r2), axis=-1, keepdims=True)
        inv0 = lax.rsqrt(ss0 / width + EPS)
        inv1 = lax.rsqrt(ss1 / width + EPS)
        qn_ref[:, (2 * p) * LANE:(2 * p + 1) * LANE] = (n0 * inv0 * gn_ref[...]).astype(qn_ref.dtype)
        qn_ref[:, (2 * p + 1) * LANE:(2 * p + 2) * LANE] = (n1 * inv1 * gn_ref[...]).astype(qn_ref.dtype)
        y = r * jnp.where(low, inv0, inv1) * gr_ref[...]
        rot = jnp.where(first, pltpu.roll(y, LANE - half, axis=1), pltpu.roll(y, half, axis=1))
        qr_ref[:, p * LANE:(p + 1) * LANE] = (y * cos_ref[...] + rot * sin_ref[...]).astype(qr_ref.dtype)


def _q_prep(x, n_heads, g_nope, g_rope2, cos_q, sin_q, tm=256):
    B, S, C = x.shape
    tm = _tile(S, tm, 16)
    nope_w, rope_w = n_heads * C_NOPE, n_heads * C_ROPE
    vec = pl.BlockSpec((1, LANE), lambda b, i: (0, 0))
    tab = pl.BlockSpec((tm, LANE), lambda b, i: (i, 0))
    est = 2 * tm * C * 6
    return pl.pallas_call(
        functools.partial(_q_prep_body, n_heads=n_heads),
        grid=(B, S // tm),
        in_specs=[pl.BlockSpec((None, tm, C), lambda b, i: (b, i, 0)), vec, vec, tab, tab],
        out_specs=[pl.BlockSpec((None, tm, nope_w), lambda b, i: (b, i, 0)),
                   pl.BlockSpec((None, tm, rope_w), lambda b, i: (b, i, 0))],
        out_shape=[jax.ShapeDtypeStruct((B, S, nope_w), BF16), jax.ShapeDtypeStruct((B, S, rope_w), BF16)],
        compiler_params=_params(("parallel", "parallel"), est),
        name="dsa_q_prep",
    )(x, g_nope.reshape(1, LANE), g_rope2.reshape(1, LANE), cos_q, sin_q)


def _indexer_body(ik_ref, iq_ref, iwt_ref, o_ref, key_ref, *, tq, n_heads, n_keep):
    S = ik_ref.shape[0]
    qi = pl.program_id(1)
    n_tiles = qi + 1
    t_idx = qi * tq + lax.broadcasted_iota(I32, (tq, tq), 1)
    s_loc = lax.broadcasted_iota(I32, (tq, tq), 0)

    def score_tile(kt, c):
        off = pl.multiple_of(kt * tq, tq)
        ikt = ik_ref[pl.ds(off, tq), :]
        acc = jnp.zeros((tq, tq), F32)
        for h in range(n_heads):
            r = _nt(ikt, iq_ref[:, h * IDX_DIM:(h + 1) * IDX_DIM])
            acc = acc + jnp.maximum(r, 0.0) * iwt_ref[h:h + 1, :]
        acc = jnp.where(off + s_loc <= t_idx, acc, -jnp.inf)
        bits = lax.bitcast_convert_type(acc, I32)
        key_ref[pl.ds(off, tq), :] = bits ^ ((bits >> 31) & 0x7FFFFFFF)
        return c

    lax.fori_loop(0, n_tiles, score_tile, 0)

    def count(hit):
        def body(kt, cnt):
            off = pl.multiple_of(kt * tq, tq)
            one = jnp.where(hit(key_ref[pl.ds(off, tq), :], off + s_loc), 1, 0).astype(I32)
            return cnt + jnp.sum(one.reshape(tq // 8, 8, tq), axis=0)
        cnt = lax.fori_loop(0, n_tiles, body, jnp.zeros((8, tq), I32))
        return jnp.sum(cnt, axis=0, keepdims=True)

    n_pos = count(lambda k, s: k >= 0)
    thr = jnp.where(n_pos >= n_keep, 0, INT_MIN).astype(I32)

    def bit_step(i, thr):
        cand = thr + lax.shift_left(jnp.int32(1), 30 - i)
        return jnp.where(count(lambda k, s: k >= cand) >= n_keep, cand, thr)

    thr = lax.fori_loop(0, 31, bit_step, thr)

    def tie_cut():
        need = n_keep - count(lambda k, s: k > thr)
        bits = S.bit_length()

        def step(i, cut):
            cand = cut + lax.shift_left(jnp.int32(1), bits - 1 - i)
            below = count(lambda k, s: (k == thr) & (s < cand))
            return jnp.where(below < need, cand, cut)

        return lax.fori_loop(0, bits, step, jnp.zeros((1, tq), I32))

    has_ties = jnp.max(count(lambda k, s: k >= thr)) > n_keep
    cut = lax.cond(has_ties, tie_cut, lambda: jnp.full((1, tq), S, I32))

    def out_tile(kt, c):
        off = pl.multiple_of(kt * tq, tq)
        key = key_ref[pl.ds(off, tq), :]
        s_idx = off + s_loc
        ok = ((key > thr) | ((key == thr) & (s_idx <= cut))) & (s_idx <= t_idx)
        o_ref[pl.ds(off, tq), :] = jnp.where(ok, 0.0, NEG).astype(o_ref.dtype)
        return c

    def neg_tile(kt, c):
        off = pl.multiple_of(kt * tq, tq)
        o_ref[pl.ds(off, tq), :] = jnp.full((tq, tq), NEG, o_ref.dtype)
        return c

    lax.fori_loop(0, n_tiles, out_tile, 0)
    lax.fori_loop(n_tiles, S // tq, neg_tile, 0)


def _indexer(ik, iq, iwt, n_keep, tq=256):
    B, S, _ = ik.shape
    n_heads = iwt.shape[1]
    tq = _tile(S, tq, LANE)
    est = 2 * (S * IDX_DIM * 2 + tq * n_heads * IDX_DIM * 2 + n_heads * tq * 4 + S * tq * 2) + S * tq * 4 + 8 * tq * tq * 4
    return pl.pallas_call(
        functools.partial(_indexer_body, tq=tq, n_heads=n_heads, n_keep=n_keep),
        grid=(B, S // tq),
        in_specs=[pl.BlockSpec((None, S, IDX_DIM), lambda b, i: (b, 0, 0)),
                  pl.BlockSpec((None, tq, n_heads * IDX_DIM), lambda b, i: (b, i, 0)),
                  pl.BlockSpec((None, n_heads, tq), lambda b, i: (b, 0, i))],
        out_specs=pl.BlockSpec((None, S, tq), lambda b, i: (b, 0, i)),
        out_shape=jax.ShapeDtypeStruct((B, S, S), BF16),
        scratch_shapes=[pltpu.VMEM((S, tq), I32)],
        compiler_params=_params(("parallel", "arbitrary"), est),
        name="dsa_indexer",
    )(ik, iq, iwt)


def _dsa_attn_body(qn_ref, qr_ref, kn_ref, kr_ref, vt_ref, bias_ref, o_ref, s_ref, acc_ref, *, tq, tk, hg):
    qi = pl.program_id(2)
    heads = [slice(hh * C_NOPE, (hh + 1) * C_NOPE) for hh in range(hg)]
    qs = [jnp.concatenate([qn_ref[:, hs], qr_ref[:, (hh // 2) * LANE:(hh // 2 + 1) * LANE]], axis=1)
          for hh, hs in enumerate(heads)]

    def scores(n):
        off = pl.multiple_of(n * tk, tk)
        bias = bias_ref[pl.ds(off, tk), :].astype(F32)
        kr = [kr_ref[par, pl.ds(off, tk), :] for par in range(2)]
        return tuple(bias + _nt(jnp.concatenate([kn_ref[pl.ds(off, tk), hs], kr[hh % 2]], axis=1), qs[hh])
                     for hh, hs in enumerate(heads))

    def body(n, carry):
        return _softmax_steps(scores(n), carry, s_ref, acc_ref, lambda hh: vt_ref[hh, n])

    acc_ref[...] = jnp.zeros(acc_ref.shape, F32)
    lax.fori_loop(0, ((qi + 1) * tq + tk - 1) // tk, body, _init_carry(hg, tq))
    for hh, hs in enumerate(heads):
        o_ref[:, hs] = _normalized(acc_ref, hh, C_NOPE).T.astype(o_ref.dtype)


def _dsa_attention(qn, qr, kv, kr2, vt, bias, n_heads, tq, hg=8):
    B, S, _ = qn.shape
    hg = min(hg, n_heads)
    assert hg % 2 == 0 and n_heads % hg == 0
    ng = n_heads // hg
    nk, tk = vt.shape[2], vt.shape[4]
    est = 2 * (tq * hg * 192 * 2 + 2 * S * hg * LANE * 2 + 2 * S * LANE * 2 + S * tq * 2 + tq * hg * LANE * 2) \
        + hg * 3 * tk * tq * 4
    return pl.pallas_call(
        functools.partial(_dsa_attn_body, tq=tq, tk=tk, hg=hg),
        grid=(B, ng, S // tq),
        in_specs=[pl.BlockSpec((None, tq, hg * C_NOPE), lambda b, g, i: (b, i, g)),
                  pl.BlockSpec((None, tq, hg * C_ROPE), lambda b, g, i: (b, i, g)),
                  pl.BlockSpec((None, S, hg * C_NOPE), lambda b, g, i: (b, 0, g)),
                  pl.BlockSpec((None, 2, S, LANE), lambda b, g, i: (b, 0, 0, 0)),
                  pl.BlockSpec((None, hg, nk, C_NOPE + ONES_ROWS, tk), lambda b, g, i: (b, g, 0, 0, 0)),
                  pl.BlockSpec((None, S, tq), lambda b, g, i: (b, 0, i))],
        out_specs=pl.BlockSpec((None, tq, hg * C_NOPE), lambda b, g, i: (b, i, g)),
        out_shape=jax.ShapeDtypeStruct((B, S, n_heads * C_NOPE), BF16),
        scratch_shapes=[pltpu.VMEM((hg, tk, tq), F32), pltpu.VMEM((hg, C_NOPE + ONES_ROWS, tq), F32)],
        compiler_params=_params(("parallel", "parallel", "arbitrary"), est),
        name="dsa_attention",
    )(qn, qr, kv, kr2, vt, bias)


def _rope_tables(seq, dim):
    inv_freq = ROPE_THETA ** (-jnp.arange(0, dim, 2, dtype=F32) / dim)
    ang = jnp.arange(seq, dtype=F32)[:, None] * inv_freq[None, :]
    return jnp.cos(ang), jnp.sin(ang)


def _cross_block(x, mem, l, norm_xattn, norm_mem, xa_wq, xa_wk, xa_wv, xa_wo, xa_qnorm, xa_knorm, B, S):
    N, D = x.shape
    X = xa_wq.shape[2]
    n_heads = X // HEAD_DIM
    M = mem.shape[1]
    h = _rmsnorm(x, norm_xattn[l], BF16)
    mn = _rmsnorm(mem.reshape(B * M, D), norm_mem[l], BF16)
    ones = jnp.ones((S, LANE), F32)
    q = _matmul(h, xa_wq[l].astype(BF16), out_dtype=BF16, name="xattn_q",
                epi=dict(flags=[EPI_NORM] * (X // _tile(X, 1024, LANE)), modes=(EPI_NORM,), rope_half=0,
                         tn=_tile(X, 1024, LANE), cos=ones, sin=ones,
                         gain=jnp.tile(xa_qnorm[l] * (HEAD_DIM ** -0.5 * LOG2E), n_heads)))
    wkv = jnp.concatenate([xa_wk[l], xa_wv[l]], axis=1).astype(BF16)
    tn = _tile(X, 1024, LANE)
    kv = _matmul(mn, wkv, out_dtype=BF16, name="xattn_kv",
                 epi=dict(flags=[EPI_NORM] * (X // tn) + [EPI_NONE] * (X // tn), modes=(EPI_NORM,), rope_half=0,
                          tn=tn, cos=jnp.ones((M, LANE), F32), sin=jnp.ones((M, LANE), F32),
                          gain=jnp.concatenate([jnp.tile(xa_knorm[l], n_heads), jnp.ones((X,), F32)])))
    o = _cross_attention(q.reshape(B, S, X), kv.reshape(B, M, 2 * X), n_heads)
    return _matmul(o.reshape(N, X), xa_wo[l].astype(BF16), out_dtype=F32, name="xattn_out", resid=x)


def _even_mixer(x, l, i, B, S, norm_mix, ev_w_in, ev_a_qnorm, ev_a_knorm, lam_params, ev_a_subln,
                ev_b_qnorm, ev_b_knorm, ev_w_out):
    N, D = x.shape
    a_heads = D // (4 * HEAD_DIM)
    b_heads = D // (2 * HEAD_DIM)
    aw = a_heads * 2 * HEAD_DIM
    bw = b_heads * HEAD_DIM
    width = 3 * aw + 3 * bw
    scale = HEAD_DIM ** -0.5 * LOG2E
    cos, sin = _rope_tables(S, HEAD_DIM)
    cos2 = jnp.concatenate([cos, cos], axis=-1)
    sin2 = jnp.concatenate([-sin, sin], axis=-1)
    tn = _tile(math.gcd(aw, bw), 1024, LANE)
    seg = [(aw, EPI_NORM_ROPE, ev_a_qnorm[i] * scale), (aw, EPI_NORM_ROPE, ev_a_knorm[i]), (aw, EPI_NONE, None),
           (bw, EPI_NORM_ROPE, ev_b_qnorm[i] * scale), (bw, EPI_NORM_ROPE, ev_b_knorm[i]), (bw, EPI_NONE, None)]
    flags, gains = [], []
    for w, flag, g in seg:
        flags += [flag] * (w // tn)
        gains.append(jnp.ones((w,), F32) if g is None else jnp.tile(g.astype(F32), w // HEAD_DIM))
    h = _rmsnorm(x, norm_mix[l], BF16)
    qkv = _matmul(h, ev_w_in[i], out_dtype=BF16, name="even_in",
                  epi=dict(flags=flags, modes=(EPI_NORM_ROPE,), rope_half=HEAD_DIM // 2, tn=tn,
                           cos=cos2, sin=sin2, gain=jnp.concatenate(gains)))
    qkv = qkv.reshape(B, S, width)
    lam_init = 0.8 - 0.6 * math.exp(-0.3 * l)
    tq = _tile(S, 256, LANE)
    ya = _diff_attention(qkv, _value_tiles(qkv[:, :, 2 * aw:3 * aw], a_heads, tq), a_heads, lam_params,
                         ev_a_subln[i], lam_init, tq)
    c0 = 3 * aw // HEAD_DIM
    yb = _moba_attention(qkv, _value_tiles(qkv[:, :, 3 * aw + 2 * bw:], b_heads, MOBA_BLOCK), b_heads,
                         c0, c0 + b_heads)
    y = jnp.concatenate([ya, yb], axis=-1).reshape(N, aw + bw)
    return _matmul(y, ev_w_out[i].astype(BF16), out_dtype=F32, name="even_out", resid=x)


def _odd_mixer(x, l, i, B, S, norm_mix, od_w_in, od_qa_norm, od_w_qb, od_q_norm, od_kv_norm, od_kr_norm,
               od_w_uk, od_w_uv, od_w_iqb, od_ik_norm, od_w_out):
    N, D = x.shape
    c1 = od_qa_norm.shape[1]
    kv_rank = od_kv_norm.shape[1]
    c2 = c1 + kv_rank
    n_heads = od_w_uk.shape[2]
    idx_heads = od_w_iqb.shape[2] // IDX_DIM
    assert od_w_in.shape[2] == c2 + C_ROPE + IDX_DIM + idx_heads and idx_heads == C_ROPE
    scale = (C_NOPE + C_ROPE) ** -0.5 * LOG2E
    cos, sin = _rope_tables(S, C_ROPE)
    one, zero = jnp.ones((S, C_ROPE), F32), jnp.zeros((S, C_ROPE), F32)
    cos_p = jnp.concatenate([cos, cos, one], axis=-1)
    sin_p = jnp.concatenate([-sin, sin, zero], axis=-1)
    cos_q = jnp.concatenate([cos, cos, cos, cos], axis=-1)
    sin_q = jnp.concatenate([-sin, sin, -sin, sin], axis=-1)

    h = _rmsnorm(x, norm_mix[l], BF16)
    proj = _matmul(h, od_w_in[i], out_dtype=F32, name="odd_in")
    g_kr = jnp.concatenate([od_kr_norm[i], jnp.zeros((LANE - C_ROPE,), F32)])
    qa, ckv, kr2, ik, iw = _odd_prep(proj.reshape(B, S, -1), c1, c2, od_qa_norm[i], od_kv_norm[i], g_kr,
                                     od_ik_norm[i], cos_p, sin_p, idx_heads ** -0.5 * IDX_DIM ** -0.5)
    qa = qa.reshape(N, c1)
    wqb = od_w_qb[i].reshape(c1, n_heads, C_NOPE + C_ROPE)
    wqb = jnp.concatenate([wqb[:, :, :C_NOPE].reshape(c1, -1), wqb[:, :, C_NOPE:].reshape(c1, -1)], axis=1)
    qraw = _matmul(qa, wqb.astype(BF16), out_dtype=F32, name="odd_qb")
    qn, qr = _q_prep(qraw.reshape(B, S, -1), n_heads, od_q_norm[i][:C_NOPE] * scale,
                     jnp.tile(od_q_norm[i][C_NOPE:], 2) * scale, cos_q, sin_q)
    tn = _tile(idx_heads * IDX_DIM, 1024, LANE)
    iq = _matmul(qa, od_w_iqb[i], out_dtype=BF16, name="odd_iqb",
                 epi=dict(flags=[EPI_ROPE] * (idx_heads * IDX_DIM // tn), modes=(EPI_ROPE,), rope_half=IDX_ROPE // 2,
                          tn=tn, cos=cos_p, sin=sin_p, gain=jnp.ones((idx_heads * IDX_DIM,), F32)))
    wkv = jnp.concatenate([od_w_uk[i].reshape(kv_rank, -1), od_w_uv[i].reshape(kv_rank, -1)], axis=1)
    kv = _matmul(ckv.reshape(N, kv_rank), wkv.astype(BF16), out_dtype=BF16, name="odd_kv")

    n_keep = min(IDX_TOPK, S // 4)
    tq = _tile(S, 256, LANE)
    iwt = jnp.swapaxes(iw[:, :, :idx_heads], 1, 2)
    bias = _indexer(ik, iq.reshape(B, S, -1), iwt, n_keep, tq)
    kv = kv.reshape(B, S, -1)
    vt = _value_tiles(kv[:, :, n_heads * C_NOPE:], n_heads, _tile(S, 2 * tq, tq))
    y = _dsa_attention(qn, qr, kv, kr2, vt, bias, n_heads, tq)
    return _matmul(y.reshape(N, -1), od_w_out[i].astype(BF16), out_dtype=F32, name="odd_out", resid=x)


def _dense_ffn(x, g, wg, wu, wd):
    N, D = x.shape
    F = wg.shape[1]
    h = _rmsnorm(x, g, BF16)
    tm = _tile(N, 2048, 16)
    tn = _tile(F, 512, LANE)
    nt = N // tm
    hid = _swiglu_up(h, wg[None], wu[None], jnp.zeros((nt,), I32), jnp.full((1,), nt, I32),
                     jnp.full((nt,), tm, I32), tm=tm, tn=tn, name="ffn_up")
    return _matmul(hid, wd.astype(BF16), out_dtype=F32, name="ffn_down", resid=x, tm=512, tn=1024,
                   tk=F if F <= 4096 else _tile(F, F // 2, LANE))


def _moe_ffn(x, g, router, wg, wu, wd, tm=512):
    N, D = x.shape
    E, _, F = wg.shape
    h, gate, sel = _norm_router(x, g, router)
    tm = _tile(N, tm, 16)
    cnt = jnp.sum(sel, axis=0)
    tiles_e = (cnt + tm - 1) // tm
    tile_end = jnp.cumsum(tiles_e)
    start = (tile_end - tiles_e) * tm
    rank = jnp.cumsum(sel, axis=0) - sel
    P = N * TOP_K + E * tm
    n_tiles = P // tm
    slot = (start[None, :] + rank).astype(I32)
    lane = jnp.arange(E, dtype=I32)[None, :]
    e2 = jnp.stack([jnp.min(jnp.where(sel > 0, lane, E), axis=1), jnp.max(jnp.where(sel > 0, lane, -1), axis=1)], 1)
    slots2 = jnp.take_along_axis(slot, e2, axis=1)
    gates2 = jnp.take_along_axis(gate, e2, axis=1)
    tok2 = jnp.broadcast_to(jnp.arange(N, dtype=I32)[:, None], (N, TOP_K))
    tok_of_slot = jnp.zeros((P,), I32).at[slots2.reshape(-1)].set(tok2.reshape(-1), unique_indices=True)
    tile_expert = jnp.minimum(jnp.searchsorted(tile_end, jnp.arange(n_tiles, dtype=I32), side="right"),
                              E - 1).astype(I32)
    n_active = tile_end[-1:].astype(I32)
    tile_row = jnp.arange(n_tiles, dtype=I32) * tm
    valid_rows = jnp.where(tile_row < n_active[0] * tm,
                           jnp.clip((start + cnt)[tile_expert] - tile_row, 0, tm), 0).astype(I32)

    xs = _gather_rows(h, tok_of_slot, BF16)
    hid = _swiglu_up(xs, wg, wu, tile_expert, n_active, valid_rows, tm=tm, tn=_tile(F, 512, LANE), name="moe_up")
    y = _grouped_down(hid, wd, tile_expert, n_active, valid_rows, tm=tm, tn=_tile(D, 1024, LANE), name="moe_down")
    return _moe_combine(x, y, slots2, gates2)


def kernel(x, mem, norm_mix, norm_xattn, norm_mem, norm_ffn, ev_w_in, ev_a_qnorm, ev_a_knorm, ev_lambda_q1, ev_lambda_k1, ev_lambda_q2, ev_lambda_k2, ev_a_subln, ev_b_qnorm, ev_b_knorm, ev_w_out, od_w_in, od_qa_norm, od_w_qb, od_q_norm, od_kv_norm, od_kr_norm, od_w_uk, od_w_uv, od_w_iqb, od_ik_norm, od_w_out, xa_wq, xa_wk, xa_wv, xa_wo, xa_qnorm, xa_knorm, ffn_wg, ffn_wu, ffn_wd, moe_router, moe_wg, moe_wu, moe_wd):
    B, S, D = x.shape
    depth = norm_mix.shape[0]
    x = x.reshape(B * S, D)
    for l in range(depth):
        i = l // 2
        if l % 2 == 0:
            x = _even_mixer(x, l, i, B, S, norm_mix, ev_w_in, ev_a_qnorm, ev_a_knorm,
                            (ev_lambda_q1[i], ev_lambda_k1[i], ev_lambda_q2[i], ev_lambda_k2[i]),
                            ev_a_subln, ev_b_qnorm, ev_b_knorm, ev_w_out)
        else:
            x = _odd_mixer(x, l, i, B, S, norm_mix, od_w_in, od_qa_norm, od_w_qb, od_q_norm, od_kv_norm,
                           od_kr_norm, od_w_uk, od_w_uv, od_w_iqb, od_ik_norm, od_w_out)
        x = _cross_block(x, mem, l, norm_xattn, norm_mem, xa_wq, xa_wk, xa_wv, xa_wo, xa_qnorm, xa_knorm, B, S)
        if l % 2 == 0:
            x = _dense_ffn(x, norm_ffn[l], ffn_wg[i], ffn_wu[i], ffn_wd[i])
        else:
            x = _moe_ffn(x, norm_ffn[l], moe_router[i], moe_wg[i], moe_wu[i], moe_wd[i])
    return x.reshape(B, S, D)
```
